```python
import jax, jax.numpy as jnp
from jax import lax
import numpy as np

D_MODEL = 1024
BATCH = 8
SEQ = 2048
DEPTH = 1

MEM_LEN = 256
ATT_HEADS = 8
ATT_KV_HEADS = 2
ATT_HEAD_DIM = 64
ATT_WIDTH = ATT_HEADS * ATT_HEAD_DIM
ATT_KV_WIDTH = ATT_KV_HEADS * ATT_HEAD_DIM
WINDOW = 128
ATT_BLOCK = 128
ROPE_THETA = 500000.0
ROPE_DIM = ATT_HEAD_DIM // 4
HG_HEADS = 4
HG_DIM = 128
HG_WIDTH = HG_HEADS * HG_DIM
HG_CHUNK = 64
MIX_WIDTH = ATT_WIDTH + HG_WIDTH
IN_COLS = ATT_WIDTH + 2 * ATT_KV_WIDTH + 4 * HG_WIDTH
X_HEADS = 4
X_HEAD_DIM = D_MODEL // X_HEADS
FFN_HIDDEN = -(-8 * D_MODEL // (3 * 256)) * 256
RMS_EPS = 1e-6

kernel_name = 'hymba_swa_sink_hgrn2_xattn_layer'


def rms_norm(x, gain):
    xf = x.astype(jnp.float32)
    y = xf * lax.rsqrt(jnp.mean(xf * xf, axis=-1, keepdims=True) + RMS_EPS)
    return (y * gain.astype(jnp.float32)).astype(x.dtype)


def partial_rotary(x, positions):
    half = ROPE_DIM // 2
    inv_freq = jnp.power(jnp.float32(ROPE_THETA), -jnp.arange(half, dtype=jnp.float32) * (2.0 / ROPE_DIM))
    ang = positions.astype(jnp.float32)[..., None] * inv_freq
    cos = jnp.cos(ang)[:, :, None, :]
    sin = jnp.sin(ang)[:, :, None, :]
    xr = x[..., :ROPE_DIM].astype(jnp.float32)
    x1, x2 = xr[..., :half], xr[..., half:]
    rot = jnp.concatenate([x1 * cos - x2 * sin, x2 * cos + x1 * sin], axis=-1).astype(x.dtype)
    return jnp.concatenate([rot, x[..., ROPE_DIM:]], axis=-1)


def sliding_window_sink_attention(q, k, v, sinks):
    B, S = q.shape[0], q.shape[1]
    nb = S // ATT_BLOCK
    G = ATT_HEADS // ATT_KV_HEADS
    qb = q.reshape(B, nb, ATT_BLOCK, ATT_KV_HEADS, G, ATT_HEAD_DIM)
    pad = ((0, 0), (ATT_BLOCK, 0), (0, 0), (0, 0))
    kp = jnp.pad(k, pad).reshape(B, nb + 1, ATT_BLOCK, ATT_KV_HEADS, ATT_HEAD_DIM)
    vp = jnp.pad(v, pad).reshape(B, nb + 1, ATT_BLOCK, ATT_KV_HEADS, ATT_HEAD_DIM)
    kb = jnp.concatenate([kp[:, :-1], kp[:, 1:]], axis=2)
    vb = jnp.concatenate([vp[:, :-1], vp[:, 1:]], axis=2)
    scores = jnp.einsum('bnqhgd,bnkhd->bnhgqk', qb, kb).astype(jnp.float32) * (ATT_HEAD_DIM ** -0.5)
    q_rel = jnp.arange(ATT_BLOCK)[:, None] + ATT_BLOCK
    k_rel = jnp.arange(2 * ATT_BLOCK)[None, :]
    diff = q_rel - k_rel
    band = (diff >= 0) & (diff < WINDOW)
    key_abs = jnp.arange(nb)[:, None] * ATT_BLOCK - ATT_BLOCK + jnp.arange(2 * ATT_BLOCK)[None, :]
    valid = band[None] & (key_abs >= 0)[:, None, :]
    scores = jnp.where(valid[None, :, None, None], scores, -jnp.inf)
    sink = jnp.broadcast_to(sinks.astype(jnp.float32).reshape(1, 1, ATT_KV_HEADS, G, 1, 1),
                            scores.shape[:-1] + (1,))
    probs = jax.nn.softmax(jnp.concatenate([scores, sink], axis=-1), axis=-1)[..., :-1]
    out = jnp.einsum('bnhgqk,bnkhd->bnqhgd', probs.astype(v.dtype), vb)
    return out.reshape(B, S, ATT_WIDTH)


def hgrn2_chunkwise(q, f_logit, i, lb):
    B, S = q.shape[0], q.shape[1]
    nc = S // HG_CHUNK
    lb = lb.reshape(HG_HEADS, HG_DIM).astype(jnp.float32)
    f = lb + (1.0 - lb) * jax.nn.sigmoid(f_logit.astype(jnp.float32))
    log_f = jnp.log(f)
    key = 1.0 - f
    qf = jax.nn.silu(q.astype(jnp.float32))
    vf = i.astype(jnp.float32)

    def to_chunks(t):
        return t.reshape(B, nc, HG_CHUNK, HG_HEADS, t.shape[-1]).transpose(1, 0, 3, 2, 4)

    causal = jnp.tril(jnp.ones((HG_CHUNK, HG_CHUNK), dtype=bool))

    def step(state, inp):
        qc, kc, vc, gc = inp
        b = jnp.cumsum(gc, axis=2)
        o_inter = jnp.einsum('bhtd,bhde->bhte', qc * jnp.exp(b), state)
        rel = b[:, :, :, None, :] - b[:, :, None, :, :]
        decay = jnp.exp(jnp.where(causal[:, :, None], rel, -jnp.inf))
        scores = jnp.einsum('bhtd,bhsd,bhtsd->bhts', qc, kc, decay)
        o_intra = jnp.einsum('bhts,bhse->bhte', scores, vc)
        b_last = b[:, :, -1:, :]
        k_dec = kc * jnp.exp(b_last - b)
        new_state = jnp.exp(b_last[:, :, 0, :])[..., None] * state + jnp.einsum('bhsd,bhse->bhde', k_dec, vc)
        return new_state, o_inter + o_intra

    s0 = jnp.zeros((B, HG_HEADS, HG_DIM, HG_DIM), jnp.float32)
    _, o = lax.scan(step, s0, (to_chunks(qf), to_chunks(key), to_chunks(vf), to_chunks(log_f)))
    return o.transpose(1, 0, 3, 2, 4).reshape(B, S, HG_HEADS, HG_DIM)


def setup_inputs(seed: int = 0) -> dict:
    key = jax.random.key(seed)
    ks = jax.random.split(key, 20)
    f32 = jnp.float32

    def w(k, shape, fan_in):
        return jax.random.normal(k, shape, f32) * (fan_in ** -0.5)

    def gain(k, shape):
        return 1.0 + 0.02 * jax.random.normal(k, shape, f32)

    x = jax.random.normal(ks[0], (BATCH, SEQ, D_MODEL), f32)
    mem = jax.random.normal(ks[1], (BATCH, MEM_LEN, D_MODEL), f32)
    offset = jax.random.randint(ks[2], (BATCH, 1), 0, 4096, dtype=jnp.int32)
    positions = offset + jnp.arange(SEQ, dtype=jnp.int32)[None, :]
    return {
        'x': x,
        'mem': mem,
        'positions': positions,
        'norm_mix': gain(ks[3], (DEPTH, D_MODEL)),
        'w_in': w(ks[4], (DEPTH, D_MODEL, IN_COLS), D_MODEL),
        'att_sinks': 0.5 * jax.random.normal(ks[5], (DEPTH, ATT_HEADS), f32),
        'att_out_gain': gain(ks[6], (DEPTH, ATT_WIDTH)),
        'hg_lb_logits': 0.5 * jax.random.normal(ks[7], (DEPTH + 1, HG_WIDTH), f32),
        'hg_out_gain': gain(ks[8], (DEPTH, HG_WIDTH)),
        'w_out': w(ks[9], (DEPTH, MIX_WIDTH, D_MODEL), MIX_WIDTH),
        'norm_xattn': gain(ks[10], (DEPTH, D_MODEL)),
        'norm_mem': gain(ks[11], (DEPTH, D_MODEL)),
        'w_xq': w(ks[12], (DEPTH, D_MODEL, D_MODEL), D_MODEL),
        'w_xkv': w(ks[13], (DEPTH, D_MODEL, 2 * D_MODEL), D_MODEL),
        'w_xo': w(ks[14], (DEPTH, D_MODEL, D_MODEL), D_MODEL),
        'norm_ffn': gain(ks[15], (DEPTH, D_MODEL)),
        'w_gate_up': w(ks[16], (DEPTH, D_MODEL, 2 * FFN_HIDDEN), D_MODEL),
        'w_down': w(ks[17], (DEPTH, FFN_HIDDEN, D_MODEL), FFN_HIDDEN),
        'norm_final': gain(ks[18], (D_MODEL,)),
    }


def reference(x, mem, positions, norm_mix, w_in, att_sinks, att_out_gain, hg_lb_logits,
              hg_out_gain, w_out, norm_xattn, norm_mem, w_xq, w_xkv, w_xo, norm_ffn,
              w_gate_up, w_down, norm_final):
    B, S = x.shape[0], x.shape[1]
    M = mem.shape[1]
    lower_bounds = jnp.cumsum(jax.nn.softmax(hg_lb_logits.astype(jnp.float32), axis=0), axis=0)
    split_at = np.cumsum([ATT_WIDTH, ATT_KV_WIDTH, ATT_KV_WIDTH, HG_WIDTH, HG_WIDTH, HG_WIDTH]).tolist()
    for l in range(DEPTH):
        h = rms_norm(x, norm_mix[l])
        proj = h @ w_in[l]
        q_a, k_a, v_a, q_r, f_r, i_r, g_r = jnp.split(proj, split_at, axis=-1)
        q_a = partial_rotary(q_a.reshape(B, S, ATT_HEADS, ATT_HEAD_DIM), positions)
        k_a = partial_rotary(k_a.reshape(B, S, ATT_KV_HEADS, ATT_HEAD_DIM), positions)
        v_a = v_a.reshape(B, S, ATT_KV_HEADS, ATT_HEAD_DIM)
        att = sliding_window_sink_attention(q_a, k_a, v_a, att_sinks[l])
        att = rms_norm(att, att_out_gain[l])

        rec = hgrn2_chunkwise(q_r.reshape(B, S, HG_HEADS, HG_DIM),
                              f_r.reshape(B, S, HG_HEADS, HG_DIM),
                              i_r.reshape(B, S, HG_HEADS, HG_DIM),
                              lower_bounds[l])
        rec = rec * lax.rsqrt(jnp.mean(rec * rec, axis=-1, keepdims=True) + RMS_EPS)
        rec = rec.reshape(B, S, HG_WIDTH) * hg_out_gain[l].astype(jnp.float32)
        rec = (rec * jax.nn.silu(g_r.astype(jnp.float32))).astype(x.dtype)

        x = x + jnp.concatenate([att, rec], axis=-1) @ w_out[l]

        hq = rms_norm(x, norm_xattn[l])
        mn = rms_norm(mem, norm_mem[l])
        xq = (hq @ w_xq[l]).reshape(B, S, X_HEADS, X_HEAD_DIM)
        xk, xv = jnp.split(mn @ w_xkv[l], 2, axis=-1)
        xk = xk.reshape(B, M, X_HEADS, X_HEAD_DIM)
        xv = xv.reshape(B, M, X_HEADS, X_HEAD_DIM)
        xs = jnp.einsum('bqhd,bkhd->bhqk', xq, xk).astype(jnp.float32) * (X_HEAD_DIM ** -0.5)
        xp = jax.nn.softmax(xs, axis=-1).astype(xv.dtype)
        xo = jnp.einsum('bhqk,bkhd->bqhd', xp, xv).reshape(B, S, D_MODEL)
        x = x + xo @ w_xo[l]

        hf = rms_norm(x, norm_ffn[l])
        gate, up = jnp.split(hf @ w_gate_up[l], 2, axis=-1)
        x = x + (jax.nn.silu(gate) * up) @ w_down[l]
    return rms_norm(x, norm_final)
```

```python
import functools

import numpy as np
import jax
import jax.numpy as jnp
from jax import lax
from jax.experimental import pallas as pl
from jax.experimental.pallas import tpu as pltpu

F32 = jnp.float32
BF16 = jnp.bfloat16

D_MODEL = 1024
MEM_LEN = 256
ATT_HEADS = 8
ATT_KV_HEADS = 2
ATT_GROUP = ATT_HEADS // ATT_KV_HEADS
ATT_HEAD_DIM = 64
ATT_WIDTH = ATT_HEADS * ATT_HEAD_DIM
ATT_KV_WIDTH = ATT_KV_HEADS * ATT_HEAD_DIM
ATT_BLOCK = 128
ROPE_THETA = 500000.0
ROPE_DIM = ATT_HEAD_DIM // 4
ROPE_HALF = ROPE_DIM // 2
HG_HEADS = 4
HG_DIM = 128
HG_WIDTH = HG_HEADS * HG_DIM
HG_CHUNK = 64
X_HEADS = 4
X_HEAD_DIM = D_MODEL // X_HEADS
FFN_HIDDEN = 2816
RMS_EPS = 1e-6
LANES = 128
MASK_VALUE = -1e30

HG_LEVELS = (32, 16, 8, 4, 2, 1)

VMEM_LIMIT_BYTES = 56 * 1024 * 1024


def _rms_scale(x):
    return lax.rsqrt(jnp.mean(x * x, axis=-1, keepdims=True) + RMS_EPS)


def _sigmoid(x):
    return 1.0 / (1.0 + jnp.exp(-x))


def _dot(a, b):
    return jnp.dot(a, b, preferred_element_type=F32)


def _dot_nt(a, b):
    return lax.dot_general(a, b, (((1,), (1,)), ((), ())), preferred_element_type=F32)


def _dot_tn(a, b):
    return lax.dot_general(a, b, (((0,), (0,)), ((), ())), preferred_element_type=F32)


def _in_proj_kernel(x_ref, gain_ref, w_ref, cos_ref, sin_ref, qa_ref, kv_ref, r_ref):
    x = x_ref[...]
    h = (x * _rms_scale(x) * gain_ref[...]).astype(BF16)
    p = _dot(h, w_ref[...])
    cos = cos_ref[...]
    sin = sin_ref[...]
    lane = lax.broadcasted_iota(jnp.int32, cos.shape, 1)
    first_half = (lane & (ATT_HEAD_DIM - 1)) < ROPE_HALF

    def rotate(blk):
        partner = jnp.where(first_half,
                            pltpu.roll(blk, LANES - ROPE_HALF, axis=1),
                            pltpu.roll(blk, ROPE_HALF, axis=1))
        return blk * cos + partner * sin

    for j in range(ATT_WIDTH // LANES):
        qa_ref[:, j * LANES:(j + 1) * LANES] = rotate(p[:, j * LANES:(j + 1) * LANES]).astype(BF16)
    kv_ref[:, :LANES] = rotate(p[:, ATT_WIDTH:ATT_WIDTH + LANES]).astype(BF16)
    kv_ref[:, LANES:] = p[:, ATT_WIDTH + LANES:ATT_WIDTH + 2 * LANES].astype(BF16)
    r_ref[...] = p[:, ATT_WIDTH + 2 * ATT_KV_WIDTH:].astype(BF16)


def _in_proj(x2d, gain, w_in, cos_t, sin_t, tm):
    n = x2d.shape[0]
    in_cols = w_in.shape[1]
    r_cols = in_cols - ATT_WIDTH - 2 * ATT_KV_WIDTH
    row = lambda i: (i, 0)
    fixed = lambda i: (0, 0)
    return pl.pallas_call(
        _in_proj_kernel,
        grid=(n // tm,),
        in_specs=[
            pl.BlockSpec((tm, D_MODEL), row),
            pl.BlockSpec((1, D_MODEL), fixed),
            pl.BlockSpec((D_MODEL, in_cols), fixed),
            pl.BlockSpec((tm, LANES), row),
            pl.BlockSpec((tm, LANES), row),
        ],
        out_specs=[
            pl.BlockSpec((tm, ATT_WIDTH), row),
            pl.BlockSpec((tm, 2 * ATT_KV_WIDTH), row),
            pl.BlockSpec((tm, r_cols), row),
        ],
        out_shape=[
            jax.ShapeDtypeStruct((n, ATT_WIDTH), BF16),
            jax.ShapeDtypeStruct((n, 2 * ATT_KV_WIDTH), BF16),
            jax.ShapeDtypeStruct((n, r_cols), BF16),
        ],
        compiler_params=pltpu.CompilerParams(
            dimension_semantics=("arbitrary",), vmem_limit_bytes=VMEM_LIMIT_BYTES),
        name="in_proj",
    )(x2d, gain, w_in, cos_t, sin_t)


def _hgrn_constants():
    c = HG_CHUNK
    r = np.arange(c)[:, None]
    j = np.arange(c)[None, :]
    mats = [(j <= r), (j > r)]
    ups, blocks = [], []
    for m in HG_LEVELS:
        ref = (r // (2 * m)) * (2 * m) + m
        upper = r >= ref
        mats.append(np.where(upper, (j >= ref) & (j <= r), (j > r) & (j < ref)))
        ups.append(np.broadcast_to(upper, (c, HG_DIM)))
        blocks.append((r // (2 * m)) == (j // (2 * m)))
    sel = np.concatenate(mats, axis=0).astype(np.float32)
    up = np.concatenate(ups, axis=0).astype(np.float32)
    blk = np.concatenate(blocks, axis=0).astype(np.float32)
    return jnp.asarray(sel, BF16), jnp.asarray(up, F32), jnp.asarray(blk, F32)


def _mixer_kernel(sinks_ref, qa_ref, kvc_ref, kvp_ref, qr_ref, fr_ref, ir_ref, gr_ref, x_ref,
                  wout_ref, gatt_ref, lbl_ref, ghg_ref, sel_ref, up_ref, blk_ref,
                  o_ref, state_ref, att_ref, rec_ref, *, ts):
    step = pl.program_id(1)
    c = HG_CHUNK

    @pl.when(step == 0)
    def _():
        state_ref[...] = jnp.zeros_like(state_ref)

    qi = lax.broadcasted_iota(jnp.int32, (ATT_BLOCK, ATT_BLOCK), 0)
    ki = lax.broadcasted_iota(jnp.int32, (ATT_BLOCK, ATT_BLOCK), 1)
    cur_ok = ki <= qi
    prev_ok = ki > qi
    for jb in range(ts // ATT_BLOCK):
        rows = slice(jb * ATT_BLOCK, (jb + 1) * ATT_BLOCK)
        q_blk = qa_ref[rows, :]
        kv_cur = kvc_ref[rows, :]
        if jb == 0:
            kv_prev = kvp_ref[...]
            prev_bias = jnp.where(step > 0, 0.0, MASK_VALUE)
        else:
            kv_prev = kvc_ref[(jb - 1) * ATT_BLOCK:jb * ATT_BLOCK, :]
            prev_bias = 0.0
        for h in range(ATT_HEADS):
            hk = h // ATT_GROUP
            q_h = q_blk[:, h * ATT_HEAD_DIM:(h + 1) * ATT_HEAD_DIM]
            ksl = slice(hk * ATT_HEAD_DIM, (hk + 1) * ATT_HEAD_DIM)
            vsl = slice(ATT_KV_WIDTH + hk * ATT_HEAD_DIM, ATT_KV_WIDTH + (hk + 1) * ATT_HEAD_DIM)
            scale = ATT_HEAD_DIM ** -0.5
            s_cur = jnp.where(cur_ok, _dot_nt(q_h, kv_cur[:, ksl]) * scale, MASK_VALUE)
            s_prev = jnp.where(prev_ok, _dot_nt(q_h, kv_prev[:, ksl]) * scale, MASK_VALUE) + prev_bias
            sink = sinks_ref[h]
            m = jnp.maximum(jnp.maximum(jnp.max(s_cur, axis=-1, keepdims=True),
                                        jnp.max(s_prev, axis=-1, keepdims=True)), sink)
            p_cur = jnp.exp(s_cur - m)
            p_prev = jnp.exp(s_prev - m)
            denom = (jnp.sum(p_cur, axis=-1, keepdims=True) + jnp.sum(p_prev, axis=-1, keepdims=True)
                     + jnp.exp(sink - m))
            out = _dot(p_cur.astype(BF16), kv_cur[:, vsl]) + _dot(p_prev.astype(BF16), kv_prev[:, vsl])
            att_ref[rows, h * ATT_HEAD_DIM:(h + 1) * ATT_HEAD_DIM] = out / denom

    lbl = lbl_ref[...]
    lexp = jnp.exp(lbl - jnp.max(lbl, axis=0, keepdims=True))
    lb = lexp[0:1, :] / jnp.sum(lexp, axis=0, keepdims=True)
    sel = sel_ref[...]
    n_lv = len(HG_LEVELS)

    def chunk_body(ci, carry):
        r0 = pl.multiple_of(ci * c, c)
        rws = pl.ds(r0, c)
        for h in range(HG_HEADS):
            cols = slice(h * HG_DIM, (h + 1) * HG_DIM)
            lb_h = lb[:, cols]
            f = lb_h + (1.0 - lb_h) * _sigmoid(fr_ref[rws, cols].astype(F32))
            g = jnp.log(f)
            kk = 1.0 - f
            qx = qr_ref[rws, cols].astype(F32)
            qf = qx * _sigmoid(qx)
            vv = ir_ref[rws, cols]
            g_hi = g.astype(BF16)
            g_lo = (g - g_hi.astype(F32)).astype(BF16)
            e2 = _dot(sel, jnp.concatenate([g_hi, g_lo], axis=1))
            w = jnp.exp(e2[:, :HG_DIM] + e2[:, HG_DIM:])
            w_incl = w[0:c]
            w_tail = w[c:2 * c]
            st = state_ref[h]
            o = _dot_nt((qf * w_incl).astype(BF16), st.astype(BF16))
            scores = jnp.zeros((c, c), F32)
            for li in range(n_lv):
                w_l = w[(2 + li) * c:(3 + li) * c]
                up_l = up_ref[li * c:(li + 1) * c, :]
                q_l = (qf * w_l * up_l).astype(BF16)
                k_l = (kk * w_l * (1.0 - up_l)).astype(BF16)
                scores = scores + _dot_nt(q_l, k_l) * blk_ref[li * c:(li + 1) * c, :]
            o = o + _dot(scores.astype(BF16), vv)
            o = o + jnp.sum(qf * kk, axis=-1, keepdims=True) * vv.astype(F32)
            rec_ref[rws, cols] = o
            k_dec = (kk * w_tail).astype(BF16)
            state_ref[h] = st * w_incl[c - 1:c, :] + _dot_tn(vv, k_dec)
        return carry

    lax.fori_loop(0, ts // c, chunk_body, 0)

    att = att_ref[...]
    att_n = (att * _rms_scale(att) * gatt_ref[...]).astype(BF16)
    rec = rec_ref[...]
    parts = []
    for h in range(HG_HEADS):
        r_h = rec[:, h * HG_DIM:(h + 1) * HG_DIM]
        parts.append(r_h * _rms_scale(r_h))
    gx = gr_ref[...].astype(F32)
    rec_n = (jnp.concatenate(parts, axis=1) * ghg_ref[...] * (gx * _sigmoid(gx))).astype(BF16)
    y = _dot(att_n, wout_ref[0:ATT_WIDTH, :]) + _dot(rec_n, wout_ref[ATT_WIDTH:, :])
    o_ref[...] = x_ref[...] + y


def _mixer(qa, kv, r, x, w_out, sinks, gatt, lbl, ghg, ts):
    b, s, _ = x.shape
    sel, up, blk = _hgrn_constants()
    nblk = ts // ATT_BLOCK
    cur = lambda bi, i: (bi, i, 0)
    prev = lambda bi, i: (bi, jnp.maximum(i * nblk - 1, 0), 0)
    fixed = lambda bi, i: (0, 0)
    rcol = lambda k: (lambda bi, i: (bi, i, k))
    return pl.pallas_call(
        functools.partial(_mixer_kernel, ts=ts),
        grid=(b, s // ts),
        in_specs=[
            pl.BlockSpec(memory_space=pltpu.SMEM),
            pl.BlockSpec((None, ts, ATT_WIDTH), cur),
            pl.BlockSpec((None, ts, 2 * ATT_KV_WIDTH), cur),
            pl.BlockSpec((None, ATT_BLOCK, 2 * ATT_KV_WIDTH), prev),
            pl.BlockSpec((None, ts, HG_WIDTH), rcol(0)),
            pl.BlockSpec((None, ts, HG_WIDTH), rcol(1)),
            pl.BlockSpec((None, ts, HG_WIDTH), rcol(2)),
            pl.BlockSpec((None, ts, HG_WIDTH), rcol(3)),
            pl.BlockSpec((None, ts, D_MODEL), cur),
            pl.BlockSpec((D_MODEL, D_MODEL), fixed),
            pl.BlockSpec((1, ATT_WIDTH), fixed),
            pl.BlockSpec((2, HG_WIDTH), fixed),
            pl.BlockSpec((1, HG_WIDTH), fixed),
            pl.BlockSpec(sel.shape, fixed),
            pl.BlockSpec(up.shape, fixed),
            pl.BlockSpec(blk.shape, fixed),
        ],
        out_specs=pl.BlockSpec((None, ts, D_MODEL), cur),
        out_shape=jax.ShapeDtypeStruct((b, s, D_MODEL), F32),
        scratch_shapes=[
            pltpu.VMEM((HG_HEADS, HG_DIM, HG_DIM), F32),
            pltpu.VMEM((ts, ATT_WIDTH), F32),
            pltpu.VMEM((ts, HG_WIDTH), F32),
        ],
        compiler_params=pltpu.CompilerParams(
            dimension_semantics=("arbitrary", "arbitrary"), vmem_limit_bytes=VMEM_LIMIT_BYTES),
        name="mixer",
    )(sinks, qa, kv, kv, r, r, r, r, x, w_out, gatt, lbl, ghg, sel, up, blk)


def _mem_kv_kernel(mem_ref, gain_ref, w_ref, k_ref, v_ref):
    m = mem_ref[...]
    mn = (m * _rms_scale(m) * gain_ref[...]).astype(BF16)
    kv = _dot(mn, w_ref[...])
    k_ref[...] = kv[:, :D_MODEL].astype(BF16)
    v_ref[...] = kv[:, D_MODEL:].astype(BF16)


def _mem_kv(mem, gain, w_xkv):
    b, m, _ = mem.shape
    blk = lambda bi: (bi, 0, 0)
    fixed = lambda bi: (0, 0)
    return pl.pallas_call(
        _mem_kv_kernel,
        grid=(b,),
        in_specs=[
            pl.BlockSpec((None, m, D_MODEL), blk),
            pl.BlockSpec((1, D_MODEL), fixed),
            pl.BlockSpec((D_MODEL, 2 * D_MODEL), fixed),
        ],
        out_specs=[pl.BlockSpec((None, m, D_MODEL), blk), pl.BlockSpec((None, m, D_MODEL), blk)],
        out_shape=[jax.ShapeDtypeStruct((b, m, D_MODEL), BF16)] * 2,
        compiler_params=pltpu.CompilerParams(
            dimension_semantics=("arbitrary",), vmem_limit_bytes=VMEM_LIMIT_BYTES),
        name="mem_kv",
    )(mem, gain, w_xkv)


def _xattn_kernel(x_ref, gain_ref, wq_ref, k_ref, v_ref, wo_ref, o_ref):
    x = x_ref[...]
    hq = (x * _rms_scale(x) * gain_ref[...]).astype(BF16)
    q = _dot(hq, wq_ref[...]).astype(BF16)
    outs = []
    for h in range(X_HEADS):
        cols = slice(h * X_HEAD_DIM, (h + 1) * X_HEAD_DIM)
        s = _dot_nt(q[:, cols], k_ref[:, cols]) * (X_HEAD_DIM ** -0.5)
        p = jnp.exp(s - jnp.max(s, axis=-1, keepdims=True))
        o = _dot(p.astype(BF16), v_ref[:, cols])
        outs.append((o / jnp.sum(p, axis=-1, keepdims=True)).astype(BF16))
    xo = jnp.concatenate(outs, axis=1)
    o_ref[...] = x + _dot(xo, wo_ref[...])


def _xattn(x, gain, w_xq, xk, xv, w_xo, tm):
    b, s, _ = x.shape
    m = xk.shape[1]
    cur = lambda bi, i: (bi, i, 0)
    mem = lambda bi, i: (bi, 0, 0)
    fixed = lambda bi, i: (0, 0)
    return pl.pallas_call(
        _xattn_kernel,
        grid=(b, s // tm),
        in_specs=[
            pl.BlockSpec((None, tm, D_MODEL), cur),
            pl.BlockSpec((1, D_MODEL), fixed),
            pl.BlockSpec((D_MODEL, D_MODEL), fixed),
            pl.BlockSpec((None, m, D_MODEL), mem),
            pl.BlockSpec((None, m, D_MODEL), mem),
            pl.BlockSpec((D_MODEL, D_MODEL), fixed),
        ],
        out_specs=pl.BlockSpec((None, tm, D_MODEL), cur),
        out_shape=jax.ShapeDtypeStruct((b, s, D_MODEL), F32),
        compiler_params=pltpu.CompilerParams(
            dimension_semantics=("arbitrary", "arbitrary"), vmem_limit_bytes=VMEM_LIMIT_BYTES),
        name="xattn",
    )(x, gain, w_xq, xk, xv, w_xo)


FFN_CHUNKS = ((0, 1024), (1024, 2048), (2048, FFN_HIDDEN))


def _ffn_kernel(x_ref, gain_ref, wgu_ref, wd_ref, gfin_ref, o_ref):
    x = x_ref[...]
    hf = (x * _rms_scale(x) * gain_ref[...]).astype(BF16)
    acc = x
    for lo, hi in FFN_CHUNKS:
        gate = _dot(hf, wgu_ref[:, lo:hi])
        upv = _dot(hf, wgu_ref[:, FFN_HIDDEN + lo:FFN_HIDDEN + hi])
        act = (gate * _sigmoid(gate) * upv).astype(BF16)
        acc = acc + _dot(act, wd_ref[lo:hi, :])
    o_ref[...] = acc * _rms_scale(acc) * gfin_ref[...]


def _ffn(x2d, gain, w_gu, w_d, gfin, tm):
    n = x2d.shape[0]
    row = lambda i: (i, 0)
    fixed = lambda i: (0, 0)
    return pl.pallas_call(
        _ffn_kernel,
        grid=(n // tm,),
        in_specs=[
            pl.BlockSpec((tm, D_MODEL), row),
            pl.BlockSpec((1, D_MODEL), fixed),
            pl.BlockSpec((D_MODEL, 2 * FFN_HIDDEN), fixed, pipeline_mode=pl.Buffered(1)),
            pl.BlockSpec((FFN_HIDDEN, D_MODEL), fixed, pipeline_mode=pl.Buffered(1)),
            pl.BlockSpec((1, D_MODEL), fixed),
        ],
        out_specs=pl.BlockSpec((tm, D_MODEL), row),
        out_shape=jax.ShapeDtypeStruct((n, D_MODEL), F32),
        compiler_params=pltpu.CompilerParams(
            dimension_semantics=("arbitrary",), vmem_limit_bytes=VMEM_LIMIT_BYTES),
        name="ffn",
    )(x2d, gain, w_gu, w_d, gfin)


def _rotary_tables(positions):
    inv_freq = jnp.power(jnp.float32(ROPE_THETA),
                         -jnp.arange(ROPE_HALF, dtype=F32) * (2.0 / ROPE_DIM))
    ang = positions.astype(F32)[..., None] * inv_freq
    cos = jnp.cos(ang)
    sin = jnp.sin(ang)
    rest = ATT_HEAD_DIM - ROPE_DIM
    ones = jnp.ones(cos.shape[:-1] + (rest,), F32)
    zeros = jnp.zeros(cos.shape[:-1] + (rest,), F32)
    cos_h = jnp.concatenate([cos, cos, ones], axis=-1)
    sin_h = jnp.concatenate([-sin, sin, zeros], axis=-1)
    reps = LANES // ATT_HEAD_DIM
    cos_t = jnp.tile(cos_h, (1, 1, reps)).reshape(-1, LANES)
    sin_t = jnp.tile(sin_h, (1, 1, reps)).reshape(-1, LANES)
    return cos_t, sin_t


def kernel(x, mem, positions, norm_mix, w_in, att_sinks, att_out_gain, hg_lb_logits, hg_out_gain,
           w_out, norm_xattn, norm_mem, w_xq, w_xkv, w_xo, norm_ffn, w_gate_up, w_down, norm_final):
    b, s, d = x.shape
    n = b * s
    assert w_in.shape[0] == 1 and hg_lb_logits.shape[0] == 2, "single-layer block only"
    row = lambda v: v.reshape(1, -1).astype(F32)
    cos_t, sin_t = _rotary_tables(positions)
    qa, kv, r = _in_proj(x.reshape(n, d), row(norm_mix[0]), w_in[0].astype(BF16), cos_t, sin_t, tm=512)
    x = _mixer(qa.reshape(b, s, -1), kv.reshape(b, s, -1), r.reshape(b, s, -1), x,
               w_out[0].astype(BF16), att_sinks[0].astype(F32), row(att_out_gain[0]),
               hg_lb_logits.astype(F32), row(hg_out_gain[0]), ts=256)
    xk, xv = _mem_kv(mem, row(norm_mem[0]), w_xkv[0].astype(BF16))
    x = _xattn(x, row(norm_xattn[0]), w_xq[0].astype(BF16), xk, xv, w_xo[0].astype(BF16), tm=512)
    y = _ffn(x.reshape(n, d), row(norm_ffn[0]), w_gate_up[0].astype(BF16), w_down[0].astype(BF16),
             row(norm_final), tm=512)
    return y.reshape(b, s, d)
```

```python
import functools

import numpy as np
import jax
import jax.numpy as jnp
from jax import lax
from jax.experimental import pallas as pl
from jax.experimental.pallas import tpu as pltpu

F32 = jnp.float32
BF16 = jnp.bfloat16

D_MODEL = 1024
MEM_LEN = 256
ATT_HEADS = 8
ATT_KV_HEADS = 2
ATT_GROUP = ATT_HEADS // ATT_KV_HEADS
ATT_HEAD_DIM = 64
ATT_WIDTH = ATT_HEADS * ATT_HEAD_DIM
ATT_KV_WIDTH = ATT_KV_HEADS * ATT_HEAD_DIM
ATT_BLOCK = 128
ROPE_THETA = 500000.0
ROPE_DIM = ATT_HEAD_DIM // 4
ROPE_HALF = ROPE_DIM // 2
HG_HEADS = 4
HG_DIM = 128
HG_WIDTH = HG_HEADS * HG_DIM
X_HEADS = 4
X_HEAD_DIM = D_MODEL // X_HEADS
FFN_HIDDEN = 2816
RMS_EPS = 1e-6
LANES = 128
SUBLANES = 8
MASK_VALUE = -1e30

MIX_TILE = 256
HG_LEVELS = (128, 64, 32, 16, 8, 4, 2, 1)
KV_PAD_WIDTH = 2 * ATT_KV_HEADS * 2 * LANES

VMEM_LIMIT_BYTES = 56 * 1024 * 1024


def _rms_scale(x):
    return lax.rsqrt(jnp.mean(x * x, axis=-1, keepdims=True) + RMS_EPS)


def _sigmoid(x):
    return 0.5 * jnp.tanh(0.5 * x) + 0.5


def _dot(a, b):
    return jnp.dot(a, b, preferred_element_type=F32)


def _dot_nt(a, b):
    return lax.dot_general(a, b, (((1,), (1,)), ((), ())), preferred_element_type=F32)


def _dot_tn(a, b):
    return lax.dot_general(a, b, (((0,), (0,)), ((), ())), preferred_element_type=F32)


def _in_proj_kernel(x_ref, gain_ref, w_ref, cos_ref, sin_ref, qa_ref, kv_ref, r_ref):
    x = x_ref[...]
    h = (x * _rms_scale(x) * gain_ref[...]).astype(BF16)
    p = _dot(h, w_ref[...])
    cos = cos_ref[...]
    sin = sin_ref[...]
    lane = lax.broadcasted_iota(jnp.int32, cos.shape, 1)
    first_half = (lane & (ATT_HEAD_DIM - 1)) < ROPE_HALF
    lo_head = lane < ATT_HEAD_DIM

    def rotate(blk):
        partner = jnp.where(first_half,
                            pltpu.roll(blk, LANES - ROPE_HALF, axis=1),
                            pltpu.roll(blk, ROPE_HALF, axis=1))
        return blk * cos + partner * sin

    q_scale = ATT_HEAD_DIM ** -0.5
    for j in range(ATT_WIDTH // LANES):
        qa_ref[:, j * LANES:(j + 1) * LANES] = (rotate(p[:, j * LANES:(j + 1) * LANES]) * q_scale).astype(BF16)

    def padded(col, base):
        swapped = pltpu.roll(col, ATT_HEAD_DIM, axis=1)
        zero = jnp.zeros_like(col)
        variants = (jnp.where(lo_head, col, zero), jnp.where(lo_head, zero, swapped),
                    jnp.where(lo_head, swapped, zero), jnp.where(lo_head, zero, col))
        for i, v in enumerate(variants):
            kv_ref[:, base + i * LANES:base + (i + 1) * LANES] = v.astype(BF16)

    padded(rotate(p[:, ATT_WIDTH:ATT_WIDTH + LANES]), 0)
    padded(p[:, ATT_WIDTH + LANES:ATT_WIDTH + 2 * LANES], 4 * LANES)
    r_ref[...] = p[:, ATT_WIDTH + 2 * ATT_KV_WIDTH:].astype(BF16)


def _in_proj(x2d, gain, w_in, cos_t, sin_t, tm):
    n = x2d.shape[0]
    in_cols = w_in.shape[1]
    r_cols = in_cols - ATT_WIDTH - 2 * ATT_KV_WIDTH
    row = lambda i: (i, 0)
    fixed = lambda i: (0, 0)
    return pl.pallas_call(
        _in_proj_kernel,
        grid=(n // tm,),
        in_specs=[
            pl.BlockSpec((tm, D_MODEL), row),
            pl.BlockSpec((1, D_MODEL), fixed),
            pl.BlockSpec((D_MODEL, in_cols), fixed),
            pl.BlockSpec((tm, LANES), row),
            pl.BlockSpec((tm, LANES), row),
        ],
        out_specs=[
            pl.BlockSpec((tm, ATT_WIDTH), row),
            pl.BlockSpec((tm, KV_PAD_WIDTH), row),
            pl.BlockSpec((tm, r_cols), row),
        ],
        out_shape=[
            jax.ShapeDtypeStruct((n, ATT_WIDTH), BF16),
            jax.ShapeDtypeStruct((n, KV_PAD_WIDTH), BF16),
            jax.ShapeDtypeStruct((n, r_cols), BF16),
        ],
        compiler_params=pltpu.CompilerParams(
            dimension_semantics=("arbitrary",), vmem_limit_bytes=VMEM_LIMIT_BYTES),
        name="in_proj",
    )(x2d, gain, w_in, cos_t, sin_t)


def _hgrn_constants():
    c = MIX_TILE
    r = np.arange(c)[:, None]
    j = np.arange(c)[None, :]
    mats = [(j <= r), (j > r)]
    rr = np.arange(LANES)[:, None]
    jj = np.arange(LANES)[None, :]
    masks = []
    for m in HG_LEVELS:
        ref = (r // (2 * m)) * (2 * m) + m
        mats.append(np.where(r >= ref, (j >= ref) & (j <= r), (j > r) & (j < ref)))
        if 2 * m <= LANES:
            masks.append(((rr // (2 * m)) == (jj // (2 * m))) & ((rr % (2 * m)) >= m) & ((jj % (2 * m)) < m))
    sel = np.concatenate(mats, axis=0).astype(np.float32)
    lmask = np.concatenate(masks, axis=0).astype(np.float32)
    return jnp.asarray(sel, BF16), jnp.asarray(lmask, F32)


def _select_rows(m, upper_src, lower_src, row_ids):
    c = upper_src.shape[0]
    if m >= SUBLANES:
        pieces = []
        for b0 in range(0, c, 2 * m):
            pieces.append(lower_src[b0:b0 + m])
            pieces.append(upper_src[b0 + m:b0 + 2 * m])
        return jnp.concatenate(pieces, axis=0)
    return jnp.where((row_ids & m) != 0, upper_src, lower_src)


def _mixer_kernel(sinks_ref, qa_ref, kvc_ref, kvp_ref, qr_ref, fr_ref, ir_ref, gr_ref, x_ref,
                  wout_ref, gatt_ref, lbl_ref, ghg_ref, sel_ref, lmask_ref,
                  o_ref, state_ref):
    step = pl.program_id(1)
    ts = MIX_TILE
    blk = ATT_BLOCK

    @pl.when(step == 0)
    def _():
        state_ref[...] = jnp.zeros_like(state_ref)

    qrow = lax.broadcasted_iota(jnp.int32, (2 * blk, 2 * blk), 0) & (blk - 1)
    kcol = lax.broadcasted_iota(jnp.int32, (2 * blk, 2 * blk), 1)
    upper_half_rows = lax.broadcasted_iota(jnp.int32, (2 * blk, 1), 0) >= blk
    lane_lo = lax.broadcasted_iota(jnp.int32, (2 * blk, LANES), 1) < ATT_HEAD_DIM
    ones_lo = jnp.where(lane_lo, 1.0, 0.0).astype(BF16)
    ones_hi = jnp.where(lane_lo, 0.0, 1.0).astype(BF16)
    first_lo = jnp.where(step > 0, -1, blk - 1)
    att_rows = []
    for jb in range(ts // blk):
        rows = slice(jb * blk, (jb + 1) * blk)
        q_blk = qa_ref[rows, :]
        if jb == 0:
            kv_keys = jnp.concatenate([kvp_ref[...], kvc_ref[rows, :]], axis=0)
            lo_bound = jnp.maximum(qrow, first_lo)
        else:
            kv_keys = kvc_ref[(jb - 1) * blk:(jb + 1) * blk, :]
            lo_bound = qrow
        valid = (kcol > lo_bound) & (kcol <= qrow + blk)
        att_cols = []
        for gk in range(ATT_KV_HEADS):
            q2 = jnp.concatenate([q_blk[:, (2 * gk) * LANES:(2 * gk + 1) * LANES],
                                  q_blk[:, (2 * gk + 1) * LANES:(2 * gk + 2) * LANES]], axis=0)
            kbase = 2 * gk * LANES
            vbase = (2 * ATT_KV_HEADS + 2 * gk) * LANES
            k_lo = kv_keys[:, kbase:kbase + LANES]
            k_hi = kv_keys[:, kbase + LANES:kbase + 2 * LANES]
            r_lo = jnp.concatenate([kv_keys[:, vbase:vbase + LANES], ones_lo], axis=1)
            r_hi = jnp.concatenate([kv_keys[:, vbase + LANES:vbase + 2 * LANES], ones_hi], axis=1)
            h0 = ATT_GROUP * gk
            out = jnp.zeros((2 * blk, 2 * LANES), F32)
            esinks = []
            for which, k_op, r_op in ((0, k_lo, r_lo), (1, k_hi, r_hi)):
                sink = jnp.where(upper_half_rows, sinks_ref[h0 + 2 + which], sinks_ref[h0 + which])
                s = jnp.where(valid, _dot_nt(q2, k_op), MASK_VALUE)
                m = jnp.maximum(jnp.max(jnp.maximum(s[:, :blk], s[:, blk:]), axis=-1, keepdims=True), sink)
                p = jnp.exp(s - m).astype(BF16)
                out = out + _dot(p, r_op)
                esinks.append(jnp.exp(sink - m))
            denom = out[:, LANES:] + jnp.where(lane_lo, esinks[0], esinks[1])
            att2 = out[:, :LANES] / denom
            att_cols += [att2[:blk], att2[blk:]]
        att_rows.append(jnp.concatenate(att_cols, axis=1))
    att = jnp.concatenate(att_rows, axis=0)

    lbl = lbl_ref[...]
    lexp = jnp.exp(lbl - jnp.max(lbl, axis=0, keepdims=True))
    lb = lexp[0:1, :] / jnp.sum(lexp, axis=0, keepdims=True)
    sel = sel_ref[...]
    row_ids = lax.broadcasted_iota(jnp.int32, (ts, HG_DIM), 0)
    ones_sq = jnp.ones((HG_DIM, HG_DIM), BF16)
    half = ts // 2
    rec_cols = []
    for h in range(HG_HEADS):
        cols = slice(h * HG_DIM, (h + 1) * HG_DIM)
        lb_h = lb[:, cols]
        f = lb_h + (1.0 - lb_h) * _sigmoid(fr_ref[:, cols].astype(F32))
        g = jnp.log(f)
        kk = 1.0 - f
        qx = qr_ref[:, cols].astype(F32)
        qf = qx * _sigmoid(qx)
        vv = ir_ref[:, cols]
        g_hi = g.astype(BF16)
        g_lo = (g - g_hi.astype(F32)).astype(BF16)
        e2 = _dot(sel, jnp.concatenate([g_hi, g_lo], axis=1))
        w = jnp.exp(e2[:, :HG_DIM] + e2[:, HG_DIM:])
        w_incl = w[0:ts]
        w_tail = w[ts:2 * ts]
        st = state_ref[h]
        o = _dot_nt((qf * w_incl).astype(BF16), st.astype(BF16))
        s00 = jnp.zeros((half, half), F32)
        s11 = jnp.zeros((half, half), F32)
        s10 = None
        for li, m in enumerate(HG_LEVELS):
            x_l = (_select_rows(m, qf, kk, row_ids) * w[(2 + li) * ts:(3 + li) * ts]).astype(BF16)
            if 2 * m == ts:
                s10 = _dot_nt(x_l[half:], x_l[:half])
            else:
                mk = lmask_ref[(li - 1) * LANES:li * LANES, :]
                s00 = s00 + _dot_nt(x_l[:half], x_l[:half]) * mk
                s11 = s11 + _dot_nt(x_l[half:], x_l[half:]) * mk
        o_top = _dot(s00.astype(BF16), vv[:half])
        o_bot = _dot(jnp.concatenate([s10, s11], axis=1).astype(BF16), vv)
        diag = _dot((qf * kk).astype(BF16), ones_sq)
        o = o + jnp.concatenate([o_top, o_bot], axis=0) + diag * vv.astype(F32)
        state_ref[h] = st * w_incl[ts - 1:ts, :] + _dot_tn(vv, (kk * w_tail).astype(BF16))
        rec_cols.append(o * _rms_scale(o))
    gx = gr_ref[...].astype(F32)
    rec_n = (jnp.concatenate(rec_cols, axis=1) * ghg_ref[...] * (gx * _sigmoid(gx))).astype(BF16)

    att_n = (att * _rms_scale(att) * gatt_ref[...]).astype(BF16)
    y = _dot(att_n, wout_ref[0:ATT_WIDTH, :]) + _dot(rec_n, wout_ref[ATT_WIDTH:, :])
    o_ref[...] = x_ref[...] + y


def _mixer(qa, kv, r, x, w_out, sinks, gatt, lbl, ghg):
    b, s, _ = x.shape
    ts = MIX_TILE
    sel, lmask = _hgrn_constants()
    nblk = ts // ATT_BLOCK
    cur = lambda bi, i: (bi, i, 0)
    prev = lambda bi, i: (bi, jnp.maximum(i * nblk - 1, 0), 0)
    fixed = lambda bi, i: (0, 0)
    rcol = lambda k: (lambda bi, i: (bi, i, k))
    return pl.pallas_call(
        _mixer_kernel,
        grid=(b, s // ts),
        in_specs=[
            pl.BlockSpec(memory_space=pltpu.SMEM),
            pl.BlockSpec((None, ts, ATT_WIDTH), cur),
            pl.BlockSpec((None, ts, KV_PAD_WIDTH), cur),
            pl.BlockSpec((None, ATT_BLOCK, KV_PAD_WIDTH), prev),
            pl.BlockSpec((None, ts, HG_WIDTH), rcol(0)),
            pl.BlockSpec((None, ts, HG_WIDTH), rcol(1)),
            pl.BlockSpec((None, ts, HG_WIDTH), rcol(2)),
            pl.BlockSpec((None, ts, HG_WIDTH), rcol(3)),
            pl.BlockSpec((None, ts, D_MODEL), cur),
            pl.BlockSpec((D_MODEL, D_MODEL), fixed),
            pl.BlockSpec((1, ATT_WIDTH), fixed),
            pl.BlockSpec(lbl.shape, fixed),
            pl.BlockSpec((1, HG_WIDTH), fixed),
            pl.BlockSpec(sel.shape, fixed),
            pl.BlockSpec(lmask.shape, fixed),
        ],
        out_specs=pl.BlockSpec((None, ts, D_MODEL), cur),
        out_shape=jax.ShapeDtypeStruct((b, s, D_MODEL), F32),
        scratch_shapes=[pltpu.VMEM((HG_HEADS, HG_DIM, HG_DIM), F32)],
        compiler_params=pltpu.CompilerParams(
            dimension_semantics=("arbitrary", "arbitrary"), vmem_limit_bytes=VMEM_LIMIT_BYTES),
        name="mixer",
    )(sinks, qa, kv, kv, r, r, r, r, x, w_out, gatt, lbl, ghg, sel, lmask)


def _mem_kv_kernel(mem_ref, gain_ref, w_ref, k_ref, v_ref):
    m = mem_ref[...]
    mn = (m * _rms_scale(m) * gain_ref[...]).astype(BF16)
    kv = _dot(mn, w_ref[...])
    k_ref[...] = kv[:, :D_MODEL].astype(BF16)
    v_ref[...] = kv[:, D_MODEL:].astype(BF16)


def _mem_kv(mem, gain, w_xkv):
    b, m, _ = mem.shape
    blk = lambda bi: (bi, 0, 0)
    fixed = lambda bi: (0, 0)
    return pl.pallas_call(
        _mem_kv_kernel,
        grid=(b,),
        in_specs=[
            pl.BlockSpec((None, m, D_MODEL), blk),
            pl.BlockSpec((1, D_MODEL), fixed),
            pl.BlockSpec((D_MODEL, 2 * D_MODEL), fixed),
        ],
        out_specs=[pl.BlockSpec((None, m, D_MODEL), blk), pl.BlockSpec((None, m, D_MODEL), blk)],
        out_shape=[jax.ShapeDtypeStruct((b, m, D_MODEL), BF16)] * 2,
        compiler_params=pltpu.CompilerParams(
            dimension_semantics=("arbitrary",), vmem_limit_bytes=VMEM_LIMIT_BYTES),
        name="mem_kv",
    )(mem, gain, w_xkv)


def _xattn_kernel(x_ref, gain_ref, wq_ref, k_ref, v_ref, wo_ref, o_ref):
    x = x_ref[...]
    hq = (x * _rms_scale(x) * gain_ref[...]).astype(BF16)
    q = _dot(hq, wq_ref[...]).astype(BF16)
    outs = []
    for h in range(X_HEADS):
        cols = slice(h * X_HEAD_DIM, (h + 1) * X_HEAD_DIM)
        s = _dot_nt(q[:, cols], k_ref[:, cols]) * (X_HEAD_DIM ** -0.5)
        p = jnp.exp(s - jnp.max(s, axis=-1, keepdims=True))
        o = _dot(p.astype(BF16), v_ref[:, cols])
        outs.append((o / jnp.sum(p, axis=-1, keepdims=True)).astype(BF16))
    xo = jnp.concatenate(outs, axis=1)
    o_ref[...] = x + _dot(xo, wo_ref[...])


def _xattn(x, gain, w_xq, xk, xv, w_xo, tm):
    b, s, _ = x.shape
    m = xk.shape[1]
    cur = lambda bi, i: (bi, i, 0)
    mem = lambda bi, i: (bi, 0, 0)
    fixed = lambda bi, i: (0, 0)
    return pl.pallas_call(
        _xattn_kernel,
        grid=(b, s // tm),
        in_specs=[
            pl.BlockSpec((None, tm, D_MODEL), cur),
            pl.BlockSpec((1, D_MODEL), fixed),
            pl.BlockSpec((D_MODEL, D_MODEL), fixed),
            pl.BlockSpec((None, m, D_MODEL), mem),
            pl.BlockSpec((None, m, D_MODEL), mem),
            pl.BlockSpec((D_MODEL, D_MODEL), fixed),
        ],
        out_specs=pl.BlockSpec((None, tm, D_MODEL), cur),
        out_shape=jax.ShapeDtypeStruct((b, s, D_MODEL), F32),
        compiler_params=pltpu.CompilerParams(
            dimension_semantics=("arbitrary", "arbitrary"), vmem_limit_bytes=VMEM_LIMIT_BYTES),
        name="xattn",
    )(x, gain, w_xq, xk, xv, w_xo)


FFN_CHUNKS = ((0, 1024), (1024, 2048), (2048, FFN_HIDDEN))


def _ffn_kernel(x_ref, gain_ref, wgu_ref, wd_ref, gfin_ref, o_ref):
    x = x_ref[...]
    hf = (x * _rms_scale(x) * gain_ref[...]).astype(BF16)
    acc = x
    for lo, hi in FFN_CHUNKS:
        gate = _dot(hf, wgu_ref[:, lo:hi])
        upv = _dot(hf, wgu_ref[:, FFN_HIDDEN + lo:FFN_HIDDEN + hi])
        act = (gate * _sigmoid(gate) * upv).astype(BF16)
        acc = acc + _dot(act, wd_ref[lo:hi, :])
    o_ref[...] = acc * _rms_scale(acc) * gfin_ref[...]


def _ffn(x2d, gain, w_gu, w_d, gfin, tm):
    n = x2d.shape[0]
    row = lambda i: (i, 0)
    fixed = lambda i: (0, 0)
    return pl.pallas_call(
        _ffn_kernel,
        grid=(n // tm,),
        in_specs=[
            pl.BlockSpec((tm, D_MODEL), row),
            pl.BlockSpec((1, D_MODEL), fixed),
            pl.BlockSpec((D_MODEL, 2 * FFN_HIDDEN), fixed, pipeline_mode=pl.Buffered(1)),
            pl.BlockSpec((FFN_HIDDEN, D_MODEL), fixed, pipeline_mode=pl.Buffered(1)),
            pl.BlockSpec((1, D_MODEL), fixed),
        ],
        out_specs=pl.BlockSpec((tm, D_MODEL), row),
        out_shape=jax.ShapeDtypeStruct((n, D_MODEL), F32),
        compiler_params=pltpu.CompilerParams(
            dimension_semantics=("arbitrary",), vmem_limit_bytes=VMEM_LIMIT_BYTES),
        name="ffn",
    )(x2d, gain, w_gu, w_d, gfin)


def _rotary_tables(positions):
    inv_freq = jnp.power(jnp.float32(ROPE_THETA),
                         -jnp.arange(ROPE_HALF, dtype=F32) * (2.0 / ROPE_DIM))
    ang = positions.astype(F32)[..., None] * inv_freq
    cos = jnp.cos(ang)
    sin = jnp.sin(ang)
    rest = ATT_HEAD_DIM - ROPE_DIM
    ones = jnp.ones(cos.shape[:-1] + (rest,), F32)
    zeros = jnp.zeros(cos.shape[:-1] + (rest,), F32)
    cos_h = jnp.concatenate([cos, cos, ones], axis=-1)
    sin_h = jnp.concatenate([-sin, sin, zeros], axis=-1)
    reps = LANES // ATT_HEAD_DIM
    cos_t = jnp.tile(cos_h, (1, 1, reps)).reshape(-1, LANES)
    sin_t = jnp.tile(sin_h, (1, 1, reps)).reshape(-1, LANES)
    return cos_t, sin_t


def kernel(x, mem, positions, norm_mix, w_in, att_sinks, att_out_gain, hg_lb_logits, hg_out_gain,
           w_out, norm_xattn, norm_mem, w_xq, w_xkv, w_xo, norm_ffn, w_gate_up, w_down, norm_final):
    b, s, d = x.shape
    n = b * s
    assert w_in.shape[0] == 1 and hg_lb_logits.shape[0] == 2, "single-layer block only"
    row = lambda v: v.reshape(1, -1).astype(F32)
    cos_t, sin_t = _rotary_tables(positions)
    qa, kv, r = _in_proj(x.reshape(n, d), row(norm_mix[0]), w_in[0].astype(BF16), cos_t, sin_t, tm=512)
    x = _mixer(qa.reshape(b, s, -1), kv.reshape(b, s, -1), r.reshape(b, s, -1), x,
               w_out[0].astype(BF16), att_sinks[0].astype(F32), row(att_out_gain[0]),
               hg_lb_logits.astype(F32), row(hg_out_gain[0]))
    xk, xv = _mem_kv(mem, row(norm_mem[0]), w_xkv[0].astype(BF16))
    x = _xattn(x, row(norm_xattn[0]), w_xq[0].astype(BF16), xk, xv, w_xo[0].astype(BF16), tm=512)
    y = _ffn(x.reshape(n, d), row(norm_ffn[0]), w_gate_up[0].astype(BF16), w_down[0].astype(BF16),
             row(norm_final), tm=512)
    return y.reshape(b, s, d)
```

```python
import functools

import numpy as np
import jax
import jax.numpy as jnp
from jax import lax
from jax.experimental import pallas as pl
from jax.experimental.pallas import tpu as pltpu

F32 = jnp.float32
BF16 = jnp.bfloat16

D_MODEL = 1024
MEM_LEN = 256
ATT_HEADS = 8
ATT_KV_HEADS = 2
ATT_GROUP = ATT_HEADS // ATT_KV_HEADS
ATT_HEAD_DIM = 64
ATT_WIDTH = ATT_HEADS * ATT_HEAD_DIM
ATT_KV_WIDTH = ATT_KV_HEADS * ATT_HEAD_DIM
ATT_BLOCK = 128
ROPE_THETA = 500000.0
ROPE_DIM = ATT_HEAD_DIM // 4
ROPE_HALF = ROPE_DIM // 2
HG_HEADS = 4
HG_DIM = 128
HG_WIDTH = HG_HEADS * HG_DIM
X_HEADS = 4
X_HEAD_DIM = D_MODEL // X_HEADS
FFN_HIDDEN = 2816
RMS_EPS = 1e-6
LANES = 128
SUBLANES = 8
MASK_VALUE = -1e30
LOG2_E = 1.4426950408889634

MIX_TILE = 256
HG_LEVELS = (128, 64, 32, 16, 8, 4, 2, 1)
KV_PAD_WIDTH = 2 * ATT_KV_HEADS * 2 * LANES

VMEM_LIMIT_BYTES = 56 * 1024 * 1024


def _rms_scale(x):
    return lax.rsqrt(jnp.mean(x * x, axis=-1, keepdims=True) + RMS_EPS)


def _sigmoid(x):
    return 0.5 * jnp.tanh(0.5 * x) + 0.5


def _dot(a, b):
    return jnp.dot(a, b, preferred_element_type=F32)


def _dot_nt(a, b):
    return lax.dot_general(a, b, (((1,), (1,)), ((), ())), preferred_element_type=F32)


def _dot_tn(a, b):
    return lax.dot_general(a, b, (((0,), (0,)), ((), ())), preferred_element_type=F32)


def _in_proj_kernel(x_ref, gain_ref, w_ref, cos_ref, sin_ref, qa_ref, kv_ref, r_ref):
    x = x_ref[...]
    h = (x * _rms_scale(x) * gain_ref[...]).astype(BF16)
    p = _dot(h, w_ref[...])
    cos = cos_ref[...]
    sin = sin_ref[...]
    lane = lax.broadcasted_iota(jnp.int32, cos.shape, 1)
    first_half = (lane & (ATT_HEAD_DIM - 1)) < ROPE_HALF
    lo_head = lane < ATT_HEAD_DIM

    def rotate(blk):
        partner = jnp.where(first_half,
                            pltpu.roll(blk, LANES - ROPE_HALF, axis=1),
                            pltpu.roll(blk, ROPE_HALF, axis=1))
        return blk * cos + partner * sin

    q_scale = ATT_HEAD_DIM ** -0.5
    for j in range(ATT_WIDTH // LANES):
        qa_ref[:, j * LANES:(j + 1) * LANES] = (rotate(p[:, j * LANES:(j + 1) * LANES]) * q_scale).astype(BF16)

    def padded(col, base):
        swapped = pltpu.roll(col, ATT_HEAD_DIM, axis=1)
        zero = jnp.zeros_like(col)
        variants = (jnp.where(lo_head, col, zero), jnp.where(lo_head, zero, swapped),
                    jnp.where(lo_head, swapped, zero), jnp.where(lo_head, zero, col))
        for i, v in enumerate(variants):
            kv_ref[:, base + i * LANES:base + (i + 1) * LANES] = v.astype(BF16)

    padded(rotate(p[:, ATT_WIDTH:ATT_WIDTH + LANES]), 0)
    padded(p[:, ATT_WIDTH + LANES:ATT_WIDTH + 2 * LANES], 4 * LANES)
    r_ref[...] = p[:, ATT_WIDTH + 2 * ATT_KV_WIDTH:].astype(BF16)


def _in_proj(x2d, gain, w_in, cos_t, sin_t, tm):
    n = x2d.shape[0]
    in_cols = w_in.shape[1]
    r_cols = in_cols - ATT_WIDTH - 2 * ATT_KV_WIDTH
    row = lambda i: (i, 0)
    fixed = lambda i: (0, 0)
    return pl.pallas_call(
        _in_proj_kernel,
        grid=(n // tm,),
        in_specs=[
            pl.BlockSpec((tm, D_MODEL), row),
            pl.BlockSpec((1, D_MODEL), fixed),
            pl.BlockSpec((D_MODEL, in_cols), fixed),
            pl.BlockSpec((tm, LANES), row),
            pl.BlockSpec((tm, LANES), row),
        ],
        out_specs=[
            pl.BlockSpec((tm, ATT_WIDTH), row),
            pl.BlockSpec((tm, KV_PAD_WIDTH), row),
            pl.BlockSpec((tm, r_cols), row),
        ],
        out_shape=[
            jax.ShapeDtypeStruct((n, ATT_WIDTH), BF16),
            jax.ShapeDtypeStruct((n, KV_PAD_WIDTH), BF16),
            jax.ShapeDtypeStruct((n, r_cols), BF16),
        ],
        compiler_params=pltpu.CompilerParams(
            dimension_semantics=("arbitrary",), vmem_limit_bytes=VMEM_LIMIT_BYTES),
        name="in_proj",
    )(x2d, gain, w_in, cos_t, sin_t)


def _hgrn_constants():
    r = np.arange(MIX_TILE)[:, None]
    j = np.arange(MIX_TILE)[None, :]
    tri = (j <= r).astype(np.float32)
    rr = np.arange(LANES)[:, None]
    jj = np.arange(LANES)[None, :]
    masks = []
    for m in HG_LEVELS:
        if 2 * m <= LANES:
            masks.append(((rr // (2 * m)) == (jj // (2 * m))) & ((rr % (2 * m)) >= m) & ((jj % (2 * m)) < m))
    lmask = np.concatenate(masks, axis=0).astype(np.float32)
    return jnp.asarray(tri, BF16), jnp.asarray(lmask, F32)


def _level_decay(m, b, f, row_ids):
    ts = b.shape[0]
    if m >= SUBLANES // 2:
        b3 = b.reshape(ts // (2 * m), 2 * m, HG_DIM)
        return jnp.exp2(-jnp.abs(b3 - b3[:, m - 1:m, :])).reshape(ts, HG_DIM)
    if m == 2:
        pos = row_ids & 3
        f_next = pltpu.roll(f, ts - 1, axis=0)
        f_prev = pltpu.roll(f, 1, axis=0)
        return jnp.where(pos == 0, f_next, jnp.where(pos == 1, 1.0, jnp.where(pos == 2, f, f * f_prev)))
    assert m == 1
    return jnp.where((row_ids & 1) != 0, f, 1.0)


def _select_rows(m, upper_src, lower_src, row_ids):
    c = upper_src.shape[0]
    if m >= SUBLANES:
        pieces = []
        for b0 in range(0, c, 2 * m):
            pieces.append(lower_src[b0:b0 + m])
            pieces.append(upper_src[b0 + m:b0 + 2 * m])
        return jnp.concatenate(pieces, axis=0)
    return jnp.where((row_ids & m) != 0, upper_src, lower_src)


def _mixer_kernel(sinks_ref, qa_ref, kvc_ref, kvp_ref, qr_ref, fr_ref, ir_ref, gr_ref, x_ref,
                  wout_ref, gatt_ref, lbl_ref, ghg_ref, tri_ref, lmask_ref,
                  o_ref, state_ref):
    step = pl.program_id(1)
    ts = MIX_TILE
    blk = ATT_BLOCK

    @pl.when(step == 0)
    def _():
        state_ref[...] = jnp.zeros_like(state_ref)

    qrow = lax.broadcasted_iota(jnp.int32, (2 * blk, 2 * blk), 0) & (blk - 1)
    kcol = lax.broadcasted_iota(jnp.int32, (2 * blk, 2 * blk), 1)
    upper_half_rows = lax.broadcasted_iota(jnp.int32, (2 * blk, 1), 0) >= blk
    lane_lo = lax.broadcasted_iota(jnp.int32, (2 * blk, LANES), 1) < ATT_HEAD_DIM
    ones_lo = jnp.where(lane_lo, 1.0, 0.0).astype(BF16)
    ones_hi = jnp.where(lane_lo, 0.0, 1.0).astype(BF16)
    first_lo = jnp.where(step > 0, -1, blk - 1)
    att_rows = []
    for jb in range(ts // blk):
        rows = slice(jb * blk, (jb + 1) * blk)
        q_blk = qa_ref[rows, :]
        if jb == 0:
            kv_keys = jnp.concatenate([kvp_ref[...], kvc_ref[rows, :]], axis=0)
            lo_bound = jnp.maximum(qrow, first_lo)
        else:
            kv_keys = kvc_ref[(jb - 1) * blk:(jb + 1) * blk, :]
            lo_bound = qrow
        valid = (kcol > lo_bound) & (kcol <= qrow + blk)
        att_cols = []
        for gk in range(ATT_KV_HEADS):
            q2 = jnp.concatenate([q_blk[:, (2 * gk) * LANES:(2 * gk + 1) * LANES],
                                  q_blk[:, (2 * gk + 1) * LANES:(2 * gk + 2) * LANES]], axis=0)
            kbase = 2 * gk * LANES
            vbase = (2 * ATT_KV_HEADS + 2 * gk) * LANES
            k_lo = kv_keys[:, kbase:kbase + LANES]
            k_hi = kv_keys[:, kbase + LANES:kbase + 2 * LANES]
            r_lo = jnp.concatenate([kv_keys[:, vbase:vbase + LANES], ones_lo], axis=1)
            r_hi = jnp.concatenate([kv_keys[:, vbase + LANES:vbase + 2 * LANES], ones_hi], axis=1)
            h0 = ATT_GROUP * gk
            out = jnp.zeros((2 * blk, 2 * LANES), F32)
            esinks = []
            for which, k_op, r_op in ((0, k_lo, r_lo), (1, k_hi, r_hi)):
                sink = jnp.where(upper_half_rows, sinks_ref[h0 + 2 + which], sinks_ref[h0 + which])
                s = jnp.where(valid, _dot_nt(q2, k_op), MASK_VALUE)
                m = jnp.maximum(jnp.max(jnp.maximum(s[:, :blk], s[:, blk:]), axis=-1, keepdims=True), sink)
                p = jnp.exp(s - m).astype(BF16)
                out = out + _dot(p, r_op)
                esinks.append(jnp.exp(sink - m))
            denom = out[:, LANES:] + jnp.where(lane_lo, esinks[0], esinks[1])
            att2 = out[:, :LANES] / denom
            att_cols += [att2[:blk], att2[blk:]]
        att_rows.append(jnp.concatenate(att_cols, axis=1))
    att = jnp.concatenate(att_rows, axis=0)

    lbl = lbl_ref[...]
    lexp = jnp.exp(lbl - jnp.max(lbl, axis=0, keepdims=True))
    lb = lexp[0:1, :] / jnp.sum(lexp, axis=0, keepdims=True)
    tri = tri_ref[...]
    row_ids = lax.broadcasted_iota(jnp.int32, (ts, HG_DIM), 0)
    ones_sq = jnp.ones((HG_DIM, HG_DIM), BF16)
    half = ts // 2
    rec_cols = []
    for h in range(HG_HEADS):
        cols = slice(h * HG_DIM, (h + 1) * HG_DIM)
        lb_h = lb[:, cols]
        f = lb_h + (1.0 - lb_h) * _sigmoid(fr_ref[:, cols].astype(F32))
        kk = 1.0 - f
        qx = qr_ref[:, cols].astype(F32)
        qf = qx * _sigmoid(qx)
        vv = ir_ref[:, cols]
        g = jnp.log(f) * LOG2_E
        g_hi = g.astype(BF16)
        g_lo = (g - g_hi.astype(F32)).astype(BF16)
        b2 = _dot(tri, jnp.concatenate([g_hi, g_lo], axis=1))
        b = b2[:, :HG_DIM] + b2[:, HG_DIM:]
        w_incl = jnp.exp2(b)
        w_tail = jnp.exp2(b[ts - 1:ts, :] - b)
        st = state_ref[h]
        o = _dot_nt((qf * w_incl).astype(BF16), st.astype(BF16))
        s00 = jnp.zeros((half, half), F32)
        s11 = jnp.zeros((half, half), F32)
        s10 = None
        for li, m in enumerate(HG_LEVELS):
            x_l = (_select_rows(m, qf, kk, row_ids) * _level_decay(m, b, f, row_ids)).astype(BF16)
            if 2 * m == ts:
                s10 = _dot_nt(x_l[half:], x_l[:half])
            else:
                mk = lmask_ref[(li - 1) * LANES:li * LANES, :]
                s00 = s00 + _dot_nt(x_l[:half], x_l[:half]) * mk
                s11 = s11 + _dot_nt(x_l[half:], x_l[half:]) * mk
        o_top = _dot(s00.astype(BF16), vv[:half])
        o_bot = _dot(jnp.concatenate([s10, s11], axis=1).astype(BF16), vv)
        diag = _dot((qf * kk).astype(BF16), ones_sq)
        o = o + jnp.concatenate([o_top, o_bot], axis=0) + diag * vv.astype(F32)
        state_ref[h] = st * w_incl[ts - 1:ts, :] + _dot_tn(vv, (kk * w_tail).astype(BF16))
        rec_cols.append(o * _rms_scale(o))
    gx = gr_ref[...].astype(F32)
    rec_n = (jnp.concatenate(rec_cols, axis=1) * ghg_ref[...] * (gx * _sigmoid(gx))).astype(BF16)

    att_n = (att * _rms_scale(att) * gatt_ref[...]).astype(BF16)
    y = _dot(att_n, wout_ref[0:ATT_WIDTH, :]) + _dot(rec_n, wout_ref[ATT_WIDTH:, :])
    o_ref[...] = x_ref[...] + y


def _mixer(qa, kv, r, x, w_out, sinks, gatt, lbl, ghg):
    b, s, _ = x.shape
    ts = MIX_TILE
    tri, lmask = _hgrn_constants()
    nblk = ts // ATT_BLOCK
    cur = lambda bi, i: (bi, i, 0)
    prev = lambda bi, i: (bi, jnp.maximum(i * nblk - 1, 0), 0)
    fixed = lambda bi, i: (0, 0)
    rcol = lambda k: (lambda bi, i: (bi, i, k))
    return pl.pallas_call(
        _mixer_kernel,
        grid=(b, s // ts),
        in_specs=[
            pl.BlockSpec(memory_space=pltpu.SMEM),
            pl.BlockSpec((None, ts, ATT_WIDTH), cur),
            pl.BlockSpec((None, ts, KV_PAD_WIDTH), cur),
            pl.BlockSpec((None, ATT_BLOCK, KV_PAD_WIDTH), prev),
            pl.BlockSpec((None, ts, HG_WIDTH), rcol(0)),
            pl.BlockSpec((None, ts, HG_WIDTH), rcol(1)),
            pl.BlockSpec((None, ts, HG_WIDTH), rcol(2)),
            pl.BlockSpec((None, ts, HG_WIDTH), rcol(3)),
            pl.BlockSpec((None, ts, D_MODEL), cur),
            pl.BlockSpec((D_MODEL, D_MODEL), fixed),
            pl.BlockSpec((1, ATT_WIDTH), fixed),
            pl.BlockSpec(lbl.shape, fixed),
            pl.BlockSpec((1, HG_WIDTH), fixed),
            pl.BlockSpec(tri.shape, fixed),
            pl.BlockSpec(lmask.shape, fixed),
        ],
        out_specs=pl.BlockSpec((None, ts, D_MODEL), cur),
        out_shape=jax.ShapeDtypeStruct((b, s, D_MODEL), F32),
        scratch_shapes=[pltpu.VMEM((HG_HEADS, HG_DIM, HG_DIM), F32)],
        compiler_params=pltpu.CompilerParams(
            dimension_semantics=("arbitrary", "arbitrary"), vmem_limit_bytes=VMEM_LIMIT_BYTES),
        name="mixer",
    )(sinks, qa, kv, kv, r, r, r, r, x, w_out, gatt, lbl, ghg, tri, lmask)


def _mem_kv_kernel(mem_ref, gain_ref, w_ref, k_ref, v_ref):
    m = mem_ref[...]
    mn = (m * _rms_scale(m) * gain_ref[...]).astype(BF16)
    kv = _dot(mn, w_ref[...])
    k_ref[...] = kv[:, :D_MODEL].astype(BF16)
    v_ref[...] = kv[:, D_MODEL:].astype(BF16)


def _mem_kv(mem, gain, w_xkv):
    b, m, _ = mem.shape
    blk = lambda bi: (bi, 0, 0)
    fixed = lambda bi: (0, 0)
    return pl.pallas_call(
        _mem_kv_kernel,
        grid=(b,),
        in_specs=[
            pl.BlockSpec((None, m, D_MODEL), blk),
            pl.BlockSpec((1, D_MODEL), fixed),
            pl.BlockSpec((D_MODEL, 2 * D_MODEL), fixed),
        ],
        out_specs=[pl.BlockSpec((None, m, D_MODEL), blk), pl.BlockSpec((None, m, D_MODEL), blk)],
        out_shape=[jax.ShapeDtypeStruct((b, m, D_MODEL), BF16)] * 2,
        compiler_params=pltpu.CompilerParams(
            dimension_semantics=("arbitrary",), vmem_limit_bytes=VMEM_LIMIT_BYTES),
        name="mem_kv",
    )(mem, gain, w_xkv)


def _xattn_kernel(x_ref, gain_ref, wq_ref, k_ref, v_ref, wo_ref, o_ref):
    x = x_ref[...]
    hq = (x * _rms_scale(x) * gain_ref[...]).astype(BF16)
    q = _dot(hq, wq_ref[...]).astype(BF16)
    outs = []
    for h in range(X_HEADS):
        cols = slice(h * X_HEAD_DIM, (h + 1) * X_HEAD_DIM)
        s = _dot_nt(q[:, cols], k_ref[:, cols]) * (X_HEAD_DIM ** -0.5)
        p = jnp.exp(s - jnp.max(s, axis=-1, keepdims=True))
        o = _dot(p.astype(BF16), v_ref[:, cols])
        outs.append((o / jnp.sum(p, axis=-1, keepdims=True)).astype(BF16))
    xo = jnp.concatenate(outs, axis=1)
    o_ref[...] = x + _dot(xo, wo_ref[...])


def _xattn(x, gain, w_xq, xk, xv, w_xo, tm):
    b, s, _ = x.shape
    m = xk.shape[1]
    cur = lambda bi, i: (bi, i, 0)
    mem = lambda bi, i: (bi, 0, 0)
    fixed = lambda bi, i: (0, 0)
    return pl.pallas_call(
        _xattn_kernel,
        grid=(b, s // tm),
        in_specs=[
            pl.BlockSpec((None, tm, D_MODEL), cur),
            pl.BlockSpec((1, D_MODEL), fixed),
            pl.BlockSpec((D_MODEL, D_MODEL), fixed),
            pl.BlockSpec((None, m, D_MODEL), mem),
            pl.BlockSpec((None, m, D_MODEL), mem),
            pl.BlockSpec((D_MODEL, D_MODEL), fixed),
        ],
        out_specs=pl.BlockSpec((None, tm, D_MODEL), cur),
        out_shape=jax.ShapeDtypeStruct((b, s, D_MODEL), F32),
        compiler_params=pltpu.CompilerParams(
            dimension_semantics=("arbitrary", "arbitrary"), vmem_limit_bytes=VMEM_LIMIT_BYTES),
        name="xattn",
    )(x, gain, w_xq, xk, xv, w_xo)


FFN_CHUNKS = ((0, 1024), (1024, 2048), (2048, FFN_HIDDEN))


def _ffn_kernel(x_ref, gain_ref, wgu_ref, wd_ref, gfin_ref, o_ref):
    x = x_ref[...]
    hf = (x * _rms_scale(x) * gain_ref[...]).astype(BF16)
    acc = x
    for lo, hi in FFN_CHUNKS:
        gate = _dot(hf, wgu_ref[:, lo:hi])
        upv = _dot(hf, wgu_ref[:, FFN_HIDDEN + lo:FFN_HIDDEN + hi])
        act = (gate * _sigmoid(gate) * upv).astype(BF16)
        acc = acc + _dot(act, wd_ref[lo:hi, :])
    o_ref[...] = acc * _rms_scale(acc) * gfin_ref[...]


def _ffn(x2d, gain, w_gu, w_d, gfin, tm):
    n = x2d.shape[0]
    row = lambda i: (i, 0)
    fixed = lambda i: (0, 0)
    return pl.pallas_call(
        _ffn_kernel,
        grid=(n // tm,),
        in_specs=[
            pl.BlockSpec((tm, D_MODEL), row),
            pl.BlockSpec((1, D_MODEL), fixed),
            pl.BlockSpec((D_MODEL, 2 * FFN_HIDDEN), fixed, pipeline_mode=pl.Buffered(1)),
            pl.BlockSpec((FFN_HIDDEN, D_MODEL), fixed, pipeline_mode=pl.Buffered(1)),
            pl.BlockSpec((1, D_MODEL), fixed),
        ],
        out_specs=pl.BlockSpec((tm, D_MODEL), row),
        out_shape=jax.ShapeDtypeStruct((n, D_MODEL), F32),
        compiler_params=pltpu.CompilerParams(
            dimension_semantics=("arbitrary",), vmem_limit_bytes=VMEM_LIMIT_BYTES),
        name="ffn",
    )(x2d, gain, w_gu, w_d, gfin)


def _rotary_tables(positions):
    inv_freq = jnp.power(jnp.float32(ROPE_THETA),
                         -jnp.arange(ROPE_HALF, dtype=F32) * (2.0 / ROPE_DIM))
    ang = positions.astype(F32)[..., None] * inv_freq
    cos = jnp.cos(ang)
    sin = jnp.sin(ang)
    rest = ATT_HEAD_DIM - ROPE_DIM
    ones = jnp.ones(cos.shape[:-1] + (rest,), F32)
    zeros = jnp.zeros(cos.shape[:-1] + (rest,), F32)
    cos_h = jnp.concatenate([cos, cos, ones], axis=-1)
    sin_h = jnp.concatenate([-sin, sin, zeros], axis=-1)
    reps = LANES // ATT_HEAD_DIM
    cos_t = jnp.tile(cos_h, (1, 1, reps)).reshape(-1, LANES)
    sin_t = jnp.tile(sin_h, (1, 1, reps)).reshape(-1, LANES)
    return cos_t, sin_t


def kernel(x, mem, positions, norm_mix, w_in, att_sinks, att_out_gain, hg_lb_logits, hg_out_gain,
           w_out, norm_xattn, norm_mem, w_xq, w_xkv, w_xo, norm_ffn, w_gate_up, w_down, norm_final):
    b, s, d = x.shape
    n = b * s
    assert w_in.shape[0] == 1 and hg_lb_logits.shape[0] == 2, "single-layer block only"
    row = lambda v: v.reshape(1, -1).astype(F32)
    cos_t, sin_t = _rotary_tables(positions)
    qa, kv, r = _in_proj(x.reshape(n, d), row(norm_mix[0]), w_in[0].astype(BF16), cos_t, sin_t, tm=512)
    x = _mixer(qa.reshape(b, s, -1), kv.reshape(b, s, -1), r.reshape(b, s, -1), x,
               w_out[0].astype(BF16), att_sinks[0].astype(F32), row(att_out_gain[0]),
               hg_lb_logits.astype(F32), row(hg_out_gain[0]))
    xk, xv = _mem_kv(mem, row(norm_mem[0]), w_xkv[0].astype(BF16))
    x = _xattn(x, row(norm_xattn[0]), w_xq[0].astype(BF16), xk, xv, w_xo[0].astype(BF16), tm=512)
    y = _ffn(x.reshape(n, d), row(norm_ffn[0]), w_gate_up[0].astype(BF16), w_down[0].astype(BF16),
             row(norm_final), tm=512)
    return y.reshape(b, s, d)
```

```python
import functools

import numpy as np
import jax
import jax.numpy as jnp
from jax import lax
from jax.experimental import pallas as pl
from jax.experimental.pallas import tpu as pltpu

F32 = jnp.float32
BF16 = jnp.bfloat16

D_MODEL = 1024
MEM_LEN = 256
ATT_HEADS = 8
ATT_KV_HEADS = 2
ATT_GROUP = ATT_HEADS // ATT_KV_HEADS
ATT_HEAD_DIM = 64
ATT_WIDTH = ATT_HEADS * ATT_HEAD_DIM
ATT_KV_WIDTH = ATT_KV_HEADS * ATT_HEAD_DIM
ATT_BLOCK = 128
ROPE_THETA = 500000.0
ROPE_DIM = ATT_HEAD_DIM // 4
ROPE_HALF = ROPE_DIM // 2
HG_HEADS = 4
HG_DIM = 128
HG_WIDTH = HG_HEADS * HG_DIM
X_HEADS = 4
X_HEAD_DIM = D_MODEL // X_HEADS
FFN_HIDDEN = 2816
RMS_EPS = 1e-6
LANES = 128
SUBLANES = 8
MASK_VALUE = -1e30
LOG2_E = 1.4426950408889634

MIX_TILE = 256
HG_LEVELS = (128, 64, 32, 16, 8, 4, 2, 1)
KV_PAD_WIDTH = 2 * ATT_KV_HEADS * 2 * LANES

VMEM_LIMIT_BYTES = 56 * 1024 * 1024


def _rms_scale(x):
    return lax.rsqrt(jnp.mean(x * x, axis=-1, keepdims=True) + RMS_EPS)


def _silu(x):
    h = 0.5 * x
    return h + h * jnp.tanh(h)


def _dot(a, b):
    return jnp.dot(a, b, preferred_element_type=F32)


def _dot_nt(a, b):
    return lax.dot_general(a, b, (((1,), (1,)), ((), ())), preferred_element_type=F32)


def _dot_tn(a, b):
    return lax.dot_general(a, b, (((0,), (0,)), ((), ())), preferred_element_type=F32)


def _in_proj_kernel(x_ref, gain_ref, w_ref, cos_ref, sin_ref, qa_ref, kv_ref, r_ref):
    x = x_ref[...]
    h = (x * _rms_scale(x) * gain_ref[...]).astype(BF16)
    p = _dot(h, w_ref[...])
    cos = cos_ref[...]
    sin = sin_ref[...]
    lane = lax.broadcasted_iota(jnp.int32, cos.shape, 1)
    first_half = (lane & (ATT_HEAD_DIM - 1)) < ROPE_HALF
    lo_head = lane < ATT_HEAD_DIM

    def rotate(blk):
        partner = jnp.where(first_half,
                            pltpu.roll(blk, LANES - ROPE_HALF, axis=1),
                            pltpu.roll(blk, ROPE_HALF, axis=1))
        return blk * cos + partner * sin

    q_scale = ATT_HEAD_DIM ** -0.5
    for j in range(ATT_WIDTH // LANES):
        qa_ref[:, j * LANES:(j + 1) * LANES] = (rotate(p[:, j * LANES:(j + 1) * LANES]) * q_scale).astype(BF16)

    def padded(col, base):
        swapped = pltpu.roll(col, ATT_HEAD_DIM, axis=1)
        zero = jnp.zeros_like(col)
        variants = (jnp.where(lo_head, col, zero), jnp.where(lo_head, zero, swapped),
                    jnp.where(lo_head, swapped, zero), jnp.where(lo_head, zero, col))
        for i, v in enumerate(variants):
            kv_ref[:, base + i * LANES:base + (i + 1) * LANES] = v.astype(BF16)

    padded(rotate(p[:, ATT_WIDTH:ATT_WIDTH + LANES]), 0)
    padded(p[:, ATT_WIDTH + LANES:ATT_WIDTH + 2 * LANES], 4 * LANES)
    r_ref[...] = p[:, ATT_WIDTH + 2 * ATT_KV_WIDTH:].astype(BF16)


def _in_proj(x2d, gain, w_in, cos_t, sin_t, tm):
    n = x2d.shape[0]
    in_cols = w_in.shape[1]
    r_cols = in_cols - ATT_WIDTH - 2 * ATT_KV_WIDTH
    row = lambda i: (i, 0)
    fixed = lambda i: (0, 0)
    return pl.pallas_call(
        _in_proj_kernel,
        grid=(n // tm,),
        in_specs=[
            pl.BlockSpec((tm, D_MODEL), row),
            pl.BlockSpec((1, D_MODEL), fixed),
            pl.BlockSpec((D_MODEL, in_cols), fixed),
            pl.BlockSpec((tm, LANES), row),
            pl.BlockSpec((tm, LANES), row),
        ],
        out_specs=[
            pl.BlockSpec((tm, ATT_WIDTH), row),
            pl.BlockSpec((tm, KV_PAD_WIDTH), row),
            pl.BlockSpec((tm, r_cols), row),
        ],
        out_shape=[
            jax.ShapeDtypeStruct((n, ATT_WIDTH), BF16),
            jax.ShapeDtypeStruct((n, KV_PAD_WIDTH), BF16),
            jax.ShapeDtypeStruct((n, r_cols), BF16),
        ],
        compiler_params=pltpu.CompilerParams(
            dimension_semantics=("arbitrary",), vmem_limit_bytes=VMEM_LIMIT_BYTES),
        name="in_proj",
    )(x2d, gain, w_in, cos_t, sin_t)


def _hgrn_constants():
    r = np.arange(MIX_TILE)[:, None]
    j = np.arange(MIX_TILE)[None, :]
    tri = (j <= r).astype(np.float32)
    rr = np.arange(LANES)[:, None]
    jj = np.arange(LANES)[None, :]
    masks = []
    for m in HG_LEVELS:
        if 2 * m <= LANES:
            masks.append(((rr // (2 * m)) == (jj // (2 * m))) & ((rr % (2 * m)) >= m) & ((jj % (2 * m)) < m))
    lmask = np.concatenate(masks, axis=0).astype(np.float32)
    return jnp.asarray(tri, BF16), jnp.asarray(lmask, F32)


def _level_decay(m, b, f, row_ids):
    ts = b.shape[0]
    if m >= SUBLANES // 2:
        b3 = b.reshape(ts // (2 * m), 2 * m, HG_DIM)
        ref = b3[:, m - 1:m, :]
        if m >= SUBLANES:
            e3 = jnp.concatenate([ref - b3[:, :m, :], b3[:, m:, :] - ref], axis=1)
        else:
            e3 = -jnp.abs(b3 - ref)
        return jnp.exp2(e3).reshape(ts, HG_DIM)
    if m == 2:
        pos = row_ids & 3
        f_next = pltpu.roll(f, ts - 1, axis=0)
        f_prev = pltpu.roll(f, 1, axis=0)
        return jnp.where(pos == 0, f_next, jnp.where(pos == 1, 1.0, jnp.where(pos == 2, f, f * f_prev)))
    assert m == 1
    return jnp.where((row_ids & 1) != 0, f, 1.0)


def _select_rows(m, upper_src, lower_src, row_ids):
    c = upper_src.shape[0]
    if m >= SUBLANES:
        pieces = []
        for b0 in range(0, c, 2 * m):
            pieces.append(lower_src[b0:b0 + m])
            pieces.append(upper_src[b0 + m:b0 + 2 * m])
        return jnp.concatenate(pieces, axis=0)
    return jnp.where((row_ids & m) != 0, upper_src, lower_src)


def _mixer_kernel(sinks_ref, qa_ref, kvc_ref, kvp_ref, qr_ref, fr_ref, ir_ref, gr_ref, x_ref,
                  wout_ref, gatt_ref, lbl_ref, ghg_ref, tri_ref, lmask_ref,
                  o_ref, state_ref):
    step = pl.program_id(1)
    ts = MIX_TILE
    blk = ATT_BLOCK

    @pl.when(step == 0)
    def _():
        state_ref[...] = jnp.zeros_like(state_ref)

    qrow = lax.broadcasted_iota(jnp.int32, (2 * blk, 2 * blk), 0) & (blk - 1)
    kcol = lax.broadcasted_iota(jnp.int32, (2 * blk, 2 * blk), 1)
    upper_half_rows = lax.broadcasted_iota(jnp.int32, (2 * blk, 1), 0) >= blk
    lane_lo = lax.broadcasted_iota(jnp.int32, (2 * blk, LANES), 1) < ATT_HEAD_DIM
    ones_lo = jnp.where(lane_lo, 1.0, 0.0).astype(BF16)
    ones_hi = jnp.where(lane_lo, 0.0, 1.0).astype(BF16)
    first_lo = jnp.where(step > 0, -1, blk - 1)
    att_rows = []
    for jb in range(ts // blk):
        rows = slice(jb * blk, (jb + 1) * blk)
        q_blk = qa_ref[rows, :]
        if jb == 0:
            kv_keys = jnp.concatenate([kvp_ref[...], kvc_ref[rows, :]], axis=0)
            lo_bound = jnp.maximum(qrow, first_lo)
        else:
            kv_keys = kvc_ref[(jb - 1) * blk:(jb + 1) * blk, :]
            lo_bound = qrow
        valid = (kcol > lo_bound) & (kcol <= qrow + blk)
        att_cols = []
        for gk in range(ATT_KV_HEADS):
            q2 = jnp.concatenate([q_blk[:, (2 * gk) * LANES:(2 * gk + 1) * LANES],
                                  q_blk[:, (2 * gk + 1) * LANES:(2 * gk + 2) * LANES]], axis=0)
            kbase = 2 * gk * LANES
            vbase = (2 * ATT_KV_HEADS + 2 * gk) * LANES
            k_lo = kv_keys[:, kbase:kbase + LANES]
            k_hi = kv_keys[:, kbase + LANES:kbase + 2 * LANES]
            r_lo = jnp.concatenate([kv_keys[:, vbase:vbase + LANES], ones_lo], axis=1)
            r_hi = jnp.concatenate([kv_keys[:, vbase + LANES:vbase + 2 * LANES], ones_hi], axis=1)
            h0 = ATT_GROUP * gk
            probs, esinks = [], []
            for which, k_op in ((0, k_lo), (1, k_hi)):
                sink = jnp.where(upper_half_rows, sinks_ref[h0 + 2 + which], sinks_ref[h0 + which])
                s = jnp.where(valid, _dot_nt(q2, k_op), MASK_VALUE)
                m = jnp.maximum(jnp.max(jnp.maximum(s[:, :blk], s[:, blk:]), axis=-1, keepdims=True), sink)
                probs.append(jnp.exp(s - m).astype(BF16))
                esinks.append(jnp.exp(sink - m))
            out = _dot(jnp.concatenate(probs, axis=1), jnp.concatenate([r_lo, r_hi], axis=0))
            denom = out[:, LANES:] + jnp.where(lane_lo, esinks[0], esinks[1])
            att2 = out[:, :LANES] / denom
            att_cols += [att2[:blk], att2[blk:]]
        att_rows.append(jnp.concatenate(att_cols, axis=1))
    att = jnp.concatenate(att_rows, axis=0)

    lbl = lbl_ref[...]
    lexp = jnp.exp(lbl - jnp.max(lbl, axis=0, keepdims=True))
    lb = lexp[0:1, :] / jnp.sum(lexp, axis=0, keepdims=True)
    tri = tri_ref[...]
    row_ids = lax.broadcasted_iota(jnp.int32, (ts, HG_DIM), 0)
    ones_sq = jnp.ones((HG_DIM, HG_DIM), BF16)
    half = ts // 2
    rec_cols = []
    for h in range(HG_HEADS):
        cols = slice(h * HG_DIM, (h + 1) * HG_DIM)
        lb_h = lb[:, cols]
        f = 0.5 * (1.0 + lb_h) + (0.5 * (1.0 - lb_h)) * jnp.tanh(0.5 * fr_ref[:, cols].astype(F32))
        kk = 1.0 - f
        qf = _silu(qr_ref[:, cols].astype(F32))
        vv = ir_ref[:, cols]
        g = jnp.log(f) * LOG2_E
        g_hi = g.astype(BF16)
        g_lo = (g - g_hi.astype(F32)).astype(BF16)
        b2 = _dot(tri, jnp.concatenate([g_hi, g_lo], axis=1))
        b = b2[:, :HG_DIM] + b2[:, HG_DIM:]
        w_incl = jnp.exp2(b)
        w_tail = jnp.exp2(b[ts - 1:ts, :] - b)
        st = state_ref[h]
        o = _dot_nt((qf * w_incl).astype(BF16), st.astype(BF16))
        n_tiles = half // SUBLANES
        quad = [[None] * n_tiles, [None] * n_tiles]
        s10 = None
        for li, m in enumerate(HG_LEVELS):
            x_f = _select_rows(m, qf, kk, row_ids) * _level_decay(m, b, f, row_ids)
            x_l = x_f.astype(BF16)
            if 2 * m == ts:
                s10 = _dot_nt(x_l[half:], x_l[:half])
                continue
            if m >= SUBLANES:
                tiles = [t for t in range(n_tiles) if (t * SUBLANES // m) % 2 == 1]
            else:
                tiles = list(range(n_tiles))
            for hf in range(2):
                r0 = hf * half
                if len(tiles) == n_tiles:
                    lhs = x_l[r0:r0 + half]
                else:
                    lhs = jnp.concatenate([x_f[r0 + t * SUBLANES:r0 + (t + 1) * SUBLANES] for t in tiles],
                                          axis=0).astype(BF16)
                sc = _dot_nt(lhs, x_l[r0:r0 + half])
                for i, t in enumerate(tiles):
                    m0 = (li - 1) * LANES + t * SUBLANES
                    part = sc[i * SUBLANES:(i + 1) * SUBLANES] * lmask_ref[m0:m0 + SUBLANES, :]
                    quad[hf][t] = part if quad[hf][t] is None else quad[hf][t] + part
        s00 = jnp.concatenate(quad[0], axis=0)
        s11 = jnp.concatenate(quad[1], axis=0)
        o_top = _dot(s00.astype(BF16), vv[:half])
        o_bot = _dot(jnp.concatenate([s10, s11], axis=1).astype(BF16), vv)
        diag = _dot((qf * kk).astype(BF16), ones_sq)
        o = o + jnp.concatenate([o_top, o_bot], axis=0) + diag * vv.astype(F32)
        state_ref[h] = st * w_incl[ts - 1:ts, :] + _dot_tn(vv, (kk * w_tail).astype(BF16))
        rec_cols.append(o * _rms_scale(o))
    rec_n = (jnp.concatenate(rec_cols, axis=1) * ghg_ref[...] * _silu(gr_ref[...].astype(F32))).astype(BF16)

    att_n = (att * _rms_scale(att) * gatt_ref[...]).astype(BF16)
    o_ref[...] = x_ref[...] + _dot(jnp.concatenate([att_n, rec_n], axis=1), wout_ref[...])


def _mixer(qa, kv, r, x, w_out, sinks, gatt, lbl, ghg):
    b, s, _ = x.shape
    ts = MIX_TILE
    tri, lmask = _hgrn_constants()
    nblk = ts // ATT_BLOCK
    cur = lambda bi, i: (bi, i, 0)
    prev = lambda bi, i: (bi, jnp.maximum(i * nblk - 1, 0), 0)
    fixed = lambda bi, i: (0, 0)
    rcol = lambda k: (lambda bi, i: (bi, i, k))
    return pl.pallas_call(
        _mixer_kernel,
        grid=(b, s // ts),
        in_specs=[
            pl.BlockSpec(memory_space=pltpu.SMEM),
            pl.BlockSpec((None, ts, ATT_WIDTH), cur),
            pl.BlockSpec((None, ts, KV_PAD_WIDTH), cur),
            pl.BlockSpec((None, ATT_BLOCK, KV_PAD_WIDTH), prev),
            pl.BlockSpec((None, ts, HG_WIDTH), rcol(0)),
            pl.BlockSpec((None, ts, HG_WIDTH), rcol(1)),
            pl.BlockSpec((None, ts, HG_WIDTH), rcol(2)),
            pl.BlockSpec((None, ts, HG_WIDTH), rcol(3)),
            pl.BlockSpec((None, ts, D_MODEL), cur),
            pl.BlockSpec((D_MODEL, D_MODEL), fixed),
            pl.BlockSpec((1, ATT_WIDTH), fixed),
            pl.BlockSpec(lbl.shape, fixed),
            pl.BlockSpec((1, HG_WIDTH), fixed),
            pl.BlockSpec(tri.shape, fixed),
            pl.BlockSpec(lmask.shape, fixed),
        ],
        out_specs=pl.BlockSpec((None, ts, D_MODEL), cur),
        out_shape=jax.ShapeDtypeStruct((b, s, D_MODEL), F32),
        scratch_shapes=[pltpu.VMEM((HG_HEADS, HG_DIM, HG_DIM), F32)],
        compiler_params=pltpu.CompilerParams(
            dimension_semantics=("arbitrary", "arbitrary"), vmem_limit_bytes=VMEM_LIMIT_BYTES),
        name="mixer",
    )(sinks, qa, kv, kv, r, r, r, r, x, w_out, gatt, lbl, ghg, tri, lmask)


def _mem_kv_kernel(mem_ref, gain_ref, w_ref, k_ref, v_ref):
    m = mem_ref[...]
    mn = (m * _rms_scale(m) * gain_ref[...]).astype(BF16)
    kv = _dot(mn, w_ref[...])
    k_ref[...] = kv[:, :D_MODEL].astype(BF16)
    v_ref[...] = kv[:, D_MODEL:].astype(BF16)


def _mem_kv(mem, gain, w_xkv):
    b, m, _ = mem.shape
    blk = lambda bi: (bi, 0, 0)
    fixed = lambda bi: (0, 0)
    return pl.pallas_call(
        _mem_kv_kernel,
        grid=(b,),
        in_specs=[
            pl.BlockSpec((None, m, D_MODEL), blk),
            pl.BlockSpec((1, D_MODEL), fixed),
            pl.BlockSpec((D_MODEL, 2 * D_MODEL), fixed),
        ],
        out_specs=[pl.BlockSpec((None, m, D_MODEL), blk), pl.BlockSpec((None, m, D_MODEL), blk)],
        out_shape=[jax.ShapeDtypeStruct((b, m, D_MODEL), BF16)] * 2,
        compiler_params=pltpu.CompilerParams(
            dimension_semantics=("arbitrary",), vmem_limit_bytes=VMEM_LIMIT_BYTES),
        name="mem_kv",
    )(mem, gain, w_xkv)


def _xattn_kernel(x_ref, gain_ref, wq_ref, k_ref, v_ref, wo_ref, o_ref):
    x = x_ref[...]
    hq = (x * _rms_scale(x) * gain_ref[...]).astype(BF16)
    q = _dot(hq, wq_ref[...]).astype(BF16)
    outs = []
    for h in range(X_HEADS):
        cols = slice(h * X_HEAD_DIM, (h + 1) * X_HEAD_DIM)
        s = _dot_nt(q[:, cols], k_ref[:, cols]) * (X_HEAD_DIM ** -0.5)
        p = jnp.exp(s - jnp.max(s, axis=-1, keepdims=True))
        o = _dot(p.astype(BF16), v_ref[:, cols])
        outs.append((o / jnp.sum(p, axis=-1, keepdims=True)).astype(BF16))
    xo = jnp.concatenate(outs, axis=1)
    o_ref[...] = x + _dot(xo, wo_ref[...])


def _xattn(x, gain, w_xq, xk, xv, w_xo, tm):
    b, s, _ = x.shape
    m = xk.shape[1]
    cur = lambda bi, i: (bi, i, 0)
    mem = lambda bi, i: (bi, 0, 0)
    fixed = lambda bi, i: (0, 0)
    return pl.pallas_call(
        _xattn_kernel,
        grid=(b, s // tm),
        in_specs=[
            pl.BlockSpec((None, tm, D_MODEL), cur),
            pl.BlockSpec((1, D_MODEL), fixed),
            pl.BlockSpec((D_MODEL, D_MODEL), fixed),
            pl.BlockSpec((None, m, D_MODEL), mem),
            pl.BlockSpec((None, m, D_MODEL), mem),
            pl.BlockSpec((D_MODEL, D_MODEL), fixed),
        ],
        out_specs=pl.BlockSpec((None, tm, D_MODEL), cur),
        out_shape=jax.ShapeDtypeStruct((b, s, D_MODEL), F32),
        compiler_params=pltpu.CompilerParams(
            dimension_semantics=("arbitrary", "arbitrary"), vmem_limit_bytes=VMEM_LIMIT_BYTES),
        name="xattn",
    )(x, gain, w_xq, xk, xv, w_xo)


FFN_CHUNKS = ((0, 1024), (1024, 2048), (2048, FFN_HIDDEN))


def _ffn_kernel(x_ref, gain_ref, wgu_ref, wd_ref, gfin_ref, o_ref):
    x = x_ref[...]
    hf = (x * _rms_scale(x) * gain_ref[...]).astype(BF16)
    acc = x
    for lo, hi in FFN_CHUNKS:
        gate = _dot(hf, wgu_ref[:, lo:hi])
        upv = _dot(hf, wgu_ref[:, FFN_HIDDEN + lo:FFN_HIDDEN + hi])
        act = (_silu(gate) * upv).astype(BF16)
        acc = acc + _dot(act, wd_ref[lo:hi, :])
    o_ref[...] = acc * _rms_scale(acc) * gfin_ref[...]


def _ffn(x2d, gain, w_gu, w_d, gfin, tm):
    n = x2d.shape[0]
    row = lambda i: (i, 0)
    fixed = lambda i: (0, 0)
    return pl.pallas_call(
        _ffn_kernel,
        grid=(n // tm,),
        in_specs=[
            pl.BlockSpec((tm, D_MODEL), row),
            pl.BlockSpec((1, D_MODEL), fixed),
            pl.BlockSpec((D_MODEL, 2 * FFN_HIDDEN), fixed, pipeline_mode=pl.Buffered(1)),
            pl.BlockSpec((FFN_HIDDEN, D_MODEL), fixed, pipeline_mode=pl.Buffered(1)),
            pl.BlockSpec((1, D_MODEL), fixed),
        ],
        out_specs=pl.BlockSpec((tm, D_MODEL), row),
        out_shape=jax.ShapeDtypeStruct((n, D_MODEL), F32),
        compiler_params=pltpu.CompilerParams(
            dimension_semantics=("arbitrary",), vmem_limit_bytes=VMEM_LIMIT_BYTES),
        name="ffn",
    )(x2d, gain, w_gu, w_d, gfin)


def _rotary_tables(positions):
    inv_freq = jnp.power(jnp.float32(ROPE_THETA),
                         -jnp.arange(ROPE_HALF, dtype=F32) * (2.0 / ROPE_DIM))
    ang = positions.astype(F32)[..., None] * inv_freq
    cos = jnp.cos(ang)
    sin = jnp.sin(ang)
    rest = ATT_HEAD_DIM - ROPE_DIM
    ones = jnp.ones(cos.shape[:-1] + (rest,), F32)
    zeros = jnp.zeros(cos.shape[:-1] + (rest,), F32)
    cos_h = jnp.concatenate([cos, cos, ones], axis=-1)
    sin_h = jnp.concatenate([-sin, sin, zeros], axis=-1)
    reps = LANES // ATT_HEAD_DIM
    cos_t = jnp.tile(cos_h, (1, 1, reps)).reshape(-1, LANES)
    sin_t = jnp.tile(sin_h, (1, 1, reps)).reshape(-1, LANES)
    return cos_t, sin_t


def kernel(x, mem, positions, norm_mix, w_in, att_sinks, att_out_gain, hg_lb_logits, hg_out_gain,
           w_out, norm_xattn, norm_mem, w_xq, w_xkv, w_xo, norm_ffn, w_gate_up, w_down, norm_final):
    b, s, d = x.shape
    n = b * s
    assert w_in.shape[0] == 1 and hg_lb_logits.shape[0] == 2, "single-layer block only"
    row = lambda v: v.reshape(1, -1).astype(F32)
    cos_t, sin_t = _rotary_tables(positions)
    qa, kv, r = _in_proj(x.reshape(n, d), row(norm_mix[0]), w_in[0].astype(BF16), cos_t, sin_t, tm=512)
    x = _mixer(qa.reshape(b, s, -1), kv.reshape(b, s, -1), r.reshape(b, s, -1), x,
               w_out[0].astype(BF16), att_sinks[0].astype(F32), row(att_out_gain[0]),
               hg_lb_logits.astype(F32), row(hg_out_gain[0]))
    xk, xv = _mem_kv(mem, row(norm_mem[0]), w_xkv[0].astype(BF16))
    x = _xattn(x, row(norm_xattn[0]), w_xq[0].astype(BF16), xk, xv, w_xo[0].astype(BF16), tm=512)
    y = _ffn(x.reshape(n, d), row(norm_ffn[0]), w_gate_up[0].astype(BF16), w_down[0].astype(BF16),
             row(norm_final), tm=512)
    return y.reshape(b, s, d)
```

```python
import numpy as np
import jax
import jax.numpy as jnp
from jax import lax
from jax.experimental import pallas as pl
from jax.experimental.pallas import tpu as pltpu

F32 = jnp.float32
BF16 = jnp.bfloat16

D_MODEL = 1024
MEM_LEN = 256
ATT_HEADS = 8
ATT_KV_HEADS = 2
ATT_GROUP = ATT_HEADS // ATT_KV_HEADS
ATT_HEAD_DIM = 64
ATT_WIDTH = ATT_HEADS * ATT_HEAD_DIM
ATT_KV_WIDTH = ATT_KV_HEADS * ATT_HEAD_DIM
ATT_BLOCK = 128
ROPE_THETA = 500000.0
ROPE_DIM = ATT_HEAD_DIM // 4
ROPE_HALF = ROPE_DIM // 2
HG_HEADS = 4
HG_DIM = 128
HG_WIDTH = HG_HEADS * HG_DIM
X_HEADS = 4
X_HEAD_DIM = D_MODEL // X_HEADS
FFN_HIDDEN = 2816
RMS_EPS = 1e-6
LANES = 128
SUBLANES = 8
MASK_VALUE = -1e30
LOG2_E = 1.4426950408889634

MIX_TILE = 256
HG_LEVELS = (128, 64, 32, 16, 8, 4, 2, 1)
KV_PAD_WIDTH = 2 * ATT_KV_HEADS * 2 * LANES

VMEM_LIMIT_BYTES = 56 * 1024 * 1024


def _rms_scale(x):
    return lax.rsqrt(jnp.mean(x * x, axis=-1, keepdims=True) + RMS_EPS)


def _silu(x):
    h = 0.5 * x
    return h + h * jnp.tanh(h)


def _dot(a, b):
    return jnp.dot(a, b, preferred_element_type=F32)


def _dot_nt(a, b):
    return lax.dot_general(a, b, (((1,), (1,)), ((), ())), preferred_element_type=F32)


def _dot_tn(a, b):
    return lax.dot_general(a, b, (((0,), (0,)), ((), ())), preferred_element_type=F32)


def _hgrn_constants():
    r = np.arange(MIX_TILE)[:, None]
    j = np.arange(MIX_TILE)[None, :]
    tri = (j <= r).astype(np.float32)
    rr = np.arange(LANES)[:, None]
    jj = np.arange(LANES)[None, :]
    masks = []
    for m in HG_LEVELS:
        if 2 * m <= LANES:
            masks.append(((rr // (2 * m)) == (jj // (2 * m))) & ((rr % (2 * m)) >= m) & ((jj % (2 * m)) < m))
    lmask = np.concatenate(masks, axis=0).astype(np.float32)
    return jnp.asarray(tri, BF16), jnp.asarray(lmask, F32)


def _level_decay(m, b, f, row_ids):
    ts = b.shape[0]
    if m >= SUBLANES // 2:
        b3 = b.reshape(ts // (2 * m), 2 * m, HG_DIM)
        ref = b3[:, m - 1:m, :]
        if m >= SUBLANES:
            e3 = jnp.concatenate([ref - b3[:, :m, :], b3[:, m:, :] - ref], axis=1)
        else:
            e3 = -jnp.abs(b3 - ref)
        return jnp.exp2(e3).reshape(ts, HG_DIM)
    if m == 2:
        pos = row_ids & 3
        f_next = pltpu.roll(f, ts - 1, axis=0)
        f_prev = pltpu.roll(f, 1, axis=0)
        return jnp.where(pos == 0, f_next, jnp.where(pos == 1, 1.0, jnp.where(pos == 2, f, f * f_prev)))
    assert m == 1
    return jnp.where((row_ids & 1) != 0, f, 1.0)


def _select_rows(m, upper_src, lower_src, row_ids):
    c = upper_src.shape[0]
    if m >= SUBLANES:
        pieces = []
        for b0 in range(0, c, 2 * m):
            pieces.append(lower_src[b0:b0 + m])
            pieces.append(upper_src[b0 + m:b0 + 2 * m])
        return jnp.concatenate(pieces, axis=0)
    return jnp.where((row_ids & m) != 0, upper_src, lower_src)


def _mixer_kernel(sinks_ref, x_ref, gmix_ref, win_ref, cos_ref, sin_ref, wout_ref, gatt_ref, lbl_ref,
                  ghg_ref, tri_ref, lmask_ref, o_ref, state_ref, kvprev_ref):
    step = pl.program_id(1)
    ts = MIX_TILE
    blk = ATT_BLOCK

    @pl.when(step == 0)
    def _():
        state_ref[...] = jnp.zeros_like(state_ref)
        kvprev_ref[...] = jnp.zeros_like(kvprev_ref)

    x = x_ref[...]
    hn = (x * _rms_scale(x) * gmix_ref[...]).astype(BF16)

    def proj(c0, c1):
        return _dot(hn, win_ref[:, c0:c1])

    cos = cos_ref[...]
    sin = sin_ref[...]
    lane = lax.broadcasted_iota(jnp.int32, cos.shape, 1)
    first_half = (lane & (ATT_HEAD_DIM - 1)) < ROPE_HALF
    lo_head = lane < ATT_HEAD_DIM

    def rotate(col):
        partner = jnp.where(first_half,
                            pltpu.roll(col, LANES - ROPE_HALF, axis=1),
                            pltpu.roll(col, ROPE_HALF, axis=1))
        return col * cos + partner * sin

    def padded(col):
        swapped = pltpu.roll(col, ATT_HEAD_DIM, axis=1)
        zero = jnp.zeros_like(col)
        return [v.astype(BF16) for v in (jnp.where(lo_head, col, zero), jnp.where(lo_head, zero, swapped),
                                         jnp.where(lo_head, swapped, zero), jnp.where(lo_head, zero, col))]

    pq = proj(0, ATT_WIDTH)
    q_cols = [(rotate(pq[:, j * LANES:(j + 1) * LANES]) * (ATT_HEAD_DIM ** -0.5)).astype(BF16)
              for j in range(ATT_WIDTH // LANES)]
    pkv = proj(ATT_WIDTH, ATT_WIDTH + 2 * ATT_KV_WIDTH)
    kv_new = padded(rotate(pkv[:, :LANES])) + padded(pkv[:, LANES:])
    kv_all = [jnp.concatenate([kvprev_ref[:, i * LANES:(i + 1) * LANES], v], axis=0)
              for i, v in enumerate(kv_new)]
    for i, v in enumerate(kv_new):
        kvprev_ref[:, i * LANES:(i + 1) * LANES] = v[ts - blk:]
    r0c = ATT_WIDTH + 2 * ATT_KV_WIDTH
    p_q, p_f, p_i, p_g = (proj(r0c + k * HG_WIDTH, r0c + (k + 1) * HG_WIDTH) for k in range(4))

    qrow = lax.broadcasted_iota(jnp.int32, (2 * blk, 2 * blk), 0) & (blk - 1)
    kcol = lax.broadcasted_iota(jnp.int32, (2 * blk, 2 * blk), 1)
    upper_half_rows = lax.broadcasted_iota(jnp.int32, (2 * blk, 1), 0) >= blk
    lane_lo = lax.broadcasted_iota(jnp.int32, (2 * blk, LANES), 1) < ATT_HEAD_DIM
    ones_lo = jnp.where(lane_lo, 1.0, 0.0).astype(BF16)
    ones_hi = jnp.where(lane_lo, 0.0, 1.0).astype(BF16)
    first_lo = jnp.where(step > 0, -1, blk - 1)
    att_rows = []
    for jb in range(ts // blk):
        keys = slice(jb * blk, (jb + 2) * blk)
        lo_bound = jnp.maximum(qrow, first_lo) if jb == 0 else qrow
        valid = (kcol > lo_bound) & (kcol <= qrow + blk)
        att_cols = []
        for gk in range(ATT_KV_HEADS):
            q2 = jnp.concatenate([q_cols[2 * gk][jb * blk:(jb + 1) * blk],
                                  q_cols[2 * gk + 1][jb * blk:(jb + 1) * blk]], axis=0)
            k_lo = kv_all[2 * gk][keys]
            k_hi = kv_all[2 * gk + 1][keys]
            r_lo = jnp.concatenate([kv_all[4 + 2 * gk][keys], ones_lo], axis=1)
            r_hi = jnp.concatenate([kv_all[5 + 2 * gk][keys], ones_hi], axis=1)
            h0 = ATT_GROUP * gk
            probs, esinks = [], []
            for which, k_op in ((0, k_lo), (1, k_hi)):
                sink = jnp.where(upper_half_rows, sinks_ref[h0 + 2 + which], sinks_ref[h0 + which])
                s = jnp.where(valid, _dot_nt(q2, k_op), MASK_VALUE)
                m = jnp.maximum(jnp.max(jnp.maximum(s[:, :blk], s[:, blk:]), axis=-1, keepdims=True), sink)
                probs.append(jnp.exp(s - m).astype(BF16))
                esinks.append(jnp.exp(sink - m))
            out = _dot(jnp.concatenate(probs, axis=1), jnp.concatenate([r_lo, r_hi], axis=0))
            denom = out[:, LANES:] + jnp.where(lane_lo, esinks[0], esinks[1])
            att2 = out[:, :LANES] / denom
            att_cols += [att2[:blk], att2[blk:]]
        att_rows.append(jnp.concatenate(att_cols, axis=1))
    att = jnp.concatenate(att_rows, axis=0)

    lbl = lbl_ref[...]
    lexp = jnp.exp(lbl - jnp.max(lbl, axis=0, keepdims=True))
    lb = lexp[0:1, :] / jnp.sum(lexp, axis=0, keepdims=True)
    tri = tri_ref[...]
    row_ids = lax.broadcasted_iota(jnp.int32, (ts, HG_DIM), 0)
    ones_sq = jnp.ones((HG_DIM, HG_DIM), BF16)
    half = ts // 2
    rec_cols = []
    for h in range(HG_HEADS):
        cols = slice(h * HG_DIM, (h + 1) * HG_DIM)
        lb_h = lb[:, cols]
        f = 0.5 * (1.0 + lb_h) + (0.5 * (1.0 - lb_h)) * jnp.tanh(0.5 * p_f[:, cols])
        kk = 1.0 - f
        qf = _silu(p_q[:, cols])
        vv = p_i[:, cols].astype(BF16)
        g = jnp.log(f) * LOG2_E
        g_hi = g.astype(BF16)
        g_lo = (g - g_hi.astype(F32)).astype(BF16)
        b2 = _dot(tri, jnp.concatenate([g_hi, g_lo], axis=1))
        b = b2[:, :HG_DIM] + b2[:, HG_DIM:]
        w_incl = jnp.exp2(b)
        w_tail = jnp.exp2(b[ts - 1:ts, :] - b)
        st = state_ref[h]
        o = _dot_nt((qf * w_incl).astype(BF16), st.astype(BF16))
        n_tiles = half // SUBLANES
        quad = [[None] * n_tiles, [None] * n_tiles]
        s10 = None
        for li, m in enumerate(HG_LEVELS):
            x_f = _select_rows(m, qf, kk, row_ids) * _level_decay(m, b, f, row_ids)
            x_l = x_f.astype(BF16)
            if 2 * m == ts:
                s10 = _dot_nt(x_l[half:], x_l[:half])
                continue
            if m >= SUBLANES:
                tiles = [t for t in range(n_tiles) if (t * SUBLANES // m) % 2 == 1]
            else:
                tiles = list(range(n_tiles))
            for hf in range(2):
                r0 = hf * half
                if len(tiles) == n_tiles:
                    lhs = x_l[r0:r0 + half]
                else:
                    lhs = jnp.concatenate([x_f[r0 + t * SUBLANES:r0 + (t + 1) * SUBLANES] for t in tiles],
                                          axis=0).astype(BF16)
                sc = _dot_nt(lhs, x_l[r0:r0 + half])
                for i, t in enumerate(tiles):
                    m0 = (li - 1) * LANES + t * SUBLANES
                    part = sc[i * SUBLANES:(i + 1) * SUBLANES] * lmask_ref[m0:m0 + SUBLANES, :]
                    quad[hf][t] = part if quad[hf][t] is None else quad[hf][t] + part
        s00 = jnp.concatenate(quad[0], axis=0)
        s11 = jnp.concatenate(quad[1], axis=0)
        o_top = _dot(s00.astype(BF16), vv[:half])
        o_bot = _dot(jnp.concatenate([s10, s11], axis=1).astype(BF16), vv)
        diag = _dot((qf * kk).astype(BF16), ones_sq)
        o = o + jnp.concatenate([o_top, o_bot], axis=0) + diag * p_i[:, cols]
        state_ref[h] = st * w_incl[ts - 1:ts, :] + _dot_tn(vv, (kk * w_tail).astype(BF16))
        rec_cols.append(o * _rms_scale(o))
    rec_n = (jnp.concatenate(rec_cols, axis=1) * ghg_ref[...] * _silu(p_g)).astype(BF16)

    att_n = (att * _rms_scale(att) * gatt_ref[...]).astype(BF16)
    o_ref[...] = x + _dot(jnp.concatenate([att_n, rec_n], axis=1), wout_ref[...])


def _mixer(x, gmix, w_in, cos_t, sin_t, w_out, sinks, gatt, lbl, ghg):
    b, s, _ = x.shape
    ts = MIX_TILE
    tri, lmask = _hgrn_constants()
    cur = lambda bi, i: (bi, i, 0)
    fixed = lambda bi, i: (0, 0)
    return pl.pallas_call(
        _mixer_kernel,
        grid=(b, s // ts),
        in_specs=[
            pl.BlockSpec(memory_space=pltpu.SMEM),
            pl.BlockSpec((None, ts, D_MODEL), cur),
            pl.BlockSpec((1, D_MODEL), fixed),
            pl.BlockSpec(w_in.shape, fixed, pipeline_mode=pl.Buffered(1)),
            pl.BlockSpec((None, ts, LANES), cur),
            pl.BlockSpec((None, ts, LANES), cur),
            pl.BlockSpec((D_MODEL, D_MODEL), fixed, pipeline_mode=pl.Buffered(1)),
            pl.BlockSpec((1, ATT_WIDTH), fixed),
            pl.BlockSpec(lbl.shape, fixed),
            pl.BlockSpec((1, HG_WIDTH), fixed),
            pl.BlockSpec(tri.shape, fixed),
            pl.BlockSpec(lmask.shape, fixed),
        ],
        out_specs=pl.BlockSpec((None, ts, D_MODEL), cur),
        out_shape=jax.ShapeDtypeStruct((b, s, D_MODEL), F32),
        scratch_shapes=[
            pltpu.VMEM((HG_HEADS, HG_DIM, HG_DIM), F32),
            pltpu.VMEM((ATT_BLOCK, KV_PAD_WIDTH), BF16),
        ],
        compiler_params=pltpu.CompilerParams(
            dimension_semantics=("arbitrary", "arbitrary"), vmem_limit_bytes=VMEM_LIMIT_BYTES),
        name="mixer",
    )(sinks, x, gmix, w_in, cos_t, sin_t, w_out, gatt, lbl, ghg, tri, lmask)


def _mem_kv_kernel(mem_ref, gain_ref, w_ref, k_ref, v_ref):
    m = mem_ref[...]
    mn = (m * _rms_scale(m) * gain_ref[...]).astype(BF16)
    kv = _dot(mn, w_ref[...])
    k_ref[...] = kv[:, :D_MODEL].astype(BF16)
    v_ref[...] = kv[:, D_MODEL:].astype(BF16)


def _mem_kv(mem, gain, w_xkv):
    b, m, _ = mem.shape
    blk = lambda bi: (bi, 0, 0)
    fixed = lambda bi: (0, 0)
    return pl.pallas_call(
        _mem_kv_kernel,
        grid=(b,),
        in_specs=[
            pl.BlockSpec((None, m, D_MODEL), blk),
            pl.BlockSpec((1, D_MODEL), fixed),
            pl.BlockSpec((D_MODEL, 2 * D_MODEL), fixed),
        ],
        out_specs=[pl.BlockSpec((None, m, D_MODEL), blk), pl.BlockSpec((None, m, D_MODEL), blk)],
        out_shape=[jax.ShapeDtypeStruct((b, m, D_MODEL), BF16)] * 2,
        compiler_params=pltpu.CompilerParams(
            dimension_semantics=("arbitrary",), vmem_limit_bytes=VMEM_LIMIT_BYTES),
        name="mem_kv",
    )(mem, gain, w_xkv)


def _xattn_kernel(x_ref, gain_ref, wq_ref, k_ref, v_ref, wo_ref, o_ref):
    x = x_ref[...]
    hq = (x * _rms_scale(x) * gain_ref[...]).astype(BF16)
    q = _dot(hq, wq_ref[...]).astype(BF16)
    outs = []
    for h in range(X_HEADS):
        cols = slice(h * X_HEAD_DIM, (h + 1) * X_HEAD_DIM)
        s = _dot_nt(q[:, cols], k_ref[:, cols]) * (X_HEAD_DIM ** -0.5)
        p = jnp.exp(s - jnp.max(s, axis=-1, keepdims=True))
        o = _dot(p.astype(BF16), v_ref[:, cols])
        outs.append((o / jnp.sum(p, axis=-1, keepdims=True)).astype(BF16))
    xo = jnp.concatenate(outs, axis=1)
    o_ref[...] = x + _dot(xo, wo_ref[...])


def _xattn(x, gain, w_xq, xk, xv, w_xo, tm):
    b, s, _ = x.shape
    m = xk.shape[1]
    cur = lambda bi, i: (bi, i, 0)
    mem = lambda bi, i: (bi, 0, 0)
    fixed = lambda bi, i: (0, 0)
    return pl.pallas_call(
        _xattn_kernel,
        grid=(b, s // tm),
        in_specs=[
            pl.BlockSpec((None, tm, D_MODEL), cur),
            pl.BlockSpec((1, D_MODEL), fixed),
            pl.BlockSpec((D_MODEL, D_MODEL), fixed),
            pl.BlockSpec((None, m, D_MODEL), mem),
            pl.BlockSpec((None, m, D_MODEL), mem),
            pl.BlockSpec((D_MODEL, D_MODEL), fixed),
        ],
        out_specs=pl.BlockSpec((None, tm, D_MODEL), cur),
        out_shape=jax.ShapeDtypeStruct((b, s, D_MODEL), F32),
        compiler_params=pltpu.CompilerParams(
            dimension_semantics=("arbitrary", "arbitrary"), vmem_limit_bytes=VMEM_LIMIT_BYTES),
        name="xattn",
    )(x, gain, w_xq, xk, xv, w_xo)


FFN_CHUNKS = ((0, 1024), (1024, 2048), (2048, FFN_HIDDEN))


def _ffn_kernel(x_ref, gain_ref, wgu_ref, wd_ref, gfin_ref, o_ref):
    x = x_ref[...]
    hf = (x * _rms_scale(x) * gain_ref[...]).astype(BF16)
    acc = x
    for lo, hi in FFN_CHUNKS:
        gate = _dot(hf, wgu_ref[:, lo:hi])
        upv = _dot(hf, wgu_ref[:, FFN_HIDDEN + lo:FFN_HIDDEN + hi])
        act = (_silu(gate) * upv).astype(BF16)
        acc = acc + _dot(act, wd_ref[lo:hi, :])
    o_ref[...] = acc * _rms_scale(acc) * gfin_ref[...]


def _ffn(x2d, gain, w_gu, w_d, gfin, tm):
    n = x2d.shape[0]
    row = lambda i: (i, 0)
    fixed = lambda i: (0, 0)
    return pl.pallas_call(
        _ffn_kernel,
        grid=(n // tm,),
        in_specs=[
            pl.BlockSpec((tm, D_MODEL), row),
            pl.BlockSpec((1, D_MODEL), fixed),
            pl.BlockSpec((D_MODEL, 2 * FFN_HIDDEN), fixed, pipeline_mode=pl.Buffered(1)),
            pl.BlockSpec((FFN_HIDDEN, D_MODEL), fixed, pipeline_mode=pl.Buffered(1)),
            pl.BlockSpec((1, D_MODEL), fixed),
        ],
        out_specs=pl.BlockSpec((tm, D_MODEL), row),
        out_shape=jax.ShapeDtypeStruct((n, D_MODEL), F32),
        compiler_params=pltpu.CompilerParams(
            dimension_semantics=("arbitrary",), vmem_limit_bytes=VMEM_LIMIT_BYTES),
        name="ffn",
    )(x2d, gain, w_gu, w_d, gfin)


def _rotary_tables(positions):
    inv_freq = jnp.power(jnp.float32(ROPE_THETA),
                         -jnp.arange(ROPE_HALF, dtype=F32) * (2.0 / ROPE_DIM))
    ang = positions.astype(F32)[..., None] * inv_freq
    cos = jnp.cos(ang)
    sin = jnp.sin(ang)
    rest = ATT_HEAD_DIM - ROPE_DIM
    ones = jnp.ones(cos.shape[:-1] + (rest,), F32)
    zeros = jnp.zeros(cos.shape[:-1] + (rest,), F32)
    cos_h = jnp.concatenate([cos, cos, ones], axis=-1)
    sin_h = jnp.concatenate([-sin, sin, zeros], axis=-1)
    reps = LANES // ATT_HEAD_DIM
    cos_t = jnp.tile(cos_h, (1, 1, reps))
    sin_t = jnp.tile(sin_h, (1, 1, reps))
    return cos_t, sin_t


def kernel(x, mem, positions, norm_mix, w_in, att_sinks, att_out_gain, hg_lb_logits, hg_out_gain,
           w_out, norm_xattn, norm_mem, w_xq, w_xkv, w_xo, norm_ffn, w_gate_up, w_down, norm_final):
    b, s, d = x.shape
    n = b * s
    assert w_in.shape[0] == 1 and hg_lb_logits.shape[0] == 2, "single-layer block only"
    row = lambda v: v.reshape(1, -1).astype(F32)
    cos_t, sin_t = _rotary_tables(positions)
    x = _mixer(x, row(norm_mix[0]), w_in[0].astype(BF16), cos_t, sin_t,
               w_out[0].astype(BF16), att_sinks[0].astype(F32), row(att_out_gain[0]),
               hg_lb_logits.astype(F32), row(hg_out_gain[0]))
    xk, xv = _mem_kv(mem, row(norm_mem[0]), w_xkv[0].astype(BF16))
    x = _xattn(x, row(norm_xattn[0]), w_xq[0].astype(BF16), xk, xv, w_xo[0].astype(BF16), tm=512)
    y = _ffn(x.reshape(n, d), row(norm_ffn[0]), w_gate_up[0].astype(BF16), w_down[0].astype(BF16),
             row(norm_final), tm=512)
    return y.reshape(b, s, d)
```

```python
import numpy as np
import jax
import jax.numpy as jnp
from jax import lax
from jax.experimental import pallas as pl
from jax.experimental.pallas import tpu as pltpu

F32 = jnp.float32
BF16 = jnp.bfloat16

D_MODEL = 1024
SEQ_LEN = 2048
MEM_LEN = 256
ATT_HEADS = 8
ATT_KV_HEADS = 2
ATT_GROUP = ATT_HEADS // ATT_KV_HEADS
ATT_HEAD_DIM = 64
ATT_WIDTH = ATT_HEADS * ATT_HEAD_DIM
ATT_KV_WIDTH = ATT_KV_HEADS * ATT_HEAD_DIM
ATT_BLOCK = 128
ROPE_THETA = 500000.0
ROPE_DIM = ATT_HEAD_DIM // 4
ROPE_HALF = ROPE_DIM // 2
HG_HEADS = 4
HG_DIM = 128
HG_WIDTH = HG_HEADS * HG_DIM
X_HEADS = 4
X_HEAD_DIM = D_MODEL // X_HEADS
FFN_HIDDEN = 2816
RMS_EPS = 1e-6
LANES = 128
SUBLANES = 8
MASK_VALUE = -1e30
LOG2_E = 1.4426950408889634

MIX_TILE = 256
HG_LEVELS = (128, 64, 32, 16, 8, 4, 2, 1)
KV_PAD_WIDTH = 2 * ATT_KV_HEADS * 2 * LANES

VMEM_LIMIT_BYTES = 56 * 1024 * 1024


def _rms_scale(x):
    return lax.rsqrt(jnp.mean(x * x, axis=-1, keepdims=True) + RMS_EPS)


def _silu(x):
    h = 0.5 * x
    return h + h * jnp.tanh(h)


def _dot(a, b):
    return jnp.dot(a, b, preferred_element_type=F32)


def _dot_nt(a, b):
    return lax.dot_general(a, b, (((1,), (1,)), ((), ())), preferred_element_type=F32)


def _dot_tn(a, b):
    return lax.dot_general(a, b, (((0,), (0,)), ((), ())), preferred_element_type=F32)


def _hgrn_constants():
    r = np.arange(MIX_TILE)[:, None]
    j = np.arange(MIX_TILE)[None, :]
    tri = (j <= r).astype(np.float32)
    rr = np.arange(LANES)[:, None]
    jj = np.arange(LANES)[None, :]
    masks = []
    for m in HG_LEVELS:
        if 2 * m <= LANES:
            masks.append(((rr // (2 * m)) == (jj // (2 * m))) & ((rr % (2 * m)) >= m) & ((jj % (2 * m)) < m))
    lmask = np.concatenate(masks, axis=0).astype(np.float32)
    return jnp.asarray(tri, BF16), jnp.asarray(lmask, F32)


def _level_decay(m, b, f, row_ids):
    ts = b.shape[0]
    if m >= SUBLANES // 2:
        b3 = b.reshape(ts // (2 * m), 2 * m, HG_DIM)
        ref = b3[:, m - 1:m, :]
        if m >= SUBLANES:
            e3 = jnp.concatenate([ref - b3[:, :m, :], b3[:, m:, :] - ref], axis=1)
        else:
            e3 = -jnp.abs(b3 - ref)
        return jnp.exp2(e3).reshape(ts, HG_DIM)
    if m == 2:
        pos = row_ids & 3
        f_next = pltpu.roll(f, ts - 1, axis=0)
        f_prev = pltpu.roll(f, 1, axis=0)
        return jnp.where(pos == 0, f_next, jnp.where(pos == 1, 1.0, jnp.where(pos == 2, f, f * f_prev)))
    assert m == 1
    return jnp.where((row_ids & 1) != 0, f, 1.0)


def _select_rows(m, upper_src, lower_src, row_ids):
    c = upper_src.shape[0]
    if m >= SUBLANES:
        pieces = []
        for b0 in range(0, c, 2 * m):
            pieces.append(lower_src[b0:b0 + m])
            pieces.append(upper_src[b0 + m:b0 + 2 * m])
        return jnp.concatenate(pieces, axis=0)
    return jnp.where((row_ids & m) != 0, upper_src, lower_src)


def _project_pieces(x_ref, cos_ref, sin_ref, rows, gmix_ref, win_ref, q_s, kv_s, r_s, v_s):
    st = {}

    def hn():
        if "hn" not in st:
            x = x_ref[rows, :]
            st["hn"] = (x * _rms_scale(x) * gmix_ref[...]).astype(BF16)
        return st["hn"]

    def proj(c0, c1):
        return _dot(hn(), win_ref[:, c0:c1])

    def rotate(col):
        cos = cos_ref[rows, :]
        sin = sin_ref[rows, :]
        lane = lax.broadcasted_iota(jnp.int32, col.shape, 1)
        first_half = (lane & (ATT_HEAD_DIM - 1)) < ROPE_HALF
        partner = jnp.where(first_half,
                            pltpu.roll(col, LANES - ROPE_HALF, axis=1),
                            pltpu.roll(col, ROPE_HALF, axis=1))
        return col * cos + partner * sin

    def padded(col):
        lo_head = lax.broadcasted_iota(jnp.int32, col.shape, 1) < ATT_HEAD_DIM
        swapped = pltpu.roll(col, ATT_HEAD_DIM, axis=1)
        zero = jnp.zeros_like(col)
        return (jnp.where(lo_head, col, zero), jnp.where(lo_head, zero, swapped),
                jnp.where(lo_head, swapped, zero), jnp.where(lo_head, zero, col))

    def piece_q():
        pq = proj(0, ATT_WIDTH)
        for j in range(ATT_WIDTH // LANES):
            q_s[:, j * LANES:(j + 1) * LANES] = (
                rotate(pq[:, j * LANES:(j + 1) * LANES]) * (ATT_HEAD_DIM ** -0.5)).astype(BF16)

    def piece_kv():
        pkv = proj(ATT_WIDTH, ATT_WIDTH + 2 * ATT_KV_WIDTH)
        for i, v in enumerate(padded(rotate(pkv[:, :LANES])) + padded(pkv[:, LANES:])):
            kv_s[:, i * LANES:(i + 1) * LANES] = v.astype(BF16)

    r0c = ATT_WIDTH + 2 * ATT_KV_WIDTH

    def piece_r(dst, src):
        def run():
            r_s[:, dst * HG_WIDTH:(dst + 1) * HG_WIDTH] = proj(r0c + src * HG_WIDTH, r0c + (src + 1) * HG_WIDTH)
        return run

    def piece_v():
        v_s[...] = proj(r0c + 2 * HG_WIDTH, r0c + 3 * HG_WIDTH).astype(BF16)

    return [piece_q, piece_kv, piece_r(0, 0), piece_r(1, 1), piece_v, piece_r(2, 3)]


def _mix_phases(first_lo, x_ref, o_ref, rows, sinks_ref, wout_ref, gatt_ref, lbl_ref, ghg_ref,
                tri_ref, lmask_ref, state_ref, kvprev_ref, q_s, kv_s, r_s, v_s):
    ts = MIX_TILE
    blk = ATT_BLOCK
    half = ts // 2
    shared = {"att": [[None] * (ATT_WIDTH // LANES) for _ in range(ts // blk)], "rec": [None] * HG_HEADS}

    def setup():
        shared["kv"] = [jnp.concatenate([kvprev_ref[:, i * LANES:(i + 1) * LANES],
                                         kv_s[:, i * LANES:(i + 1) * LANES]], axis=0)
                        for i in range(KV_PAD_WIDTH // LANES)]
        kvprev_ref[...] = kv_s[ts - blk:, :]

    def attention(jb, gk):
        def run():
            qrow = lax.broadcasted_iota(jnp.int32, (2 * blk, 2 * blk), 0) & (blk - 1)
            kcol = lax.broadcasted_iota(jnp.int32, (2 * blk, 2 * blk), 1)
            upper_half_rows = lax.broadcasted_iota(jnp.int32, (2 * blk, 1), 0) >= blk
            lane_lo = lax.broadcasted_iota(jnp.int32, (2 * blk, LANES), 1) < ATT_HEAD_DIM
            ones_lo = jnp.where(lane_lo, 1.0, 0.0).astype(BF16)
            ones_hi = jnp.where(lane_lo, 0.0, 1.0).astype(BF16)
            keys = slice(jb * blk, (jb + 2) * blk)
            lo_bound = jnp.maximum(qrow, first_lo) if (jb == 0 and first_lo is not None) else qrow
            valid = (kcol > lo_bound) & (kcol <= qrow + blk)
            kv_all = shared["kv"]
            q2 = jnp.concatenate([q_s[jb * blk:(jb + 1) * blk, (2 * gk) * LANES:(2 * gk + 1) * LANES],
                                  q_s[jb * blk:(jb + 1) * blk, (2 * gk + 1) * LANES:(2 * gk + 2) * LANES]],
                                 axis=0)
            k_lo = kv_all[2 * gk][keys]
            k_hi = kv_all[2 * gk + 1][keys]
            r_lo = jnp.concatenate([kv_all[4 + 2 * gk][keys], ones_lo], axis=1)
            r_hi = jnp.concatenate([kv_all[5 + 2 * gk][keys], ones_hi], axis=1)
            h0 = ATT_GROUP * gk
            probs, esinks = [], []
            for which, k_op in ((0, k_lo), (1, k_hi)):
                sink = jnp.where(upper_half_rows, sinks_ref[h0 + 2 + which], sinks_ref[h0 + which])
                s = jnp.where(valid, _dot_nt(q2, k_op), MASK_VALUE)
                m = jnp.maximum(jnp.max(jnp.maximum(s[:, :blk], s[:, blk:]), axis=-1, keepdims=True), sink)
                probs.append(jnp.exp(s - m).astype(BF16))
                esinks.append(jnp.exp(sink - m))
            out = _dot(jnp.concatenate(probs, axis=1), jnp.concatenate([r_lo, r_hi], axis=0))
            denom = out[:, LANES:] + jnp.where(lane_lo, esinks[0], esinks[1])
            att2 = out[:, :LANES] / denom
            shared["att"][jb][2 * gk] = att2[:blk]
            shared["att"][jb][2 * gk + 1] = att2[blk:]
        return run

    def hgrn(h):
        def run():
            lbl = lbl_ref[...]
            lexp = jnp.exp(lbl - jnp.max(lbl, axis=0, keepdims=True))
            lb = lexp[0:1, :] / jnp.sum(lexp, axis=0, keepdims=True)
            tri = tri_ref[...]
            row_ids = lax.broadcasted_iota(jnp.int32, (ts, HG_DIM), 0)
            ones_sq = jnp.ones((HG_DIM, HG_DIM), BF16)
            cols = slice(h * HG_DIM, (h + 1) * HG_DIM)
            lb_h = lb[:, cols]
            f = 0.5 * (1.0 + lb_h) + (0.5 * (1.0 - lb_h)) * jnp.tanh(
                0.5 * r_s[:, HG_WIDTH + h * HG_DIM:HG_WIDTH + (h + 1) * HG_DIM])
            kk = 1.0 - f
            qf = _silu(r_s[:, cols])
            vv = v_s[:, cols]
            g = jnp.log(f) * LOG2_E
            g_hi = g.astype(BF16)
            g_lo = (g - g_hi.astype(F32)).astype(BF16)
            b2 = _dot(tri, jnp.concatenate([g_hi, g_lo], axis=1))
            b = b2[:, :HG_DIM] + b2[:, HG_DIM:]
            w_incl = jnp.exp2(b)
            w_tail = jnp.exp2(b[ts - 1:ts, :] - b)
            st = state_ref[h]
            o = _dot_nt((qf * w_incl).astype(BF16), st.astype(BF16))
            n_tiles = half // SUBLANES
            quad = [[None] * n_tiles, [None] * n_tiles]
            s10 = None
            for li, m in enumerate(HG_LEVELS):
                x_f = _select_rows(m, qf, kk, row_ids) * _level_decay(m, b, f, row_ids)
                x_l = x_f.astype(BF16)
                if 2 * m == ts:
                    s10 = _dot_nt(x_l[half:], x_l[:half])
                    continue
                if m >= SUBLANES:
                    tiles = [t for t in range(n_tiles) if (t * SUBLANES // m) % 2 == 1]
                else:
                    tiles = list(range(n_tiles))
                for hf in range(2):
                    r0 = hf * half
                    if len(tiles) == n_tiles:
                        lhs = x_l[r0:r0 + half]
                    else:
                        lhs = jnp.concatenate([x_f[r0 + t * SUBLANES:r0 + (t + 1) * SUBLANES] for t in tiles],
                                              axis=0).astype(BF16)
                    sc = _dot_nt(lhs, x_l[r0:r0 + half])
                    for i, t in enumerate(tiles):
                        m0 = (li - 1) * LANES + t * SUBLANES
                        part = sc[i * SUBLANES:(i + 1) * SUBLANES] * lmask_ref[m0:m0 + SUBLANES, :]
                        quad[hf][t] = part if quad[hf][t] is None else quad[hf][t] + part
            s00 = jnp.concatenate(quad[0], axis=0)
            s11 = jnp.concatenate(quad[1], axis=0)
            o_top = _dot(s00.astype(BF16), vv[:half])
            o_bot = _dot(jnp.concatenate([s10, s11], axis=1).astype(BF16), vv)
            diag = _dot((qf * kk).astype(BF16), ones_sq)
            o = o + jnp.concatenate([o_top, o_bot], axis=0) + diag * vv.astype(F32)
            state_ref[h] = st * w_incl[ts - 1:ts, :] + _dot_tn(vv, (kk * w_tail).astype(BF16))
            shared["rec"][h] = o * _rms_scale(o)
        return run

    def tail():
        att = jnp.concatenate([jnp.concatenate(cols_, axis=1) for cols_ in shared["att"]], axis=0)
        rec_n = (jnp.concatenate(shared["rec"], axis=1) * ghg_ref[...]
                 * _silu(r_s[:, 2 * HG_WIDTH:3 * HG_WIDTH])).astype(BF16)
        att_n = (att * _rms_scale(att) * gatt_ref[...]).astype(BF16)
        o_ref[rows, :] = x_ref[rows, :] + _dot(jnp.concatenate([att_n, rec_n], axis=1), wout_ref[...])

    phases = [setup]
    phases += [attention(jb, gk) for jb in range(ts // blk) for gk in range(ATT_KV_HEADS)]
    phases += [hgrn(h) for h in range(HG_HEADS)]
    return phases + [tail]


def _interleave(main, side):
    side = list(side)
    for k, phase in enumerate(main):
        phase()
        if k >= 1 and side:
            side.pop(0)()
    for piece in side:
        piece()


def _mixer_kernel(sinks_ref, xc_ref, xn_ref, cosc_ref, sinc_ref, cosn_ref, sinn_ref, gmix_ref, win_ref,
                  wout_ref, gatt_ref, lbl_ref, ghg_ref, tri_ref, lmask_ref, o_ref,
                  state_ref, kvprev_ref, *slot_refs):
    j = pl.program_id(0)
    ts = MIX_TILE
    steps_per_seq = SEQ_LEN // (2 * ts)
    slot_a, slot_b = slot_refs[:4], slot_refs[4:]
    consts = (sinks_ref, wout_ref, gatt_ref, lbl_ref, ghg_ref, tri_ref, lmask_ref, state_ref, kvprev_ref)
    lo_rows, hi_rows, all_rows = slice(0, ts), slice(ts, 2 * ts), slice(None)

    @pl.when(j == 0)
    def _():
        kvprev_ref[...] = jnp.zeros_like(kvprev_ref)
        for piece in _project_pieces(xc_ref, cosc_ref, sinc_ref, lo_rows, gmix_ref, win_ref, *slot_a):
            piece()

    seq_start = lax.rem(j, steps_per_seq) == 0

    @pl.when(seq_start)
    def _():
        state_ref[...] = jnp.zeros_like(state_ref)

    first_lo = jnp.where(seq_start, ATT_BLOCK - 1, -1)
    _interleave(_mix_phases(first_lo, xc_ref, o_ref, lo_rows, *consts, *slot_a),
                _project_pieces(xc_ref, cosc_ref, sinc_ref, hi_rows, gmix_ref, win_ref, *slot_b))
    _interleave(_mix_phases(None, xc_ref, o_ref, hi_rows, *consts, *slot_b),
                _project_pieces(xn_ref, cosn_ref, sinn_ref, all_rows, gmix_ref, win_ref, *slot_a))


def _mixer(x2d, gmix, w_in, cos_t, sin_t, w_out, sinks, gatt, lbl, ghg):
    n = x2d.shape[0]
    ts = MIX_TILE
    n_tiles = n // ts
    tri, lmask = _hgrn_constants()
    cur = lambda j: (j, 0)
    nxt = lambda j: (jnp.minimum(2 * j + 2, n_tiles - 1), 0)
    fixed = lambda j: (0, 0)
    once = dict(pipeline_mode=pl.Buffered(1))
    return pl.pallas_call(
        _mixer_kernel,
        grid=(n_tiles // 2,),
        in_specs=[
            pl.BlockSpec(memory_space=pltpu.SMEM),
            pl.BlockSpec((2 * ts, D_MODEL), cur),
            pl.BlockSpec((ts, D_MODEL), nxt),
            pl.BlockSpec((2 * ts, LANES), cur),
            pl.BlockSpec((2 * ts, LANES), cur),
            pl.BlockSpec((ts, LANES), nxt),
            pl.BlockSpec((ts, LANES), nxt),
            pl.BlockSpec((1, D_MODEL), fixed),
            pl.BlockSpec(w_in.shape, fixed, **once),
            pl.BlockSpec((D_MODEL, D_MODEL), fixed, **once),
            pl.BlockSpec((1, ATT_WIDTH), fixed),
            pl.BlockSpec(lbl.shape, fixed),
            pl.BlockSpec((1, HG_WIDTH), fixed),
            pl.BlockSpec(tri.shape, fixed),
            pl.BlockSpec(lmask.shape, fixed),
        ],
        out_specs=pl.BlockSpec((2 * ts, D_MODEL), cur),
        out_shape=jax.ShapeDtypeStruct((n, D_MODEL), F32),
        scratch_shapes=[
            pltpu.VMEM((HG_HEADS, HG_DIM, HG_DIM), F32),
            pltpu.VMEM((ATT_BLOCK, KV_PAD_WIDTH), BF16),
        ] + 2 * [
            pltpu.VMEM((ts, ATT_WIDTH), BF16),
            pltpu.VMEM((ts, KV_PAD_WIDTH), BF16),
            pltpu.VMEM((ts, 3 * HG_WIDTH), F32),
            pltpu.VMEM((ts, HG_WIDTH), BF16),
        ],
        compiler_params=pltpu.CompilerParams(
            dimension_semantics=("arbitrary",), vmem_limit_bytes=VMEM_LIMIT_BYTES),
        name="mixer",
    )(sinks, x2d, x2d, cos_t, sin_t, cos_t, sin_t, gmix, w_in, w_out, gatt, lbl, ghg, tri, lmask)


def _mem_kv_kernel(mem_ref, gain_ref, w_ref, k_ref, v_ref):
    m = mem_ref[...]
    mn = (m * _rms_scale(m) * gain_ref[...]).astype(BF16)
    kv = _dot(mn, w_ref[...])
    k_ref[...] = kv[:, :D_MODEL].astype(BF16)
    v_ref[...] = kv[:, D_MODEL:].astype(BF16)


def _mem_kv(mem, gain, w_xkv):
    b, m, _ = mem.shape
    blk = lambda bi: (bi, 0, 0)
    fixed = lambda bi: (0, 0)
    return pl.pallas_call(
        _mem_kv_kernel,
        grid=(b,),
        in_specs=[
            pl.BlockSpec((None, m, D_MODEL), blk),
            pl.BlockSpec((1, D_MODEL), fixed),
            pl.BlockSpec((D_MODEL, 2 * D_MODEL), fixed),
        ],
        out_specs=[pl.BlockSpec((None, m, D_MODEL), blk), pl.BlockSpec((None, m, D_MODEL), blk)],
        out_shape=[jax.ShapeDtypeStruct((b, m, D_MODEL), BF16)] * 2,
        compiler_params=pltpu.CompilerParams(
            dimension_semantics=("arbitrary",), vmem_limit_bytes=VMEM_LIMIT_BYTES),
        name="mem_kv",
    )(mem, gain, w_xkv)


def _xattn_kernel(x_ref, gain_ref, wq_ref, k_ref, v_ref, wo_ref, o_ref):
    x = x_ref[...]
    hq = (x * _rms_scale(x) * gain_ref[...]).astype(BF16)
    q = _dot(hq, wq_ref[...]).astype(BF16)
    outs = []
    for h in range(X_HEADS):
        cols = slice(h * X_HEAD_DIM, (h + 1) * X_HEAD_DIM)
        s = _dot_nt(q[:, cols], k_ref[:, cols]) * (X_HEAD_DIM ** -0.5)
        p = jnp.exp(s - jnp.max(s, axis=-1, keepdims=True))
        o = _dot(p.astype(BF16), v_ref[:, cols])
        outs.append((o / jnp.sum(p, axis=-1, keepdims=True)).astype(BF16))
    xo = jnp.concatenate(outs, axis=1)
    o_ref[...] = x + _dot(xo, wo_ref[...])


def _xattn(x, gain, w_xq, xk, xv, w_xo, tm):
    b, s, _ = x.shape
    m = xk.shape[1]
    cur = lambda bi, i: (bi, i, 0)
    mem = lambda bi, i: (bi, 0, 0)
    fixed = lambda bi, i: (0, 0)
    return pl.pallas_call(
        _xattn_kernel,
        grid=(b, s // tm),
        in_specs=[
            pl.BlockSpec((None, tm, D_MODEL), cur),
            pl.BlockSpec((1, D_MODEL), fixed),
            pl.BlockSpec((D_MODEL, D_MODEL), fixed),
            pl.BlockSpec((None, m, D_MODEL), mem),
            pl.BlockSpec((None, m, D_MODEL), mem),
            pl.BlockSpec((D_MODEL, D_MODEL), fixed),
        ],
        out_specs=pl.BlockSpec((None, tm, D_MODEL), cur),
        out_shape=jax.ShapeDtypeStruct((b, s, D_MODEL), F32),
        compiler_params=pltpu.CompilerParams(
            dimension_semantics=("arbitrary", "arbitrary"), vmem_limit_bytes=VMEM_LIMIT_BYTES),
        name="xattn",
    )(x, gain, w_xq, xk, xv, w_xo)


FFN_CHUNKS = ((0, 1024), (1024, 2048), (2048, FFN_HIDDEN))


def _ffn_kernel(x_ref, gain_ref, wgu_ref, wd_ref, gfin_ref, o_ref):
    x = x_ref[...]
    hf = (x * _rms_scale(x) * gain_ref[...]).astype(BF16)
    acc = x
    for lo, hi in FFN_CHUNKS:
        gate = _dot(hf, wgu_ref[:, lo:hi])
        upv = _dot(hf, wgu_ref[:, FFN_HIDDEN + lo:FFN_HIDDEN + hi])
        act = (_silu(gate) * upv).astype(BF16)
        acc = acc + _dot(act, wd_ref[lo:hi, :])
    o_ref[...] = acc * _rms_scale(acc) * gfin_ref[...]


def _ffn(x2d, gain, w_gu, w_d, gfin, tm):
    n = x2d.shape[0]
    row = lambda i: (i, 0)
    fixed = lambda i: (0, 0)
    return pl.pallas_call(
        _ffn_kernel,
        grid=(n // tm,),
        in_specs=[
            pl.BlockSpec((tm, D_MODEL), row),
            pl.BlockSpec((1, D_MODEL), fixed),
            pl.BlockSpec((D_MODEL, 2 * FFN_HIDDEN), fixed, pipeline_mode=pl.Buffered(1)),
            pl.BlockSpec((FFN_HIDDEN, D_MODEL), fixed, pipeline_mode=pl.Buffered(1)),
            pl.BlockSpec((1, D_MODEL), fixed),
        ],
        out_specs=pl.BlockSpec((tm, D_MODEL), row),
        out_shape=jax.ShapeDtypeStruct((n, D_MODEL), F32),
        compiler_params=pltpu.CompilerParams(
            dimension_semantics=("arbitrary",), vmem_limit_bytes=VMEM_LIMIT_BYTES),
        name="ffn",
    )(x2d, gain, w_gu, w_d, gfin)


def _rotary_tables(positions):
    inv_freq = jnp.power(jnp.float32(ROPE_THETA),
                         -jnp.arange(ROPE_HALF, dtype=F32) * (2.0 / ROPE_DIM))
    ang = positions.astype(F32)[..., None] * inv_freq
    cos = jnp.cos(ang)
    sin = jnp.sin(ang)
    rest = ATT_HEAD_DIM - ROPE_DIM
    ones = jnp.ones(cos.shape[:-1] + (rest,), F32)
    zeros = jnp.zeros(cos.shape[:-1] + (rest,), F32)
    cos_h = jnp.concatenate([cos, cos, ones], axis=-1)
    sin_h = jnp.concatenate([-sin, sin, zeros], axis=-1)
    reps = LANES // ATT_HEAD_DIM
    cos_t = jnp.tile(cos_h, (1, 1, reps))
    sin_t = jnp.tile(sin_h, (1, 1, reps))
    return cos_t, sin_t


def kernel(x, mem, positions, norm_mix, w_in, att_sinks, att_out_gain, hg_lb_logits, hg_out_gain,
           w_out, norm_xattn, norm_mem, w_xq, w_xkv, w_xo, norm_ffn, w_gate_up, w_down, norm_final):
    b, s, d = x.shape
    n = b * s
    assert w_in.shape[0] == 1 and hg_lb_logits.shape[0] == 2, "single-layer block only"
    row = lambda v: v.reshape(1, -1).astype(F32)
    cos_t, sin_t = _rotary_tables(positions)
    assert s == SEQ_LEN and SEQ_LEN % (2 * MIX_TILE) == 0
    x = _mixer(x.reshape(n, d), row(norm_mix[0]), w_in[0].astype(BF16),
               cos_t.reshape(n, LANES), sin_t.reshape(n, LANES),
               w_out[0].astype(BF16), att_sinks[0].astype(F32), row(att_out_gain[0]),
               hg_lb_logits.astype(F32), row(hg_out_gain[0])).reshape(b, s, d)
    xk, xv = _mem_kv(mem, row(norm_mem[0]), w_xkv[0].astype(BF16))
    x = _xattn(x, row(norm_xattn[0]), w_xq[0].astype(BF16), xk, xv, w_xo[0].astype(BF16), tm=512)
    y = _ffn(x.reshape(n, d), row(norm_ffn[0]), w_gate_up[0].astype(BF16), w_down[0].astype(BF16),
             row(norm_final), tm=512)
    return y.reshape(b, s, d)
```

```python
import numpy as np
import jax
import jax.numpy as jnp
from jax import lax
from jax.experimental import pallas as pl
from jax.experimental.pallas import tpu as pltpu

F32 = jnp.float32
BF16 = jnp.bfloat16

D_MODEL = 1024
MEM_LEN = 256
ATT_HEADS = 8
ATT_KV_HEADS = 2
ATT_GROUP = ATT_HEADS // ATT_KV_HEADS
ATT_HEAD_DIM = 64
ATT_WIDTH = ATT_HEADS * ATT_HEAD_DIM
ATT_KV_WIDTH = ATT_KV_HEADS * ATT_HEAD_DIM
ATT_BLOCK = 128
ROPE_THETA = 500000.0
ROPE_DIM = ATT_HEAD_DIM // 4
ROPE_HALF = ROPE_DIM // 2
HG_HEADS = 4
HG_DIM = 128
HG_WIDTH = HG_HEADS * HG_DIM
X_HEADS = 4
X_HEAD_DIM = D_MODEL // X_HEADS
FFN_HIDDEN = 2816
RMS_EPS = 1e-6
LANES = 128
SUBLANES = 8
MASK_VALUE = -1e30
LOG2_E = 1.4426950408889634

MIX_TILE = 256
HG_LEVELS = (128, 64, 32, 16, 8, 4, 2, 1)
KV_PAD_WIDTH = 2 * ATT_KV_HEADS * 2 * LANES

VMEM_LIMIT_BYTES = 56 * 1024 * 1024


def _rms_scale(x):
    return lax.rsqrt(jnp.mean(x * x, axis=-1, keepdims=True) + RMS_EPS)


def _silu(x):
    h = 0.5 * x
    return h + h * jnp.tanh(h)


def _dot(a, b):
    return jnp.dot(a, b, preferred_element_type=F32)


def _dot_nt(a, b):
    return lax.dot_general(a, b, (((1,), (1,)), ((), ())), preferred_element_type=F32)


def _dot_tn(a, b):
    return lax.dot_general(a, b, (((0,), (0,)), ((), ())), preferred_element_type=F32)


def _hgrn_constants():
    r = np.arange(MIX_TILE)[:, None]
    j = np.arange(MIX_TILE)[None, :]
    tri = (j <= r).astype(np.float32)
    rr = np.arange(LANES)[:, None]
    jj = np.arange(LANES)[None, :]
    masks = []
    for m in HG_LEVELS:
        if 2 * m <= LANES:
            masks.append(((rr // (2 * m)) == (jj // (2 * m))) & ((rr % (2 * m)) >= m) & ((jj % (2 * m)) < m))
    lmask = np.concatenate(masks, axis=0).astype(np.float32)
    return jnp.asarray(tri, BF16), jnp.asarray(lmask, F32)


def _rotary_constants():
    inv_freq = jnp.power(jnp.float32(ROPE_THETA),
                         -jnp.arange(ROPE_HALF, dtype=F32) * (2.0 / ROPE_DIM))
    invf = jnp.broadcast_to(inv_freq[:, None], (ROPE_HALF, MIX_TILE))
    lane = np.arange(LANES)
    in_head = lane % ATT_HEAD_DIM
    freq_row = (np.arange(ROPE_HALF)[:, None] == (in_head % ROPE_HALF)[None, :])
    rot = freq_row & (in_head < ROPE_DIM)[None, :]
    ecos = np.zeros((4 * ROPE_HALF, LANES), np.float32)
    ecos[:ROPE_HALF] = rot
    ecos[ROPE_HALF:2 * ROPE_HALF] = rot
    ecos[2 * ROPE_HALF] = in_head >= ROPE_DIM
    sign = np.where(in_head < ROPE_HALF, -1.0, 1.0)[None, :]
    esin = np.concatenate([rot * sign, rot * sign], axis=0).astype(np.float32)
    return invf, jnp.asarray(ecos, BF16), jnp.asarray(esin, BF16)


def _level_decay(m, b, f, row_ids):
    ts = b.shape[0]
    if m >= SUBLANES // 2:
        b3 = b.reshape(ts // (2 * m), 2 * m, HG_DIM)
        ref = b3[:, m - 1:m, :]
        if m >= SUBLANES:
            e3 = jnp.concatenate([ref - b3[:, :m, :], b3[:, m:, :] - ref], axis=1)
        else:
            e3 = -jnp.abs(b3 - ref)
        return jnp.exp2(e3).reshape(ts, HG_DIM)
    if m == 2:
        pos = row_ids & 3
        f_next = pltpu.roll(f, ts - 1, axis=0)
        f_prev = pltpu.roll(f, 1, axis=0)
        return jnp.where(pos == 0, f_next, jnp.where(pos == 1, 1.0, jnp.where(pos == 2, f, f * f_prev)))
    assert m == 1
    return jnp.where((row_ids & 1) != 0, f, 1.0)


def _select_rows(m, upper_src, lower_src, row_ids):
    c = upper_src.shape[0]
    if m >= SUBLANES:
        pieces = []
        for b0 in range(0, c, 2 * m):
            pieces.append(lower_src[b0:b0 + m])
            pieces.append(upper_src[b0 + m:b0 + 2 * m])
        return jnp.concatenate(pieces, axis=0)
    return jnp.where((row_ids & m) != 0, upper_src, lower_src)


def _rotary_tables(pos_ref, invf_ref, ecos_ref, esin_ref):
    ang = invf_ref[...] * pos_ref[...].astype(F32)

    def spread(t, e_ref, extra):
        t_hi = t.astype(BF16).astype(F32)
        rows = jnp.concatenate([t_hi, t - t_hi] + extra, axis=0).astype(BF16)
        return _dot_tn(rows, e_ref[...])

    ones = jnp.ones_like(ang)
    cos_t = spread(jnp.cos(ang), ecos_ref, [ones, jnp.zeros_like(ang)])
    sin_t = spread(jnp.sin(ang), esin_ref, [])
    return cos_t, sin_t


def _mixer_kernel(sinks_ref, x_ref, gmix_ref, win_ref, pos_ref, invf_ref, ecos_ref, esin_ref, wout_ref,
                  gatt_ref, lbl_ref, ghg_ref, tri_ref, lmask_ref, o_ref, state_ref, kvprev_ref):
    step = pl.program_id(1)
    ts = MIX_TILE
    blk = ATT_BLOCK

    @pl.when(step == 0)
    def _():
        state_ref[...] = jnp.zeros_like(state_ref)
        kvprev_ref[...] = jnp.zeros_like(kvprev_ref)

    x = x_ref[...]
    hn = (x * _rms_scale(x) * gmix_ref[...]).astype(BF16)

    def proj(c0, c1):
        return _dot(hn, win_ref[:, c0:c1])

    cos, sin = _rotary_tables(pos_ref, invf_ref, ecos_ref, esin_ref)
    lane = lax.broadcasted_iota(jnp.int32, cos.shape, 1)
    first_half = (lane & (ATT_HEAD_DIM - 1)) < ROPE_HALF
    lo_head = lane < ATT_HEAD_DIM

    def rotate(col):
        partner = jnp.where(first_half,
                            pltpu.roll(col, LANES - ROPE_HALF, axis=1),
                            pltpu.roll(col, ROPE_HALF, axis=1))
        return col * cos + partner * sin

    def padded(col):
        swapped = pltpu.roll(col, ATT_HEAD_DIM, axis=1)
        zero = jnp.zeros_like(col)
        return [v.astype(BF16) for v in (jnp.where(lo_head, col, zero), jnp.where(lo_head, zero, swapped),
                                         jnp.where(lo_head, swapped, zero), jnp.where(lo_head, zero, col))]

    pq = proj(0, ATT_WIDTH)
    q_cols = [(rotate(pq[:, j * LANES:(j + 1) * LANES]) * (ATT_HEAD_DIM ** -0.5)).astype(BF16)
              for j in range(ATT_WIDTH // LANES)]
    pkv = proj(ATT_WIDTH, ATT_WIDTH + 2 * ATT_KV_WIDTH)
    kv_new = padded(rotate(pkv[:, :LANES])) + padded(pkv[:, LANES:])
    kv_all = [jnp.concatenate([kvprev_ref[:, i * LANES:(i + 1) * LANES], v], axis=0)
              for i, v in enumerate(kv_new)]
    for i, v in enumerate(kv_new):
        kvprev_ref[:, i * LANES:(i + 1) * LANES] = v[ts - blk:]
    r0c = ATT_WIDTH + 2 * ATT_KV_WIDTH
    p_q, p_f, p_i, p_g = (proj(r0c + k * HG_WIDTH, r0c + (k + 1) * HG_WIDTH) for k in range(4))

    qrow = lax.broadcasted_iota(jnp.int32, (2 * blk, 2 * blk), 0) & (blk - 1)
    kcol = lax.broadcasted_iota(jnp.int32, (2 * blk, 2 * blk), 1)
    upper_half_rows = lax.broadcasted_iota(jnp.int32, (2 * blk, 1), 0) >= blk
    lane_lo = lax.broadcasted_iota(jnp.int32, (2 * blk, LANES), 1) < ATT_HEAD_DIM
    ones_lo = jnp.where(lane_lo, 1.0, 0.0).astype(BF16)
    ones_hi = jnp.where(lane_lo, 0.0, 1.0).astype(BF16)
    first_lo = jnp.where(step > 0, -1, blk - 1)
    att_rows = []
    for jb in range(ts // blk):
        keys = slice(jb * blk, (jb + 2) * blk)
        lo_bound = jnp.maximum(qrow, first_lo) if jb == 0 else qrow
        valid = (kcol > lo_bound) & (kcol <= qrow + blk)
        att_cols = []
        for gk in range(ATT_KV_HEADS):
            q2 = jnp.concatenate([q_cols[2 * gk][jb * blk:(jb + 1) * blk],
                                  q_cols[2 * gk + 1][jb * blk:(jb + 1) * blk]], axis=0)
            k_lo = kv_all[2 * gk][keys]
            k_hi = kv_all[2 * gk + 1][keys]
            r_lo = jnp.concatenate([kv_all[4 + 2 * gk][keys], ones_lo], axis=1)
            r_hi = jnp.concatenate([kv_all[5 + 2 * gk][keys], ones_hi], axis=1)
            h0 = ATT_GROUP * gk
            probs, esinks = [], []
            for which, k_op in ((0, k_lo), (1, k_hi)):
                sink = jnp.where(upper_half_rows, sinks_ref[h0 + 2 + which], sinks_ref[h0 + which])
                s = jnp.where(valid, _dot_nt(q2, k_op), MASK_VALUE)
                m = jnp.maximum(jnp.max(jnp.maximum(s[:, :blk], s[:, blk:]), axis=-1, keepdims=True), sink)
                probs.append(jnp.exp(s - m).astype(BF16))
                esinks.append(jnp.exp(sink - m))
            out = _dot(jnp.concatenate(probs, axis=1), jnp.concatenate([r_lo, r_hi], axis=0))
            denom = out[:, LANES:] + jnp.where(lane_lo, esinks[0], esinks[1])
            att2 = out[:, :LANES] / denom
            att_cols += [att2[:blk], att2[blk:]]
        att_rows.append(jnp.concatenate(att_cols, axis=1))
    att = jnp.concatenate(att_rows, axis=0)

    lbl = lbl_ref[...]
    lexp = jnp.exp(lbl - jnp.max(lbl, axis=0, keepdims=True))
    lb = lexp[0:1, :] / jnp.sum(lexp, axis=0, keepdims=True)
    tri = tri_ref[...]
    row_ids = lax.broadcasted_iota(jnp.int32, (ts, HG_DIM), 0)
    half = ts // 2
    rec_cols = []
    for h in range(HG_HEADS):
        cols = slice(h * HG_DIM, (h + 1) * HG_DIM)
        lb_h = lb[:, cols]
        f = 0.5 * (1.0 + lb_h) + (0.5 * (1.0 - lb_h)) * jnp.tanh(0.5 * p_f[:, cols])
        kk = 1.0 - f
        qf = _silu(p_q[:, cols])
        vv = p_i[:, cols].astype(BF16)
        g = jnp.log(f) * LOG2_E
        g_hi = g.astype(BF16)
        g_lo = (g - g_hi.astype(F32)).astype(BF16)
        b2 = _dot(tri, jnp.concatenate([g_hi, g_lo], axis=1))
        b = b2[:, :HG_DIM] + b2[:, HG_DIM:]
        w_incl = jnp.exp2(b)
        w_tail = jnp.exp2(b[ts - 1:ts, :] - b)
        st = state_ref[h]
        o = _dot_nt((qf * w_incl).astype(BF16), st.astype(BF16))
        n_tiles = half // SUBLANES
        quad = [[None] * n_tiles, [None] * n_tiles]
        s10 = None
        for li, m in enumerate(HG_LEVELS):
            x_f = _select_rows(m, qf, kk, row_ids) * _level_decay(m, b, f, row_ids)
            x_l = x_f.astype(BF16)
            if 2 * m == ts:
                s10 = _dot_nt(x_l[half:], x_l[:half])
                continue
            if m >= SUBLANES:
                tiles = [t for t in range(n_tiles) if (t * SUBLANES // m) % 2 == 1]
            else:
                tiles = list(range(n_tiles))
            for hf in range(2):
                r0 = hf * half
                if len(tiles) == n_tiles:
                    lhs = x_l[r0:r0 + half]
                else:
                    lhs = jnp.concatenate([x_f[r0 + t * SUBLANES:r0 + (t + 1) * SUBLANES] for t in tiles],
                                          axis=0).astype(BF16)
                sc = _dot_nt(lhs, x_l[r0:r0 + half])
                for i, t in enumerate(tiles):
                    m0 = (li - 1) * LANES + t * SUBLANES
                    part = sc[i * SUBLANES:(i + 1) * SUBLANES] * lmask_ref[m0:m0 + SUBLANES, :]
                    quad[hf][t] = part if quad[hf][t] is None else quad[hf][t] + part
        s00 = jnp.concatenate(quad[0], axis=0)
        s11 = jnp.concatenate(quad[1], axis=0)
        o_top = _dot(s00.astype(BF16), vv[:half])
        o_bot = _dot(jnp.concatenate([s10, s11], axis=1).astype(BF16), vv)
        diag = jnp.sum(qf * kk, axis=-1, keepdims=True)
        o = o + jnp.concatenate([o_top, o_bot], axis=0) + diag * p_i[:, cols]
        state_ref[h] = st * w_incl[ts - 1:ts, :] + _dot_tn(vv, (kk * w_tail).astype(BF16))
        rec_cols.append(o * _rms_scale(o))
    rec_n = (jnp.concatenate(rec_cols, axis=1) * ghg_ref[...] * _silu(p_g)).astype(BF16)

    att_n = (att * _rms_scale(att) * gatt_ref[...]).astype(BF16)
    o_ref[...] = x + _dot(jnp.concatenate([att_n, rec_n], axis=1), wout_ref[...])


def _mixer(x, gmix, w_in, positions, w_out, sinks, gatt, lbl, ghg):
    b, s, _ = x.shape
    ts = MIX_TILE
    tri, lmask = _hgrn_constants()
    invf, ecos, esin = _rotary_constants()
    pos = positions.reshape(b * (s // ts), 1, ts)
    cur = lambda bi, i: (bi, i, 0)
    fixed = lambda bi, i: (0, 0)
    return pl.pallas_call(
        _mixer_kernel,
        grid=(b, s // ts),
        in_specs=[
            pl.BlockSpec(memory_space=pltpu.SMEM),
            pl.BlockSpec((None, ts, D_MODEL), cur),
            pl.BlockSpec((1, D_MODEL), fixed),
            pl.BlockSpec(w_in.shape, fixed, pipeline_mode=pl.Buffered(1)),
            pl.BlockSpec((None, 1, ts), lambda bi, i: (bi * (s // ts) + i, 0, 0)),
            pl.BlockSpec(invf.shape, fixed),
            pl.BlockSpec(ecos.shape, fixed),
            pl.BlockSpec(esin.shape, fixed),
            pl.BlockSpec((D_MODEL, D_MODEL), fixed, pipeline_mode=pl.Buffered(1)),
            pl.BlockSpec((1, ATT_WIDTH), fixed),
            pl.BlockSpec(lbl.shape, fixed),
            pl.BlockSpec((1, HG_WIDTH), fixed),
            pl.BlockSpec(tri.shape, fixed),
            pl.BlockSpec(lmask.shape, fixed),
        ],
        out_specs=pl.BlockSpec((None, ts, D_MODEL), cur),
        out_shape=jax.ShapeDtypeStruct((b, s, D_MODEL), F32),
        scratch_shapes=[
            pltpu.VMEM((HG_HEADS, HG_DIM, HG_DIM), F32),
            pltpu.VMEM((ATT_BLOCK, KV_PAD_WIDTH), BF16),
        ],
        compiler_params=pltpu.CompilerParams(
            dimension_semantics=("arbitrary", "arbitrary"), vmem_limit_bytes=VMEM_LIMIT_BYTES),
        name="mixer",
    )(sinks, x, gmix, w_in, pos, invf, ecos, esin, w_out, gatt, lbl, ghg, tri, lmask)


def _mem_kv_kernel(mem_ref, gain_ref, w_ref, k_ref, v_ref):
    m = mem_ref[...]
    mn = (m * _rms_scale(m) * gain_ref[...]).astype(BF16)
    kv = _dot(mn, w_ref[...])
    k_ref[...] = kv[:, :D_MODEL].astype(BF16)
    v_ref[...] = kv[:, D_MODEL:].astype(BF16)


def _mem_kv(mem, gain, w_xkv):
    b, m, _ = mem.shape
    blk = lambda bi: (bi, 0, 0)
    fixed = lambda bi: (0, 0)
    return pl.pallas_call(
        _mem_kv_kernel,
        grid=(b,),
        in_specs=[
            pl.BlockSpec((None, m, D_MODEL), blk),
            pl.BlockSpec((1, D_MODEL), fixed),
            pl.BlockSpec((D_MODEL, 2 * D_MODEL), fixed),
        ],
        out_specs=[pl.BlockSpec((None, m, D_MODEL), blk), pl.BlockSpec((None, m, D_MODEL), blk)],
        out_shape=[jax.ShapeDtypeStruct((b, m, D_MODEL), BF16)] * 2,
        compiler_params=pltpu.CompilerParams(
            dimension_semantics=("arbitrary",), vmem_limit_bytes=VMEM_LIMIT_BYTES),
        name="mem_kv",
    )(mem, gain, w_xkv)


def _xattn_kernel(x_ref, gain_ref, wq_ref, k_ref, v_ref, wo_ref, o_ref):
    x = x_ref[...]
    hq = (x * _rms_scale(x) * gain_ref[...]).astype(BF16)
    q = _dot(hq, wq_ref[...]).astype(BF16)
    outs = []
    for h in range(X_HEADS):
        cols = slice(h * X_HEAD_DIM, (h + 1) * X_HEAD_DIM)
        s = _dot_nt(q[:, cols], k_ref[:, cols]) * (X_HEAD_DIM ** -0.5)
        p = jnp.exp(s - jnp.max(s, axis=-1, keepdims=True))
        o = _dot(p.astype(BF16), v_ref[:, cols])
        outs.append((o / jnp.sum(p, axis=-1, keepdims=True)).astype(BF16))
    xo = jnp.concatenate(outs, axis=1)
    o_ref[...] = x + _dot(xo, wo_ref[...])


def _xattn(x, gain, w_xq, xk, xv, w_xo, tm):
    b, s, _ = x.shape
    m = xk.shape[1]
    cur = lambda bi, i: (bi, i, 0)
    mem = lambda bi, i: (bi, 0, 0)
    fixed = lambda bi, i: (0, 0)
    return pl.pallas_call(
        _xattn_kernel,
        grid=(b, s // tm),
        in_specs=[
            pl.BlockSpec((None, tm, D_MODEL), cur),
            pl.BlockSpec((1, D_MODEL), fixed),
            pl.BlockSpec((D_MODEL, D_MODEL), fixed),
            pl.BlockSpec((None, m, D_MODEL), mem),
            pl.BlockSpec((None, m, D_MODEL), mem),
            pl.BlockSpec((D_MODEL, D_MODEL), fixed),
        ],
        out_specs=pl.BlockSpec((None, tm, D_MODEL), cur),
        out_shape=jax.ShapeDtypeStruct((b, s, D_MODEL), F32),
        compiler_params=pltpu.CompilerParams(
            dimension_semantics=("arbitrary", "arbitrary"), vmem_limit_bytes=VMEM_LIMIT_BYTES),
        name="xattn",
    )(x, gain, w_xq, xk, xv, w_xo)


FFN_CHUNKS = ((0, 1024), (1024, 2048), (2048, FFN_HIDDEN))


def _ffn_kernel(x_ref, gain_ref, wgu_ref, wd_ref, gfin_ref, o_ref):
    x = x_ref[...]
    hf = (x * _rms_scale(x) * gain_ref[...]).astype(BF16)
    acc = x
    for lo, hi in FFN_CHUNKS:
        gate = _dot(hf, wgu_ref[:, lo:hi])
        upv = _dot(hf, wgu_ref[:, FFN_HIDDEN + lo:FFN_HIDDEN + hi])
        act = (_silu(gate) * upv).astype(BF16)
        acc = acc + _dot(act, wd_ref[lo:hi, :])
    o_ref[...] = acc * _rms_scale(acc) * gfin_ref[...]


def _ffn(x2d, gain, w_gu, w_d, gfin, tm):
    n = x2d.shape[0]
    row = lambda i: (i, 0)
    fixed = lambda i: (0, 0)
    return pl.pallas_call(
        _ffn_kernel,
        grid=(n // tm,),
        in_specs=[
            pl.BlockSpec((tm, D_MODEL), row),
            pl.BlockSpec((1, D_MODEL), fixed),
            pl.BlockSpec((D_MODEL, 2 * FFN_HIDDEN), fixed, pipeline_mode=pl.Buffered(1)),
            pl.BlockSpec((FFN_HIDDEN, D_MODEL), fixed, pipeline_mode=pl.Buffered(1)),
            pl.BlockSpec((1, D_MODEL), fixed),
        ],
        out_specs=pl.BlockSpec((tm, D_MODEL), row),
        out_shape=jax.ShapeDtypeStruct((n, D_MODEL), F32),
        compiler_params=pltpu.CompilerParams(
            dimension_semantics=("arbitrary",), vmem_limit_bytes=VMEM_LIMIT_BYTES),
        name="ffn",
    )(x2d, gain, w_gu, w_d, gfin)


def kernel(x, mem, positions, norm_mix, w_in, att_sinks, att_out_gain, hg_lb_logits, hg_out_gain,
           w_out, norm_xattn, norm_mem, w_xq, w_xkv, w_xo, norm_ffn, w_gate_up, w_down, norm_final):
    b, s, d = x.shape
    n = b * s
    assert w_in.shape[0] == 1 and hg_lb_logits.shape[0] == 2, "single-layer block only"
    row = lambda v: v.reshape(1, -1).astype(F32)
    x = _mixer(x, row(norm_mix[0]), w_in[0].astype(BF16), positions,
               w_out[0].astype(BF16), att_sinks[0].astype(F32), row(att_out_gain[0]),
               hg_lb_logits.astype(F32), row(hg_out_gain[0]))
    xk, xv = _mem_kv(mem, row(norm_mem[0]), w_xkv[0].astype(BF16))
    x = _xattn(x, row(norm_xattn[0]), w_xq[0].astype(BF16), xk, xv, w_xo[0].astype(BF16), tm=512)
    y = _ffn(x.reshape(n, d), row(norm_ffn[0]), w_gate_up[0].astype(BF16), w_down[0].astype(BF16),
             row(norm_final), tm=512)
    return y.reshape(b, s, d)
```

```python
import numpy as np
import jax
import jax.numpy as jnp
from jax import lax
from jax.experimental import pallas as pl
from jax.experimental.pallas import tpu as pltpu

F32 = jnp.float32
BF16 = jnp.bfloat16

D_MODEL = 1024
MEM_LEN = 256
ATT_HEADS = 8
ATT_KV_HEADS = 2
ATT_GROUP = ATT_HEADS // ATT_KV_HEADS
ATT_HEAD_DIM = 64
ATT_WIDTH = ATT_HEADS * ATT_HEAD_DIM
ATT_KV_WIDTH = ATT_KV_HEADS * ATT_HEAD_DIM
ATT_BLOCK = 128
ROPE_THETA = 500000.0
ROPE_DIM = ATT_HEAD_DIM // 4
ROPE_HALF = ROPE_DIM // 2
HG_HEADS = 4
HG_DIM = 128
HG_WIDTH = HG_HEADS * HG_DIM
X_HEADS = 4
X_HEAD_DIM = D_MODEL // X_HEADS
FFN_HIDDEN = 2816
RMS_EPS = 1e-6
LANES = 128
SUBLANES = 8
MASK_VALUE = -1e30
LOG2_E = 1.4426950408889634

MIX_TILE = 256
HG_LEVELS = (128, 64, 32, 16, 8, 4, 2, 1)
KV_PAD_WIDTH = 2 * ATT_KV_HEADS * 2 * LANES

VMEM_LIMIT_BYTES = 56 * 1024 * 1024


def _rms_scale(x):
    return lax.rsqrt(jnp.mean(x * x, axis=-1, keepdims=True) + RMS_EPS)


def _silu(x):
    h = 0.5 * x
    return h + h * jnp.tanh(h)


def _dot(a, b):
    return jnp.dot(a, b, preferred_element_type=F32)


def _dot_nt(a, b):
    return lax.dot_general(a, b, (((1,), (1,)), ((), ())), preferred_element_type=F32)


def _dot_tn(a, b):
    return lax.dot_general(a, b, (((0,), (0,)), ((), ())), preferred_element_type=F32)


def _hgrn_constants():
    r = np.arange(MIX_TILE)[:, None]
    j = np.arange(MIX_TILE)[None, :]
    tri = (j <= r).astype(np.float32)
    rr = np.arange(LANES)[:, None]
    jj = np.arange(LANES)[None, :]
    masks = []
    for m in HG_LEVELS:
        if 2 * m <= LANES:
            masks.append(((rr // (2 * m)) == (jj // (2 * m))) & ((rr % (2 * m)) >= m) & ((jj % (2 * m)) < m))
    lmask = np.concatenate(masks, axis=0).astype(np.float32)
    return jnp.asarray(tri, BF16), jnp.asarray(lmask, F32)


def _rotary_constants():
    inv_freq = jnp.power(jnp.float32(ROPE_THETA),
                         -jnp.arange(ROPE_HALF, dtype=F32) * (2.0 / ROPE_DIM))
    invf = jnp.broadcast_to(inv_freq[:, None], (ROPE_HALF, MIX_TILE))
    lane = np.arange(LANES)
    in_head = lane % ATT_HEAD_DIM
    freq_row = (np.arange(ROPE_HALF)[:, None] == (in_head % ROPE_HALF)[None, :])
    rot = freq_row & (in_head < ROPE_DIM)[None, :]
    ecos = np.zeros((4 * ROPE_HALF, LANES), np.float32)
    ecos[:ROPE_HALF] = rot
    ecos[ROPE_HALF:2 * ROPE_HALF] = rot
    ecos[2 * ROPE_HALF] = in_head >= ROPE_DIM
    sign = np.where(in_head < ROPE_HALF, -1.0, 1.0)[None, :]
    esin = np.concatenate([rot * sign, rot * sign], axis=0).astype(np.float32)
    return invf, jnp.asarray(ecos, BF16), jnp.asarray(esin, BF16)


def _level_decay(m, b, f, row_ids):
    ts = b.shape[0]
    if m >= SUBLANES // 2:
        b3 = b.reshape(ts // (2 * m), 2 * m, HG_DIM)
        ref = b3[:, m - 1:m, :]
        if m >= SUBLANES:
            e3 = jnp.concatenate([ref - b3[:, :m, :], b3[:, m:, :] - ref], axis=1)
        else:
            e3 = -jnp.abs(b3 - ref)
        return jnp.exp2(e3).reshape(ts, HG_DIM)
    if m == 2:
        pos = row_ids & 3
        f_next = pltpu.roll(f, ts - 1, axis=0)
        f_prev = pltpu.roll(f, 1, axis=0)
        return jnp.where(pos == 0, f_next, jnp.where(pos == 1, 1.0, jnp.where(pos == 2, f, f * f_prev)))
    assert m == 1
    return jnp.where((row_ids & 1) != 0, f, 1.0)


def _select_rows(m, upper_src, lower_src, row_ids):
    c = upper_src.shape[0]
    if m >= SUBLANES:
        pieces = []
        for b0 in range(0, c, 2 * m):
            pieces.append(lower_src[b0:b0 + m])
            pieces.append(upper_src[b0 + m:b0 + 2 * m])
        return jnp.concatenate(pieces, axis=0)
    return jnp.where((row_ids & m) != 0, upper_src, lower_src)


def _rotary_tables(pos_ref, invf_ref, ecos_ref, esin_ref):
    ang = invf_ref[...] * pos_ref[...].astype(F32)

    def spread(t, e_ref, extra):
        t_hi = t.astype(BF16).astype(F32)
        rows = jnp.concatenate([t_hi, t - t_hi] + extra, axis=0).astype(BF16)
        return _dot_tn(rows, e_ref[...])

    ones = jnp.ones_like(ang)
    cos_t = spread(jnp.cos(ang), ecos_ref, [ones, jnp.zeros_like(ang)])
    sin_t = spread(jnp.sin(ang), esin_ref, [])
    return cos_t, sin_t


def _mixer_kernel(sinks_ref, x_ref, gmix_ref, win_ref, pos_ref, invf_ref, ecos_ref, esin_ref, wout_ref,
                  gatt_ref, lbl_ref, ghg_ref, tri_ref, lmask_ref, o_ref, state_ref, kvprev_ref):
    step = pl.program_id(1)
    ts = MIX_TILE
    blk = ATT_BLOCK

    @pl.when(step == 0)
    def _():
        state_ref[...] = jnp.zeros_like(state_ref)
        kvprev_ref[...] = jnp.zeros_like(kvprev_ref)

    x = x_ref[...]
    hn = (x * _rms_scale(x) * gmix_ref[...]).astype(BF16)

    def proj(c0, c1):
        return _dot(hn, win_ref[:, c0:c1])

    cos, sin = _rotary_tables(pos_ref, invf_ref, ecos_ref, esin_ref)
    lane = lax.broadcasted_iota(jnp.int32, cos.shape, 1)
    first_half = (lane & (ATT_HEAD_DIM - 1)) < ROPE_HALF
    lo_head = lane < ATT_HEAD_DIM

    def rotate(col):
        partner = jnp.where(first_half,
                            pltpu.roll(col, LANES - ROPE_HALF, axis=1),
                            pltpu.roll(col, ROPE_HALF, axis=1))
        return col * cos + partner * sin

    def padded(col):
        swapped = pltpu.roll(col, ATT_HEAD_DIM, axis=1)
        zero = jnp.zeros_like(col)
        return [v.astype(BF16) for v in (jnp.where(lo_head, col, zero), jnp.where(lo_head, zero, swapped),
                                         jnp.where(lo_head, swapped, zero), jnp.where(lo_head, zero, col))]

    pq = proj(0, ATT_WIDTH)
    q_cols = [(rotate(pq[:, j * LANES:(j + 1) * LANES]) * (ATT_HEAD_DIM ** -0.5)).astype(BF16)
              for j in range(ATT_WIDTH // LANES)]
    pkv = proj(ATT_WIDTH, ATT_WIDTH + 2 * ATT_KV_WIDTH)
    kv_new = padded(rotate(pkv[:, :LANES])) + padded(pkv[:, LANES:])
    kv_all = [jnp.concatenate([kvprev_ref[:, i * LANES:(i + 1) * LANES], v], axis=0)
              for i, v in enumerate(kv_new)]
    for i, v in enumerate(kv_new):
        kvprev_ref[:, i * LANES:(i + 1) * LANES] = v[ts - blk:]
    r0c = ATT_WIDTH + 2 * ATT_KV_WIDTH
    p_q, p_f, p_i, p_g = (proj(r0c + k * HG_WIDTH, r0c + (k + 1) * HG_WIDTH) for k in range(4))

    qrow = lax.broadcasted_iota(jnp.int32, (2 * blk, 2 * blk), 0) & (blk - 1)
    kcol = lax.broadcasted_iota(jnp.int32, (2 * blk, 2 * blk), 1)
    upper_half_rows = lax.broadcasted_iota(jnp.int32, (2 * blk, 1), 0) >= blk
    lane_lo = lax.broadcasted_iota(jnp.int32, (2 * blk, LANES), 1) < ATT_HEAD_DIM
    ones_lo = jnp.where(lane_lo, 1.0, 0.0).astype(BF16)
    ones_hi = jnp.where(lane_lo, 0.0, 1.0).astype(BF16)
    first_lo = jnp.where(step > 0, -1, blk - 1)
    att_rows = []
    for jb in range(ts // blk):
        keys = slice(jb * blk, (jb + 2) * blk)
        lo_bound = jnp.maximum(qrow, first_lo) if jb == 0 else qrow
        valid = (kcol > lo_bound) & (kcol <= qrow + blk)
        att_cols = []
        for gk in range(ATT_KV_HEADS):
            q2 = jnp.concatenate([q_cols[2 * gk][jb * blk:(jb + 1) * blk],
                                  q_cols[2 * gk + 1][jb * blk:(jb + 1) * blk]], axis=0)
            k_lo = kv_all[2 * gk][keys]
            k_hi = kv_all[2 * gk + 1][keys]
            r_lo = jnp.concatenate([kv_all[4 + 2 * gk][keys], ones_lo], axis=1)
            r_hi = jnp.concatenate([kv_all[5 + 2 * gk][keys], ones_hi], axis=1)
            h0 = ATT_GROUP * gk
            probs, esinks = [], []
            for which, k_op in ((0, k_lo), (1, k_hi)):
                sink = jnp.where(upper_half_rows, sinks_ref[h0 + 2 + which], sinks_ref[h0 + which])
                s = jnp.where(valid, _dot_nt(q2, k_op), MASK_VALUE)
                m = jnp.maximum(jnp.max(jnp.maximum(s[:, :blk], s[:, blk:]), axis=-1, keepdims=True), sink)
                probs.append(jnp.exp(s - m).astype(BF16))
                esinks.append(jnp.exp(sink - m))
            out = _dot(jnp.concatenate(probs, axis=1), jnp.concatenate([r_lo, r_hi], axis=0))
            denom = out[:, LANES:] + jnp.where(lane_lo, esinks[0], esinks[1])
            att2 = out[:, :LANES] / denom
            att_cols += [att2[:blk], att2[blk:]]
        att_rows.append(jnp.concatenate(att_cols, axis=1))
    att = jnp.concatenate(att_rows, axis=0)

    lbl = lbl_ref[...]
    lexp = jnp.exp(lbl - jnp.max(lbl, axis=0, keepdims=True))
    lb = lexp[0:1, :] / jnp.sum(lexp, axis=0, keepdims=True)
    tri = tri_ref[...]
    row_ids = lax.broadcasted_iota(jnp.int32, (ts, HG_DIM), 0)
    half = ts // 2
    rec_cols = []
    for h in range(HG_HEADS):
        cols = slice(h * HG_DIM, (h + 1) * HG_DIM)
        lb_h = lb[:, cols]
        f = 0.5 * (1.0 + lb_h) + (0.5 * (1.0 - lb_h)) * jnp.tanh(0.5 * p_f[:, cols])
        kk = 1.0 - f
        qf = _silu(p_q[:, cols])
        vv = p_i[:, cols].astype(BF16)
        g = jnp.log(f) * LOG2_E
        g_hi = g.astype(BF16)
        g_lo = (g - g_hi.astype(F32)).astype(BF16)
        b2 = _dot(tri, jnp.concatenate([g_hi, g_lo], axis=1))
        b = b2[:, :HG_DIM] + b2[:, HG_DIM:]
        w_incl = jnp.exp2(b)
        w_tail = jnp.exp2(b[ts - 1:ts, :] - b)
        st = state_ref[h]
        o = _dot_nt((qf * w_incl).astype(BF16), st.astype(BF16))
        n_tiles = half // SUBLANES
        quad = [[None] * n_tiles, [None] * n_tiles]
        s10 = None
        for li, m in enumerate(HG_LEVELS):
            x_f = _select_rows(m, qf, kk, row_ids) * _level_decay(m, b, f, row_ids)
            x_l = x_f.astype(BF16)
            if 2 * m == ts:
                s10 = _dot_nt(x_l[half:], x_l[:half])
                continue
            if m >= SUBLANES:
                tiles = [t for t in range(n_tiles) if (t * SUBLANES // m) % 2 == 1]
            else:
                tiles = list(range(n_tiles))
            for hf in range(2):
                r0 = hf * half
                if len(tiles) == n_tiles:
                    lhs = x_l[r0:r0 + half]
                else:
                    lhs = jnp.concatenate([x_f[r0 + t * SUBLANES:r0 + (t + 1) * SUBLANES] for t in tiles],
                                          axis=0).astype(BF16)
                sc = _dot_nt(lhs, x_l[r0:r0 + half])
                for i, t in enumerate(tiles):
                    m0 = (li - 1) * LANES + t * SUBLANES
                    part = sc[i * SUBLANES:(i + 1) * SUBLANES] * lmask_ref[m0:m0 + SUBLANES, :]
                    quad[hf][t] = part if quad[hf][t] is None else quad[hf][t] + part
        s00 = jnp.concatenate(quad[0], axis=0)
        s11 = jnp.concatenate(quad[1], axis=0)
        o_top = _dot(s00.astype(BF16), vv[:half])
        o_bot = _dot(jnp.concatenate([s10, s11], axis=1).astype(BF16), vv)
        diag = jnp.sum(qf * kk, axis=-1, keepdims=True)
        o = o + jnp.concatenate([o_top, o_bot], axis=0) + diag * p_i[:, cols]
        state_ref[h] = st * w_incl[ts - 1:ts, :] + _dot_tn(vv, (kk * w_tail).astype(BF16))
        rec_cols.append(o * _rms_scale(o))
    rec_n = (jnp.concatenate(rec_cols, axis=1) * ghg_ref[...] * _silu(p_g)).astype(BF16)

    att_n = (att * _rms_scale(att) * gatt_ref[...]).astype(BF16)
    o_ref[...] = x + _dot(jnp.concatenate([att_n, rec_n], axis=1), wout_ref[...])


def _mixer(x, gmix, w_in, positions, w_out, sinks, gatt, lbl, ghg):
    b, s, _ = x.shape
    ts = MIX_TILE
    tri, lmask = _hgrn_constants()
    invf, ecos, esin = _rotary_constants()
    pos = positions.reshape(b * (s // ts), 1, ts)
    cur = lambda bi, i: (bi, i, 0)
    fixed = lambda bi, i: (0, 0)
    return pl.pallas_call(
        _mixer_kernel,
        grid=(b, s // ts),
        in_specs=[
            pl.BlockSpec(memory_space=pltpu.SMEM),
            pl.BlockSpec((None, ts, D_MODEL), cur),
            pl.BlockSpec((1, D_MODEL), fixed),
            pl.BlockSpec(w_in.shape, fixed, pipeline_mode=pl.Buffered(1)),
            pl.BlockSpec((None, 1, ts), lambda bi, i: (bi * (s // ts) + i, 0, 0)),
            pl.BlockSpec(invf.shape, fixed),
            pl.BlockSpec(ecos.shape, fixed),
            pl.BlockSpec(esin.shape, fixed),
            pl.BlockSpec((D_MODEL, D_MODEL), fixed, pipeline_mode=pl.Buffered(1)),
            pl.BlockSpec((1, ATT_WIDTH), fixed),
            pl.BlockSpec(lbl.shape, fixed),
            pl.BlockSpec((1, HG_WIDTH), fixed),
            pl.BlockSpec(tri.shape, fixed),
            pl.BlockSpec(lmask.shape, fixed),
        ],
        out_specs=pl.BlockSpec((None, ts, D_MODEL), cur),
        out_shape=jax.ShapeDtypeStruct((b, s, D_MODEL), F32),
        scratch_shapes=[
            pltpu.VMEM((HG_HEADS, HG_DIM, HG_DIM), F32),
            pltpu.VMEM((ATT_BLOCK, KV_PAD_WIDTH), BF16),
        ],
        compiler_params=pltpu.CompilerParams(
            dimension_semantics=("arbitrary", "arbitrary"), vmem_limit_bytes=VMEM_LIMIT_BYTES),
        name="mixer",
    )(sinks, x, gmix, w_in, pos, invf, ecos, esin, w_out, gatt, lbl, ghg, tri, lmask)


def _mem_kv_kernel(mem_ref, gain_ref, w_ref, k_ref, v_ref):
    m = mem_ref[...]
    mn = (m * _rms_scale(m) * gain_ref[...]).astype(BF16)
    kv = _dot(mn, w_ref[...])
    k_ref[...] = kv[:, :D_MODEL].astype(BF16)
    v_ref[...] = kv[:, D_MODEL:].astype(BF16)


def _mem_kv(mem, gain, w_xkv):
    b, m, _ = mem.shape
    blk = lambda bi: (bi, 0, 0)
    fixed = lambda bi: (0, 0)
    return pl.pallas_call(
        _mem_kv_kernel,
        grid=(b,),
        in_specs=[
            pl.BlockSpec((None, m, D_MODEL), blk),
            pl.BlockSpec((1, D_MODEL), fixed),
            pl.BlockSpec((D_MODEL, 2 * D_MODEL), fixed),
        ],
        out_specs=[pl.BlockSpec((None, m, D_MODEL), blk), pl.BlockSpec((None, m, D_MODEL), blk)],
        out_shape=[jax.ShapeDtypeStruct((b, m, D_MODEL), BF16)] * 2,
        compiler_params=pltpu.CompilerParams(
            dimension_semantics=("arbitrary",), vmem_limit_bytes=VMEM_LIMIT_BYTES),
        name="mem_kv",
    )(mem, gain, w_xkv)


def _xattn_kernel(x_ref, gain_ref, wq_ref, k_ref, v_ref, wo_ref, o_ref):
    x = x_ref[...]
    hq = (x * _rms_scale(x) * gain_ref[...]).astype(BF16)
    q = _dot(hq, wq_ref[...]).astype(BF16)
    outs = []
    for h in range(X_HEADS):
        cols = slice(h * X_HEAD_DIM, (h + 1) * X_HEAD_DIM)
        s = _dot_nt(q[:, cols], k_ref[:, cols]) * (X_HEAD_DIM ** -0.5)
        p = jnp.exp(s - jnp.max(s, axis=-1, keepdims=True))
        o = _dot(p.astype(BF16), v_ref[:, cols])
        outs.append((o / jnp.sum(p, axis=-1, keepdims=True)).astype(BF16))
    xo = jnp.concatenate(outs, axis=1)
    o_ref[...] = x + _dot(xo, wo_ref[...])


def _xattn(x, gain, w_xq, xk, xv, w_xo, tm):
    b, s, _ = x.shape
    m = xk.shape[1]
    cur = lambda bi, i: (bi, i, 0)
    mem = lambda bi, i: (bi, 0, 0)
    fixed = lambda bi, i: (0, 0)
    return pl.pallas_call(
        _xattn_kernel,
        grid=(b, s // tm),
        in_specs=[
            pl.BlockSpec((None, tm, D_MODEL), cur),
            pl.BlockSpec((1, D_MODEL), fixed),
            pl.BlockSpec((D_MODEL, D_MODEL), fixed),
            pl.BlockSpec((None, m, D_MODEL), mem),
            pl.BlockSpec((None, m, D_MODEL), mem),
            pl.BlockSpec((D_MODEL, D_MODEL), fixed),
        ],
        out_specs=pl.BlockSpec((None, tm, D_MODEL), cur),
        out_shape=jax.ShapeDtypeStruct((b, s, D_MODEL), F32),
        compiler_params=pltpu.CompilerParams(
            dimension_semantics=("arbitrary", "arbitrary"), vmem_limit_bytes=VMEM_LIMIT_BYTES),
        name="xattn",
    )(x, gain, w_xq, xk, xv, w_xo)


FFN_CHUNKS = ((0, 1024), (1024, 2048), (2048, FFN_HIDDEN))


def _ffn_kernel(x_ref, gain_ref, wgu_ref, wd_ref, gfin_ref, o_ref):
    x = x_ref[...]
    hf = (x * _rms_scale(x) * gain_ref[...]).astype(BF16)
    acc = x
    for lo, hi in FFN_CHUNKS:
        gate = _dot(hf, wgu_ref[:, lo:hi])
        upv = _dot(hf, wgu_ref[:, FFN_HIDDEN + lo:FFN_HIDDEN + hi])
        act = (_silu(gate) * upv).astype(BF16)
        acc = acc + _dot(act, wd_ref[lo:hi, :])
    o_ref[...] = acc * _rms_scale(acc) * gfin_ref[...]


def _ffn(x2d, gain, w_gu, w_d, gfin, tm):
    n = x2d.shape[0]
    row = lambda i: (i, 0)
    fixed = lambda i: (0, 0)
    return pl.pallas_call(
        _ffn_kernel,
        grid=(n // tm,),
        in_specs=[
            pl.BlockSpec((tm, D_MODEL), row),
            pl.BlockSpec((1, D_MODEL), fixed),
            pl.BlockSpec((D_MODEL, 2 * FFN_HIDDEN), fixed, pipeline_mode=pl.Buffered(1)),
            pl.BlockSpec((FFN_HIDDEN, D_MODEL), fixed, pipeline_mode=pl.Buffered(1)),
            pl.BlockSpec((1, D_MODEL), fixed),
        ],
        out_specs=pl.BlockSpec((tm, D_MODEL), row),
        out_shape=jax.ShapeDtypeStruct((n, D_MODEL), F32),
        compiler_params=pltpu.CompilerParams(
            dimension_semantics=("arbitrary",), vmem_limit_bytes=VMEM_LIMIT_BYTES),
        name="ffn",
    )(x2d, gain, w_gu, w_d, gfin)


def kernel(x, mem, positions, norm_mix, w_in, att_sinks, att_out_gain, hg_lb_logits, hg_out_gain,
           w_out, norm_xattn, norm_mem, w_xq, w_xkv, w_xo, norm_ffn, w_gate_up, w_down, norm_final):
    b, s, d = x.shape
    n = b * s
    assert w_in.shape[0] == 1 and hg_lb_logits.shape[0] == 2, "single-layer block only"
    row = lambda v: v.reshape(1, -1).astype(F32)
    x = _mixer(x, row(norm_mix[0]), w_in[0].astype(BF16), positions,
               w_out[0].astype(BF16), att_sinks[0].astype(F32), row(att_out_gain[0]),
               hg_lb_logits.astype(F32), row(hg_out_gain[0]))
    xk, xv = _mem_kv(mem, row(norm_mem[0]), w_xkv[0].astype(BF16))
    x = _xattn(x, row(norm_xattn[0]), w_xq[0].astype(BF16), xk, xv, w_xo[0].astype(BF16), tm=1024)
    y = _ffn(x.reshape(n, d), row(norm_ffn[0]), w_gate_up[0].astype(BF16), w_down[0].astype(BF16),
             row(norm_final), tm=1024)
    return y.reshape(b, s, d)
```

```python
import numpy as np
import jax
import jax.numpy as jnp
from jax import lax
from jax.experimental import pallas as pl
from jax.experimental.pallas import tpu as pltpu

F32 = jnp.float32
BF16 = jnp.bfloat16

D_MODEL = 1024
MEM_LEN = 256
ATT_HEADS = 8
ATT_KV_HEADS = 2
ATT_GROUP = ATT_HEADS // ATT_KV_HEADS
ATT_HEAD_DIM = 64
ATT_WIDTH = ATT_HEADS * ATT_HEAD_DIM
ATT_KV_WIDTH = ATT_KV_HEADS * ATT_HEAD_DIM
ATT_BLOCK = 128
ROPE_THETA = 500000.0
ROPE_DIM = ATT_HEAD_DIM // 4
ROPE_HALF = ROPE_DIM // 2
HG_HEADS = 4
HG_DIM = 128
HG_WIDTH = HG_HEADS * HG_DIM
X_HEADS = 4
X_HEAD_DIM = D_MODEL // X_HEADS
FFN_HIDDEN = 2816
RMS_EPS = 1e-6
LANES = 128
SUBLANES = 8
MASK_VALUE = -1e30
LOG2_E = 1.4426950408889634

MIX_TILE = 256
HG_LEVELS = (128, 64, 32, 16, 8, 4, 2, 1)
KV_PAD_WIDTH = 2 * ATT_KV_HEADS * 2 * LANES

VMEM_LIMIT_BYTES = 56 * 1024 * 1024


def _rms_scale(x):
    return lax.rsqrt(jnp.mean(x * x, axis=-1, keepdims=True) + RMS_EPS)


def _silu(x):
    h = 0.5 * x
    return h + h * jnp.tanh(h)


def _dot(a, b):
    return jnp.dot(a, b, preferred_element_type=F32)


def _dot_nt(a, b):
    return lax.dot_general(a, b, (((1,), (1,)), ((), ())), preferred_element_type=F32)


def _dot_tn(a, b):
    return lax.dot_general(a, b, (((0,), (0,)), ((), ())), preferred_element_type=F32)


def _hgrn_constants():
    r = np.arange(MIX_TILE)[:, None]
    j = np.arange(MIX_TILE)[None, :]
    tri = (j <= r).astype(np.float32)
    rr = np.arange(LANES)[:, None]
    jj = np.arange(LANES)[None, :]
    masks = []
    for m in HG_LEVELS:
        if 2 * m <= LANES:
            masks.append(((rr // (2 * m)) == (jj // (2 * m))) & ((rr % (2 * m)) >= m) & ((jj % (2 * m)) < m))
    lmask = np.concatenate(masks, axis=0).astype(np.float32)
    return jnp.asarray(tri, BF16), jnp.asarray(lmask, F32)


def _rotary_constants():
    inv_freq = jnp.power(jnp.float32(ROPE_THETA),
                         -jnp.arange(ROPE_HALF, dtype=F32) * (2.0 / ROPE_DIM))
    invf = jnp.broadcast_to(inv_freq[:, None], (ROPE_HALF, MIX_TILE))
    lane = np.arange(LANES)
    in_head = lane % ATT_HEAD_DIM
    freq_row = (np.arange(ROPE_HALF)[:, None] == (in_head % ROPE_HALF)[None, :])
    rot = freq_row & (in_head < ROPE_DIM)[None, :]
    ecos = np.zeros((4 * ROPE_HALF, LANES), np.float32)
    ecos[:ROPE_HALF] = rot
    ecos[ROPE_HALF:2 * ROPE_HALF] = rot
    ecos[2 * ROPE_HALF] = in_head >= ROPE_DIM
    sign = np.where(in_head < ROPE_HALF, -1.0, 1.0)[None, :]
    esin = np.concatenate([rot * sign, rot * sign], axis=0).astype(np.float32)
    return invf, jnp.asarray(ecos, BF16), jnp.asarray(esin, BF16)


def _level_decay(m, b, f, row_ids):
    ts = b.shape[0]
    if m >= SUBLANES // 2:
        b3 = b.reshape(ts // (2 * m), 2 * m, HG_DIM)
        ref = b3[:, m - 1:m, :]
        if m >= SUBLANES:
            e3 = jnp.concatenate([ref - b3[:, :m, :], b3[:, m:, :] - ref], axis=1)
        else:
            e3 = -jnp.abs(b3 - ref)
        return jnp.exp2(e3).reshape(ts, HG_DIM)
    if m == 2:
        pos = row_ids & 3
        f_next = pltpu.roll(f, ts - 1, axis=0)
        f_prev = pltpu.roll(f, 1, axis=0)
        return jnp.where(pos == 0, f_next, jnp.where(pos == 1, 1.0, jnp.where(pos == 2, f, f * f_prev)))
    assert m == 1
    return jnp.where((row_ids & 1) != 0, f, 1.0)


def _select_rows(m, upper_src, lower_src, row_ids):
    c = upper_src.shape[0]
    if m >= SUBLANES:
        pieces = []
        for b0 in range(0, c, 2 * m):
            pieces.append(lower_src[b0:b0 + m])
            pieces.append(upper_src[b0 + m:b0 + 2 * m])
        return jnp.concatenate(pieces, axis=0)
    return jnp.where((row_ids & m) != 0, upper_src, lower_src)


def _rotary_tables(pos_ref, invf_ref, ecos_ref, esin_ref):
    ang = invf_ref[...] * pos_ref[...].astype(F32)

    def spread(t, e_ref, extra):
        t_hi = t.astype(BF16).astype(F32)
        rows = jnp.concatenate([t_hi, t - t_hi] + extra, axis=0).astype(BF16)
        return _dot_tn(rows, e_ref[...])

    ones = jnp.ones_like(ang)
    cos_t = spread(jnp.cos(ang), ecos_ref, [ones, jnp.zeros_like(ang)])
    sin_t = spread(jnp.sin(ang), esin_ref, [])
    return cos_t, sin_t


def _mixer_kernel(sinks_ref, x_ref, gmix_ref, win_ref, pos_ref, invf_ref, ecos_ref, esin_ref, wout_ref,
                  gatt_ref, lbl_ref, ghg_ref, tri_ref, lmask_ref, o_ref, state_ref, kvprev_ref):
    step = pl.program_id(1)
    ts = MIX_TILE
    blk = ATT_BLOCK

    @pl.when(step == 0)
    def _():
        state_ref[...] = jnp.zeros_like(state_ref)
        kvprev_ref[...] = jnp.zeros_like(kvprev_ref)

    x = x_ref[...]
    hn = (x * _rms_scale(x) * gmix_ref[...]).astype(BF16)

    def proj(c0, c1):
        return _dot(hn, win_ref[:, c0:c1])

    cos, sin = _rotary_tables(pos_ref, invf_ref, ecos_ref, esin_ref)
    lane = lax.broadcasted_iota(jnp.int32, cos.shape, 1)
    first_half = (lane & (ATT_HEAD_DIM - 1)) < ROPE_HALF
    lo_head = lane < ATT_HEAD_DIM

    def rotate(col):
        partner = jnp.where(first_half,
                            pltpu.roll(col, LANES - ROPE_HALF, axis=1),
                            pltpu.roll(col, ROPE_HALF, axis=1))
        return col * cos + partner * sin

    def padded(col):
        swapped = pltpu.roll(col, ATT_HEAD_DIM, axis=1)
        zero = jnp.zeros_like(col)
        return [v.astype(BF16) for v in (jnp.where(lo_head, col, zero), jnp.where(lo_head, zero, swapped),
                                         jnp.where(lo_head, swapped, zero), jnp.where(lo_head, zero, col))]

    pq = proj(0, ATT_WIDTH)
    q_cols = [(rotate(pq[:, j * LANES:(j + 1) * LANES]) * (ATT_HEAD_DIM ** -0.5)).astype(BF16)
              for j in range(ATT_WIDTH // LANES)]
    pkv = proj(ATT_WIDTH, ATT_WIDTH + 2 * ATT_KV_WIDTH)
    kv_new = padded(rotate(pkv[:, :LANES])) + padded(pkv[:, LANES:])
    kv_all = [jnp.concatenate([kvprev_ref[:, i * LANES:(i + 1) * LANES], v], axis=0)
              for i, v in enumerate(kv_new)]
    for i, v in enumerate(kv_new):
        kvprev_ref[:, i * LANES:(i + 1) * LANES] = v[ts - blk:]
    r0c = ATT_WIDTH + 2 * ATT_KV_WIDTH
    p_q, p_f, p_i, p_g = (proj(r0c + k * HG_WIDTH, r0c + (k + 1) * HG_WIDTH) for k in range(4))

    qrow = lax.broadcasted_iota(jnp.int32, (2 * blk, 2 * blk), 0) & (blk - 1)
    kcol = lax.broadcasted_iota(jnp.int32, (2 * blk, 2 * blk), 1)
    upper_half_rows = lax.broadcasted_iota(jnp.int32, (2 * blk, 1), 0) >= blk
    lane_lo = lax.broadcasted_iota(jnp.int32, (2 * blk, LANES), 1) < ATT_HEAD_DIM
    ones_lo = jnp.where(lane_lo, 1.0, 0.0).astype(BF16)
    ones_hi = jnp.where(lane_lo, 0.0, 1.0).astype(BF16)
    first_lo = jnp.where(step > 0, -1, blk - 1)
    att_rows = []
    for jb in range(ts // blk):
        keys = slice(jb * blk, (jb + 2) * blk)
        lo_bound = jnp.maximum(qrow, first_lo) if jb == 0 else qrow
        valid = (kcol > lo_bound) & (kcol <= qrow + blk)
        att_cols = []
        for gk in range(ATT_KV_HEADS):
            q2 = jnp.concatenate([q_cols[2 * gk][jb * blk:(jb + 1) * blk],
                                  q_cols[2 * gk + 1][jb * blk:(jb + 1) * blk]], axis=0)
            k_lo = kv_all[2 * gk][keys]
            k_hi = kv_all[2 * gk + 1][keys]
            r_lo = jnp.concatenate([kv_all[4 + 2 * gk][keys], ones_lo], axis=1)
            r_hi = jnp.concatenate([kv_all[5 + 2 * gk][keys], ones_hi], axis=1)
            h0 = ATT_GROUP * gk
            probs, esinks = [], []
            for which, k_op in ((0, k_lo), (1, k_hi)):
                sink = jnp.where(upper_half_rows, sinks_ref[h0 + 2 + which], sinks_ref[h0 + which])
                s = jnp.where(valid, _dot_nt(q2, k_op), MASK_VALUE)
                m = jnp.maximum(jnp.max(jnp.maximum(s[:, :blk], s[:, blk:]), axis=-1, keepdims=True), sink)
                probs.append(jnp.exp(s - m).astype(BF16))
                esinks.append(jnp.exp(sink - m))
            out = _dot(jnp.concatenate(probs, axis=1), jnp.concatenate([r_lo, r_hi], axis=0))
            denom = out[:, LANES:] + jnp.where(lane_lo, esinks[0], esinks[1])
            att2 = out[:, :LANES] / denom
            att_cols += [att2[:blk], att2[blk:]]
        att_rows.append(jnp.concatenate(att_cols, axis=1))
    att = jnp.concatenate(att_rows, axis=0)

    lbl = lbl_ref[...]
    lexp = jnp.exp(lbl - jnp.max(lbl, axis=0, keepdims=True))
    lb = lexp[0:1, :] / jnp.sum(lexp, axis=0, keepdims=True)
    tri = tri_ref[...]
    row_ids = lax.broadcasted_iota(jnp.int32, (ts, HG_DIM), 0)
    half = ts // 2
    rec_cols = []
    for h in range(HG_HEADS):
        cols = slice(h * HG_DIM, (h + 1) * HG_DIM)
        lb_h = lb[:, cols]
        f = 0.5 * (1.0 + lb_h) + (0.5 * (1.0 - lb_h)) * jnp.tanh(0.5 * p_f[:, cols])
        kk = 1.0 - f
        qf = _silu(p_q[:, cols])
        vv = p_i[:, cols].astype(BF16)
        g = jnp.log(f) * LOG2_E
        g_hi = g.astype(BF16)
        g_lo = (g - g_hi.astype(F32)).astype(BF16)
        b2 = _dot(tri, jnp.concatenate([g_hi, g_lo], axis=1))
        b = b2[:, :HG_DIM] + b2[:, HG_DIM:]
        w_incl = jnp.exp2(b)
        w_tail = jnp.exp2(b[ts - 1:ts, :] - b)
        st = state_ref[h]
        o = _dot_nt((qf * w_incl).astype(BF16), st.astype(BF16))
        n_tiles = half // SUBLANES
        quad = [[None] * n_tiles, [None] * n_tiles]
        s10 = None
        for li, m in enumerate(HG_LEVELS):
            x_f = _select_rows(m, qf, kk, row_ids) * _level_decay(m, b, f, row_ids)
            x_l = x_f.astype(BF16)
            if 2 * m == ts:
                s10 = _dot_nt(x_l[half:], x_l[:half])
                continue
            if m >= SUBLANES:
                tiles = [t for t in range(n_tiles) if (t * SUBLANES // m) % 2 == 1]
            else:
                tiles = list(range(n_tiles))
            for hf in range(2):
                r0 = hf * half
                if len(tiles) == n_tiles:
                    lhs = x_l[r0:r0 + half]
                else:
                    lhs = jnp.concatenate([x_f[r0 + t * SUBLANES:r0 + (t + 1) * SUBLANES] for t in tiles],
                                          axis=0).astype(BF16)
                sc = _dot_nt(lhs, x_l[r0:r0 + half])
                for i, t in enumerate(tiles):
                    m0 = (li - 1) * LANES + t * SUBLANES
                    part = sc[i * SUBLANES:(i + 1) * SUBLANES] * lmask_ref[m0:m0 + SUBLANES, :]
                    quad[hf][t] = part if quad[hf][t] is None else quad[hf][t] + part
        s00 = jnp.concatenate(quad[0], axis=0)
        s11 = jnp.concatenate(quad[1], axis=0)
        o_top = _dot(s00.astype(BF16), vv[:half])
        o_bot = _dot(jnp.concatenate([s10, s11], axis=1).astype(BF16), vv)
        diag = jnp.sum(qf * kk, axis=-1, keepdims=True)
        o = o + jnp.concatenate([o_top, o_bot], axis=0) + diag * p_i[:, cols]
        state_ref[h] = st * w_incl[ts - 1:ts, :] + _dot_tn(vv, (kk * w_tail).astype(BF16))
        rec_cols.append(o * _rms_scale(o))
    rec_n = (jnp.concatenate(rec_cols, axis=1) * ghg_ref[...] * _silu(p_g)).astype(BF16)

    att_n = (att * _rms_scale(att) * gatt_ref[...]).astype(BF16)
    o_ref[...] = x + _dot(jnp.concatenate([att_n, rec_n], axis=1), wout_ref[...])


def _mixer(x, gmix, w_in, positions, w_out, sinks, gatt, lbl, ghg):
    b, s, _ = x.shape
    ts = MIX_TILE
    tri, lmask = _hgrn_constants()
    invf, ecos, esin = _rotary_constants()
    pos = positions.reshape(b * (s // ts), 1, ts)
    cur = lambda bi, i: (bi, i, 0)
    fixed = lambda bi, i: (0, 0)
    return pl.pallas_call(
        _mixer_kernel,
        grid=(b, s // ts),
        in_specs=[
            pl.BlockSpec(memory_space=pltpu.SMEM),
            pl.BlockSpec((None, ts, D_MODEL), cur),
            pl.BlockSpec((1, D_MODEL), fixed),
            pl.BlockSpec(w_in.shape, fixed, pipeline_mode=pl.Buffered(1)),
            pl.BlockSpec((None, 1, ts), lambda bi, i: (bi * (s // ts) + i, 0, 0)),
            pl.BlockSpec(invf.shape, fixed),
            pl.BlockSpec(ecos.shape, fixed),
            pl.BlockSpec(esin.shape, fixed),
            pl.BlockSpec((D_MODEL, D_MODEL), fixed, pipeline_mode=pl.Buffered(1)),
            pl.BlockSpec((1, ATT_WIDTH), fixed),
            pl.BlockSpec(lbl.shape, fixed),
            pl.BlockSpec((1, HG_WIDTH), fixed),
            pl.BlockSpec(tri.shape, fixed),
            pl.BlockSpec(lmask.shape, fixed),
        ],
        out_specs=pl.BlockSpec((None, ts, D_MODEL), cur),
        out_shape=jax.ShapeDtypeStruct((b, s, D_MODEL), F32),
        scratch_shapes=[
            pltpu.VMEM((HG_HEADS, HG_DIM, HG_DIM), F32),
            pltpu.VMEM((ATT_BLOCK, KV_PAD_WIDTH), BF16),
        ],
        compiler_params=pltpu.CompilerParams(
            dimension_semantics=("arbitrary", "arbitrary"), vmem_limit_bytes=VMEM_LIMIT_BYTES,
            allow_input_fusion=[i in (3, 8) for i in range(14)]),
        name="mixer",
    )(sinks, x, gmix, w_in, pos, invf, ecos, esin, w_out, gatt, lbl, ghg, tri, lmask)


def _mem_kv_kernel(mem_ref, gain_ref, w_ref, k_ref, v_ref):
    m = mem_ref[...]
    mn = (m * _rms_scale(m) * gain_ref[...]).astype(BF16)
    kv = _dot(mn, w_ref[...])
    k_ref[...] = kv[:, :D_MODEL].astype(BF16)
    v_ref[...] = kv[:, D_MODEL:].astype(BF16)


def _mem_kv(mem, gain, w_xkv):
    b, m, _ = mem.shape
    blk = lambda bi: (bi, 0, 0)
    fixed = lambda bi: (0, 0)
    return pl.pallas_call(
        _mem_kv_kernel,
        grid=(b,),
        in_specs=[
            pl.BlockSpec((None, m, D_MODEL), blk),
            pl.BlockSpec((1, D_MODEL), fixed),
            pl.BlockSpec((D_MODEL, 2 * D_MODEL), fixed),
        ],
        out_specs=[pl.BlockSpec((None, m, D_MODEL), blk), pl.BlockSpec((None, m, D_MODEL), blk)],
        out_shape=[jax.ShapeDtypeStruct((b, m, D_MODEL), BF16)] * 2,
        compiler_params=pltpu.CompilerParams(
            dimension_semantics=("arbitrary",), vmem_limit_bytes=VMEM_LIMIT_BYTES,
            allow_input_fusion=[False, False, True]),
        name="mem_kv",
    )(mem, gain, w_xkv)


def _xattn_kernel(x_ref, gain_ref, wq_ref, k_ref, v_ref, wo_ref, o_ref):
    x = x_ref[...]
    hq = (x * _rms_scale(x) * gain_ref[...]).astype(BF16)
    q = _dot(hq, wq_ref[...]).astype(BF16)
    outs = []
    for h in range(X_HEADS):
        cols = slice(h * X_HEAD_DIM, (h + 1) * X_HEAD_DIM)
        s = _dot_nt(q[:, cols], k_ref[:, cols]) * (X_HEAD_DIM ** -0.5)
        p = jnp.exp(s - jnp.max(s, axis=-1, keepdims=True))
        o = _dot(p.astype(BF16), v_ref[:, cols])
        outs.append((o / jnp.sum(p, axis=-1, keepdims=True)).astype(BF16))
    xo = jnp.concatenate(outs, axis=1)
    o_ref[...] = x + _dot(xo, wo_ref[...])


def _xattn(x, gain, w_xq, xk, xv, w_xo, tm):
    b, s, _ = x.shape
    m = xk.shape[1]
    cur = lambda bi, i: (bi, i, 0)
    mem = lambda bi, i: (bi, 0, 0)
    fixed = lambda bi, i: (0, 0)
    return pl.pallas_call(
        _xattn_kernel,
        grid=(b, s // tm),
        in_specs=[
            pl.BlockSpec((None, tm, D_MODEL), cur),
            pl.BlockSpec((1, D_MODEL), fixed),
            pl.BlockSpec((D_MODEL, D_MODEL), fixed),
            pl.BlockSpec((None, m, D_MODEL), mem),
            pl.BlockSpec((None, m, D_MODEL), mem),
            pl.BlockSpec((D_MODEL, D_MODEL), fixed),
        ],
        out_specs=pl.BlockSpec((None, tm, D_MODEL), cur),
        out_shape=jax.ShapeDtypeStruct((b, s, D_MODEL), F32),
        compiler_params=pltpu.CompilerParams(
            dimension_semantics=("arbitrary", "arbitrary"), vmem_limit_bytes=VMEM_LIMIT_BYTES,
            allow_input_fusion=[False, False, True, False, False, True]),
        name="xattn",
    )(x, gain, w_xq, xk, xv, w_xo)


FFN_CHUNKS = ((0, 1024), (1024, 2048), (2048, FFN_HIDDEN))


def _ffn_kernel(x_ref, gain_ref, wgu_ref, wd_ref, gfin_ref, o_ref):
    x = x_ref[...]
    hf = (x * _rms_scale(x) * gain_ref[...]).astype(BF16)
    acc = x
    for lo, hi in FFN_CHUNKS:
        gate = _dot(hf, wgu_ref[:, lo:hi])
        upv = _dot(hf, wgu_ref[:, FFN_HIDDEN + lo:FFN_HIDDEN + hi])
        act = (_silu(gate) * upv).astype(BF16)
        acc = acc + _dot(act, wd_ref[lo:hi, :])
    o_ref[...] = acc * _rms_scale(acc) * gfin_ref[...]


def _ffn(x2d, gain, w_gu, w_d, gfin, tm):
    n = x2d.shape[0]
    row = lambda i: (i, 0)
    fixed = lambda i: (0, 0)
    return pl.pallas_call(
        _ffn_kernel,
        grid=(n // tm,),
        in_specs=[
            pl.BlockSpec((tm, D_MODEL), row),
            pl.BlockSpec((1, D_MODEL), fixed),
            pl.BlockSpec((D_MODEL, 2 * FFN_HIDDEN), fixed, pipeline_mode=pl.Buffered(1)),
            pl.BlockSpec((FFN_HIDDEN, D_MODEL), fixed, pipeline_mode=pl.Buffered(1)),
            pl.BlockSpec((1, D_MODEL), fixed),
        ],
        out_specs=pl.BlockSpec((tm, D_MODEL), row),
        out_shape=jax.ShapeDtypeStruct((n, D_MODEL), F32),
        compiler_params=pltpu.CompilerParams(
            dimension_semantics=("arbitrary",), vmem_limit_bytes=VMEM_LIMIT_BYTES,
            allow_input_fusion=[False, False, True, True, False]),
        name="ffn",
    )(x2d, gain, w_gu, w_d, gfin)


def kernel(x, mem, positions, norm_mix, w_in, att_sinks, att_out_gain, hg_lb_logits, hg_out_gain,
           w_out, norm_xattn, norm_mem, w_xq, w_xkv, w_xo, norm_ffn, w_gate_up, w_down, norm_final):
    b, s, d = x.shape
    n = b * s
    assert w_in.shape[0] == 1 and hg_lb_logits.shape[0] == 2, "single-layer block only"
    row = lambda v: v.reshape(1, -1).astype(F32)
    x = _mixer(x, row(norm_mix[0]), w_in[0].astype(BF16), positions,
               w_out[0].astype(BF16), att_sinks[0].astype(F32), row(att_out_gain[0]),
               hg_lb_logits.astype(F32), row(hg_out_gain[0]))
    xk, xv = _mem_kv(mem, row(norm_mem[0]), w_xkv[0].astype(BF16))
    x = _xattn(x, row(norm_xattn[0]), w_xq[0].astype(BF16), xk, xv, w_xo[0].astype(BF16), tm=1024)
    y = _ffn(x.reshape(n, d), row(norm_ffn[0]), w_gate_up[0].astype(BF16), w_down[0].astype(BF16),
             row(norm_final), tm=1024)
    return y.reshape(b, s, d)
```

```python
import numpy as np
import jax
import jax.numpy as jnp
from jax import lax
from jax.experimental import pallas as pl
from jax.experimental.pallas import tpu as pltpu

F32 = jnp.float32
BF16 = jnp.bfloat16

D_MODEL = 1024
MEM_LEN = 256
ATT_HEADS = 8
ATT_KV_HEADS = 2
ATT_GROUP = ATT_HEADS // ATT_KV_HEADS
ATT_HEAD_DIM = 64
ATT_WIDTH = ATT_HEADS * ATT_HEAD_DIM
ATT_KV_WIDTH = ATT_KV_HEADS * ATT_HEAD_DIM
ATT_BLOCK = 128
ROPE_THETA = 500000.0
ROPE_DIM = ATT_HEAD_DIM // 4
ROPE_HALF = ROPE_DIM // 2
HG_HEADS = 4
HG_DIM = 128
HG_WIDTH = HG_HEADS * HG_DIM
X_HEADS = 4
X_HEAD_DIM = D_MODEL // X_HEADS
FFN_HIDDEN = 2816
RMS_EPS = 1e-6
LANES = 128
SUBLANES = 8
MASK_VALUE = -1e30
LOG2_E = 1.4426950408889634

MIX_TILE = 256
HG_LEVELS = (128, 64, 32, 16, 8, 4, 2, 1)
KV_PAD_WIDTH = 2 * ATT_KV_HEADS * 2 * LANES

VMEM_LIMIT_BYTES = 56 * 1024 * 1024


def _rms_scale(x):
    return lax.rsqrt(jnp.mean(x * x, axis=-1, keepdims=True) + RMS_EPS)


def _silu(x):
    h = 0.5 * x
    return h + h * jnp.tanh(h)


def _dot(a, b):
    return jnp.dot(a, b, preferred_element_type=F32)


def _dot_nt(a, b):
    return lax.dot_general(a, b, (((1,), (1,)), ((), ())), preferred_element_type=F32)


def _dot_tn(a, b):
    return lax.dot_general(a, b, (((0,), (0,)), ((), ())), preferred_element_type=F32)


def _hgrn_constants():
    r = np.arange(MIX_TILE)[:, None]
    j = np.arange(MIX_TILE)[None, :]
    tri = (j <= r).astype(np.float32)
    rr = np.arange(LANES)[:, None]
    jj = np.arange(LANES)[None, :]
    masks = []
    for m in HG_LEVELS:
        if 2 * m <= LANES:
            masks.append(((rr // (2 * m)) == (jj // (2 * m))) & ((rr % (2 * m)) >= m) & ((jj % (2 * m)) < m))
    lmask = np.concatenate(masks, axis=0).astype(np.float32)
    return jnp.asarray(tri, BF16), jnp.asarray(lmask, F32)


def _attention_bias():
    i = (np.arange(2 * ATT_BLOCK) % ATT_BLOCK)[:, None]
    j = np.arange(2 * ATT_BLOCK)[None, :]
    band = (j > i) & (j <= i + ATT_BLOCK)
    variants = (band, band & (j >= ATT_BLOCK))
    return jnp.asarray(np.stack([np.where(v | (j == 0), 0.0, MASK_VALUE) for v in variants]), F32)


def _rotary_constants():
    inv_freq = jnp.power(jnp.float32(ROPE_THETA),
                         -jnp.arange(ROPE_HALF, dtype=F32) * (2.0 / ROPE_DIM))
    invf = jnp.broadcast_to(inv_freq[:, None], (ROPE_HALF, MIX_TILE))
    lane = np.arange(LANES)
    in_head = lane % ATT_HEAD_DIM
    freq_row = (np.arange(ROPE_HALF)[:, None] == (in_head % ROPE_HALF)[None, :])
    rot = freq_row & (in_head < ROPE_DIM)[None, :]
    ecos = np.zeros((4 * ROPE_HALF, LANES), np.float32)
    ecos[:ROPE_HALF] = rot
    ecos[ROPE_HALF:2 * ROPE_HALF] = rot
    ecos[2 * ROPE_HALF] = in_head >= ROPE_DIM
    sign = np.where(in_head < ROPE_HALF, -1.0, 1.0)[None, :]
    esin = np.concatenate([rot * sign, rot * sign], axis=0).astype(np.float32)
    return invf, jnp.asarray(ecos, BF16), jnp.asarray(esin, BF16)


def _level_decay(m, b, f, row_ids):
    ts = b.shape[0]
    if m >= SUBLANES // 2:
        b3 = b.reshape(ts // (2 * m), 2 * m, HG_DIM)
        ref = b3[:, m - 1:m, :]
        if m >= SUBLANES:
            e3 = jnp.concatenate([ref - b3[:, :m, :], b3[:, m:, :] - ref], axis=1)
        else:
            e3 = -jnp.abs(b3 - ref)
        return jnp.exp2(e3).reshape(ts, HG_DIM)
    if m == 2:
        pos = row_ids & 3
        f_next = pltpu.roll(f, ts - 1, axis=0)
        f_prev = pltpu.roll(f, 1, axis=0)
        return jnp.where(pos == 0, f_next, jnp.where(pos == 1, 1.0, jnp.where(pos == 2, f, f * f_prev)))
    assert m == 1
    return jnp.where((row_ids & 1) != 0, f, 1.0)


def _select_rows(m, upper_src, lower_src, row_ids):
    c = upper_src.shape[0]
    if m >= SUBLANES:
        pieces = []
        for b0 in range(0, c, 2 * m):
            pieces.append(lower_src[b0:b0 + m])
            pieces.append(upper_src[b0 + m:b0 + 2 * m])
        return jnp.concatenate(pieces, axis=0)
    return jnp.where((row_ids & m) != 0, upper_src, lower_src)


def _rotary_tables(pos_ref, invf_ref, ecos_ref, esin_ref):
    ang = invf_ref[...] * pos_ref[...].astype(F32)

    def spread(t, e_ref, extra):
        t_hi = t.astype(BF16).astype(F32)
        rows = jnp.concatenate([t_hi, t - t_hi] + extra, axis=0).astype(BF16)
        return _dot_tn(rows, e_ref[...])

    ones = jnp.ones_like(ang)
    cos_t = spread(jnp.cos(ang), ecos_ref, [ones, jnp.zeros_like(ang)])
    sin_t = spread(jnp.sin(ang), esin_ref, [])
    return cos_t, sin_t


def _mixer_kernel(sinks_ref, x_ref, gmix_ref, win_ref, pos_ref, invf_ref, ecos_ref, esin_ref, wout_ref,
                  gatt_ref, lbl_ref, ghg_ref, tri_ref, lmask_ref, bias_ref, o_ref, state_ref, kvprev_ref):
    step = pl.program_id(1)
    ts = MIX_TILE
    blk = ATT_BLOCK

    @pl.when(step == 0)
    def _():
        state_ref[...] = jnp.zeros_like(state_ref)
        kvprev_ref[...] = jnp.zeros_like(kvprev_ref)

    x = x_ref[...]
    hn = (x * _rms_scale(x) * gmix_ref[...]).astype(BF16)

    def proj(c0, c1):
        return _dot(hn, win_ref[:, c0:c1])

    cos, sin = _rotary_tables(pos_ref, invf_ref, ecos_ref, esin_ref)
    lane = lax.broadcasted_iota(jnp.int32, cos.shape, 1)
    first_half = (lane & (ATT_HEAD_DIM - 1)) < ROPE_HALF
    lo_head = lane < ATT_HEAD_DIM

    def rotate(col):
        partner = jnp.where(first_half,
                            pltpu.roll(col, LANES - ROPE_HALF, axis=1),
                            pltpu.roll(col, ROPE_HALF, axis=1))
        return col * cos + partner * sin

    def padded(col):
        swapped = pltpu.roll(col, ATT_HEAD_DIM, axis=1)
        zero = jnp.zeros_like(col)
        return [v.astype(BF16) for v in (jnp.where(lo_head, col, zero), jnp.where(lo_head, zero, swapped),
                                         jnp.where(lo_head, swapped, zero), jnp.where(lo_head, zero, col))]

    pq = proj(0, ATT_WIDTH)
    q_cols = [(rotate(pq[:, j * LANES:(j + 1) * LANES]) * (ATT_HEAD_DIM ** -0.5 * LOG2_E)).astype(BF16)
              for j in range(ATT_WIDTH // LANES)]
    pkv = proj(ATT_WIDTH, ATT_WIDTH + 2 * ATT_KV_WIDTH)
    kv_new = padded(rotate(pkv[:, :LANES])) + padded(pkv[:, LANES:])
    kv_all = [jnp.concatenate([kvprev_ref[:, i * LANES:(i + 1) * LANES], v], axis=0)
              for i, v in enumerate(kv_new)]
    for i, v in enumerate(kv_new):
        kvprev_ref[:, i * LANES:(i + 1) * LANES] = v[ts - blk:]
    r0c = ATT_WIDTH + 2 * ATT_KV_WIDTH
    p_q, p_f, p_i, p_g = (proj(r0c + k * HG_WIDTH, r0c + (k + 1) * HG_WIDTH) for k in range(4))

    upper_half_rows = lax.broadcasted_iota(jnp.int32, (2 * blk, 1), 0) >= blk
    lane_ids = lax.broadcasted_iota(jnp.int32, (2 * blk, LANES), 1)
    lane_lo = lane_ids < ATT_HEAD_DIM
    ones_lo = jnp.where(lane_lo, 1.0, 0.0).astype(BF16)
    ones_hi = jnp.where(lane_lo, 0.0, 1.0).astype(BF16)
    bf16_rows = 2 * SUBLANES
    drop_row0 = jnp.where(lax.broadcasted_iota(jnp.int32, (bf16_rows, LANES), 0) > 0, 1.0, 0.0).astype(BF16)

    def sink_slot(op):
        return jnp.concatenate([op[:bf16_rows] * drop_row0, op[bf16_rows:]], axis=0)

    att_rows = []
    for jb in range(ts // blk):
        keys = slice(jb * blk, (jb + 2) * blk)
        bias = bias_ref[jnp.where(step > 0, 0, 1)] if jb == 0 else bias_ref[0]
        att_cols = []
        for gk in range(ATT_KV_HEADS):
            q2 = jnp.concatenate([q_cols[2 * gk][jb * blk:(jb + 1) * blk],
                                  q_cols[2 * gk + 1][jb * blk:(jb + 1) * blk]], axis=0)
            k_lo = sink_slot(kv_all[2 * gk][keys])
            k_hi = sink_slot(kv_all[2 * gk + 1][keys])
            r_lo = jnp.concatenate([sink_slot(kv_all[4 + 2 * gk][keys]), ones_lo], axis=1)
            r_hi = jnp.concatenate([sink_slot(kv_all[5 + 2 * gk][keys]), ones_hi], axis=1)
            h0 = ATT_GROUP * gk
            probs = []
            for which, k_op in ((0, k_lo), (1, k_hi)):
                sink = jnp.where(upper_half_rows, sinks_ref[h0 + 2 + which], sinks_ref[h0 + which]) * LOG2_E
                s = _dot_nt(q2, k_op) + bias
                s_first = s[:, :LANES] + jnp.where(lane_ids == 0, sink, 0.0)
                m = jnp.max(jnp.maximum(s_first, s[:, LANES:]), axis=-1, keepdims=True)
                probs.append(jnp.exp2(jnp.concatenate([s_first, s[:, LANES:]], axis=1) - m).astype(BF16))
            out = _dot(jnp.concatenate(probs, axis=1), jnp.concatenate([r_lo, r_hi], axis=0))
            att2 = out[:, :LANES] / out[:, LANES:]
            att_cols += [att2[:blk], att2[blk:]]
        att_rows.append(jnp.concatenate(att_cols, axis=1))
    att = jnp.concatenate(att_rows, axis=0)

    lbl = lbl_ref[...]
    lexp = jnp.exp(lbl - jnp.max(lbl, axis=0, keepdims=True))
    lb = lexp[0:1, :] / jnp.sum(lexp, axis=0, keepdims=True)
    tri = tri_ref[...]
    row_ids = lax.broadcasted_iota(jnp.int32, (ts, HG_DIM), 0)
    half = ts // 2
    rec_cols = []
    for h in range(HG_HEADS):
        cols = slice(h * HG_DIM, (h + 1) * HG_DIM)
        lb_h = lb[:, cols]
        f = 0.5 * (1.0 + lb_h) + (0.5 * (1.0 - lb_h)) * jnp.tanh(0.5 * p_f[:, cols])
        kk = 1.0 - f
        qf = _silu(p_q[:, cols])
        vv = p_i[:, cols].astype(BF16)
        g = jnp.log(f) * LOG2_E
        g_hi = g.astype(BF16)
        g_lo = (g - g_hi.astype(F32)).astype(BF16)
        b2 = _dot(tri, jnp.concatenate([g_hi, g_lo], axis=1))
        b = b2[:, :HG_DIM] + b2[:, HG_DIM:]
        w_incl = jnp.exp2(b)
        w_tail = jnp.exp2(b[ts - 1:ts, :] - b)
        st = state_ref[h]
        o = _dot_nt((qf * w_incl).astype(BF16), st.astype(BF16))
        n_tiles = half // SUBLANES
        quad = [[None] * n_tiles, [None] * n_tiles]
        s10 = None
        for li, m in enumerate(HG_LEVELS):
            x_f = _select_rows(m, qf, kk, row_ids) * _level_decay(m, b, f, row_ids)
            x_l = x_f.astype(BF16)
            if 2 * m == ts:
                s10 = _dot_nt(x_l[half:], x_l[:half])
                continue
            if m >= SUBLANES:
                tiles = [t for t in range(n_tiles) if (t * SUBLANES // m) % 2 == 1]
            else:
                tiles = list(range(n_tiles))
            for hf in range(2):
                r0 = hf * half
                if len(tiles) == n_tiles:
                    lhs = x_l[r0:r0 + half]
                else:
                    lhs = jnp.concatenate([x_f[r0 + t * SUBLANES:r0 + (t + 1) * SUBLANES] for t in tiles],
                                          axis=0).astype(BF16)
                sc = _dot_nt(lhs, x_l[r0:r0 + half])
                for i, t in enumerate(tiles):
                    m0 = (li - 1) * LANES + t * SUBLANES
                    part = sc[i * SUBLANES:(i + 1) * SUBLANES] * lmask_ref[m0:m0 + SUBLANES, :]
                    quad[hf][t] = part if quad[hf][t] is None else quad[hf][t] + part
        s00 = jnp.concatenate(quad[0], axis=0)
        s11 = jnp.concatenate(quad[1], axis=0)
        o_top = _dot(s00.astype(BF16), vv[:half])
        o_bot = _dot(jnp.concatenate([s10, s11], axis=1).astype(BF16), vv)
        diag = jnp.sum(qf * kk, axis=-1, keepdims=True)
        o = o + jnp.concatenate([o_top, o_bot], axis=0) + diag * p_i[:, cols]
        state_ref[h] = st * w_incl[ts - 1:ts, :] + _dot_tn(vv, (kk * w_tail).astype(BF16))
        rec_cols.append(o * _rms_scale(o))
    rec_n = (jnp.concatenate(rec_cols, axis=1) * ghg_ref[...] * _silu(p_g)).astype(BF16)

    att_n = (att * _rms_scale(att) * gatt_ref[...]).astype(BF16)
    o_ref[...] = x + _dot(jnp.concatenate([att_n, rec_n], axis=1), wout_ref[...])


def _mixer(x, gmix, w_in, positions, w_out, sinks, gatt, lbl, ghg):
    b, s, _ = x.shape
    ts = MIX_TILE
    tri, lmask = _hgrn_constants()
    invf, ecos, esin = _rotary_constants()
    bias = _attention_bias()
    pos = positions.reshape(b * (s // ts), 1, ts)
    cur = lambda bi, i: (bi, i, 0)
    fixed = lambda bi, i: (0, 0)
    return pl.pallas_call(
        _mixer_kernel,
        grid=(b, s // ts),
        in_specs=[
            pl.BlockSpec(memory_space=pltpu.SMEM),
            pl.BlockSpec((None, ts, D_MODEL), cur),
            pl.BlockSpec((1, D_MODEL), fixed),
            pl.BlockSpec(w_in.shape, fixed, pipeline_mode=pl.Buffered(1)),
            pl.BlockSpec((None, 1, ts), lambda bi, i: (bi * (s // ts) + i, 0, 0)),
            pl.BlockSpec(invf.shape, fixed),
            pl.BlockSpec(ecos.shape, fixed),
            pl.BlockSpec(esin.shape, fixed),
            pl.BlockSpec((D_MODEL, D_MODEL), fixed, pipeline_mode=pl.Buffered(1)),
            pl.BlockSpec((1, ATT_WIDTH), fixed),
            pl.BlockSpec(lbl.shape, fixed),
            pl.BlockSpec((1, HG_WIDTH), fixed),
            pl.BlockSpec(tri.shape, fixed),
            pl.BlockSpec(lmask.shape, fixed),
            pl.BlockSpec(bias.shape, lambda bi, i: (0, 0, 0)),
        ],
        out_specs=pl.BlockSpec((None, ts, D_MODEL), cur),
        out_shape=jax.ShapeDtypeStruct((b, s, D_MODEL), F32),
        scratch_shapes=[
            pltpu.VMEM((HG_HEADS, HG_DIM, HG_DIM), F32),
            pltpu.VMEM((ATT_BLOCK, KV_PAD_WIDTH), BF16),
        ],
        compiler_params=pltpu.CompilerParams(
            dimension_semantics=("arbitrary", "arbitrary"), vmem_limit_bytes=VMEM_LIMIT_BYTES),
        name="mixer",
    )(sinks, x, gmix, w_in, pos, invf, ecos, esin, w_out, gatt, lbl, ghg, tri, lmask, bias)


def _mem_kv_kernel(mem_ref, gain_ref, w_ref, k_ref, v_ref):
    m = mem_ref[...]
    mn = (m * _rms_scale(m) * gain_ref[...]).astype(BF16)
    kv = _dot(mn, w_ref[...])
    k_ref[...] = kv[:, :D_MODEL].astype(BF16)
    v_ref[...] = kv[:, D_MODEL:].astype(BF16)


def _mem_kv(mem, gain, w_xkv):
    b, m, _ = mem.shape
    blk = lambda bi: (bi, 0, 0)
    fixed = lambda bi: (0, 0)
    return pl.pallas_call(
        _mem_kv_kernel,
        grid=(b,),
        in_specs=[
            pl.BlockSpec((None, m, D_MODEL), blk),
            pl.BlockSpec((1, D_MODEL), fixed),
            pl.BlockSpec((D_MODEL, 2 * D_MODEL), fixed),
        ],
        out_specs=[pl.BlockSpec((None, m, D_MODEL), blk), pl.BlockSpec((None, m, D_MODEL), blk)],
        out_shape=[jax.ShapeDtypeStruct((b, m, D_MODEL), BF16)] * 2,
        compiler_params=pltpu.CompilerParams(
            dimension_semantics=("arbitrary",), vmem_limit_bytes=VMEM_LIMIT_BYTES),
        name="mem_kv",
    )(mem, gain, w_xkv)


def _xattn_kernel(x_ref, gain_ref, wq_ref, k_ref, v_ref, wo_ref, o_ref):
    x = x_ref[...]
    hq = (x * _rms_scale(x) * gain_ref[...]).astype(BF16)
    q = _dot(hq, wq_ref[...]).astype(BF16)
    outs = []
    for h in range(X_HEADS):
        cols = slice(h * X_HEAD_DIM, (h + 1) * X_HEAD_DIM)
        s = _dot_nt(q[:, cols], k_ref[:, cols]) * (X_HEAD_DIM ** -0.5)
        p = jnp.exp(s - jnp.max(s, axis=-1, keepdims=True))
        o = _dot(p.astype(BF16), v_ref[:, cols])
        outs.append((o / jnp.sum(p, axis=-1, keepdims=True)).astype(BF16))
    xo = jnp.concatenate(outs, axis=1)
    o_ref[...] = x + _dot(xo, wo_ref[...])


def _xattn(x, gain, w_xq, xk, xv, w_xo, tm):
    b, s, _ = x.shape
    m = xk.shape[1]
    cur = lambda bi, i: (bi, i, 0)
    mem = lambda bi, i: (bi, 0, 0)
    fixed = lambda bi, i: (0, 0)
    return pl.pallas_call(
        _xattn_kernel,
        grid=(b, s // tm),
        in_specs=[
            pl.BlockSpec((None, tm, D_MODEL), cur),
            pl.BlockSpec((1, D_MODEL), fixed),
            pl.BlockSpec((D_MODEL, D_MODEL), fixed),
            pl.BlockSpec((None, m, D_MODEL), mem),
            pl.BlockSpec((None, m, D_MODEL), mem),
            pl.BlockSpec((D_MODEL, D_MODEL), fixed),
        ],
        out_specs=pl.BlockSpec((None, tm, D_MODEL), cur),
        out_shape=jax.ShapeDtypeStruct((b, s, D_MODEL), F32),
        compiler_params=pltpu.CompilerParams(
            dimension_semantics=("arbitrary", "arbitrary"), vmem_limit_bytes=VMEM_LIMIT_BYTES),
        name="xattn",
    )(x, gain, w_xq, xk, xv, w_xo)


FFN_CHUNKS = ((0, 1024), (1024, 2048), (2048, FFN_HIDDEN))


def _ffn_kernel(x_ref, gain_ref, wgu_ref, wd_ref, gfin_ref, o_ref):
    x = x_ref[...]
    hf = (x * _rms_scale(x) * gain_ref[...]).astype(BF16)
    acc = x
    for lo, hi in FFN_CHUNKS:
        gate = _dot(hf, wgu_ref[:, lo:hi])
        upv = _dot(hf, wgu_ref[:, FFN_HIDDEN + lo:FFN_HIDDEN + hi])
        act = (_silu(gate) * upv).astype(BF16)
        acc = acc + _dot(act, wd_ref[lo:hi, :])
    o_ref[...] = acc * _rms_scale(acc) * gfin_ref[...]


def _ffn(x2d, gain, w_gu, w_d, gfin, tm):
    n = x2d.shape[0]
    row = lambda i: (i, 0)
    fixed = lambda i: (0, 0)
    return pl.pallas_call(
        _ffn_kernel,
        grid=(n // tm,),
        in_specs=[
            pl.BlockSpec((tm, D_MODEL), row),
            pl.BlockSpec((1, D_MODEL), fixed),
            pl.BlockSpec((D_MODEL, 2 * FFN_HIDDEN), fixed, pipeline_mode=pl.Buffered(1)),
            pl.BlockSpec((FFN_HIDDEN, D_MODEL), fixed, pipeline_mode=pl.Buffered(1)),
            pl.BlockSpec((1, D_MODEL), fixed),
        ],
        out_specs=pl.BlockSpec((tm, D_MODEL), row),
        out_shape=jax.ShapeDtypeStruct((n, D_MODEL), F32),
        compiler_params=pltpu.CompilerParams(
            dimension_semantics=("arbitrary",), vmem_limit_bytes=VMEM_LIMIT_BYTES),
        name="ffn",
    )(x2d, gain, w_gu, w_d, gfin)


def kernel(x, mem, positions, norm_mix, w_in, att_sinks, att_out_gain, hg_lb_logits, hg_out_gain,
           w_out, norm_xattn, norm_mem, w_xq, w_xkv, w_xo, norm_ffn, w_gate_up, w_down, norm_final):
    b, s, d = x.shape
    n = b * s
    assert w_in.shape[0] == 1 and hg_lb_logits.shape[0] == 2, "single-layer block only"
    row = lambda v: v.reshape(1, -1).astype(F32)
    x = _mixer(x, row(norm_mix[0]), w_in[0].astype(BF16), positions,
               w_out[0].astype(BF16), att_sinks[0].astype(F32), row(att_out_gain[0]),
               hg_lb_logits.astype(F32), row(hg_out_gain[0]))
    xk, xv = _mem_kv(mem, row(norm_mem[0]), w_xkv[0].astype(BF16))
    x = _xattn(x, row(norm_xattn[0]), w_xq[0].astype(BF16), xk, xv, w_xo[0].astype(BF16), tm=1024)
    y = _ffn(x.reshape(n, d), row(norm_ffn[0]), w_gate_up[0].astype(BF16), w_down[0].astype(BF16),
             row(norm_final), tm=1024)
    return y.reshape(b, s, d)
```

```python
import numpy as np
import jax
import jax.numpy as jnp
from jax import lax
from jax.experimental import pallas as pl
from jax.experimental.pallas import tpu as pltpu

F32 = jnp.float32
BF16 = jnp.bfloat16

D_MODEL = 1024
MEM_LEN = 256
ATT_HEADS = 8
ATT_KV_HEADS = 2
ATT_GROUP = ATT_HEADS // ATT_KV_HEADS
ATT_HEAD_DIM = 64
ATT_WIDTH = ATT_HEADS * ATT_HEAD_DIM
ATT_KV_WIDTH = ATT_KV_HEADS * ATT_HEAD_DIM
ATT_BLOCK = 128
ROPE_THETA = 500000.0
ROPE_DIM = ATT_HEAD_DIM // 4
ROPE_HALF = ROPE_DIM // 2
HG_HEADS = 4
HG_DIM = 128
HG_WIDTH = HG_HEADS * HG_DIM
X_HEADS = 4
X_HEAD_DIM = D_MODEL // X_HEADS
FFN_HIDDEN = 2816
RMS_EPS = 1e-6
LANES = 128
SUBLANES = 8
MASK_VALUE = -1e30
LOG2_E = 1.4426950408889634

MIX_TILE = 256
HG_LEVELS = (128, 64, 32, 16, 8, 4, 2, 1)
KV_PAD_WIDTH = 2 * ATT_KV_HEADS * 2 * LANES

VMEM_LIMIT_BYTES = 56 * 1024 * 1024


def _rms_scale(x):
    return lax.rsqrt(jnp.mean(x * x, axis=-1, keepdims=True) + RMS_EPS)


def _silu(x):
    h = 0.5 * x
    return h + h * jnp.tanh(h)


def _dot(a, b):
    return jnp.dot(a, b, preferred_element_type=F32)


def _dot_nt(a, b):
    return lax.dot_general(a, b, (((1,), (1,)), ((), ())), preferred_element_type=F32)


def _dot_tn(a, b):
    return lax.dot_general(a, b, (((0,), (0,)), ((), ())), preferred_element_type=F32)


def _hgrn_constants():
    r = np.arange(MIX_TILE)[:, None]
    j = np.arange(MIX_TILE)[None, :]
    tri = (j <= r).astype(np.float32)
    rr = np.arange(LANES)[:, None]
    jj = np.arange(LANES)[None, :]
    masks = []
    for m in HG_LEVELS:
        if 2 * m <= LANES:
            masks.append(((rr // (2 * m)) == (jj // (2 * m))) & ((rr % (2 * m)) >= m) & ((jj % (2 * m)) < m))
    lmask = np.concatenate(masks, axis=0).astype(np.float32)
    return jnp.asarray(tri, BF16), jnp.asarray(lmask, F32)


def _attention_bias():
    i = (np.arange(2 * ATT_BLOCK) % ATT_BLOCK)[:, None]
    j = np.arange(2 * ATT_BLOCK)[None, :]
    band = (j > i) & (j <= i + ATT_BLOCK)
    variants = (band, band & (j >= ATT_BLOCK))
    return jnp.asarray(np.stack([np.where(v | (j == 0), 0.0, MASK_VALUE) for v in variants]), F32)


def _rotary_constants():
    inv_freq = jnp.power(jnp.float32(ROPE_THETA),
                         -jnp.arange(ROPE_HALF, dtype=F32) * (2.0 / ROPE_DIM))
    invf = jnp.broadcast_to(inv_freq[:, None], (ROPE_HALF, MIX_TILE))
    lane = np.arange(LANES)
    in_head = lane % ATT_HEAD_DIM
    freq_row = (np.arange(ROPE_HALF)[:, None] == (in_head % ROPE_HALF)[None, :])
    rot = freq_row & (in_head < ROPE_DIM)[None, :]
    ecos = np.zeros((4 * ROPE_HALF, LANES), np.float32)
    ecos[:ROPE_HALF] = rot
    ecos[ROPE_HALF:2 * ROPE_HALF] = rot
    ecos[2 * ROPE_HALF] = in_head >= ROPE_DIM
    sign = np.where(in_head < ROPE_HALF, -1.0, 1.0)[None, :]
    esin = np.concatenate([rot * sign, rot * sign], axis=0).astype(np.float32)
    return invf, jnp.asarray(ecos, BF16), jnp.asarray(esin, BF16)


def _level_decay(m, b, f, row_ids):
    ts = b.shape[0]
    if m >= SUBLANES // 2:
        b3 = b.reshape(ts // (2 * m), 2 * m, HG_DIM)
        ref = b3[:, m - 1:m, :]
        if m >= SUBLANES:
            e3 = jnp.concatenate([ref - b3[:, :m, :], b3[:, m:, :] - ref], axis=1)
        else:
            e3 = -jnp.abs(b3 - ref)
        return jnp.exp2(e3).reshape(ts, HG_DIM)
    if m == 2:
        pos = row_ids & 3
        f_next = pltpu.roll(f, ts - 1, axis=0)
        f_prev = pltpu.roll(f, 1, axis=0)
        return jnp.where(pos == 0, f_next, jnp.where(pos == 1, 1.0, jnp.where(pos == 2, f, f * f_prev)))
    assert m == 1
    return jnp.where((row_ids & 1) != 0, f, 1.0)


def _select_rows(m, upper_src, lower_src, row_ids):
    c = upper_src.shape[0]
    if m >= SUBLANES:
        pieces = []
        for b0 in range(0, c, 2 * m):
            pieces.append(lower_src[b0:b0 + m])
            pieces.append(upper_src[b0 + m:b0 + 2 * m])
        return jnp.concatenate(pieces, axis=0)
    return jnp.where((row_ids & m) != 0, upper_src, lower_src)


def _rotary_tables(pos_ref, invf_ref, ecos_ref, esin_ref):
    ang = invf_ref[...] * pos_ref[...].astype(F32)

    def spread(t, e_ref, extra):
        t_hi = t.astype(BF16).astype(F32)
        rows = jnp.concatenate([t_hi, t - t_hi] + extra, axis=0).astype(BF16)
        return _dot_tn(rows, e_ref[...])

    ones = jnp.ones_like(ang)
    cos_t = spread(jnp.cos(ang), ecos_ref, [ones, jnp.zeros_like(ang)])
    sin_t = spread(jnp.sin(ang), esin_ref, [])
    return cos_t, sin_t


def _mixer_kernel(sinks_ref, x_ref, gmix_ref, win_ref, pos_ref, invf_ref, ecos_ref, esin_ref, wout_ref,
                  gatt_ref, lbl_ref, ghg_ref, tri_ref, lmask_ref, bias_ref, o_ref, state_ref, kvprev_ref):
    step = pl.program_id(1)
    ts = MIX_TILE
    blk = ATT_BLOCK

    @pl.when(step == 0)
    def _():
        state_ref[...] = jnp.zeros_like(state_ref)
        kvprev_ref[...] = jnp.zeros_like(kvprev_ref)

    x = x_ref[...]
    hn = (x * _rms_scale(x) * gmix_ref[...]).astype(BF16)

    def proj(c0, c1):
        return _dot(hn, win_ref[:, c0:c1])

    cos, sin = _rotary_tables(pos_ref, invf_ref, ecos_ref, esin_ref)
    lane = lax.broadcasted_iota(jnp.int32, cos.shape, 1)
    first_half = (lane & (ATT_HEAD_DIM - 1)) < ROPE_HALF
    lo_head = lane < ATT_HEAD_DIM

    def rotate(col):
        partner = jnp.where(first_half,
                            pltpu.roll(col, LANES - ROPE_HALF, axis=1),
                            pltpu.roll(col, ROPE_HALF, axis=1))
        return col * cos + partner * sin

    def padded(col):
        swapped = pltpu.roll(col, ATT_HEAD_DIM, axis=1)
        zero = jnp.zeros_like(col)
        return [v.astype(BF16) for v in (jnp.where(lo_head, col, zero), jnp.where(lo_head, zero, swapped),
                                         jnp.where(lo_head, swapped, zero), jnp.where(lo_head, zero, col))]

    pq = proj(0, ATT_WIDTH)
    q_cols = [(rotate(pq[:, j * LANES:(j + 1) * LANES]) * (ATT_HEAD_DIM ** -0.5 * LOG2_E)).astype(BF16)
              for j in range(ATT_WIDTH // LANES)]
    pkv = proj(ATT_WIDTH, ATT_WIDTH + 2 * ATT_KV_WIDTH)
    kv_new = padded(rotate(pkv[:, :LANES])) + padded(pkv[:, LANES:])
    kv_all = [jnp.concatenate([kvprev_ref[:, i * LANES:(i + 1) * LANES], v], axis=0)
              for i, v in enumerate(kv_new)]
    for i, v in enumerate(kv_new):
        kvprev_ref[:, i * LANES:(i + 1) * LANES] = v[ts - blk:]
    r0c = ATT_WIDTH + 2 * ATT_KV_WIDTH
    p_q, p_f, p_i, p_g = (proj(r0c + k * HG_WIDTH, r0c + (k + 1) * HG_WIDTH) for k in range(4))

    upper_half_rows = lax.broadcasted_iota(jnp.int32, (2 * blk, 1), 0) >= blk
    lane_ids = lax.broadcasted_iota(jnp.int32, (2 * blk, LANES), 1)
    lane_lo = lane_ids < ATT_HEAD_DIM
    ones_lo = jnp.where(lane_lo, 1.0, 0.0).astype(BF16)
    ones_hi = jnp.where(lane_lo, 0.0, 1.0).astype(BF16)
    bf16_rows = 2 * SUBLANES
    drop_row0 = jnp.where(lax.broadcasted_iota(jnp.int32, (bf16_rows, LANES), 0) > 0, 1.0, 0.0).astype(BF16)

    def sink_slot(op):
        return jnp.concatenate([op[:bf16_rows] * drop_row0, op[bf16_rows:]], axis=0)

    att_units = {}

    def att_scores(jb, gk):
        keys = slice(jb * blk, (jb + 2) * blk)
        bias = bias_ref[jnp.where(step > 0, 0, 1)] if jb == 0 else bias_ref[0]
        q2 = jnp.concatenate([q_cols[2 * gk][jb * blk:(jb + 1) * blk],
                              q_cols[2 * gk + 1][jb * blk:(jb + 1) * blk]], axis=0)
        k_lo = sink_slot(kv_all[2 * gk][keys])
        k_hi = sink_slot(kv_all[2 * gk + 1][keys])
        h0 = ATT_GROUP * gk
        probs = []
        for which, k_op in ((0, k_lo), (1, k_hi)):
            sink = jnp.where(upper_half_rows, sinks_ref[h0 + 2 + which], sinks_ref[h0 + which]) * LOG2_E
            s = _dot_nt(q2, k_op) + bias
            s_first = s[:, :LANES] + jnp.where(lane_ids == 0, sink, 0.0)
            m = jnp.max(jnp.maximum(s_first, s[:, LANES:]), axis=-1, keepdims=True)
            probs.append(jnp.exp2(jnp.concatenate([s_first, s[:, LANES:]], axis=1) - m).astype(BF16))
        att_units[jb, gk] = jnp.concatenate(probs, axis=1)

    def att_values(jb, gk):
        keys = slice(jb * blk, (jb + 2) * blk)
        r_lo = jnp.concatenate([sink_slot(kv_all[4 + 2 * gk][keys]), ones_lo], axis=1)
        r_hi = jnp.concatenate([sink_slot(kv_all[5 + 2 * gk][keys]), ones_hi], axis=1)
        out = _dot(att_units[jb, gk], jnp.concatenate([r_lo, r_hi], axis=0))
        att_units[jb, gk] = out[:, :LANES] / out[:, LANES:]

    units = [(jb, gk) for jb in range(ts // blk) for gk in range(ATT_KV_HEADS)]
    att_scores(*units[0])
    for prev_u, u in zip(units[:-1], units[1:]):
        att_scores(*u)
        att_values(*prev_u)
    att_values(*units[-1])
    att_rows = []
    for jb in range(ts // blk):
        cols_ = []
        for gk in range(ATT_KV_HEADS):
            cols_ += [att_units[jb, gk][:blk], att_units[jb, gk][blk:]]
        att_rows.append(jnp.concatenate(cols_, axis=1))
    att = jnp.concatenate(att_rows, axis=0)

    lbl = lbl_ref[...]
    lexp = jnp.exp(lbl - jnp.max(lbl, axis=0, keepdims=True))
    lb = lexp[0:1, :] / jnp.sum(lexp, axis=0, keepdims=True)
    tri = tri_ref[...]
    row_ids = lax.broadcasted_iota(jnp.int32, (ts, HG_DIM), 0)
    half = ts // 2
    f_all = 0.5 * (1.0 + lb) + (0.5 * (1.0 - lb)) * jnp.tanh(0.5 * p_f)
    g_all = jnp.log(f_all) * LOG2_E
    g_hi = g_all.astype(BF16)
    g_lo = (g_all - g_hi.astype(F32)).astype(BF16)
    b2 = _dot(tri, jnp.concatenate([g_hi, g_lo], axis=1))
    b_all = b2[:, :HG_WIDTH] + b2[:, HG_WIDTH:]
    n_tiles = half // SUBLANES
    hs = [dict() for _ in range(HG_HEADS)]

    def prep(h):
        cols = slice(h * HG_DIM, (h + 1) * HG_DIM)
        f = f_all[:, cols]
        b = b_all[:, cols]
        kk = 1.0 - f
        qf = _silu(p_q[:, cols])
        vv = p_i[:, cols].astype(BF16)
        w_incl = jnp.exp2(b)
        w_tail = jnp.exp2(b[ts - 1:ts, :] - b)
        st = state_ref[h]
        o = _dot_nt((qf * w_incl).astype(BF16), st.astype(BF16))
        state_ref[h] = st * w_incl[ts - 1:ts, :] + _dot_tn(vv, (kk * w_tail).astype(BF16))
        hs[h].update(f=f, b=b, kk=kk, qf=qf, vv=vv, o=o, cols=cols)

    def levels(h):
        d = hs[h]
        qf, kk, b, f = d["qf"], d["kk"], d["b"], d["f"]
        quad = [[None] * n_tiles, [None] * n_tiles]
        for li, m in enumerate(HG_LEVELS):
            x_f = _select_rows(m, qf, kk, row_ids) * _level_decay(m, b, f, row_ids)
            x_l = x_f.astype(BF16)
            if 2 * m == ts:
                d["s10"] = _dot_nt(x_l[half:], x_l[:half])
                continue
            if m >= SUBLANES:
                tiles = [t for t in range(n_tiles) if (t * SUBLANES // m) % 2 == 1]
            else:
                tiles = list(range(n_tiles))
            for hf in range(2):
                r0 = hf * half
                if len(tiles) == n_tiles:
                    lhs = x_l[r0:r0 + half]
                else:
                    lhs = jnp.concatenate([x_f[r0 + t * SUBLANES:r0 + (t + 1) * SUBLANES] for t in tiles],
                                          axis=0).astype(BF16)
                sc = _dot_nt(lhs, x_l[r0:r0 + half])
                for i, t in enumerate(tiles):
                    m0 = (li - 1) * LANES + t * SUBLANES
                    part = sc[i * SUBLANES:(i + 1) * SUBLANES] * lmask_ref[m0:m0 + SUBLANES, :]
                    quad[hf][t] = part if quad[hf][t] is None else quad[hf][t] + part
        d["s00"] = jnp.concatenate(quad[0], axis=0).astype(BF16)
        d["s1"] = jnp.concatenate([d["s10"], jnp.concatenate(quad[1], axis=0)], axis=1).astype(BF16)

    def finish(h):
        d = hs[h]
        vv = d["vv"]
        o_top = _dot(d["s00"], vv[:half])
        o_bot = _dot(d["s1"], vv)
        diag = jnp.sum(d["qf"] * d["kk"], axis=-1, keepdims=True)
        o = d["o"] + jnp.concatenate([o_top, o_bot], axis=0) + diag * p_i[:, d["cols"]]
        d["rec"] = o * _rms_scale(o)

    prep(0)
    prep(1)
    levels(0)
    prep(2)
    levels(1)
    prep(3)
    levels(2)
    finish(0)
    levels(3)
    finish(1)
    finish(2)
    finish(3)
    rec_cols = [hs[h]["rec"] for h in range(HG_HEADS)]
    rec_n = (jnp.concatenate(rec_cols, axis=1) * ghg_ref[...] * _silu(p_g)).astype(BF16)

    att_n = (att * _rms_scale(att) * gatt_ref[...]).astype(BF16)
    o_ref[...] = x + _dot(jnp.concatenate([att_n, rec_n], axis=1), wout_ref[...])


def _mixer(x, gmix, w_in, positions, w_out, sinks, gatt, lbl, ghg):
    b, s, _ = x.shape
    ts = MIX_TILE
    tri, lmask = _hgrn_constants()
    invf, ecos, esin = _rotary_constants()
    bias = _attention_bias()
    pos = positions.reshape(b * (s // ts), 1, ts)
    cur = lambda bi, i: (bi, i, 0)
    fixed = lambda bi, i: (0, 0)
    return pl.pallas_call(
        _mixer_kernel,
        grid=(b, s // ts),
        in_specs=[
            pl.BlockSpec(memory_space=pltpu.SMEM),
            pl.BlockSpec((None, ts, D_MODEL), cur),
            pl.BlockSpec((1, D_MODEL), fixed),
            pl.BlockSpec(w_in.shape, fixed, pipeline_mode=pl.Buffered(1)),
            pl.BlockSpec((None, 1, ts), lambda bi, i: (bi * (s // ts) + i, 0, 0)),
            pl.BlockSpec(invf.shape, fixed),
            pl.BlockSpec(ecos.shape, fixed),
            pl.BlockSpec(esin.shape, fixed),
            pl.BlockSpec((D_MODEL, D_MODEL), fixed, pipeline_mode=pl.Buffered(1)),
            pl.BlockSpec((1, ATT_WIDTH), fixed),
            pl.BlockSpec(lbl.shape, fixed),
            pl.BlockSpec((1, HG_WIDTH), fixed),
            pl.BlockSpec(tri.shape, fixed),
            pl.BlockSpec(lmask.shape, fixed),
            pl.BlockSpec(bias.shape, lambda bi, i: (0, 0, 0)),
        ],
        out_specs=pl.BlockSpec((None, ts, D_MODEL), cur),
        out_shape=jax.ShapeDtypeStruct((b, s, D_MODEL), F32),
        scratch_shapes=[
            pltpu.VMEM((HG_HEADS, HG_DIM, HG_DIM), F32),
            pltpu.VMEM((ATT_BLOCK, KV_PAD_WIDTH), BF16),
        ],
        compiler_params=pltpu.CompilerParams(
            dimension_semantics=("arbitrary", "arbitrary"), vmem_limit_bytes=VMEM_LIMIT_BYTES),
        name="mixer",
    )(sinks, x, gmix, w_in, pos, invf, ecos, esin, w_out, gatt, lbl, ghg, tri, lmask, bias)


def _mem_kv_kernel(mem_ref, gain_ref, w_ref, k_ref, v_ref):
    m = mem_ref[...]
    mn = (m * _rms_scale(m) * gain_ref[...]).astype(BF16)
    kv = _dot(mn, w_ref[...])
    k_ref[...] = kv[:, :D_MODEL].astype(BF16)
    v_ref[...] = kv[:, D_MODEL:].astype(BF16)


def _mem_kv(mem, gain, w_xkv):
    b, m, _ = mem.shape
    blk = lambda bi: (bi, 0, 0)
    fixed = lambda bi: (0, 0)
    return pl.pallas_call(
        _mem_kv_kernel,
        grid=(b,),
        in_specs=[
            pl.BlockSpec((None, m, D_MODEL), blk),
            pl.BlockSpec((1, D_MODEL), fixed),
            pl.BlockSpec((D_MODEL, 2 * D_MODEL), fixed),
        ],
        out_specs=[pl.BlockSpec((None, m, D_MODEL), blk), pl.BlockSpec((None, m, D_MODEL), blk)],
        out_shape=[jax.ShapeDtypeStruct((b, m, D_MODEL), BF16)] * 2,
        compiler_params=pltpu.CompilerParams(
            dimension_semantics=("arbitrary",), vmem_limit_bytes=VMEM_LIMIT_BYTES),
        name="mem_kv",
    )(mem, gain, w_xkv)


def _xattn_kernel(x_ref, gain_ref, wq_ref, k_ref, v_ref, wo_ref, o_ref):
    x = x_ref[...]
    hq = (x * _rms_scale(x) * gain_ref[...]).astype(BF16)
    q = _dot(hq, wq_ref[...]).astype(BF16)
    outs = []
    for h in range(X_HEADS):
        cols = slice(h * X_HEAD_DIM, (h + 1) * X_HEAD_DIM)
        s = _dot_nt(q[:, cols], k_ref[:, cols]) * (X_HEAD_DIM ** -0.5)
        p = jnp.exp(s - jnp.max(s, axis=-1, keepdims=True))
        o = _dot(p.astype(BF16), v_ref[:, cols])
        outs.append((o / jnp.sum(p, axis=-1, keepdims=True)).astype(BF16))
    xo = jnp.concatenate(outs, axis=1)
    o_ref[...] = x + _dot(xo, wo_ref[...])


def _xattn(x, gain, w_xq, xk, xv, w_xo, tm):
    b, s, _ = x.shape
    m = xk.shape[1]
    cur = lambda bi, i: (bi, i, 0)
    mem = lambda bi, i: (bi, 0, 0)
    fixed = lambda bi, i: (0, 0)
    return pl.pallas_call(
        _xattn_kernel,
        grid=(b, s // tm),
        in_specs=[
            pl.BlockSpec((None, tm, D_MODEL), cur),
            pl.BlockSpec((1, D_MODEL), fixed),
            pl.BlockSpec((D_MODEL, D_MODEL), fixed),
            pl.BlockSpec((None, m, D_MODEL), mem),
            pl.BlockSpec((None, m, D_MODEL), mem),
            pl.BlockSpec((D_MODEL, D_MODEL), fixed),
        ],
        out_specs=pl.BlockSpec((None, tm, D_MODEL), cur),
        out_shape=jax.ShapeDtypeStruct((b, s, D_MODEL), F32),
        compiler_params=pltpu.CompilerParams(
            dimension_semantics=("arbitrary", "arbitrary"), vmem_limit_bytes=VMEM_LIMIT_BYTES),
        name="xattn",
    )(x, gain, w_xq, xk, xv, w_xo)


FFN_CHUNKS = ((0, 1024), (1024, 2048), (2048, FFN_HIDDEN))


def _ffn_kernel(x_ref, gain_ref, wgu_ref, wd_ref, gfin_ref, o_ref):
    x = x_ref[...]
    hf = (x * _rms_scale(x) * gain_ref[...]).astype(BF16)
    acc = x
    for lo, hi in FFN_CHUNKS:
        gate = _dot(hf, wgu_ref[:, lo:hi])
        upv = _dot(hf, wgu_ref[:, FFN_HIDDEN + lo:FFN_HIDDEN + hi])
        act = (_silu(gate) * upv).astype(BF16)
        acc = acc + _dot(act, wd_ref[lo:hi, :])
    o_ref[...] = acc * _rms_scale(acc) * gfin_ref[...]


def _ffn(x2d, gain, w_gu, w_d, gfin, tm):
    n = x2d.shape[0]
    row = lambda i: (i, 0)
    fixed = lambda i: (0, 0)
    return pl.pallas_call(
        _ffn_kernel,
        grid=(n // tm,),
        in_specs=[
            pl.BlockSpec((tm, D_MODEL), row),
            pl.BlockSpec((1, D_MODEL), fixed),
            pl.BlockSpec((D_MODEL, 2 * FFN_HIDDEN), fixed, pipeline_mode=pl.Buffered(1)),
            pl.BlockSpec((FFN_HIDDEN, D_MODEL), fixed, pipeline_mode=pl.Buffered(1)),
            pl.BlockSpec((1, D_MODEL), fixed),
        ],
        out_specs=pl.BlockSpec((tm, D_MODEL), row),
        out_shape=jax.ShapeDtypeStruct((n, D_MODEL), F32),
        compiler_params=pltpu.CompilerParams(
            dimension_semantics=("arbitrary",), vmem_limit_bytes=VMEM_LIMIT_BYTES),
        name="ffn",
    )(x2d, gain, w_gu, w_d, gfin)


def kernel(x, mem, positions, norm_mix, w_in, att_sinks, att_out_gain, hg_lb_logits, hg_out_gain,
           w_out, norm_xattn, norm_mem, w_xq, w_xkv, w_xo, norm_ffn, w_gate_up, w_down, norm_final):
    b, s, d = x.shape
    n = b * s
    assert w_in.shape[0] == 1 and hg_lb_logits.shape[0] == 2, "single-layer block only"
    row = lambda v: v.reshape(1, -1).astype(F32)
    x = _mixer(x, row(norm_mix[0]), w_in[0].astype(BF16), positions,
               w_out[0].astype(BF16), att_sinks[0].astype(F32), row(att_out_gain[0]),
               hg_lb_logits.astype(F32), row(hg_out_gain[0]))
    xk, xv = _mem_kv(mem, row(norm_mem[0]), w_xkv[0].astype(BF16))
    x = _xattn(x, row(norm_xattn[0]), w_xq[0].astype(BF16), xk, xv, w_xo[0].astype(BF16), tm=1024)
    y = _ffn(x.reshape(n, d), row(norm_ffn[0]), w_gate_up[0].astype(BF16), w_down[0].astype(BF16),
             row(norm_final), tm=1024)
    return y.reshape(b, s, d)
```

```python
import numpy as np
import jax
import jax.numpy as jnp
from jax import lax
from jax.experimental import pallas as pl
from jax.experimental.pallas import tpu as pltpu

F32 = jnp.float32
BF16 = jnp.bfloat16

D_MODEL = 1024
MEM_LEN = 256
ATT_HEADS = 8
ATT_KV_HEADS = 2
ATT_GROUP = ATT_HEADS // ATT_KV_HEADS
ATT_HEAD_DIM = 64
ATT_WIDTH = ATT_HEADS * ATT_HEAD_DIM
ATT_KV_WIDTH = ATT_KV_HEADS * ATT_HEAD_DIM
ATT_BLOCK = 128
ROPE_THETA = 500000.0
ROPE_DIM = ATT_HEAD_DIM // 4
ROPE_HALF = ROPE_DIM // 2
HG_HEADS = 4
HG_DIM = 128
HG_WIDTH = HG_HEADS * HG_DIM
X_HEADS = 4
X_HEAD_DIM = D_MODEL // X_HEADS
FFN_HIDDEN = 2816
RMS_EPS = 1e-6
LANES = 128
SUBLANES = 8
MASK_VALUE = -1e30
LOG2_E = 1.4426950408889634

MIX_TILE = 256
HG_LEVELS = (128, 64, 32, 16, 8, 4, 2, 1)
KV_PAD_WIDTH = 2 * ATT_KV_HEADS * 2 * LANES

VMEM_LIMIT_BYTES = 56 * 1024 * 1024


def _rms_scale(x):
    return lax.rsqrt(jnp.mean(x * x, axis=-1, keepdims=True) + RMS_EPS)


def _silu(x):
    h = 0.5 * x
    return h + h * jnp.tanh(h)


def _dot(a, b):
    return jnp.dot(a, b, preferred_element_type=F32)


def _dot_nt(a, b):
    return lax.dot_general(a, b, (((1,), (1,)), ((), ())), preferred_element_type=F32)


def _dot_tn(a, b):
    return lax.dot_general(a, b, (((0,), (0,)), ((), ())), preferred_element_type=F32)


def _stage_weight(src_hbm, dst_ref, stage_ref, sem_ref, axis):
    size = stage_ref.shape[1 + axis]
    n_chunks = dst_ref.shape[axis] // size
    assert n_chunks * size == dst_ref.shape[axis] and src_hbm.shape == dst_ref.shape

    def window(c):
        return (pl.ds(c * size, size), slice(None)) if axis == 0 else (slice(None), pl.ds(c * size, size))

    def copy(c):
        return pltpu.make_async_copy(src_hbm.at[window(c)], stage_ref.at[c % 2], sem_ref.at[c % 2])

    copy(0).start()
    for c in range(n_chunks):
        if c + 1 < n_chunks:
            copy(c + 1).start()
        copy(c).wait()
        dst_ref[window(c)] = stage_ref[c % 2].astype(BF16)


def _hgrn_constants():
    r = np.arange(MIX_TILE)[:, None]
    j = np.arange(MIX_TILE)[None, :]
    tri = (j <= r).astype(np.float32)
    rr = np.arange(LANES)[:, None]
    jj = np.arange(LANES)[None, :]
    masks = []
    for m in HG_LEVELS:
        if 2 * m <= LANES:
            masks.append(((rr // (2 * m)) == (jj // (2 * m))) & ((rr % (2 * m)) >= m) & ((jj % (2 * m)) < m))
    lmask = np.concatenate(masks, axis=0).astype(np.float32)
    return jnp.asarray(tri, BF16), jnp.asarray(lmask, F32)


def _attention_bias():
    i = (np.arange(2 * ATT_BLOCK) % ATT_BLOCK)[:, None]
    j = np.arange(2 * ATT_BLOCK)[None, :]
    band = (j > i) & (j <= i + ATT_BLOCK)
    variants = (band, band & (j >= ATT_BLOCK))
    return jnp.asarray(np.stack([np.where(v | (j == 0), 0.0, MASK_VALUE) for v in variants]), F32)


def _rotary_constants():
    inv_freq = jnp.power(jnp.float32(ROPE_THETA),
                         -jnp.arange(ROPE_HALF, dtype=F32) * (2.0 / ROPE_DIM))
    invf = jnp.broadcast_to(inv_freq[:, None], (ROPE_HALF, MIX_TILE))
    lane = np.arange(LANES)
    in_head = lane % ATT_HEAD_DIM
    freq_row = (np.arange(ROPE_HALF)[:, None] == (in_head % ROPE_HALF)[None, :])
    rot = freq_row & (in_head < ROPE_DIM)[None, :]
    ecos = np.zeros((4 * ROPE_HALF, LANES), np.float32)
    ecos[:ROPE_HALF] = rot
    ecos[ROPE_HALF:2 * ROPE_HALF] = rot
    ecos[2 * ROPE_HALF] = in_head >= ROPE_DIM
    sign = np.where(in_head < ROPE_HALF, -1.0, 1.0)[None, :]
    esin = np.concatenate([rot * sign, rot * sign], axis=0).astype(np.float32)
    return invf, jnp.asarray(ecos, BF16), jnp.asarray(esin, BF16)


def _level_decay(m, b, f, row_ids):
    ts = b.shape[0]
    if m >= SUBLANES // 2:
        b3 = b.reshape(ts // (2 * m), 2 * m, HG_DIM)
        ref = b3[:, m - 1:m, :]
        if m >= SUBLANES:
            e3 = jnp.concatenate([ref - b3[:, :m, :], b3[:, m:, :] - ref], axis=1)
        else:
            e3 = -jnp.abs(b3 - ref)
        return jnp.exp2(e3).reshape(ts, HG_DIM)
    if m == 2:
        pos = row_ids & 3
        f_next = pltpu.roll(f, ts - 1, axis=0)
        f_prev = pltpu.roll(f, 1, axis=0)
        return jnp.where(pos == 0, f_next, jnp.where(pos == 1, 1.0, jnp.where(pos == 2, f, f * f_prev)))
    assert m == 1
    return jnp.where((row_ids & 1) != 0, f, 1.0)


def _select_rows(m, upper_src, lower_src, row_ids):
    c = upper_src.shape[0]
    if m >= SUBLANES:
        pieces = []
        for b0 in range(0, c, 2 * m):
            pieces.append(lower_src[b0:b0 + m])
            pieces.append(upper_src[b0 + m:b0 + 2 * m])
        return jnp.concatenate(pieces, axis=0)
    return jnp.where((row_ids & m) != 0, upper_src, lower_src)


def _rotary_tables(pos_ref, invf_ref, ecos_ref, esin_ref):
    ang = invf_ref[...] * pos_ref[...].astype(F32)

    def spread(t, e_ref, extra):
        t_hi = t.astype(BF16).astype(F32)
        rows = jnp.concatenate([t_hi, t - t_hi] + extra, axis=0).astype(BF16)
        return _dot_tn(rows, e_ref[...])

    ones = jnp.ones_like(ang)
    cos_t = spread(jnp.cos(ang), ecos_ref, [ones, jnp.zeros_like(ang)])
    sin_t = spread(jnp.sin(ang), esin_ref, [])
    return cos_t, sin_t


def _mixer_kernel(sinks_ref, x_ref, gmix_ref, win_ref, pos_ref, invf_ref, ecos_ref, esin_ref, wout_ref,
                  gatt_ref, lbl_ref, ghg_ref, tri_ref, lmask_ref, bias_ref, o_ref, state_ref, kvprev_ref):
    step = pl.program_id(1)
    ts = MIX_TILE
    blk = ATT_BLOCK

    @pl.when(step == 0)
    def _():
        state_ref[...] = jnp.zeros_like(state_ref)
        kvprev_ref[...] = jnp.zeros_like(kvprev_ref)

    x = x_ref[...]
    hn = (x * _rms_scale(x) * gmix_ref[...]).astype(BF16)

    def proj(c0, c1):
        return _dot(hn, win_ref[:, c0:c1])

    cos, sin = _rotary_tables(pos_ref, invf_ref, ecos_ref, esin_ref)
    lane = lax.broadcasted_iota(jnp.int32, cos.shape, 1)
    first_half = (lane & (ATT_HEAD_DIM - 1)) < ROPE_HALF
    lo_head = lane < ATT_HEAD_DIM

    def rotate(col):
        partner = jnp.where(first_half,
                            pltpu.roll(col, LANES - ROPE_HALF, axis=1),
                            pltpu.roll(col, ROPE_HALF, axis=1))
        return col * cos + partner * sin

    def padded(col):
        swapped = pltpu.roll(col, ATT_HEAD_DIM, axis=1)
        zero = jnp.zeros_like(col)
        return [v.astype(BF16) for v in (jnp.where(lo_head, col, zero), jnp.where(lo_head, zero, swapped),
                                         jnp.where(lo_head, swapped, zero), jnp.where(lo_head, zero, col))]

    pq = proj(0, ATT_WIDTH)
    q_cols = [(rotate(pq[:, j * LANES:(j + 1) * LANES]) * (ATT_HEAD_DIM ** -0.5 * LOG2_E)).astype(BF16)
              for j in range(ATT_WIDTH // LANES)]
    pkv = proj(ATT_WIDTH, ATT_WIDTH + 2 * ATT_KV_WIDTH)
    kv_new = padded(rotate(pkv[:, :LANES])) + padded(pkv[:, LANES:])
    kv_all = [jnp.concatenate([kvprev_ref[:, i * LANES:(i + 1) * LANES], v], axis=0)
              for i, v in enumerate(kv_new)]
    for i, v in enumerate(kv_new):
        kvprev_ref[:, i * LANES:(i + 1) * LANES] = v[ts - blk:]
    r0c = ATT_WIDTH + 2 * ATT_KV_WIDTH
    p_q, p_f, p_i, p_g = (proj(r0c + k * HG_WIDTH, r0c + (k + 1) * HG_WIDTH) for k in range(4))

    upper_half_rows = lax.broadcasted_iota(jnp.int32, (2 * blk, 1), 0) >= blk
    lane_ids = lax.broadcasted_iota(jnp.int32, (2 * blk, LANES), 1)
    lane_lo = lane_ids < ATT_HEAD_DIM
    ones_lo = jnp.where(lane_lo, 1.0, 0.0).astype(BF16)
    ones_hi = jnp.where(lane_lo, 0.0, 1.0).astype(BF16)
    bf16_rows = 2 * SUBLANES
    drop_row0 = jnp.where(lax.broadcasted_iota(jnp.int32, (bf16_rows, LANES), 0) > 0, 1.0, 0.0).astype(BF16)

    def sink_slot(op):
        return jnp.concatenate([op[:bf16_rows] * drop_row0, op[bf16_rows:]], axis=0)

    att_units = {}

    def att_scores(jb, gk):
        keys = slice(jb * blk, (jb + 2) * blk)
        bias = bias_ref[jnp.where(step > 0, 0, 1)] if jb == 0 else bias_ref[0]
        q2 = jnp.concatenate([q_cols[2 * gk][jb * blk:(jb + 1) * blk],
                              q_cols[2 * gk + 1][jb * blk:(jb + 1) * blk]], axis=0)
        k_lo = sink_slot(kv_all[2 * gk][keys])
        k_hi = sink_slot(kv_all[2 * gk + 1][keys])
        h0 = ATT_GROUP * gk
        probs = []
        for which, k_op in ((0, k_lo), (1, k_hi)):
            sink = jnp.where(upper_half_rows, sinks_ref[h0 + 2 + which], sinks_ref[h0 + which]) * LOG2_E
            s = _dot_nt(q2, k_op) + bias
            s_first = s[:, :LANES] + jnp.where(lane_ids == 0, sink, 0.0)
            m = jnp.max(jnp.maximum(s_first, s[:, LANES:]), axis=-1, keepdims=True)
            probs.append(jnp.exp2(jnp.concatenate([s_first, s[:, LANES:]], axis=1) - m).astype(BF16))
        att_units[jb, gk] = jnp.concatenate(probs, axis=1)

    def att_values(jb, gk):
        keys = slice(jb * blk, (jb + 2) * blk)
        r_lo = jnp.concatenate([sink_slot(kv_all[4 + 2 * gk][keys]), ones_lo], axis=1)
        r_hi = jnp.concatenate([sink_slot(kv_all[5 + 2 * gk][keys]), ones_hi], axis=1)
        out = _dot(att_units[jb, gk], jnp.concatenate([r_lo, r_hi], axis=0))
        att_units[jb, gk] = out[:, :LANES] / out[:, LANES:]

    units = [(jb, gk) for jb in range(ts // blk) for gk in range(ATT_KV_HEADS)]
    att_scores(*units[0])
    for prev_u, u in zip(units[:-1], units[1:]):
        att_scores(*u)
        att_values(*prev_u)
    att_values(*units[-1])
    att_rows = []
    for jb in range(ts // blk):
        cols_ = []
        for gk in range(ATT_KV_HEADS):
            cols_ += [att_units[jb, gk][:blk], att_units[jb, gk][blk:]]
        att_rows.append(jnp.concatenate(cols_, axis=1))
    att = jnp.concatenate(att_rows, axis=0)

    lbl = lbl_ref[...]
    lexp = jnp.exp(lbl - jnp.max(lbl, axis=0, keepdims=True))
    lb = lexp[0:1, :] / jnp.sum(lexp, axis=0, keepdims=True)
    tri = tri_ref[...]
    row_ids = lax.broadcasted_iota(jnp.int32, (ts, HG_DIM), 0)
    half = ts // 2
    f_all = 0.5 * (1.0 + lb) + (0.5 * (1.0 - lb)) * jnp.tanh(0.5 * p_f)
    g_all = jnp.log(f_all) * LOG2_E
    g_hi = g_all.astype(BF16)
    g_lo = (g_all - g_hi.astype(F32)).astype(BF16)
    b2 = _dot(tri, jnp.concatenate([g_hi, g_lo], axis=1))
    b_all = b2[:, :HG_WIDTH] + b2[:, HG_WIDTH:]
    n_tiles = half // SUBLANES
    hs = [dict() for _ in range(HG_HEADS)]

    def prep(h):
        cols = slice(h * HG_DIM, (h + 1) * HG_DIM)
        f = f_all[:, cols]
        b = b_all[:, cols]
        kk = 1.0 - f
        qf = _silu(p_q[:, cols])
        vv = p_i[:, cols].astype(BF16)
        w_incl = jnp.exp2(b)
        w_tail = jnp.exp2(b[ts - 1:ts, :] - b)
        st = state_ref[h]
        o = _dot_nt((qf * w_incl).astype(BF16), st.astype(BF16))
        state_ref[h] = st * w_incl[ts - 1:ts, :] + _dot_tn(vv, (kk * w_tail).astype(BF16))
        hs[h].update(f=f, b=b, kk=kk, qf=qf, vv=vv, o=o, cols=cols)

    def levels(h):
        d = hs[h]
        qf, kk, b, f = d["qf"], d["kk"], d["b"], d["f"]
        quad = [[None] * n_tiles, [None] * n_tiles]
        for li, m in enumerate(HG_LEVELS):
            x_f = _select_rows(m, qf, kk, row_ids) * _level_decay(m, b, f, row_ids)
            x_l = x_f.astype(BF16)
            if 2 * m == ts:
                d["s10"] = _dot_nt(x_l[half:], x_l[:half])
                continue
            if m >= SUBLANES:
                tiles = [t for t in range(n_tiles) if (t * SUBLANES // m) % 2 == 1]
            else:
                tiles = list(range(n_tiles))
            for hf in range(2):
                r0 = hf * half
                if len(tiles) == n_tiles:
                    lhs = x_l[r0:r0 + half]
                else:
                    lhs = jnp.concatenate([x_f[r0 + t * SUBLANES:r0 + (t + 1) * SUBLANES] for t in tiles],
                                          axis=0).astype(BF16)
                sc = _dot_nt(lhs, x_l[r0:r0 + half])
                for i, t in enumerate(tiles):
                    m0 = (li - 1) * LANES + t * SUBLANES
                    part = sc[i * SUBLANES:(i + 1) * SUBLANES] * lmask_ref[m0:m0 + SUBLANES, :]
                    quad[hf][t] = part if quad[hf][t] is None else quad[hf][t] + part
        d["s00"] = jnp.concatenate(quad[0], axis=0).astype(BF16)
        d["s1"] = jnp.concatenate([d["s10"], jnp.concatenate(quad[1], axis=0)], axis=1).astype(BF16)

    def finish(h):
        d = hs[h]
        vv = d["vv"]
        o_top = _dot(d["s00"], vv[:half])
        o_bot = _dot(d["s1"], vv)
        diag = jnp.sum(d["qf"] * d["kk"], axis=-1, keepdims=True)
        o = d["o"] + jnp.concatenate([o_top, o_bot], axis=0) + diag * p_i[:, d["cols"]]
        d["rec"] = o * _rms_scale(o)

    prep(0)
    prep(1)
    levels(0)
    prep(2)
    levels(1)
    prep(3)
    levels(2)
    finish(0)
    levels(3)
    finish(1)
    finish(2)
    finish(3)
    rec_cols = [hs[h]["rec"] for h in range(HG_HEADS)]
    rec_n = (jnp.concatenate(rec_cols, axis=1) * ghg_ref[...] * _silu(p_g)).astype(BF16)

    att_n = (att * _rms_scale(att) * gatt_ref[...]).astype(BF16)
    o_ref[...] = x + _dot(jnp.concatenate([att_n, rec_n], axis=1), wout_ref[...])


def _mixer(x, gmix, w_in, positions, w_out, sinks, gatt, lbl, ghg):
    b, s, _ = x.shape
    ts = MIX_TILE
    tri, lmask = _hgrn_constants()
    invf, ecos, esin = _rotary_constants()
    bias = _attention_bias()
    pos = positions.reshape(b * (s // ts), 1, ts)
    cur = lambda bi, i: (bi, i, 0)
    fixed = lambda bi, i: (0, 0)
    return pl.pallas_call(
        _mixer_kernel,
        grid=(b, s // ts),
        in_specs=[
            pl.BlockSpec(memory_space=pltpu.SMEM),
            pl.BlockSpec((None, ts, D_MODEL), cur),
            pl.BlockSpec((1, D_MODEL), fixed),
            pl.BlockSpec(w_in.shape, fixed, pipeline_mode=pl.Buffered(1)),
            pl.BlockSpec((None, 1, ts), lambda bi, i: (bi * (s // ts) + i, 0, 0)),
            pl.BlockSpec(invf.shape, fixed),
            pl.BlockSpec(ecos.shape, fixed),
            pl.BlockSpec(esin.shape, fixed),
            pl.BlockSpec((D_MODEL, D_MODEL), fixed, pipeline_mode=pl.Buffered(1)),
            pl.BlockSpec((1, ATT_WIDTH), fixed),
            pl.BlockSpec(lbl.shape, fixed),
            pl.BlockSpec((1, HG_WIDTH), fixed),
            pl.BlockSpec(tri.shape, fixed),
            pl.BlockSpec(lmask.shape, fixed),
            pl.BlockSpec(bias.shape, lambda bi, i: (0, 0, 0)),
        ],
        out_specs=pl.BlockSpec((None, ts, D_MODEL), cur),
        out_shape=jax.ShapeDtypeStruct((b, s, D_MODEL), F32),
        scratch_shapes=[
            pltpu.VMEM((HG_HEADS, HG_DIM, HG_DIM), F32),
            pltpu.VMEM((ATT_BLOCK, KV_PAD_WIDTH), BF16),
        ],
        compiler_params=pltpu.CompilerParams(
            dimension_semantics=("arbitrary", "arbitrary"), vmem_limit_bytes=VMEM_LIMIT_BYTES),
        name="mixer",
    )(sinks, x, gmix, w_in, pos, invf, ecos, esin, w_out, gatt, lbl, ghg, tri, lmask, bias)


def _mem_kv_kernel(mem_ref, gain_ref, w_ref, k_ref, v_ref):
    m = mem_ref[...]
    mn = (m * _rms_scale(m) * gain_ref[...]).astype(BF16)
    kv = _dot(mn, w_ref[...])
    k_ref[...] = kv[:, :D_MODEL].astype(BF16)
    v_ref[...] = kv[:, D_MODEL:].astype(BF16)


def _mem_kv(mem, gain, w_xkv):
    b, m, _ = mem.shape
    blk = lambda bi: (bi, 0, 0)
    fixed = lambda bi: (0, 0)
    return pl.pallas_call(
        _mem_kv_kernel,
        grid=(b,),
        in_specs=[
            pl.BlockSpec((None, m, D_MODEL), blk),
            pl.BlockSpec((1, D_MODEL), fixed),
            pl.BlockSpec((D_MODEL, 2 * D_MODEL), fixed),
        ],
        out_specs=[pl.BlockSpec((None, m, D_MODEL), blk), pl.BlockSpec((None, m, D_MODEL), blk)],
        out_shape=[jax.ShapeDtypeStruct((b, m, D_MODEL), BF16)] * 2,
        compiler_params=pltpu.CompilerParams(
            dimension_semantics=("arbitrary",), vmem_limit_bytes=VMEM_LIMIT_BYTES),
        name="mem_kv",
    )(mem, gain, w_xkv)


def _xattn_kernel(x_ref, gain_ref, wq_ref, k_ref, v_ref, wo_ref, o_ref):
    x = x_ref[...]
    hq = (x * _rms_scale(x) * gain_ref[...]).astype(BF16)
    q = _dot(hq, wq_ref[...]).astype(BF16)
    outs = []
    for h in range(X_HEADS):
        cols = slice(h * X_HEAD_DIM, (h + 1) * X_HEAD_DIM)
        s = _dot_nt(q[:, cols], k_ref[:, cols]) * (X_HEAD_DIM ** -0.5)
        p = jnp.exp(s - jnp.max(s, axis=-1, keepdims=True))
        o = _dot(p.astype(BF16), v_ref[:, cols])
        outs.append((o / jnp.sum(p, axis=-1, keepdims=True)).astype(BF16))
    xo = jnp.concatenate(outs, axis=1)
    o_ref[...] = x + _dot(xo, wo_ref[...])


def _xattn(x, gain, w_xq, xk, xv, w_xo, tm):
    b, s, _ = x.shape
    m = xk.shape[1]
    cur = lambda bi, i: (bi, i, 0)
    mem = lambda bi, i: (bi, 0, 0)
    fixed = lambda bi, i: (0, 0)
    return pl.pallas_call(
        _xattn_kernel,
        grid=(b, s // tm),
        in_specs=[
            pl.BlockSpec((None, tm, D_MODEL), cur),
            pl.BlockSpec((1, D_MODEL), fixed),
            pl.BlockSpec((D_MODEL, D_MODEL), fixed),
            pl.BlockSpec((None, m, D_MODEL), mem),
            pl.BlockSpec((None, m, D_MODEL), mem),
            pl.BlockSpec((D_MODEL, D_MODEL), fixed),
        ],
        out_specs=pl.BlockSpec((None, tm, D_MODEL), cur),
        out_shape=jax.ShapeDtypeStruct((b, s, D_MODEL), F32),
        compiler_params=pltpu.CompilerParams(
            dimension_semantics=("arbitrary", "arbitrary"), vmem_limit_bytes=VMEM_LIMIT_BYTES),
        name="xattn",
    )(x, gain, w_xq, xk, xv, w_xo)


FFN_CHUNKS = ((0, 1024), (1024, 2048), (2048, FFN_HIDDEN))
FFN_STAGE = 512


def _ffn_kernel(x_ref, gain_ref, wgu_hbm, wd_hbm, gfin_ref, o_ref,
                wgu_ref, wd_ref, gu_stage, d_stage, gu_sem, d_sem):
    @pl.when(pl.program_id(0) == 0)
    def _():
        _stage_weight(wgu_hbm, wgu_ref, gu_stage, gu_sem, axis=1)
        _stage_weight(wd_hbm, wd_ref, d_stage, d_sem, axis=0)

    x = x_ref[...]
    hf = (x * _rms_scale(x) * gain_ref[...]).astype(BF16)
    acc = x
    for lo, hi in FFN_CHUNKS:
        gate = _dot(hf, wgu_ref[:, lo:hi])
        upv = _dot(hf, wgu_ref[:, FFN_HIDDEN + lo:FFN_HIDDEN + hi])
        act = (_silu(gate) * upv).astype(BF16)
        acc = acc + _dot(act, wd_ref[lo:hi, :])
    o_ref[...] = acc * _rms_scale(acc) * gfin_ref[...]


def _ffn(x2d, gain, w_gu, w_d, gfin, tm):
    n = x2d.shape[0]
    row = lambda i: (i, 0)
    fixed = lambda i: (0, 0)
    return pl.pallas_call(
        _ffn_kernel,
        grid=(n // tm,),
        in_specs=[
            pl.BlockSpec((tm, D_MODEL), row),
            pl.BlockSpec((1, D_MODEL), fixed),
            pl.BlockSpec(memory_space=pl.ANY),
            pl.BlockSpec(memory_space=pl.ANY),
            pl.BlockSpec((1, D_MODEL), fixed),
        ],
        out_specs=pl.BlockSpec((tm, D_MODEL), row),
        out_shape=jax.ShapeDtypeStruct((n, D_MODEL), F32),
        scratch_shapes=[
            pltpu.VMEM((D_MODEL, 2 * FFN_HIDDEN), BF16),
            pltpu.VMEM((FFN_HIDDEN, D_MODEL), BF16),
            pltpu.VMEM((2, D_MODEL, FFN_STAGE), F32),
            pltpu.VMEM((2, FFN_STAGE // 2, D_MODEL), F32),
            pltpu.SemaphoreType.DMA((2,)),
            pltpu.SemaphoreType.DMA((2,)),
        ],
        compiler_params=pltpu.CompilerParams(
            dimension_semantics=("arbitrary",), vmem_limit_bytes=VMEM_LIMIT_BYTES),
        name="ffn",
    )(x2d, gain, w_gu, w_d, gfin)


def kernel(x, mem, positions, norm_mix, w_in, att_sinks, att_out_gain, hg_lb_logits, hg_out_gain,
           w_out, norm_xattn, norm_mem, w_xq, w_xkv, w_xo, norm_ffn, w_gate_up, w_down, norm_final):
    b, s, d = x.shape
    n = b * s
    assert w_in.shape[0] == 1 and hg_lb_logits.shape[0] == 2, "single-layer block only"
    row = lambda v: v.reshape(1, -1).astype(F32)
    x = _mixer(x, row(norm_mix[0]), w_in[0].astype(BF16), positions,
               w_out[0].astype(BF16), att_sinks[0].astype(F32), row(att_out_gain[0]),
               hg_lb_logits.astype(F32), row(hg_out_gain[0]))
    xk, xv = _mem_kv(mem, row(norm_mem[0]), w_xkv[0].astype(BF16))
    x = _xattn(x, row(norm_xattn[0]), w_xq[0].astype(BF16), xk, xv, w_xo[0].astype(BF16), tm=1024)
    y = _ffn(x.reshape(n, d), row(norm_ffn[0]), w_gate_up[0].astype(F32), w_down[0].astype(F32),
             row(norm_final), tm=1024)
    return y.reshape(b, s, d)
```

```python
import numpy as np
import jax
import jax.numpy as jnp
from jax import lax
from jax.experimental import pallas as pl
from jax.experimental.pallas import tpu as pltpu

F32 = jnp.float32
BF16 = jnp.bfloat16

D_MODEL = 1024
MEM_LEN = 256
ATT_HEADS = 8
ATT_KV_HEADS = 2
ATT_GROUP = ATT_HEADS // ATT_KV_HEADS
ATT_HEAD_DIM = 64
ATT_WIDTH = ATT_HEADS * ATT_HEAD_DIM
ATT_KV_WIDTH = ATT_KV_HEADS * ATT_HEAD_DIM
ATT_BLOCK = 128
ROPE_THETA = 500000.0
ROPE_DIM = ATT_HEAD_DIM // 4
ROPE_HALF = ROPE_DIM // 2
HG_HEADS = 4
HG_DIM = 128
HG_WIDTH = HG_HEADS * HG_DIM
X_HEADS = 4
X_HEAD_DIM = D_MODEL // X_HEADS
FFN_HIDDEN = 2816
RMS_EPS = 1e-6
LANES = 128
SUBLANES = 8
MASK_VALUE = -1e30
LOG2_E = 1.4426950408889634

MIX_TILE = 256
HG_LEVELS = (128, 64, 32, 16, 8, 4, 2, 1)
KV_PAD_WIDTH = 2 * ATT_KV_HEADS * 2 * LANES

VMEM_LIMIT_BYTES = 56 * 1024 * 1024


def _rms_scale(x):
    return lax.rsqrt(jnp.mean(x * x, axis=-1, keepdims=True) + RMS_EPS)


def _silu(x):
    h = 0.5 * x
    return h + h * jnp.tanh(h)


def _dot(a, b):
    return jnp.dot(a, b, preferred_element_type=F32)


def _dot_nt(a, b):
    return lax.dot_general(a, b, (((1,), (1,)), ((), ())), preferred_element_type=F32)


def _dot_tn(a, b):
    return lax.dot_general(a, b, (((0,), (0,)), ((), ())), preferred_element_type=F32)


def _cast_step(t, src_hbm, dst_hbm, in_stage, out_stage, in_sem, out_sem):
    rows = in_stage.shape[1]
    n = src_hbm.shape[0] // rows
    assert n * rows == src_hbm.shape[0] and src_hbm.shape == dst_hbm.shape
    slot = lax.rem(t, 2)

    def read(c, s):
        return pltpu.make_async_copy(src_hbm.at[pl.ds(c * rows, rows), :], in_stage.at[s], in_sem.at[s])

    def write(c, s):
        return pltpu.make_async_copy(out_stage.at[s], dst_hbm.at[pl.ds(c * rows, rows), :], out_sem.at[s])

    @pl.when(t == 0)
    def _():
        read(0, 0).start()

    @pl.when(t + 1 < n)
    def _():
        read(t + 1, 1 - slot).start()

    @pl.when(t < n)
    def _():
        read(t, slot).wait()

        @pl.when(t >= 2)
        def _():
            write(t - 2, slot).wait()

        out_stage[slot] = in_stage[slot].astype(BF16)
        write(t, slot).start()

    @pl.when(t == n - 1)
    def _():
        write(t - 1, 1 - slot).wait()
        write(t, slot).wait()


def _hgrn_constants():
    r = np.arange(MIX_TILE)[:, None]
    j = np.arange(MIX_TILE)[None, :]
    tri = (j <= r).astype(np.float32)
    rr = np.arange(LANES)[:, None]
    jj = np.arange(LANES)[None, :]
    masks = []
    for m in HG_LEVELS:
        if 2 * m <= LANES:
            masks.append(((rr // (2 * m)) == (jj // (2 * m))) & ((rr % (2 * m)) >= m) & ((jj % (2 * m)) < m))
    lmask = np.concatenate(masks, axis=0).astype(np.float32)
    return jnp.asarray(tri, BF16), jnp.asarray(lmask, F32)


def _attention_bias():
    i = (np.arange(2 * ATT_BLOCK) % ATT_BLOCK)[:, None]
    j = np.arange(2 * ATT_BLOCK)[None, :]
    band = (j > i) & (j <= i + ATT_BLOCK)
    variants = (band, band & (j >= ATT_BLOCK))
    return jnp.asarray(np.stack([np.where(v | (j == 0), 0.0, MASK_VALUE) for v in variants]), F32)


def _rotary_constants():
    inv_freq = jnp.power(jnp.float32(ROPE_THETA),
                         -jnp.arange(ROPE_HALF, dtype=F32) * (2.0 / ROPE_DIM))
    invf = jnp.broadcast_to(inv_freq[:, None], (ROPE_HALF, MIX_TILE))
    lane = np.arange(LANES)
    in_head = lane % ATT_HEAD_DIM
    freq_row = (np.arange(ROPE_HALF)[:, None] == (in_head % ROPE_HALF)[None, :])
    rot = freq_row & (in_head < ROPE_DIM)[None, :]
    ecos = np.zeros((4 * ROPE_HALF, LANES), np.float32)
    ecos[:ROPE_HALF] = rot
    ecos[ROPE_HALF:2 * ROPE_HALF] = rot
    ecos[2 * ROPE_HALF] = in_head >= ROPE_DIM
    sign = np.where(in_head < ROPE_HALF, -1.0, 1.0)[None, :]
    esin = np.concatenate([rot * sign, rot * sign], axis=0).astype(np.float32)
    return invf, jnp.asarray(ecos, BF16), jnp.asarray(esin, BF16)


def _level_decay(m, b, f, row_ids):
    ts = b.shape[0]
    if m >= SUBLANES // 2:
        b3 = b.reshape(ts // (2 * m), 2 * m, HG_DIM)
        ref = b3[:, m - 1:m, :]
        if m >= SUBLANES:
            e3 = jnp.concatenate([ref - b3[:, :m, :], b3[:, m:, :] - ref], axis=1)
        else:
            e3 = -jnp.abs(b3 - ref)
        return jnp.exp2(e3).reshape(ts, HG_DIM)
    if m == 2:
        pos = row_ids & 3
        f_next = pltpu.roll(f, ts - 1, axis=0)
        f_prev = pltpu.roll(f, 1, axis=0)
        return jnp.where(pos == 0, f_next, jnp.where(pos == 1, 1.0, jnp.where(pos == 2, f, f * f_prev)))
    assert m == 1
    return jnp.where((row_ids & 1) != 0, f, 1.0)


def _select_rows(m, upper_src, lower_src, row_ids):
    c = upper_src.shape[0]
    if m >= SUBLANES:
        pieces = []
        for b0 in range(0, c, 2 * m):
            pieces.append(lower_src[b0:b0 + m])
            pieces.append(upper_src[b0 + m:b0 + 2 * m])
        return jnp.concatenate(pieces, axis=0)
    return jnp.where((row_ids & m) != 0, upper_src, lower_src)


def _rotary_tables(pos_ref, invf_ref, ecos_ref, esin_ref):
    ang = invf_ref[...] * pos_ref[...].astype(F32)

    def spread(t, e_ref, extra):
        t_hi = t.astype(BF16).astype(F32)
        rows = jnp.concatenate([t_hi, t - t_hi] + extra, axis=0).astype(BF16)
        return _dot_tn(rows, e_ref[...])

    ones = jnp.ones_like(ang)
    cos_t = spread(jnp.cos(ang), ecos_ref, [ones, jnp.zeros_like(ang)])
    sin_t = spread(jnp.sin(ang), esin_ref, [])
    return cos_t, sin_t


def _mixer_kernel(sinks_ref, x_ref, gmix_ref, win_ref, pos_ref, invf_ref, ecos_ref, esin_ref, wout_ref,
                  gatt_ref, lbl_ref, ghg_ref, tri_ref, lmask_ref, bias_ref, *rest):
    n_side = (len(rest) - 3) // 6
    side_src, o_ref, side_dst = rest[:n_side], rest[n_side], rest[n_side + 1:2 * n_side + 1]
    state_ref, kvprev_ref = rest[2 * n_side + 1:2 * n_side + 3]
    side_scratch = rest[2 * n_side + 3:]
    step = pl.program_id(1)
    ts = MIX_TILE
    blk = ATT_BLOCK

    t = pl.program_id(0) * pl.num_programs(1) + step
    for k in range(n_side):
        _cast_step(t, side_src[k], side_dst[k], *side_scratch[4 * k:4 * k + 4])

    @pl.when(step == 0)
    def _():
        state_ref[...] = jnp.zeros_like(state_ref)
        kvprev_ref[...] = jnp.zeros_like(kvprev_ref)

    x = x_ref[...]
    hn = (x * _rms_scale(x) * gmix_ref[...]).astype(BF16)

    def proj(c0, c1):
        return _dot(hn, win_ref[:, c0:c1])

    cos, sin = _rotary_tables(pos_ref, invf_ref, ecos_ref, esin_ref)
    lane = lax.broadcasted_iota(jnp.int32, cos.shape, 1)
    first_half = (lane & (ATT_HEAD_DIM - 1)) < ROPE_HALF
    lo_head = lane < ATT_HEAD_DIM

    def rotate(col):
        partner = jnp.where(first_half,
                            pltpu.roll(col, LANES - ROPE_HALF, axis=1),
                            pltpu.roll(col, ROPE_HALF, axis=1))
        return col * cos + partner * sin

    def padded(col):
        swapped = pltpu.roll(col, ATT_HEAD_DIM, axis=1)
        zero = jnp.zeros_like(col)
        return [v.astype(BF16) for v in (jnp.where(lo_head, col, zero), jnp.where(lo_head, zero, swapped),
                                         jnp.where(lo_head, swapped, zero), jnp.where(lo_head, zero, col))]

    pq = proj(0, ATT_WIDTH)
    q_cols = [(rotate(pq[:, j * LANES:(j + 1) * LANES]) * (ATT_HEAD_DIM ** -0.5 * LOG2_E)).astype(BF16)
              for j in range(ATT_WIDTH // LANES)]
    pkv = proj(ATT_WIDTH, ATT_WIDTH + 2 * ATT_KV_WIDTH)
    kv_new = padded(rotate(pkv[:, :LANES])) + padded(pkv[:, LANES:])
    kv_all = [jnp.concatenate([kvprev_ref[:, i * LANES:(i + 1) * LANES], v], axis=0)
              for i, v in enumerate(kv_new)]
    for i, v in enumerate(kv_new):
        kvprev_ref[:, i * LANES:(i + 1) * LANES] = v[ts - blk:]
    r0c = ATT_WIDTH + 2 * ATT_KV_WIDTH
    p_q, p_f, p_i, p_g = (proj(r0c + k * HG_WIDTH, r0c + (k + 1) * HG_WIDTH) for k in range(4))

    upper_half_rows = lax.broadcasted_iota(jnp.int32, (2 * blk, 1), 0) >= blk
    lane_ids = lax.broadcasted_iota(jnp.int32, (2 * blk, LANES), 1)
    lane_lo = lane_ids < ATT_HEAD_DIM
    ones_lo = jnp.where(lane_lo, 1.0, 0.0).astype(BF16)
    ones_hi = jnp.where(lane_lo, 0.0, 1.0).astype(BF16)
    bf16_rows = 2 * SUBLANES
    drop_row0 = jnp.where(lax.broadcasted_iota(jnp.int32, (bf16_rows, LANES), 0) > 0, 1.0, 0.0).astype(BF16)

    def sink_slot(op):
        return jnp.concatenate([op[:bf16_rows] * drop_row0, op[bf16_rows:]], axis=0)

    att_units = {}

    def att_scores(jb, gk):
        keys = slice(jb * blk, (jb + 2) * blk)
        bias = bias_ref[jnp.where(step > 0, 0, 1)] if jb == 0 else bias_ref[0]
        q2 = jnp.concatenate([q_cols[2 * gk][jb * blk:(jb + 1) * blk],
                              q_cols[2 * gk + 1][jb * blk:(jb + 1) * blk]], axis=0)
        k_lo = sink_slot(kv_all[2 * gk][keys])
        k_hi = sink_slot(kv_all[2 * gk + 1][keys])
        h0 = ATT_GROUP * gk
        probs = []
        for which, k_op in ((0, k_lo), (1, k_hi)):
            sink = jnp.where(upper_half_rows, sinks_ref[h0 + 2 + which], sinks_ref[h0 + which]) * LOG2_E
            s = _dot_nt(q2, k_op) + bias
            s_first = s[:, :LANES] + jnp.where(lane_ids == 0, sink, 0.0)
            m = jnp.max(jnp.maximum(s_first, s[:, LANES:]), axis=-1, keepdims=True)
            probs.append(jnp.exp2(jnp.concatenate([s_first, s[:, LANES:]], axis=1) - m).astype(BF16))
        att_units[jb, gk] = jnp.concatenate(probs, axis=1)

    def att_values(jb, gk):
        keys = slice(jb * blk, (jb + 2) * blk)
        r_lo = jnp.concatenate([sink_slot(kv_all[4 + 2 * gk][keys]), ones_lo], axis=1)
        r_hi = jnp.concatenate([sink_slot(kv_all[5 + 2 * gk][keys]), ones_hi], axis=1)
        out = _dot(att_units[jb, gk], jnp.concatenate([r_lo, r_hi], axis=0))
        att_units[jb, gk] = out[:, :LANES] / out[:, LANES:]

    units = [(jb, gk) for jb in range(ts // blk) for gk in range(ATT_KV_HEADS)]
    att_scores(*units[0])
    for prev_u, u in zip(units[:-1], units[1:]):
        att_scores(*u)
        att_values(*prev_u)
    att_values(*units[-1])
    att_rows = []
    for jb in range(ts // blk):
        cols_ = []
        for gk in range(ATT_KV_HEADS):
            cols_ += [att_units[jb, gk][:blk], att_units[jb, gk][blk:]]
        att_rows.append(jnp.concatenate(cols_, axis=1))
    att = jnp.concatenate(att_rows, axis=0)

    lbl = lbl_ref[...]
    lexp = jnp.exp(lbl - jnp.max(lbl, axis=0, keepdims=True))
    lb = lexp[0:1, :] / jnp.sum(lexp, axis=0, keepdims=True)
    tri = tri_ref[...]
    row_ids = lax.broadcasted_iota(jnp.int32, (ts, HG_DIM), 0)
    half = ts // 2
    f_all = 0.5 * (1.0 + lb) + (0.5 * (1.0 - lb)) * jnp.tanh(0.5 * p_f)
    g_all = jnp.log(f_all) * LOG2_E
    g_hi = g_all.astype(BF16)
    g_lo = (g_all - g_hi.astype(F32)).astype(BF16)
    b2 = _dot(tri, jnp.concatenate([g_hi, g_lo], axis=1))
    b_all = b2[:, :HG_WIDTH] + b2[:, HG_WIDTH:]
    n_tiles = half // SUBLANES
    hs = [dict() for _ in range(HG_HEADS)]

    def prep(h):
        cols = slice(h * HG_DIM, (h + 1) * HG_DIM)
        f = f_all[:, cols]
        b = b_all[:, cols]
        kk = 1.0 - f
        qf = _silu(p_q[:, cols])
        vv = p_i[:, cols].astype(BF16)
        w_incl = jnp.exp2(b)
        w_tail = jnp.exp2(b[ts - 1:ts, :] - b)
        st = state_ref[h]
        o = _dot_nt((qf * w_incl).astype(BF16), st.astype(BF16))
        state_ref[h] = st * w_incl[ts - 1:ts, :] + _dot_tn(vv, (kk * w_tail).astype(BF16))
        hs[h].update(f=f, b=b, kk=kk, qf=qf, vv=vv, o=o, cols=cols)

    def levels(h):
        d = hs[h]
        qf, kk, b, f = d["qf"], d["kk"], d["b"], d["f"]
        quad = [[None] * n_tiles, [None] * n_tiles]
        for li, m in enumerate(HG_LEVELS):
            x_f = _select_rows(m, qf, kk, row_ids) * _level_decay(m, b, f, row_ids)
            x_l = x_f.astype(BF16)
            if 2 * m == ts:
                d["s10"] = _dot_nt(x_l[half:], x_l[:half])
                continue
            if m >= SUBLANES:
                tiles = [t for t in range(n_tiles) if (t * SUBLANES // m) % 2 == 1]
            else:
                tiles = list(range(n_tiles))
            for hf in range(2):
                r0 = hf * half
                if len(tiles) == n_tiles:
                    lhs = x_l[r0:r0 + half]
                else:
                    lhs = jnp.concatenate([x_f[r0 + t * SUBLANES:r0 + (t + 1) * SUBLANES] for t in tiles],
                                          axis=0).astype(BF16)
                sc = _dot_nt(lhs, x_l[r0:r0 + half])
                for i, t in enumerate(tiles):
                    m0 = (li - 1) * LANES + t * SUBLANES
                    part = sc[i * SUBLANES:(i + 1) * SUBLANES] * lmask_ref[m0:m0 + SUBLANES, :]
                    quad[hf][t] = part if quad[hf][t] is None else quad[hf][t] + part
        d["s00"] = jnp.concatenate(quad[0], axis=0).astype(BF16)
        d["s1"] = jnp.concatenate([d["s10"], jnp.concatenate(quad[1], axis=0)], axis=1).astype(BF16)

    def finish(h):
        d = hs[h]
        vv = d["vv"]
        o_top = _dot(d["s00"], vv[:half])
        o_bot = _dot(d["s1"], vv)
        diag = jnp.sum(d["qf"] * d["kk"], axis=-1, keepdims=True)
        o = d["o"] + jnp.concatenate([o_top, o_bot], axis=0) + diag * p_i[:, d["cols"]]
        d["rec"] = o * _rms_scale(o)

    prep(0)
    prep(1)
    levels(0)
    prep(2)
    levels(1)
    prep(3)
    levels(2)
    finish(0)
    levels(3)
    finish(1)
    finish(2)
    finish(3)
    rec_cols = [hs[h]["rec"] for h in range(HG_HEADS)]
    rec_n = (jnp.concatenate(rec_cols, axis=1) * ghg_ref[...] * _silu(p_g)).astype(BF16)

    att_n = (att * _rms_scale(att) * gatt_ref[...]).astype(BF16)
    o_ref[...] = x + _dot(jnp.concatenate([att_n, rec_n], axis=1), wout_ref[...])


def _cast_chunk_rows(n_rows, n_steps):
    tile = 2 * SUBLANES
    rows = tile
    while n_rows % rows or n_rows // rows > n_steps:
        rows += tile
    return rows


def _mixer(x, gmix, w_in, positions, w_out, sinks, gatt, lbl, ghg, side_weights):
    b, s, _ = x.shape
    ts = MIX_TILE
    tri, lmask = _hgrn_constants()
    invf, ecos, esin = _rotary_constants()
    bias = _attention_bias()
    pos = positions.reshape(b * (s // ts), 1, ts)
    cur = lambda bi, i: (bi, i, 0)
    fixed = lambda bi, i: (0, 0)
    n_steps = b * (s // ts)
    side_scratch = []
    for w in side_weights:
        stage = (2, _cast_chunk_rows(w.shape[0], n_steps), w.shape[1])
        side_scratch += [pltpu.VMEM(stage, F32), pltpu.VMEM(stage, BF16),
                         pltpu.SemaphoreType.DMA((2,)), pltpu.SemaphoreType.DMA((2,))]
    hbm = pl.BlockSpec(memory_space=pl.ANY)
    outs = pl.pallas_call(
        _mixer_kernel,
        grid=(b, s // ts),
        in_specs=[
            pl.BlockSpec(memory_space=pltpu.SMEM),
            pl.BlockSpec((None, ts, D_MODEL), cur),
            pl.BlockSpec((1, D_MODEL), fixed),
            pl.BlockSpec(w_in.shape, fixed, pipeline_mode=pl.Buffered(1)),
            pl.BlockSpec((None, 1, ts), lambda bi, i: (bi * (s // ts) + i, 0, 0)),
            pl.BlockSpec(invf.shape, fixed),
            pl.BlockSpec(ecos.shape, fixed),
            pl.BlockSpec(esin.shape, fixed),
            pl.BlockSpec((D_MODEL, D_MODEL), fixed, pipeline_mode=pl.Buffered(1)),
            pl.BlockSpec((1, ATT_WIDTH), fixed),
            pl.BlockSpec(lbl.shape, fixed),
            pl.BlockSpec((1, HG_WIDTH), fixed),
            pl.BlockSpec(tri.shape, fixed),
            pl.BlockSpec(lmask.shape, fixed),
            pl.BlockSpec(bias.shape, lambda bi, i: (0, 0, 0)),
        ] + [hbm] * len(side_weights),
        out_specs=[pl.BlockSpec((None, ts, D_MODEL), cur)] + [hbm] * len(side_weights),
        out_shape=[jax.ShapeDtypeStruct((b, s, D_MODEL), F32)]
        + [jax.ShapeDtypeStruct(w.shape, BF16) for w in side_weights],
        scratch_shapes=[
            pltpu.VMEM((HG_HEADS, HG_DIM, HG_DIM), F32),
            pltpu.VMEM((ATT_BLOCK, KV_PAD_WIDTH), BF16),
        ] + side_scratch,
        compiler_params=pltpu.CompilerParams(
            dimension_semantics=("arbitrary", "arbitrary"), vmem_limit_bytes=VMEM_LIMIT_BYTES),
        name="mixer",
    )(sinks, x, gmix, w_in, pos, invf, ecos, esin, w_out, gatt, lbl, ghg, tri, lmask, bias, *side_weights)
    return outs[0], outs[1:]


def _mem_kv_kernel(mem_ref, gain_ref, w_ref, k_ref, v_ref):
    m = mem_ref[...]
    mn = (m * _rms_scale(m) * gain_ref[...]).astype(BF16)
    kv = _dot(mn, w_ref[...])
    k_ref[...] = kv[:, :D_MODEL].astype(BF16)
    v_ref[...] = kv[:, D_MODEL:].astype(BF16)


def _mem_kv(mem, gain, w_xkv):
    b, m, _ = mem.shape
    blk = lambda bi: (bi, 0, 0)
    fixed = lambda bi: (0, 0)
    return pl.pallas_call(
        _mem_kv_kernel,
        grid=(b,),
        in_specs=[
            pl.BlockSpec((None, m, D_MODEL), blk),
            pl.BlockSpec((1, D_MODEL), fixed),
            pl.BlockSpec((D_MODEL, 2 * D_MODEL), fixed),
        ],
        out_specs=[pl.BlockSpec((None, m, D_MODEL), blk), pl.BlockSpec((None, m, D_MODEL), blk)],
        out_shape=[jax.ShapeDtypeStruct((b, m, D_MODEL), BF16)] * 2,
        compiler_params=pltpu.CompilerParams(
            dimension_semantics=("arbitrary",), vmem_limit_bytes=VMEM_LIMIT_BYTES),
        name="mem_kv",
    )(mem, gain, w_xkv)


def _xattn_kernel(x_ref, gain_ref, wq_ref, k_ref, v_ref, wo_ref, o_ref):
    x = x_ref[...]
    hq = (x * _rms_scale(x) * gain_ref[...]).astype(BF16)
    q = _dot(hq, wq_ref[...]).astype(BF16)
    outs = []
    for h in range(X_HEADS):
        cols = slice(h * X_HEAD_DIM, (h + 1) * X_HEAD_DIM)
        s = _dot_nt(q[:, cols], k_ref[:, cols]) * (X_HEAD_DIM ** -0.5)
        p = jnp.exp(s - jnp.max(s, axis=-1, keepdims=True))
        o = _dot(p.astype(BF16), v_ref[:, cols])
        outs.append((o / jnp.sum(p, axis=-1, keepdims=True)).astype(BF16))
    xo = jnp.concatenate(outs, axis=1)
    o_ref[...] = x + _dot(xo, wo_ref[...])


def _xattn(x, gain, w_xq, xk, xv, w_xo, tm):
    b, s, _ = x.shape
    m = xk.shape[1]
    cur = lambda bi, i: (bi, i, 0)
    mem = lambda bi, i: (bi, 0, 0)
    fixed = lambda bi, i: (0, 0)
    return pl.pallas_call(
        _xattn_kernel,
        grid=(b, s // tm),
        in_specs=[
            pl.BlockSpec((None, tm, D_MODEL), cur),
            pl.BlockSpec((1, D_MODEL), fixed),
            pl.BlockSpec((D_MODEL, D_MODEL), fixed),
            pl.BlockSpec((None, m, D_MODEL), mem),
            pl.BlockSpec((None, m, D_MODEL), mem),
            pl.BlockSpec((D_MODEL, D_MODEL), fixed),
        ],
        out_specs=pl.BlockSpec((None, tm, D_MODEL), cur),
        out_shape=jax.ShapeDtypeStruct((b, s, D_MODEL), F32),
        compiler_params=pltpu.CompilerParams(
            dimension_semantics=("arbitrary", "arbitrary"), vmem_limit_bytes=VMEM_LIMIT_BYTES),
        name="xattn",
    )(x, gain, w_xq, xk, xv, w_xo)


FFN_CHUNKS = ((0, 1024), (1024, 2048), (2048, FFN_HIDDEN))

def _ffn_kernel(x_ref, gain_ref, wgu_ref, wd_ref, gfin_ref, o_ref):
    x = x_ref[...]
    hf = (x * _rms_scale(x) * gain_ref[...]).astype(BF16)
    acc = x
    for lo, hi in FFN_CHUNKS:
        gate = _dot(hf, wgu_ref[:, lo:hi])
        upv = _dot(hf, wgu_ref[:, FFN_HIDDEN + lo:FFN_HIDDEN + hi])
        act = (_silu(gate) * upv).astype(BF16)
        acc = acc + _dot(act, wd_ref[lo:hi, :])
    o_ref[...] = acc * _rms_scale(acc) * gfin_ref[...]


def _ffn(x2d, gain, w_gu, w_d, gfin, tm):
    n = x2d.shape[0]
    row = lambda i: (i, 0)
    fixed = lambda i: (0, 0)
    return pl.pallas_call(
        _ffn_kernel,
        grid=(n // tm,),
        in_specs=[
            pl.BlockSpec((tm, D_MODEL), row),
            pl.BlockSpec((1, D_MODEL), fixed),
            pl.BlockSpec((D_MODEL, 2 * FFN_HIDDEN), fixed, pipeline_mode=pl.Buffered(1)),
            pl.BlockSpec((FFN_HIDDEN, D_MODEL), fixed, pipeline_mode=pl.Buffered(1)),
            pl.BlockSpec((1, D_MODEL), fixed),
        ],
        out_specs=pl.BlockSpec((tm, D_MODEL), row),
        out_shape=jax.ShapeDtypeStruct((n, D_MODEL), F32),
        compiler_params=pltpu.CompilerParams(
            dimension_semantics=("arbitrary",), vmem_limit_bytes=VMEM_LIMIT_BYTES),
        name="ffn",
    )(x2d, gain, w_gu, w_d, gfin)


def kernel(x, mem, positions, norm_mix, w_in, att_sinks, att_out_gain, hg_lb_logits, hg_out_gain,
           w_out, norm_xattn, norm_mem, w_xq, w_xkv, w_xo, norm_ffn, w_gate_up, w_down, norm_final):
    b, s, d = x.shape
    n = b * s
    assert w_in.shape[0] == 1 and hg_lb_logits.shape[0] == 2, "single-layer block only"
    row = lambda v: v.reshape(1, -1).astype(F32)
    side = [w[0].astype(F32) for w in (w_xkv, w_xq, w_xo, w_gate_up, w_down)]
    x, (w_xkv_b, w_xq_b, w_xo_b, w_gu_b, w_d_b) = _mixer(
        x, row(norm_mix[0]), w_in[0].astype(BF16), positions,
        w_out[0].astype(BF16), att_sinks[0].astype(F32), row(att_out_gain[0]),
        hg_lb_logits.astype(F32), row(hg_out_gain[0]), side)
    xk, xv = _mem_kv(mem, row(norm_mem[0]), w_xkv_b)
    x = _xattn(x, row(norm_xattn[0]), w_xq_b, xk, xv, w_xo_b, tm=1024)
    y = _ffn(x.reshape(n, d), row(norm_ffn[0]), w_gu_b, w_d_b, row(norm_final), tm=1024)
    return y.reshape(b, s, d)
```

```python
import numpy as np
import jax
import jax.numpy as jnp
from jax import lax
from jax.experimental import pallas as pl
from jax.experimental.pallas import tpu as pltpu

F32 = jnp.float32
BF16 = jnp.bfloat16

D_MODEL = 1024
MEM_LEN = 256
ATT_HEADS = 8
ATT_KV_HEADS = 2
ATT_GROUP = ATT_HEADS // ATT_KV_HEADS
ATT_HEAD_DIM = 64
ATT_WIDTH = ATT_HEADS * ATT_HEAD_DIM
ATT_KV_WIDTH = ATT_KV_HEADS * ATT_HEAD_DIM
ATT_BLOCK = 128
ROPE_THETA = 500000.0
ROPE_DIM = ATT_HEAD_DIM // 4
ROPE_HALF = ROPE_DIM // 2
HG_HEADS = 4
HG_DIM = 128
HG_WIDTH = HG_HEADS * HG_DIM
X_HEADS = 4
X_HEAD_DIM = D_MODEL // X_HEADS
FFN_HIDDEN = 2816
RMS_EPS = 1e-6
LANES = 128
SUBLANES = 8
MASK_VALUE = -1e30
LOG2_E = 1.4426950408889634

MIX_TILE = 256
HG_LEVELS = (128, 64, 32, 16, 8, 4, 2, 1)
KV_PAD_WIDTH = 2 * ATT_KV_HEADS * 2 * LANES
CAST_STRIDE = 4

VMEM_LIMIT_BYTES = 56 * 1024 * 1024


def _rms_scale(x):
    return lax.rsqrt(jnp.mean(x * x, axis=-1, keepdims=True) + RMS_EPS)


def _silu(x):
    h = 0.5 * x
    return h + h * jnp.tanh(h)


def _dot(a, b):
    return jnp.dot(a, b, preferred_element_type=F32)


def _dot_nt(a, b):
    return lax.dot_general(a, b, (((1,), (1,)), ((), ())), preferred_element_type=F32)


def _dot_tn(a, b):
    return lax.dot_general(a, b, (((0,), (0,)), ((), ())), preferred_element_type=F32)


def _cast_step(t, src_hbm, dst_hbm, in_stage, out_stage, in_sem, out_sem):
    rows = in_stage.shape[1]
    n = src_hbm.shape[0] // rows
    assert n * rows == src_hbm.shape[0] and src_hbm.shape == dst_hbm.shape
    slot = lax.rem(t, 2)

    def read(c, s):
        return pltpu.make_async_copy(src_hbm.at[pl.ds(c * rows, rows), :], in_stage.at[s], in_sem.at[s])

    def write(c, s):
        return pltpu.make_async_copy(out_stage.at[s], dst_hbm.at[pl.ds(c * rows, rows), :], out_sem.at[s])

    @pl.when(t == 0)
    def _():
        read(0, 0).start()

    @pl.when(t + 1 < n)
    def _():
        read(t + 1, 1 - slot).start()

    @pl.when(t < n)
    def _():
        read(t, slot).wait()

        @pl.when(t >= 2)
        def _():
            write(t - 2, slot).wait()

        out_stage[slot] = in_stage[slot].astype(BF16)
        write(t, slot).start()

    @pl.when(t == n - 1)
    def _():
        write(t - 1, 1 - slot).wait()
        write(t, slot).wait()


def _hgrn_constants():
    r = np.arange(MIX_TILE)[:, None]
    j = np.arange(MIX_TILE)[None, :]
    tri = (j <= r).astype(np.float32)
    rr = np.arange(LANES)[:, None]
    jj = np.arange(LANES)[None, :]
    masks = []
    for m in HG_LEVELS:
        if 2 * m <= LANES:
            masks.append(((rr // (2 * m)) == (jj // (2 * m))) & ((rr % (2 * m)) >= m) & ((jj % (2 * m)) < m))
    lmask = np.concatenate(masks, axis=0).astype(np.float32)
    return jnp.asarray(tri, BF16), jnp.asarray(lmask, F32)


def _attention_bias():
    i = (np.arange(2 * ATT_BLOCK) % ATT_BLOCK)[:, None]
    j = np.arange(2 * ATT_BLOCK)[None, :]
    band = (j > i) & (j <= i + ATT_BLOCK)
    variants = (band, band & (j >= ATT_BLOCK))
    return jnp.asarray(np.stack([np.where(v | (j == 0), 0.0, MASK_VALUE) for v in variants]), F32)


def _rotary_constants():
    inv_freq = jnp.power(jnp.float32(ROPE_THETA),
                         -jnp.arange(ROPE_HALF, dtype=F32) * (2.0 / ROPE_DIM))
    invf = jnp.broadcast_to(inv_freq[:, None], (ROPE_HALF, MIX_TILE))
    lane = np.arange(LANES)
    in_head = lane % ATT_HEAD_DIM
    freq_row = (np.arange(ROPE_HALF)[:, None] == (in_head % ROPE_HALF)[None, :])
    rot = freq_row & (in_head < ROPE_DIM)[None, :]
    ecos = np.zeros((4 * ROPE_HALF, LANES), np.float32)
    ecos[:ROPE_HALF] = rot
    ecos[ROPE_HALF:2 * ROPE_HALF] = rot
    ecos[2 * ROPE_HALF] = in_head >= ROPE_DIM
    sign = np.where(in_head < ROPE_HALF, -1.0, 1.0)[None, :]
    esin = np.concatenate([rot * sign, rot * sign], axis=0).astype(np.float32)
    return invf, jnp.asarray(ecos, BF16), jnp.asarray(esin, BF16)


def _level_decay(m, b, f, row_ids):
    ts = b.shape[0]
    if m >= SUBLANES // 2:
        b3 = b.reshape(ts // (2 * m), 2 * m, HG_DIM)
        ref = b3[:, m - 1:m, :]
        if m >= SUBLANES:
            e3 = jnp.concatenate([ref - b3[:, :m, :], b3[:, m:, :] - ref], axis=1)
        else:
            e3 = -jnp.abs(b3 - ref)
        return jnp.exp2(e3).reshape(ts, HG_DIM)
    if m == 2:
        pos = row_ids & 3
        f_next = pltpu.roll(f, ts - 1, axis=0)
        f_prev = pltpu.roll(f, 1, axis=0)
        return jnp.where(pos == 0, f_next, jnp.where(pos == 1, 1.0, jnp.where(pos == 2, f, f * f_prev)))
    assert m == 1
    return jnp.where((row_ids & 1) != 0, f, 1.0)


def _select_rows(m, upper_src, lower_src, row_ids):
    c = upper_src.shape[0]
    if m >= SUBLANES:
        pieces = []
        for b0 in range(0, c, 2 * m):
            pieces.append(lower_src[b0:b0 + m])
            pieces.append(upper_src[b0 + m:b0 + 2 * m])
        return jnp.concatenate(pieces, axis=0)
    return jnp.where((row_ids & m) != 0, upper_src, lower_src)


def _rotary_tables(pos_ref, invf_ref, ecos_ref, esin_ref):
    ang = invf_ref[...] * pos_ref[...].astype(F32)

    def spread(t, e_ref, extra):
        t_hi = t.astype(BF16).astype(F32)
        rows = jnp.concatenate([t_hi, t - t_hi] + extra, axis=0).astype(BF16)
        return _dot_tn(rows, e_ref[...])

    ones = jnp.ones_like(ang)
    cos_t = spread(jnp.cos(ang), ecos_ref, [ones, jnp.zeros_like(ang)])
    sin_t = spread(jnp.sin(ang), esin_ref, [])
    return cos_t, sin_t


def _mixer_kernel(sinks_ref, x_ref, gmix_ref, win_ref, pos_ref, invf_ref, ecos_ref, esin_ref, wout_ref,
                  gatt_ref, lbl_ref, ghg_ref, tri_ref, lmask_ref, bias_ref, *rest):
    n_side = (len(rest) - 3) // 6
    side_src, o_ref, side_dst = rest[:n_side], rest[n_side], rest[n_side + 1:2 * n_side + 1]
    state_ref, kvprev_ref = rest[2 * n_side + 1:2 * n_side + 3]
    side_scratch = rest[2 * n_side + 3:]
    step = pl.program_id(1)
    ts = MIX_TILE
    blk = ATT_BLOCK

    t = pl.program_id(0) * pl.num_programs(1) + step

    @pl.when(lax.rem(t, CAST_STRIDE) == 0)
    def _():
        for k in range(n_side):
            _cast_step(lax.div(t, CAST_STRIDE), side_src[k], side_dst[k], *side_scratch[4 * k:4 * k + 4])

    @pl.when(step == 0)
    def _():
        state_ref[...] = jnp.zeros_like(state_ref)
        kvprev_ref[...] = jnp.zeros_like(kvprev_ref)

    x = x_ref[...]
    hn = (x * _rms_scale(x) * gmix_ref[...]).astype(BF16)

    def proj(c0, c1):
        return _dot(hn, win_ref[:, c0:c1])

    cos, sin = _rotary_tables(pos_ref, invf_ref, ecos_ref, esin_ref)
    lane = lax.broadcasted_iota(jnp.int32, cos.shape, 1)
    first_half = (lane & (ATT_HEAD_DIM - 1)) < ROPE_HALF
    lo_head = lane < ATT_HEAD_DIM

    def rotate(col):
        partner = jnp.where(first_half,
                            pltpu.roll(col, LANES - ROPE_HALF, axis=1),
                            pltpu.roll(col, ROPE_HALF, axis=1))
        return col * cos + partner * sin

    def padded(col):
        swapped = pltpu.roll(col, ATT_HEAD_DIM, axis=1)
        zero = jnp.zeros_like(col)
        return [v.astype(BF16) for v in (jnp.where(lo_head, col, zero), jnp.where(lo_head, zero, swapped),
                                         jnp.where(lo_head, swapped, zero), jnp.where(lo_head, zero, col))]

    pq = proj(0, ATT_WIDTH)
    q_cols = [(rotate(pq[:, j * LANES:(j + 1) * LANES]) * (ATT_HEAD_DIM ** -0.5 * LOG2_E)).astype(BF16)
              for j in range(ATT_WIDTH // LANES)]
    pkv = proj(ATT_WIDTH, ATT_WIDTH + 2 * ATT_KV_WIDTH)
    kv_new = padded(rotate(pkv[:, :LANES])) + padded(pkv[:, LANES:])
    kv_all = [jnp.concatenate([kvprev_ref[:, i * LANES:(i + 1) * LANES], v], axis=0)
              for i, v in enumerate(kv_new)]
    for i, v in enumerate(kv_new):
        kvprev_ref[:, i * LANES:(i + 1) * LANES] = v[ts - blk:]
    r0c = ATT_WIDTH + 2 * ATT_KV_WIDTH
    p_q, p_f, p_i, p_g = (proj(r0c + k * HG_WIDTH, r0c + (k + 1) * HG_WIDTH) for k in range(4))

    upper_half_rows = lax.broadcasted_iota(jnp.int32, (2 * blk, 1), 0) >= blk
    lane_ids = lax.broadcasted_iota(jnp.int32, (2 * blk, LANES), 1)
    lane_lo = lane_ids < ATT_HEAD_DIM
    ones_lo = jnp.where(lane_lo, 1.0, 0.0).astype(BF16)
    ones_hi = jnp.where(lane_lo, 0.0, 1.0).astype(BF16)
    bf16_rows = 2 * SUBLANES
    drop_row0 = jnp.where(lax.broadcasted_iota(jnp.int32, (bf16_rows, LANES), 0) > 0, 1.0, 0.0).astype(BF16)

    def sink_slot(op):
        return jnp.concatenate([op[:bf16_rows] * drop_row0, op[bf16_rows:]], axis=0)

    att_units = {}

    def att_scores(jb, gk):
        keys = slice(jb * blk, (jb + 2) * blk)
        bias = bias_ref[jnp.where(step > 0, 0, 1)] if jb == 0 else bias_ref[0]
        q2 = jnp.concatenate([q_cols[2 * gk][jb * blk:(jb + 1) * blk],
                              q_cols[2 * gk + 1][jb * blk:(jb + 1) * blk]], axis=0)
        k_lo = sink_slot(kv_all[2 * gk][keys])
        k_hi = sink_slot(kv_all[2 * gk + 1][keys])
        h0 = ATT_GROUP * gk
        probs = []
        for which, k_op in ((0, k_lo), (1, k_hi)):
            sink = jnp.where(upper_half_rows, sinks_ref[h0 + 2 + which], sinks_ref[h0 + which]) * LOG2_E
            s = _dot_nt(q2, k_op) + bias
            s_first = s[:, :LANES] + jnp.where(lane_ids == 0, sink, 0.0)
            m = jnp.max(jnp.maximum(s_first, s[:, LANES:]), axis=-1, keepdims=True)
            probs.append(jnp.exp2(jnp.concatenate([s_first, s[:, LANES:]], axis=1) - m).astype(BF16))
        att_units[jb, gk] = jnp.concatenate(probs, axis=1)

    def att_values(jb, gk):
        keys = slice(jb * blk, (jb + 2) * blk)
        r_lo = jnp.concatenate([sink_slot(kv_all[4 + 2 * gk][keys]), ones_lo], axis=1)
        r_hi = jnp.concatenate([sink_slot(kv_all[5 + 2 * gk][keys]), ones_hi], axis=1)
        out = _dot(att_units[jb, gk], jnp.concatenate([r_lo, r_hi], axis=0))
        att_units[jb, gk] = out[:, :LANES] / out[:, LANES:]

    units = [(jb, gk) for jb in range(ts // blk) for gk in range(ATT_KV_HEADS)]
    att_scores(*units[0])
    for prev_u, u in zip(units[:-1], units[1:]):
        att_scores(*u)
        att_values(*prev_u)
    att_values(*units[-1])
    att_rows = []
    for jb in range(ts // blk):
        cols_ = []
        for gk in range(ATT_KV_HEADS):
            cols_ += [att_units[jb, gk][:blk], att_units[jb, gk][blk:]]
        att_rows.append(jnp.concatenate(cols_, axis=1))
    att = jnp.concatenate(att_rows, axis=0)

    lbl = lbl_ref[...]
    lexp = jnp.exp(lbl - jnp.max(lbl, axis=0, keepdims=True))
    lb = lexp[0:1, :] / jnp.sum(lexp, axis=0, keepdims=True)
    tri = tri_ref[...]
    row_ids = lax.broadcasted_iota(jnp.int32, (ts, HG_DIM), 0)
    half = ts // 2
    f_all = 0.5 * (1.0 + lb) + (0.5 * (1.0 - lb)) * jnp.tanh(0.5 * p_f)
    g_all = jnp.log(f_all) * LOG2_E
    g_hi = g_all.astype(BF16)
    g_lo = (g_all - g_hi.astype(F32)).astype(BF16)
    b2 = _dot(tri, jnp.concatenate([g_hi, g_lo], axis=1))
    b_all = b2[:, :HG_WIDTH] + b2[:, HG_WIDTH:]
    n_tiles = half // SUBLANES
    hs = [dict() for _ in range(HG_HEADS)]

    def prep(h):
        cols = slice(h * HG_DIM, (h + 1) * HG_DIM)
        f = f_all[:, cols]
        b = b_all[:, cols]
        kk = 1.0 - f
        qf = _silu(p_q[:, cols])
        vv = p_i[:, cols].astype(BF16)
        w_incl = jnp.exp2(b)
        w_tail = jnp.exp2(b[ts - 1:ts, :] - b)
        st = state_ref[h]
        o = _dot_nt((qf * w_incl).astype(BF16), st.astype(BF16))
        state_ref[h] = st * w_incl[ts - 1:ts, :] + _dot_tn(vv, (kk * w_tail).astype(BF16))
        hs[h].update(f=f, b=b, kk=kk, qf=qf, vv=vv, o=o, cols=cols)

    def levels(h):
        d = hs[h]
        qf, kk, b, f = d["qf"], d["kk"], d["b"], d["f"]
        quad = [[None] * n_tiles, [None] * n_tiles]
        for li, m in enumerate(HG_LEVELS):
            x_f = _select_rows(m, qf, kk, row_ids) * _level_decay(m, b, f, row_ids)
            x_l = x_f.astype(BF16)
            if 2 * m == ts:
                d["s10"] = _dot_nt(x_l[half:], x_l[:half])
                continue
            if m >= SUBLANES:
                tiles = [t for t in range(n_tiles) if (t * SUBLANES // m) % 2 == 1]
            else:
                tiles = list(range(n_tiles))
            for hf in range(2):
                r0 = hf * half
                if len(tiles) == n_tiles:
                    lhs = x_l[r0:r0 + half]
                else:
                    lhs = jnp.concatenate([x_f[r0 + t * SUBLANES:r0 + (t + 1) * SUBLANES] for t in tiles],
                                          axis=0).astype(BF16)
                sc = _dot_nt(lhs, x_l[r0:r0 + half])
                for i, t in enumerate(tiles):
                    m0 = (li - 1) * LANES + t * SUBLANES
                    part = sc[i * SUBLANES:(i + 1) * SUBLANES] * lmask_ref[m0:m0 + SUBLANES, :]
                    quad[hf][t] = part if quad[hf][t] is None else quad[hf][t] + part
        d["s00"] = jnp.concatenate(quad[0], axis=0).astype(BF16)
        d["s1"] = jnp.concatenate([d["s10"], jnp.concatenate(quad[1], axis=0)], axis=1).astype(BF16)

    def finish(h):
        d = hs[h]
        vv = d["vv"]
        o_top = _dot(d["s00"], vv[:half])
        o_bot = _dot(d["s1"], vv)
        diag = jnp.sum(d["qf"] * d["kk"], axis=-1, keepdims=True)
        o = d["o"] + jnp.concatenate([o_top, o_bot], axis=0) + diag * p_i[:, d["cols"]]
        d["rec"] = o * _rms_scale(o)

    prep(0)
    prep(1)
    levels(0)
    prep(2)
    levels(1)
    prep(3)
    levels(2)
    finish(0)
    levels(3)
    finish(1)
    finish(2)
    finish(3)
    rec_cols = [hs[h]["rec"] for h in range(HG_HEADS)]
    rec_n = (jnp.concatenate(rec_cols, axis=1) * ghg_ref[...] * _silu(p_g)).astype(BF16)

    att_n = (att * _rms_scale(att) * gatt_ref[...]).astype(BF16)
    o_ref[...] = x + _dot(jnp.concatenate([att_n, rec_n], axis=1), wout_ref[...])


def _cast_chunk_rows(n_rows, n_steps):
    tile = 2 * SUBLANES
    rows = tile
    while n_rows % rows or n_rows // rows > n_steps:
        rows += tile
    return rows


def _mixer(x, gmix, w_in, positions, w_out, sinks, gatt, lbl, ghg, side_weights):
    b, s, _ = x.shape
    ts = MIX_TILE
    tri, lmask = _hgrn_constants()
    invf, ecos, esin = _rotary_constants()
    bias = _attention_bias()
    pos = positions.reshape(b * (s // ts), 1, ts)
    cur = lambda bi, i: (bi, i, 0)
    fixed = lambda bi, i: (0, 0)
    n_cast_steps = b * (s // ts) // CAST_STRIDE
    side_scratch = []
    for w in side_weights:
        stage = (2, _cast_chunk_rows(w.shape[0], n_cast_steps), w.shape[1])
        side_scratch += [pltpu.VMEM(stage, F32), pltpu.VMEM(stage, BF16),
                         pltpu.SemaphoreType.DMA((2,)), pltpu.SemaphoreType.DMA((2,))]
    hbm = pl.BlockSpec(memory_space=pl.ANY)
    outs = pl.pallas_call(
        _mixer_kernel,
        grid=(b, s // ts),
        in_specs=[
            pl.BlockSpec(memory_space=pltpu.SMEM),
            pl.BlockSpec((None, ts, D_MODEL), cur),
            pl.BlockSpec((1, D_MODEL), fixed),
            pl.BlockSpec(w_in.shape, fixed, pipeline_mode=pl.Buffered(1)),
            pl.BlockSpec((None, 1, ts), lambda bi, i: (bi * (s // ts) + i, 0, 0)),
            pl.BlockSpec(invf.shape, fixed),
            pl.BlockSpec(ecos.shape, fixed),
            pl.BlockSpec(esin.shape, fixed),
            pl.BlockSpec((D_MODEL, D_MODEL), fixed, pipeline_mode=pl.Buffered(1)),
            pl.BlockSpec((1, ATT_WIDTH), fixed),
            pl.BlockSpec(lbl.shape, fixed),
            pl.BlockSpec((1, HG_WIDTH), fixed),
            pl.BlockSpec(tri.shape, fixed),
            pl.BlockSpec(lmask.shape, fixed),
            pl.BlockSpec(bias.shape, lambda bi, i: (0, 0, 0)),
        ] + [hbm] * len(side_weights),
        out_specs=[pl.BlockSpec((None, ts, D_MODEL), cur)] + [hbm] * len(side_weights),
        out_shape=[jax.ShapeDtypeStruct((b, s, D_MODEL), F32)]
        + [jax.ShapeDtypeStruct(w.shape, BF16) for w in side_weights],
        scratch_shapes=[
            pltpu.VMEM((HG_HEADS, HG_DIM, HG_DIM), F32),
            pltpu.VMEM((ATT_BLOCK, KV_PAD_WIDTH), BF16),
        ] + side_scratch,
        compiler_params=pltpu.CompilerParams(
            dimension_semantics=("arbitrary", "arbitrary"), vmem_limit_bytes=VMEM_LIMIT_BYTES),
        name="mixer",
    )(sinks, x, gmix, w_in, pos, invf, ecos, esin, w_out, gatt, lbl, ghg, tri, lmask, bias, *side_weights)
    return outs[0], outs[1:]


def _mem_kv_kernel(mem_ref, gain_ref, w_ref, k_ref, v_ref):
    m = mem_ref[...]
    mn = (m * _rms_scale(m) * gain_ref[...]).astype(BF16)
    kv = _dot(mn, w_ref[...])
    k_ref[...] = kv[:, :D_MODEL].astype(BF16)
    v_ref[...] = kv[:, D_MODEL:].astype(BF16)


def _mem_kv(mem, gain, w_xkv):
    b, m, _ = mem.shape
    blk = lambda bi: (bi, 0, 0)
    fixed = lambda bi: (0, 0)
    return pl.pallas_call(
        _mem_kv_kernel,
        grid=(b,),
        in_specs=[
            pl.BlockSpec((None, m, D_MODEL), blk),
            pl.BlockSpec((1, D_MODEL), fixed),
            pl.BlockSpec((D_MODEL, 2 * D_MODEL), fixed),
        ],
        out_specs=[pl.BlockSpec((None, m, D_MODEL), blk), pl.BlockSpec((None, m, D_MODEL), blk)],
        out_shape=[jax.ShapeDtypeStruct((b, m, D_MODEL), BF16)] * 2,
        compiler_params=pltpu.CompilerParams(
            dimension_semantics=("arbitrary",), vmem_limit_bytes=VMEM_LIMIT_BYTES),
        name="mem_kv",
    )(mem, gain, w_xkv)


def _xattn_kernel(x_ref, gain_ref, wq_ref, k_ref, v_ref, wo_ref, o_ref):
    x = x_ref[...]
    hq = (x * _rms_scale(x) * gain_ref[...]).astype(BF16)
    q = _dot(hq, wq_ref[...]).astype(BF16)
    outs = []
    for h in range(X_HEADS):
        cols = slice(h * X_HEAD_DIM, (h + 1) * X_HEAD_DIM)
        s = _dot_nt(q[:, cols], k_ref[:, cols]) * (X_HEAD_DIM ** -0.5)
        p = jnp.exp(s - jnp.max(s, axis=-1, keepdims=True))
        o = _dot(p.astype(BF16), v_ref[:, cols])
        outs.append((o / jnp.sum(p, axis=-1, keepdims=True)).astype(BF16))
    xo = jnp.concatenate(outs, axis=1)
    o_ref[...] = x + _dot(xo, wo_ref[...])


def _xattn(x, gain, w_xq, xk, xv, w_xo, tm):
    b, s, _ = x.shape
    m = xk.shape[1]
    cur = lambda bi, i: (bi, i, 0)
    mem = lambda bi, i: (bi, 0, 0)
    fixed = lambda bi, i: (0, 0)
    return pl.pallas_call(
        _xattn_kernel,
        grid=(b, s // tm),
        in_specs=[
            pl.BlockSpec((None, tm, D_MODEL), cur),
            pl.BlockSpec((1, D_MODEL), fixed),
            pl.BlockSpec((D_MODEL, D_MODEL), fixed),
            pl.BlockSpec((None, m, D_MODEL), mem),
            pl.BlockSpec((None, m, D_MODEL), mem),
            pl.BlockSpec((D_MODEL, D_MODEL), fixed),
        ],
        out_specs=pl.BlockSpec((None, tm, D_MODEL), cur),
        out_shape=jax.ShapeDtypeStruct((b, s, D_MODEL), F32),
        compiler_params=pltpu.CompilerParams(
            dimension_semantics=("arbitrary", "arbitrary"), vmem_limit_bytes=VMEM_LIMIT_BYTES),
        name="xattn",
    )(x, gain, w_xq, xk, xv, w_xo)


FFN_CHUNKS = ((0, 1024), (1024, 2048), (2048, FFN_HIDDEN))

def _ffn_kernel(x_ref, gain_ref, wgu_ref, wd_ref, gfin_ref, o_ref):
    x = x_ref[...]
    hf = (x * _rms_scale(x) * gain_ref[...]).astype(BF16)
    acc = x
    for lo, hi in FFN_CHUNKS:
        gate = _dot(hf, wgu_ref[:, lo:hi])
        upv = _dot(hf, wgu_ref[:, FFN_HIDDEN + lo:FFN_HIDDEN + hi])
        act = (_silu(gate) * upv).astype(BF16)
        acc = acc + _dot(act, wd_ref[lo:hi, :])
    o_ref[...] = acc * _rms_scale(acc) * gfin_ref[...]


def _ffn(x2d, gain, w_gu, w_d, gfin, tm):
    n = x2d.shape[0]
    row = lambda i: (i, 0)
    fixed = lambda i: (0, 0)
    return pl.pallas_call(
        _ffn_kernel,
        grid=(n // tm,),
        in_specs=[
            pl.BlockSpec((tm, D_MODEL), row),
            pl.BlockSpec((1, D_MODEL), fixed),
            pl.BlockSpec((D_MODEL, 2 * FFN_HIDDEN), fixed, pipeline_mode=pl.Buffered(1)),
            pl.BlockSpec((FFN_HIDDEN, D_MODEL), fixed, pipeline_mode=pl.Buffered(1)),
            pl.BlockSpec((1, D_MODEL), fixed),
        ],
        out_specs=pl.BlockSpec((tm, D_MODEL), row),
        out_shape=jax.ShapeDtypeStruct((n, D_MODEL), F32),
        compiler_params=pltpu.CompilerParams(
            dimension_semantics=("arbitrary",), vmem_limit_bytes=VMEM_LIMIT_BYTES),
        name="ffn",
    )(x2d, gain, w_gu, w_d, gfin)


def kernel(x, mem, positions, norm_mix, w_in, att_sinks, att_out_gain, hg_lb_logits, hg_out_gain,
           w_out, norm_xattn, norm_mem, w_xq, w_xkv, w_xo, norm_ffn, w_gate_up, w_down, norm_final):
    b, s, d = x.shape
    n = b * s
    assert w_in.shape[0] == 1 and hg_lb_logits.shape[0] == 2, "single-layer block only"
    row = lambda v: v.reshape(1, -1).astype(F32)
    side = [w[0].astype(F32) for w in (w_xkv, w_xq, w_xo, w_gate_up, w_down)]
    x, (w_xkv_b, w_xq_b, w_xo_b, w_gu_b, w_d_b) = _mixer(
        x, row(norm_mix[0]), w_in[0].astype(BF16), positions,
        w_out[0].astype(BF16), att_sinks[0].astype(F32), row(att_out_gain[0]),
        hg_lb_logits.astype(F32), row(hg_out_gain[0]), side)
    xk, xv = _mem_kv(mem, row(norm_mem[0]), w_xkv_b)
    x = _xattn(x, row(norm_xattn[0]), w_xq_b, xk, xv, w_xo_b, tm=1024)
    y = _ffn(x.reshape(n, d), row(norm_ffn[0]), w_gu_b, w_d_b, row(norm_final), tm=1024)
    return y.reshape(b, s, d)
```

```python
import numpy as np
import jax
import jax.numpy as jnp
from jax import lax
from jax.experimental import pallas as pl
from jax.experimental.pallas import tpu as pltpu

F32 = jnp.float32
BF16 = jnp.bfloat16

D_MODEL = 1024
MEM_LEN = 256
ATT_HEADS = 8
ATT_KV_HEADS = 2
ATT_GROUP = ATT_HEADS // ATT_KV_HEADS
ATT_HEAD_DIM = 64
ATT_WIDTH = ATT_HEADS * ATT_HEAD_DIM
ATT_KV_WIDTH = ATT_KV_HEADS * ATT_HEAD_DIM
ATT_BLOCK = 128
ROPE_THETA = 500000.0
ROPE_DIM = ATT_HEAD_DIM // 4
ROPE_HALF = ROPE_DIM // 2
HG_HEADS = 4
HG_DIM = 128
HG_WIDTH = HG_HEADS * HG_DIM
X_HEADS = 4
X_HEAD_DIM = D_MODEL // X_HEADS
FFN_HIDDEN = 2816
RMS_EPS = 1e-6
LANES = 128
SUBLANES = 8
MASK_VALUE = -1e30
LOG2_E = 1.4426950408889634

MIX_TILE = 256
HG_LEVELS = (128, 64, 32, 16, 8, 4, 2, 1)
KV_PAD_WIDTH = 2 * ATT_KV_HEADS * 2 * LANES
CAST_STRIDE = 4

VMEM_LIMIT_BYTES = 56 * 1024 * 1024


def _rms_scale(x):
    return lax.rsqrt(jnp.mean(x * x, axis=-1, keepdims=True) + RMS_EPS)


def _silu(x):
    h = 0.5 * x
    return h + h * jnp.tanh(h)


def _dot(a, b):
    return jnp.dot(a, b, preferred_element_type=F32)


def _dot_nt(a, b):
    return lax.dot_general(a, b, (((1,), (1,)), ((), ())), preferred_element_type=F32)


def _dot_tn(a, b):
    return lax.dot_general(a, b, (((0,), (0,)), ((), ())), preferred_element_type=F32)


def _cast_step(t, issue, src_hbm, dst_hbm, in_stage, out_stage, in_sem, out_sem):
    rows = in_stage.shape[1]
    n = src_hbm.shape[0] // rows
    assert n * rows == src_hbm.shape[0] and src_hbm.shape == dst_hbm.shape
    slot = lax.rem(t, 2)

    def read(c, s):
        return pltpu.make_async_copy(src_hbm.at[pl.ds(c * rows, rows), :], in_stage.at[s], in_sem.at[s])

    def write(c, s):
        return pltpu.make_async_copy(out_stage.at[s], dst_hbm.at[pl.ds(c * rows, rows), :], out_sem.at[s])

    if issue:
        @pl.when(t == 0)
        def _():
            read(0, 0).start()

        @pl.when(t + 1 < n)
        def _():
            read(t + 1, 1 - slot).start()

        return

    @pl.when(t < n)
    def _():
        read(t, slot).wait()

        @pl.when(t >= 2)
        def _():
            write(t - 2, slot).wait()

        out_stage[slot] = in_stage[slot].astype(BF16)
        write(t, slot).start()

    @pl.when(t == n - 1)
    def _():
        write(t - 1, 1 - slot).wait()
        write(t, slot).wait()


def _hgrn_constants():
    r = np.arange(MIX_TILE)[:, None]
    j = np.arange(MIX_TILE)[None, :]
    tri = (j <= r).astype(np.float32)
    rr = np.arange(LANES)[:, None]
    jj = np.arange(LANES)[None, :]
    masks = []
    for m in HG_LEVELS:
        if 2 * m <= LANES:
            masks.append(((rr // (2 * m)) == (jj // (2 * m))) & ((rr % (2 * m)) >= m) & ((jj % (2 * m)) < m))
    lmask = np.concatenate(masks, axis=0).astype(np.float32)
    return jnp.asarray(tri, BF16), jnp.asarray(lmask, F32)


def _attention_bias():
    i = (np.arange(2 * ATT_BLOCK) % ATT_BLOCK)[:, None]
    j = np.arange(2 * ATT_BLOCK)[None, :]
    band = (j > i) & (j <= i + ATT_BLOCK)
    variants = (band, band & (j >= ATT_BLOCK))
    return jnp.asarray(np.stack([np.where(v | (j == 0), 0.0, MASK_VALUE) for v in variants]), F32)


def _rotary_constants():
    inv_freq = jnp.power(jnp.float32(ROPE_THETA),
                         -jnp.arange(ROPE_HALF, dtype=F32) * (2.0 / ROPE_DIM))
    invf = jnp.broadcast_to(inv_freq[:, None], (ROPE_HALF, MIX_TILE))
    lane = np.arange(LANES)
    in_head = lane % ATT_HEAD_DIM
    freq_row = (np.arange(ROPE_HALF)[:, None] == (in_head % ROPE_HALF)[None, :])
    rot = freq_row & (in_head < ROPE_DIM)[None, :]
    ecos = np.zeros((4 * ROPE_HALF, LANES), np.float32)
    ecos[:ROPE_HALF] = rot
    ecos[ROPE_HALF:2 * ROPE_HALF] = rot
    ecos[2 * ROPE_HALF] = in_head >= ROPE_DIM
    sign = np.where(in_head < ROPE_HALF, -1.0, 1.0)[None, :]
    esin = np.concatenate([rot * sign, rot * sign], axis=0).astype(np.float32)
    return invf, jnp.asarray(ecos, BF16), jnp.asarray(esin, BF16)


def _level_decay(m, b, f, row_ids):
    ts = b.shape[0]
    if m >= SUBLANES // 2:
        b3 = b.reshape(ts // (2 * m), 2 * m, HG_DIM)
        ref = b3[:, m - 1:m, :]
        if m >= SUBLANES:
            e3 = jnp.concatenate([ref - b3[:, :m, :], b3[:, m:, :] - ref], axis=1)
        else:
            e3 = -jnp.abs(b3 - ref)
        return jnp.exp2(e3).reshape(ts, HG_DIM)
    if m == 2:
        pos = row_ids & 3
        f_next = pltpu.roll(f, ts - 1, axis=0)
        f_prev = pltpu.roll(f, 1, axis=0)
        return jnp.where(pos == 0, f_next, jnp.where(pos == 1, 1.0, jnp.where(pos == 2, f, f * f_prev)))
    assert m == 1
    return jnp.where((row_ids & 1) != 0, f, 1.0)


def _select_rows(m, upper_src, lower_src, row_ids):
    c = upper_src.shape[0]
    if m >= SUBLANES:
        pieces = []
        for b0 in range(0, c, 2 * m):
            pieces.append(lower_src[b0:b0 + m])
            pieces.append(upper_src[b0 + m:b0 + 2 * m])
        return jnp.concatenate(pieces, axis=0)
    return jnp.where((row_ids & m) != 0, upper_src, lower_src)


def _rotary_tables(pos_ref, invf_ref, ecos_ref, esin_ref):
    ang = invf_ref[...] * pos_ref[...].astype(F32)

    def spread(t, e_ref, extra):
        t_hi = t.astype(BF16).astype(F32)
        rows = jnp.concatenate([t_hi, t - t_hi] + extra, axis=0).astype(BF16)
        return _dot_tn(rows, e_ref[...])

    ones = jnp.ones_like(ang)
    cos_t = spread(jnp.cos(ang), ecos_ref, [ones, jnp.zeros_like(ang)])
    sin_t = spread(jnp.sin(ang), esin_ref, [])
    return cos_t, sin_t


def _mixer_kernel(sinks_ref, x_ref, gmix_ref, win_ref, pos_ref, invf_ref, ecos_ref, esin_ref, wout_ref,
                  gatt_ref, lbl_ref, ghg_ref, tri_ref, lmask_ref, bias_ref, *rest):
    n_side = (len(rest) - 3) // 6
    side_src, o_ref, side_dst = rest[:n_side], rest[n_side], rest[n_side + 1:2 * n_side + 1]
    state_ref, kvprev_ref = rest[2 * n_side + 1:2 * n_side + 3]
    side_scratch = rest[2 * n_side + 3:]
    step = pl.program_id(1)
    ts = MIX_TILE
    blk = ATT_BLOCK

    t = pl.program_id(0) * pl.num_programs(1) + step

    @pl.when(lax.rem(t, CAST_STRIDE) == 0)
    def _():
        for issue in (True, False):
            for k in range(n_side):
                _cast_step(lax.div(t, CAST_STRIDE), issue, side_src[k], side_dst[k],
                           *side_scratch[4 * k:4 * k + 4])

    @pl.when(step == 0)
    def _():
        state_ref[...] = jnp.zeros_like(state_ref)
        kvprev_ref[...] = jnp.zeros_like(kvprev_ref)

    x = x_ref[...]
    hn = (x * _rms_scale(x) * gmix_ref[...]).astype(BF16)

    def proj(c0, c1):
        return _dot(hn, win_ref[:, c0:c1])

    cos, sin = _rotary_tables(pos_ref, invf_ref, ecos_ref, esin_ref)
    lane = lax.broadcasted_iota(jnp.int32, cos.shape, 1)
    first_half = (lane & (ATT_HEAD_DIM - 1)) < ROPE_HALF
    lo_head = lane < ATT_HEAD_DIM

    def rotate(col):
        partner = jnp.where(first_half,
                            pltpu.roll(col, LANES - ROPE_HALF, axis=1),
                            pltpu.roll(col, ROPE_HALF, axis=1))
        return col * cos + partner * sin

    def padded(col):
        swapped = pltpu.roll(col, ATT_HEAD_DIM, axis=1)
        zero = jnp.zeros_like(col)
        return [v.astype(BF16) for v in (jnp.where(lo_head, col, zero), jnp.where(lo_head, zero, swapped),
                                         jnp.where(lo_head, swapped, zero), jnp.where(lo_head, zero, col))]

    pq = proj(0, ATT_WIDTH)
    q_cols = [(rotate(pq[:, j * LANES:(j + 1) * LANES]) * (ATT_HEAD_DIM ** -0.5 * LOG2_E)).astype(BF16)
              for j in range(ATT_WIDTH // LANES)]
    pkv = proj(ATT_WIDTH, ATT_WIDTH + 2 * ATT_KV_WIDTH)
    kv_new = padded(rotate(pkv[:, :LANES])) + padded(pkv[:, LANES:])
    kv_all = [jnp.concatenate([kvprev_ref[:, i * LANES:(i + 1) * LANES], v], axis=0)
              for i, v in enumerate(kv_new)]
    for i, v in enumerate(kv_new):
        kvprev_ref[:, i * LANES:(i + 1) * LANES] = v[ts - blk:]
    r0c = ATT_WIDTH + 2 * ATT_KV_WIDTH
    p_q, p_f, p_i, p_g = (proj(r0c + k * HG_WIDTH, r0c + (k + 1) * HG_WIDTH) for k in range(4))

    upper_half_rows = lax.broadcasted_iota(jnp.int32, (2 * blk, 1), 0) >= blk
    lane_ids = lax.broadcasted_iota(jnp.int32, (2 * blk, LANES), 1)
    lane_lo = lane_ids < ATT_HEAD_DIM
    ones_lo = jnp.where(lane_lo, 1.0, 0.0).astype(BF16)
    ones_hi = jnp.where(lane_lo, 0.0, 1.0).astype(BF16)
    bf16_rows = 2 * SUBLANES
    drop_row0 = jnp.where(lax.broadcasted_iota(jnp.int32, (bf16_rows, LANES), 0) > 0, 1.0, 0.0).astype(BF16)

    def sink_slot(op):
        return jnp.concatenate([op[:bf16_rows] * drop_row0, op[bf16_rows:]], axis=0)

    att_units = {}

    def att_scores(jb, gk):
        keys = slice(jb * blk, (jb + 2) * blk)
        bias = bias_ref[jnp.where(step > 0, 0, 1)] if jb == 0 else bias_ref[0]
        q2 = jnp.concatenate([q_cols[2 * gk][jb * blk:(jb + 1) * blk],
                              q_cols[2 * gk + 1][jb * blk:(jb + 1) * blk]], axis=0)
        k_lo = sink_slot(kv_all[2 * gk][keys])
        k_hi = sink_slot(kv_all[2 * gk + 1][keys])
        h0 = ATT_GROUP * gk
        probs = []
        for which, k_op in ((0, k_lo), (1, k_hi)):
            sink = jnp.where(upper_half_rows, sinks_ref[h0 + 2 + which], sinks_ref[h0 + which]) * LOG2_E
            s = _dot_nt(q2, k_op) + bias
            s_first = s[:, :LANES] + jnp.where(lane_ids == 0, sink, 0.0)
            m = jnp.max(jnp.maximum(s_first, s[:, LANES:]), axis=-1, keepdims=True)
            probs.append(jnp.exp2(jnp.concatenate([s_first, s[:, LANES:]], axis=1) - m).astype(BF16))
        att_units[jb, gk] = jnp.concatenate(probs, axis=1)

    def att_values(jb, gk):
        keys = slice(jb * blk, (jb + 2) * blk)
        r_lo = jnp.concatenate([sink_slot(kv_all[4 + 2 * gk][keys]), ones_lo], axis=1)
        r_hi = jnp.concatenate([sink_slot(kv_all[5 + 2 * gk][keys]), ones_hi], axis=1)
        out = _dot(att_units[jb, gk], jnp.concatenate([r_lo, r_hi], axis=0))
        att_units[jb, gk] = out[:, :LANES] / out[:, LANES:]

    units = [(jb, gk) for jb in range(ts // blk) for gk in range(ATT_KV_HEADS)]
    att_scores(*units[0])
    for prev_u, u in zip(units[:-1], units[1:]):
        att_scores(*u)
        att_values(*prev_u)
    att_values(*units[-1])
    att_rows = []
    for jb in range(ts // blk):
        cols_ = []
        for gk in range(ATT_KV_HEADS):
            cols_ += [att_units[jb, gk][:blk], att_units[jb, gk][blk:]]
        att_rows.append(jnp.concatenate(cols_, axis=1))
    att = jnp.concatenate(att_rows, axis=0)

    lbl = lbl_ref[...]
    lexp = jnp.exp(lbl - jnp.max(lbl, axis=0, keepdims=True))
    lb = lexp[0:1, :] / jnp.sum(lexp, axis=0, keepdims=True)
    tri = tri_ref[...]
    row_ids = lax.broadcasted_iota(jnp.int32, (ts, HG_DIM), 0)
    half = ts // 2
    f_all = 0.5 * (1.0 + lb) + (0.5 * (1.0 - lb)) * jnp.tanh(0.5 * p_f)
    g_all = jnp.log(f_all) * LOG2_E
    g_hi = g_all.astype(BF16)
    g_lo = (g_all - g_hi.astype(F32)).astype(BF16)
    b2 = _dot(tri, jnp.concatenate([g_hi, g_lo], axis=1))
    b_all = b2[:, :HG_WIDTH] + b2[:, HG_WIDTH:]
    n_tiles = half // SUBLANES
    hs = [dict() for _ in range(HG_HEADS)]

    def prep(h):
        cols = slice(h * HG_DIM, (h + 1) * HG_DIM)
        f = f_all[:, cols]
        b = b_all[:, cols]
        kk = 1.0 - f
        qf = _silu(p_q[:, cols])
        vv = p_i[:, cols].astype(BF16)
        w_incl = jnp.exp2(b)
        w_tail = jnp.exp2(b[ts - 1:ts, :] - b)
        st = state_ref[h]
        o = _dot_nt((qf * w_incl).astype(BF16), st.astype(BF16))
        state_ref[h] = st * w_incl[ts - 1:ts, :] + _dot_tn(vv, (kk * w_tail).astype(BF16))
        hs[h].update(f=f, b=b, kk=kk, qf=qf, vv=vv, o=o, cols=cols)

    def levels(h):
        d = hs[h]
        qf, kk, b, f = d["qf"], d["kk"], d["b"], d["f"]
        quad = [[None] * n_tiles, [None] * n_tiles]
        for li, m in enumerate(HG_LEVELS):
            x_f = _select_rows(m, qf, kk, row_ids) * _level_decay(m, b, f, row_ids)
            x_l = x_f.astype(BF16)
            if 2 * m == ts:
                d["s10"] = _dot_nt(x_l[half:], x_l[:half])
                continue
            if m >= SUBLANES:
                tiles = [t for t in range(n_tiles) if (t * SUBLANES // m) % 2 == 1]
            else:
                tiles = list(range(n_tiles))
            for hf in range(2):
                r0 = hf * half
                if len(tiles) == n_tiles:
                    lhs = x_l[r0:r0 + half]
                else:
                    lhs = jnp.concatenate([x_f[r0 + t * SUBLANES:r0 + (t + 1) * SUBLANES] for t in tiles],
                                          axis=0).astype(BF16)
                sc = _dot_nt(lhs, x_l[r0:r0 + half])
                for i, t in enumerate(tiles):
                    m0 = (li - 1) * LANES + t * SUBLANES
                    part = sc[i * SUBLANES:(i + 1) * SUBLANES] * lmask_ref[m0:m0 + SUBLANES, :]
                    quad[hf][t] = part if quad[hf][t] is None else quad[hf][t] + part
        d["s00"] = jnp.concatenate(quad[0], axis=0).astype(BF16)
        d["s1"] = jnp.concatenate([d["s10"], jnp.concatenate(quad[1], axis=0)], axis=1).astype(BF16)

    def finish(h):
        d = hs[h]
        vv = d["vv"]
        o_top = _dot(d["s00"], vv[:half])
        o_bot = _dot(d["s1"], vv)
        diag = jnp.sum(d["qf"] * d["kk"], axis=-1, keepdims=True)
        o = d["o"] + jnp.concatenate([o_top, o_bot], axis=0) + diag * p_i[:, d["cols"]]
        d["rec"] = o * _rms_scale(o)

    prep(0)
    prep(1)
    levels(0)
    prep(2)
    levels(1)
    prep(3)
    levels(2)
    finish(0)
    levels(3)
    finish(1)
    finish(2)
    finish(3)
    rec_cols = [hs[h]["rec"] for h in range(HG_HEADS)]
    rec_n = (jnp.concatenate(rec_cols, axis=1) * ghg_ref[...] * _silu(p_g)).astype(BF16)

    att_n = (att * _rms_scale(att) * gatt_ref[...]).astype(BF16)
    o_ref[...] = x + _dot(jnp.concatenate([att_n, rec_n], axis=1), wout_ref[...])


def _cast_chunk_rows(n_rows, n_steps):
    tile = 2 * SUBLANES
    rows = tile
    while n_rows % rows or n_rows // rows > n_steps:
        rows += tile
    return rows


def _mixer(x, gmix, w_in, positions, w_out, sinks, gatt, lbl, ghg, side_weights):
    b, s, _ = x.shape
    ts = MIX_TILE
    tri, lmask = _hgrn_constants()
    invf, ecos, esin = _rotary_constants()
    bias = _attention_bias()
    pos = positions.reshape(b * (s // ts), 1, ts)
    cur = lambda bi, i: (bi, i, 0)
    fixed = lambda bi, i: (0, 0)
    n_cast_steps = b * (s // ts) // CAST_STRIDE
    side_scratch = []
    for w in side_weights:
        stage = (2, _cast_chunk_rows(w.shape[0], n_cast_steps), w.shape[1])
        side_scratch += [pltpu.VMEM(stage, F32), pltpu.VMEM(stage, BF16),
                         pltpu.SemaphoreType.DMA((2,)), pltpu.SemaphoreType.DMA((2,))]
    hbm = pl.BlockSpec(memory_space=pl.ANY)
    outs = pl.pallas_call(
        _mixer_kernel,
        grid=(b, s // ts),
        in_specs=[
            pl.BlockSpec(memory_space=pltpu.SMEM),
            pl.BlockSpec((None, ts, D_MODEL), cur),
            pl.BlockSpec((1, D_MODEL), fixed),
            pl.BlockSpec(w_in.shape, fixed, pipeline_mode=pl.Buffered(1)),
            pl.BlockSpec((None, 1, ts), lambda bi, i: (bi * (s // ts) + i, 0, 0)),
            pl.BlockSpec(invf.shape, fixed),
            pl.BlockSpec(ecos.shape, fixed),
            pl.BlockSpec(esin.shape, fixed),
            pl.BlockSpec((D_MODEL, D_MODEL), fixed, pipeline_mode=pl.Buffered(1)),
            pl.BlockSpec((1, ATT_WIDTH), fixed),
            pl.BlockSpec(lbl.shape, fixed),
            pl.BlockSpec((1, HG_WIDTH), fixed),
            pl.BlockSpec(tri.shape, fixed),
            pl.BlockSpec(lmask.shape, fixed),
            pl.BlockSpec(bias.shape, lambda bi, i: (0, 0, 0)),
        ] + [hbm] * len(side_weights),
        out_specs=[pl.BlockSpec((None, ts, D_MODEL), cur)] + [hbm] * len(side_weights),
        out_shape=[jax.ShapeDtypeStruct((b, s, D_MODEL), F32)]
        + [jax.ShapeDtypeStruct(w.shape, BF16) for w in side_weights],
        scratch_shapes=[
            pltpu.VMEM((HG_HEADS, HG_DIM, HG_DIM), F32),
            pltpu.VMEM((ATT_BLOCK, KV_PAD_WIDTH), BF16),
        ] + side_scratch,
        compiler_params=pltpu.CompilerParams(
            dimension_semantics=("arbitrary", "arbitrary"), vmem_limit_bytes=VMEM_LIMIT_BYTES),
        name="mixer",
    )(sinks, x, gmix, w_in, pos, invf, ecos, esin, w_out, gatt, lbl, ghg, tri, lmask, bias, *side_weights)
    return outs[0], outs[1:]


def _mem_kv_kernel(mem_ref, gain_ref, w_ref, k_ref, v_ref):
    m = mem_ref[...]
    mn = (m * _rms_scale(m) * gain_ref[...]).astype(BF16)
    kv = _dot(mn, w_ref[...])
    k_ref[...] = kv[:, :D_MODEL].astype(BF16)
    v_ref[...] = kv[:, D_MODEL:].astype(BF16)


def _mem_kv(mem, gain, w_xkv):
    b, m, _ = mem.shape
    blk = lambda bi: (bi, 0, 0)
    fixed = lambda bi: (0, 0)
    return pl.pallas_call(
        _mem_kv_kernel,
        grid=(b,),
        in_specs=[
            pl.BlockSpec((None, m, D_MODEL), blk),
            pl.BlockSpec((1, D_MODEL), fixed),
            pl.BlockSpec((D_MODEL, 2 * D_MODEL), fixed),
        ],
        out_specs=[pl.BlockSpec((None, m, D_MODEL), blk), pl.BlockSpec((None, m, D_MODEL), blk)],
        out_shape=[jax.ShapeDtypeStruct((b, m, D_MODEL), BF16)] * 2,
        compiler_params=pltpu.CompilerParams(
            dimension_semantics=("arbitrary",), vmem_limit_bytes=VMEM_LIMIT_BYTES),
        name="mem_kv",
    )(mem, gain, w_xkv)


def _xattn_kernel(x_ref, gain_ref, wq_ref, k_ref, v_ref, wo_ref, o_ref):
    x = x_ref[...]
    hq = (x * _rms_scale(x) * gain_ref[...]).astype(BF16)
    q = _dot(hq, wq_ref[...]).astype(BF16)
    outs = []
    for h in range(X_HEADS):
        cols = slice(h * X_HEAD_DIM, (h + 1) * X_HEAD_DIM)
        s = _dot_nt(q[:, cols], k_ref[:, cols]) * (X_HEAD_DIM ** -0.5)
        p = jnp.exp(s - jnp.max(s, axis=-1, keepdims=True))
        o = _dot(p.astype(BF16), v_ref[:, cols])
        outs.append((o / jnp.sum(p, axis=-1, keepdims=True)).astype(BF16))
    xo = jnp.concatenate(outs, axis=1)
    o_ref[...] = x + _dot(xo, wo_ref[...])


def _xattn(x, gain, w_xq, xk, xv, w_xo, tm):
    b, s, _ = x.shape
    m = xk.shape[1]
    cur = lambda bi, i: (bi, i, 0)
    mem = lambda bi, i: (bi, 0, 0)
    fixed = lambda bi, i: (0, 0)
    return pl.pallas_call(
        _xattn_kernel,
        grid=(b, s // tm),
        in_specs=[
            pl.BlockSpec((None, tm, D_MODEL), cur),
            pl.BlockSpec((1, D_MODEL), fixed),
            pl.BlockSpec((D_MODEL, D_MODEL), fixed),
            pl.BlockSpec((None, m, D_MODEL), mem),
            pl.BlockSpec((None, m, D_MODEL), mem),
            pl.BlockSpec((D_MODEL, D_MODEL), fixed),
        ],
        out_specs=pl.BlockSpec((None, tm, D_MODEL), cur),
        out_shape=jax.ShapeDtypeStruct((b, s, D_MODEL), F32),
        compiler_params=pltpu.CompilerParams(
            dimension_semantics=("arbitrary", "arbitrary"), vmem_limit_bytes=VMEM_LIMIT_BYTES),
        name="xattn",
    )(x, gain, w_xq, xk, xv, w_xo)


FFN_CHUNKS = ((0, 1024), (1024, 2048), (2048, FFN_HIDDEN))

def _ffn_kernel(x_ref, gain_ref, wgu_ref, wd_ref, gfin_ref, o_ref):
    x = x_ref[...]
    hf = (x * _rms_scale(x) * gain_ref[...]).astype(BF16)
    acc = x
    for lo, hi in FFN_CHUNKS:
        gate = _dot(hf, wgu_ref[:, lo:hi])
        upv = _dot(hf, wgu_ref[:, FFN_HIDDEN + lo:FFN_HIDDEN + hi])
        act = (_silu(gate) * upv).astype(BF16)
        acc = acc + _dot(act, wd_ref[lo:hi, :])
    o_ref[...] = acc * _rms_scale(acc) * gfin_ref[...]


def _ffn(x2d, gain, w_gu, w_d, gfin, tm):
    n = x2d.shape[0]
    row = lambda i: (i, 0)
    fixed = lambda i: (0, 0)
    return pl.pallas_call(
        _ffn_kernel,
        grid=(n // tm,),
        in_specs=[
            pl.BlockSpec((tm, D_MODEL), row),
            pl.BlockSpec((1, D_MODEL), fixed),
            pl.BlockSpec((D_MODEL, 2 * FFN_HIDDEN), fixed, pipeline_mode=pl.Buffered(1)),
            pl.BlockSpec((FFN_HIDDEN, D_MODEL), fixed, pipeline_mode=pl.Buffered(1)),
            pl.BlockSpec((1, D_MODEL), fixed),
        ],
        out_specs=pl.BlockSpec((tm, D_MODEL), row),
        out_shape=jax.ShapeDtypeStruct((n, D_MODEL), F32),
        compiler_params=pltpu.CompilerParams(
            dimension_semantics=("arbitrary",), vmem_limit_bytes=VMEM_LIMIT_BYTES),
        name="ffn",
    )(x2d, gain, w_gu, w_d, gfin)


def kernel(x, mem, positions, norm_mix, w_in, att_sinks, att_out_gain, hg_lb_logits, hg_out_gain,
           w_out, norm_xattn, norm_mem, w_xq, w_xkv, w_xo, norm_ffn, w_gate_up, w_down, norm_final):
    b, s, d = x.shape
    n = b * s
    assert w_in.shape[0] == 1 and hg_lb_logits.shape[0] == 2, "single-layer block only"
    row = lambda v: v.reshape(1, -1).astype(F32)
    side = [w[0].astype(F32) for w in (w_xkv, w_xq, w_xo, w_gate_up, w_down)]
    x, (w_xkv_b, w_xq_b, w_xo_b, w_gu_b, w_d_b) = _mixer(
        x, row(norm_mix[0]), w_in[0].astype(BF16), positions,
        w_out[0].astype(BF16), att_sinks[0].astype(F32), row(att_out_gain[0]),
        hg_lb_logits.astype(F32), row(hg_out_gain[0]), side)
    xk, xv = _mem_kv(mem, row(norm_mem[0]), w_xkv_b)
    x = _xattn(x, row(norm_xattn[0]), w_xq_b, xk, xv, w_xo_b, tm=1024)
    y = _ffn(x.reshape(n, d), row(norm_ffn[0]), w_gu_b, w_d_b, row(norm_final), tm=1024)
    return y.reshape(b, s, d)
```

```python
import numpy as np
import jax
import jax.numpy as jnp
from jax import lax
from jax.experimental import pallas as pl
from jax.experimental.pallas import tpu as pltpu

F32 = jnp.float32
BF16 = jnp.bfloat16

D_MODEL = 1024
ATT_HEADS = 8
ATT_KV_HEADS = 2
ATT_GROUP = ATT_HEADS // ATT_KV_HEADS
ATT_HEAD_DIM = 64
ATT_WIDTH = ATT_HEADS * ATT_HEAD_DIM
ATT_KV_WIDTH = ATT_KV_HEADS * ATT_HEAD_DIM
ATT_BLOCK = 128
ROPE_THETA = 500000.0
ROPE_DIM = ATT_HEAD_DIM // 4
ROPE_HALF = ROPE_DIM // 2
HG_HEADS = 4
HG_DIM = 128
HG_WIDTH = HG_HEADS * HG_DIM
X_HEADS = 4
X_HEAD_DIM = D_MODEL // X_HEADS
FFN_HIDDEN = 2816
RMS_EPS = 1e-6
LANES = 128
SUBLANES = 8
MASK_VALUE = -1e30
LOG2_E = 1.4426950408889634

MIX_TILE = 256
HG_LEVELS = (128, 64, 32, 16, 8, 4, 2, 1)
KV_PAD_WIDTH = 2 * ATT_KV_HEADS * 2 * LANES
CAST_STRIDE = 4
DENSE_TILE = 1024
OWN_STAGE_ROWS = 128

VMEM_LIMIT_BYTES = 56 * 1024 * 1024


def _rms_scale(x):
    return lax.rsqrt(jnp.mean(x * x, axis=-1, keepdims=True) + RMS_EPS)


def _silu(x):
    h = 0.5 * x
    return h + h * jnp.tanh(h)


def _dot(a, b):
    return jnp.dot(a, b, preferred_element_type=F32)


def _dot_nt(a, b):
    return lax.dot_general(a, b, (((1,), (1,)), ((), ())), preferred_element_type=F32)


def _dot_tn(a, b):
    return lax.dot_general(a, b, (((0,), (0,)), ((), ())), preferred_element_type=F32)


def _stage_weight(src_hbm, dst_ref, stage_ref, sem_ref):
    rows = stage_ref.shape[1]
    n_chunks = dst_ref.shape[0] // rows
    assert n_chunks * rows == dst_ref.shape[0] and src_hbm.shape == dst_ref.shape

    def copy(c):
        return pltpu.make_async_copy(src_hbm.at[pl.ds(c * rows, rows), :], stage_ref.at[c % 2], sem_ref.at[c % 2])

    copy(0).start()
    for c in range(n_chunks):
        if c + 1 < n_chunks:
            copy(c + 1).start()
        copy(c).wait()
        dst_ref[c * rows:(c + 1) * rows, :] = stage_ref[c % 2].astype(BF16)


def _cast_step(t, issue, src_hbm, dst_hbm, in_stage, out_stage, in_sem, out_sem):
    rows = in_stage.shape[1]
    n = src_hbm.shape[0] // rows
    assert n * rows == src_hbm.shape[0] and src_hbm.shape == dst_hbm.shape
    slot = lax.rem(t, 2)

    def read(c, s):
        return pltpu.make_async_copy(src_hbm.at[pl.ds(c * rows, rows), :], in_stage.at[s], in_sem.at[s])

    def write(c, s):
        return pltpu.make_async_copy(out_stage.at[s], dst_hbm.at[pl.ds(c * rows, rows), :], out_sem.at[s])

    if issue:
        @pl.when(t == 0)
        def _():
            read(0, 0).start()

        @pl.when(t + 1 < n)
        def _():
            read(t + 1, 1 - slot).start()

        return

    @pl.when(t < n)
    def _():
        read(t, slot).wait()

        @pl.when(t >= 2)
        def _():
            write(t - 2, slot).wait()

        out_stage[slot] = in_stage[slot].astype(BF16)
        write(t, slot).start()

    @pl.when(t == n - 1)
    def _():
        write(t - 1, 1 - slot).wait()
        write(t, slot).wait()


def _hgrn_constants():
    r = np.arange(MIX_TILE)[:, None]
    j = np.arange(MIX_TILE)[None, :]
    tri = (j <= r).astype(np.float32)
    rr = np.arange(LANES)[:, None]
    jj = np.arange(LANES)[None, :]
    masks = []
    for m in HG_LEVELS:
        if 2 * m <= LANES:
            masks.append(((rr // (2 * m)) == (jj // (2 * m))) & ((rr % (2 * m)) >= m) & ((jj % (2 * m)) < m))
    lmask = np.concatenate(masks, axis=0).astype(np.float32)
    return jnp.asarray(tri, BF16), jnp.asarray(lmask, F32)


def _attention_bias():
    i = (np.arange(2 * ATT_BLOCK) % ATT_BLOCK)[:, None]
    j = np.arange(2 * ATT_BLOCK)[None, :]
    band = (j > i) & (j <= i + ATT_BLOCK)
    variants = (band, band & (j >= ATT_BLOCK))
    return jnp.asarray(np.stack([np.where(v | (j == 0), 0.0, MASK_VALUE) for v in variants]), F32)


def _rotary_constants():
    inv_freq = jnp.power(jnp.float32(ROPE_THETA),
                         -jnp.arange(ROPE_HALF, dtype=F32) * (2.0 / ROPE_DIM))
    invf = jnp.broadcast_to(inv_freq[:, None], (ROPE_HALF, MIX_TILE))
    lane = np.arange(LANES)
    in_head = lane % ATT_HEAD_DIM
    freq_row = (np.arange(ROPE_HALF)[:, None] == (in_head % ROPE_HALF)[None, :])
    rot = freq_row & (in_head < ROPE_DIM)[None, :]
    ecos = np.zeros((4 * ROPE_HALF, LANES), np.float32)
    ecos[:ROPE_HALF] = rot
    ecos[ROPE_HALF:2 * ROPE_HALF] = rot
    ecos[2 * ROPE_HALF] = in_head >= ROPE_DIM
    sign = np.where(in_head < ROPE_HALF, -1.0, 1.0)[None, :]
    esin = np.concatenate([rot * sign, rot * sign], axis=0).astype(np.float32)
    return invf, jnp.asarray(ecos, BF16), jnp.asarray(esin, BF16)


def _level_decay(m, b, f, row_ids):
    ts = b.shape[0]
    if m >= SUBLANES // 2:
        b3 = b.reshape(ts // (2 * m), 2 * m, HG_DIM)
        ref = b3[:, m - 1:m, :]
        if m >= SUBLANES:
            e3 = jnp.concatenate([ref - b3[:, :m, :], b3[:, m:, :] - ref], axis=1)
        else:
            e3 = -jnp.abs(b3 - ref)
        return jnp.exp2(e3).reshape(ts, HG_DIM)
    if m == 2:
        pos = row_ids & 3
        f_next = pltpu.roll(f, ts - 1, axis=0)
        f_prev = pltpu.roll(f, 1, axis=0)
        return jnp.where(pos == 0, f_next, jnp.where(pos == 1, 1.0, jnp.where(pos == 2, f, f * f_prev)))
    assert m == 1
    return jnp.where((row_ids & 1) != 0, f, 1.0)


def _select_rows(m, upper_src, lower_src, row_ids):
    c = upper_src.shape[0]
    if m >= SUBLANES:
        pieces = []
        for b0 in range(0, c, 2 * m):
            pieces.append(lower_src[b0:b0 + m])
            pieces.append(upper_src[b0 + m:b0 + 2 * m])
        return jnp.concatenate(pieces, axis=0)
    return jnp.where((row_ids & m) != 0, upper_src, lower_src)


def _rotary_tables(pos_ref, invf_ref, ecos_ref, esin_ref):
    ang = invf_ref[...] * pos_ref[...].astype(F32)

    def spread(t, e_ref, extra):
        t_hi = t.astype(BF16).astype(F32)
        rows = jnp.concatenate([t_hi, t - t_hi] + extra, axis=0).astype(BF16)
        return _dot_tn(rows, e_ref[...])

    ones = jnp.ones_like(ang)
    cos_t = spread(jnp.cos(ang), ecos_ref, [ones, jnp.zeros_like(ang)])
    sin_t = spread(jnp.sin(ang), esin_ref, [])
    return cos_t, sin_t


def _mixer_kernel(sinks_ref, x_ref, gmix_ref, win_hbm, pos_ref, invf_ref, ecos_ref, esin_ref, wout_hbm,
                  gatt_ref, lbl_ref, ghg_ref, tri_ref, lmask_ref, bias_ref, *rest):
    n_own = 8
    n_side = (len(rest) - 1 - n_own) // 6
    side_src, o_ref, side_dst = rest[:n_side], rest[n_side], rest[n_side + 1:2 * n_side + 1]
    own = rest[2 * n_side + 1:2 * n_side + 1 + n_own]
    state_ref, kvprev_ref, win_ref, wout_ref, win_stage, wout_stage, win_sem, wout_sem = own
    side_scratch = rest[2 * n_side + 1 + n_own:]
    step = pl.program_id(1)
    ts = MIX_TILE
    blk = ATT_BLOCK

    t = pl.program_id(0) * pl.num_programs(1) + step

    @pl.when(t == 0)
    def _():
        _stage_weight(win_hbm, win_ref, win_stage, win_sem)
        _stage_weight(wout_hbm, wout_ref, wout_stage, wout_sem)

    @pl.when(lax.rem(t, CAST_STRIDE) == 0)
    def _():
        for issue in (True, False):
            for k in range(n_side):
                _cast_step(lax.div(t, CAST_STRIDE), issue, side_src[k], side_dst[k],
                           *side_scratch[4 * k:4 * k + 4])

    @pl.when(step == 0)
    def _():
        state_ref[...] = jnp.zeros_like(state_ref)
        kvprev_ref[...] = jnp.zeros_like(kvprev_ref)

    x = x_ref[...]
    hn = (x * _rms_scale(x) * gmix_ref[...]).astype(BF16)

    def proj(c0, c1):
        return _dot(hn, win_ref[:, c0:c1])

    cos, sin = _rotary_tables(pos_ref, invf_ref, ecos_ref, esin_ref)
    lane = lax.broadcasted_iota(jnp.int32, cos.shape, 1)
    first_half = (lane & (ATT_HEAD_DIM - 1)) < ROPE_HALF
    lo_head = lane < ATT_HEAD_DIM

    def rotate(col):
        partner = jnp.where(first_half,
                            pltpu.roll(col, LANES - ROPE_HALF, axis=1),
                            pltpu.roll(col, ROPE_HALF, axis=1))
        return col * cos + partner * sin

    def padded(col):
        swapped = pltpu.roll(col, ATT_HEAD_DIM, axis=1)
        zero = jnp.zeros_like(col)
        return [v.astype(BF16) for v in (jnp.where(lo_head, col, zero), jnp.where(lo_head, zero, swapped),
                                         jnp.where(lo_head, swapped, zero), jnp.where(lo_head, zero, col))]

    pq = proj(0, ATT_WIDTH)
    q_cols = [(rotate(pq[:, j * LANES:(j + 1) * LANES]) * (ATT_HEAD_DIM ** -0.5 * LOG2_E)).astype(BF16)
              for j in range(ATT_WIDTH // LANES)]
    pkv = proj(ATT_WIDTH, ATT_WIDTH + 2 * ATT_KV_WIDTH)
    kv_new = padded(rotate(pkv[:, :LANES])) + padded(pkv[:, LANES:])
    kv_all = [jnp.concatenate([kvprev_ref[:, i * LANES:(i + 1) * LANES], v], axis=0)
              for i, v in enumerate(kv_new)]
    for i, v in enumerate(kv_new):
        kvprev_ref[:, i * LANES:(i + 1) * LANES] = v[ts - blk:]
    r0c = ATT_WIDTH + 2 * ATT_KV_WIDTH
    p_q, p_f, p_i, p_g = (proj(r0c + k * HG_WIDTH, r0c + (k + 1) * HG_WIDTH) for k in range(4))

    upper_half_rows = lax.broadcasted_iota(jnp.int32, (2 * blk, 1), 0) >= blk
    lane_ids = lax.broadcasted_iota(jnp.int32, (2 * blk, LANES), 1)
    lane_lo = lane_ids < ATT_HEAD_DIM
    ones_lo = jnp.where(lane_lo, 1.0, 0.0).astype(BF16)
    ones_hi = jnp.where(lane_lo, 0.0, 1.0).astype(BF16)
    bf16_rows = 2 * SUBLANES
    drop_row0 = jnp.where(lax.broadcasted_iota(jnp.int32, (bf16_rows, LANES), 0) > 0, 1.0, 0.0).astype(BF16)

    def sink_slot(op):
        return jnp.concatenate([op[:bf16_rows] * drop_row0, op[bf16_rows:]], axis=0)

    att_units = {}

    def att_scores(jb, gk):
        keys = slice(jb * blk, (jb + 2) * blk)
        bias = bias_ref[jnp.where(step > 0, 0, 1)] if jb == 0 else bias_ref[0]
        q2 = jnp.concatenate([q_cols[2 * gk][jb * blk:(jb + 1) * blk],
                              q_cols[2 * gk + 1][jb * blk:(jb + 1) * blk]], axis=0)
        k_lo = sink_slot(kv_all[2 * gk][keys])
        k_hi = sink_slot(kv_all[2 * gk + 1][keys])
        h0 = ATT_GROUP * gk
        probs = []
        for which, k_op in ((0, k_lo), (1, k_hi)):
            sink = jnp.where(upper_half_rows, sinks_ref[h0 + 2 + which], sinks_ref[h0 + which]) * LOG2_E
            s = _dot_nt(q2, k_op) + bias
            s_first = s[:, :LANES] + jnp.where(lane_ids == 0, sink, 0.0)
            m = jnp.max(jnp.maximum(s_first, s[:, LANES:]), axis=-1, keepdims=True)
            probs.append(jnp.exp2(jnp.concatenate([s_first, s[:, LANES:]], axis=1) - m).astype(BF16))
        att_units[jb, gk] = jnp.concatenate(probs, axis=1)

    def att_values(jb, gk):
        keys = slice(jb * blk, (jb + 2) * blk)
        r_lo = jnp.concatenate([sink_slot(kv_all[4 + 2 * gk][keys]), ones_lo], axis=1)
        r_hi = jnp.concatenate([sink_slot(kv_all[5 + 2 * gk][keys]), ones_hi], axis=1)
        out = _dot(att_units[jb, gk], jnp.concatenate([r_lo, r_hi], axis=0))
        att_units[jb, gk] = out[:, :LANES] / out[:, LANES:]

    units = [(jb, gk) for jb in range(ts // blk) for gk in range(ATT_KV_HEADS)]
    att_scores(*units[0])
    for prev_u, u in zip(units[:-1], units[1:]):
        att_scores(*u)
        att_values(*prev_u)
    att_values(*units[-1])
    att_rows = []
    for jb in range(ts // blk):
        cols_ = []
        for gk in range(ATT_KV_HEADS):
            cols_ += [att_units[jb, gk][:blk], att_units[jb, gk][blk:]]
        att_rows.append(jnp.concatenate(cols_, axis=1))
    att = jnp.concatenate(att_rows, axis=0)

    lbl = lbl_ref[...]
    lexp = jnp.exp(lbl - jnp.max(lbl, axis=0, keepdims=True))
    lb = lexp[0:1, :] / jnp.sum(lexp, axis=0, keepdims=True)
    tri = tri_ref[...]
    row_ids = lax.broadcasted_iota(jnp.int32, (ts, HG_DIM), 0)
    half = ts // 2
    f_all = 0.5 * (1.0 + lb) + (0.5 * (1.0 - lb)) * jnp.tanh(0.5 * p_f)
    g_all = jnp.log(f_all) * LOG2_E
    g_hi = g_all.astype(BF16)
    g_lo = (g_all - g_hi.astype(F32)).astype(BF16)
    b2 = _dot(tri, jnp.concatenate([g_hi, g_lo], axis=1))
    b_all = b2[:, :HG_WIDTH] + b2[:, HG_WIDTH:]
    n_tiles = half // SUBLANES
    hs = [dict() for _ in range(HG_HEADS)]

    def prep(h):
        cols = slice(h * HG_DIM, (h + 1) * HG_DIM)
        f = f_all[:, cols]
        b = b_all[:, cols]
        kk = 1.0 - f
        qf = _silu(p_q[:, cols])
        vv = p_i[:, cols].astype(BF16)
        w_incl = jnp.exp2(b)
        w_tail = jnp.exp2(b[ts - 1:ts, :] - b)
        st = state_ref[h]
        o = _dot_nt((qf * w_incl).astype(BF16), st.astype(BF16))
        state_ref[h] = st * w_incl[ts - 1:ts, :] + _dot_tn(vv, (kk * w_tail).astype(BF16))
        hs[h].update(f=f, b=b, kk=kk, qf=qf, vv=vv, o=o, cols=cols)

    def levels(h):
        d = hs[h]
        qf, kk, b, f = d["qf"], d["kk"], d["b"], d["f"]
        quad = [[None] * n_tiles, [None] * n_tiles]
        for li, m in enumerate(HG_LEVELS):
            x_f = _select_rows(m, qf, kk, row_ids) * _level_decay(m, b, f, row_ids)
            x_l = x_f.astype(BF16)
            if 2 * m == ts:
                d["s10"] = _dot_nt(x_l[half:], x_l[:half])
                continue
            if m >= SUBLANES:
                tiles = [t for t in range(n_tiles) if (t * SUBLANES // m) % 2 == 1]
            else:
                tiles = list(range(n_tiles))
            for hf in range(2):
                r0 = hf * half
                if len(tiles) == n_tiles:
                    lhs = x_l[r0:r0 + half]
                else:
                    lhs = jnp.concatenate([x_f[r0 + t * SUBLANES:r0 + (t + 1) * SUBLANES] for t in tiles],
                                          axis=0).astype(BF16)
                sc = _dot_nt(lhs, x_l[r0:r0 + half])
                for i, t in enumerate(tiles):
                    m0 = (li - 1) * LANES + t * SUBLANES
                    part = sc[i * SUBLANES:(i + 1) * SUBLANES] * lmask_ref[m0:m0 + SUBLANES, :]
                    quad[hf][t] = part if quad[hf][t] is None else quad[hf][t] + part
        d["s00"] = jnp.concatenate(quad[0], axis=0).astype(BF16)
        d["s1"] = jnp.concatenate([d["s10"], jnp.concatenate(quad[1], axis=0)], axis=1).astype(BF16)

    def finish(h):
        d = hs[h]
        vv = d["vv"]
        o_top = _dot(d["s00"], vv[:half])
        o_bot = _dot(d["s1"], vv)
        diag = jnp.sum(d["qf"] * d["kk"], axis=-1, keepdims=True)
        o = d["o"] + jnp.concatenate([o_top, o_bot], axis=0) + diag * p_i[:, d["cols"]]
        d["rec"] = o * _rms_scale(o)

    prep(0)
    prep(1)
    levels(0)
    prep(2)
    levels(1)
    prep(3)
    levels(2)
    finish(0)
    levels(3)
    finish(1)
    finish(2)
    finish(3)
    rec_cols = [hs[h]["rec"] for h in range(HG_HEADS)]
    rec_n = (jnp.concatenate(rec_cols, axis=1) * ghg_ref[...] * _silu(p_g)).astype(BF16)

    att_n = (att * _rms_scale(att) * gatt_ref[...]).astype(BF16)
    o_ref[...] = x + _dot(jnp.concatenate([att_n, rec_n], axis=1), wout_ref[...])


def _cast_chunk_rows(n_rows, n_steps):
    tile = 2 * SUBLANES
    rows = tile
    while n_rows % rows or n_rows // rows > n_steps:
        rows += tile
    return rows


def _mixer(x, gmix, w_in, positions, w_out, sinks, gatt, lbl, ghg, side_weights):
    b, s, _ = x.shape
    ts = MIX_TILE
    tri, lmask = _hgrn_constants()
    invf, ecos, esin = _rotary_constants()
    bias = _attention_bias()
    pos = positions.reshape(b * (s // ts), 1, ts)
    cur = lambda bi, i: (bi, i, 0)
    fixed = lambda bi, i: (0, 0)
    hbm = pl.BlockSpec(memory_space=pl.ANY)
    n_cast_steps = b * (s // ts) // CAST_STRIDE
    side_scratch = []
    for w in side_weights:
        stage = (2, _cast_chunk_rows(w.shape[0], n_cast_steps), w.shape[1])
        side_scratch += [pltpu.VMEM(stage, F32), pltpu.VMEM(stage, BF16),
                         pltpu.SemaphoreType.DMA((2,)), pltpu.SemaphoreType.DMA((2,))]
    outs = pl.pallas_call(
        _mixer_kernel,
        grid=(b, s // ts),
        in_specs=[
            pl.BlockSpec(memory_space=pltpu.SMEM),
            pl.BlockSpec((None, ts, D_MODEL), cur),
            pl.BlockSpec((1, D_MODEL), fixed),
            hbm,
            pl.BlockSpec((None, 1, ts), lambda bi, i: (bi * (s // ts) + i, 0, 0)),
            pl.BlockSpec(invf.shape, fixed),
            pl.BlockSpec(ecos.shape, fixed),
            pl.BlockSpec(esin.shape, fixed),
            hbm,
            pl.BlockSpec((1, ATT_WIDTH), fixed),
            pl.BlockSpec(lbl.shape, fixed),
            pl.BlockSpec((1, HG_WIDTH), fixed),
            pl.BlockSpec(tri.shape, fixed),
            pl.BlockSpec(lmask.shape, fixed),
            pl.BlockSpec(bias.shape, lambda bi, i: (0, 0, 0)),
        ] + [hbm] * len(side_weights),
        out_specs=[pl.BlockSpec((None, ts, D_MODEL), cur)] + [hbm] * len(side_weights),
        out_shape=[jax.ShapeDtypeStruct((b, s, D_MODEL), F32)]
        + [jax.ShapeDtypeStruct(w.shape, BF16) for w in side_weights],
        scratch_shapes=[
            pltpu.VMEM((HG_HEADS, HG_DIM, HG_DIM), F32),
            pltpu.VMEM((ATT_BLOCK, KV_PAD_WIDTH), BF16),
            pltpu.VMEM(w_in.shape, BF16),
            pltpu.VMEM(w_out.shape, BF16),
            pltpu.VMEM((2, OWN_STAGE_ROWS, w_in.shape[1]), F32),
            pltpu.VMEM((2, OWN_STAGE_ROWS, w_out.shape[1]), F32),
            pltpu.SemaphoreType.DMA((2,)),
            pltpu.SemaphoreType.DMA((2,)),
        ] + side_scratch,
        compiler_params=pltpu.CompilerParams(
            dimension_semantics=("arbitrary", "arbitrary"), vmem_limit_bytes=VMEM_LIMIT_BYTES),
        name="mixer",
    )(sinks, x, gmix, w_in, pos, invf, ecos, esin, w_out, gatt, lbl, ghg, tri, lmask, bias, *side_weights)
    return outs[0], outs[1:]


def _mem_kv_kernel(mem_ref, gain_ref, w_ref, k_ref, v_ref):
    m = mem_ref[...]
    mn = (m * _rms_scale(m) * gain_ref[...]).astype(BF16)
    kv = _dot(mn, w_ref[...])
    k_ref[...] = kv[:, :D_MODEL].astype(BF16)
    v_ref[...] = kv[:, D_MODEL:].astype(BF16)


def _mem_kv(mem, gain, w_xkv):
    b, m, _ = mem.shape
    blk = lambda bi: (bi, 0, 0)
    fixed = lambda bi: (0, 0)
    return pl.pallas_call(
        _mem_kv_kernel,
        grid=(b,),
        in_specs=[
            pl.BlockSpec((None, m, D_MODEL), blk),
            pl.BlockSpec((1, D_MODEL), fixed),
            pl.BlockSpec((D_MODEL, 2 * D_MODEL), fixed),
        ],
        out_specs=[pl.BlockSpec((None, m, D_MODEL), blk), pl.BlockSpec((None, m, D_MODEL), blk)],
        out_shape=[jax.ShapeDtypeStruct((b, m, D_MODEL), BF16)] * 2,
        compiler_params=pltpu.CompilerParams(
            dimension_semantics=("arbitrary",), vmem_limit_bytes=VMEM_LIMIT_BYTES),
        name="mem_kv",
    )(mem, gain, w_xkv)


def _xattn_kernel(x_ref, gain_ref, wq_ref, k_ref, v_ref, wo_ref, o_ref):
    x = x_ref[...]
    hq = (x * _rms_scale(x) * gain_ref[...]).astype(BF16)
    q = _dot(hq, wq_ref[...]).astype(BF16)
    outs = []
    for h in range(X_HEADS):
        cols = slice(h * X_HEAD_DIM, (h + 1) * X_HEAD_DIM)
        s = _dot_nt(q[:, cols], k_ref[:, cols]) * (X_HEAD_DIM ** -0.5)
        p = jnp.exp(s - jnp.max(s, axis=-1, keepdims=True))
        o = _dot(p.astype(BF16), v_ref[:, cols])
        outs.append((o / jnp.sum(p, axis=-1, keepdims=True)).astype(BF16))
    xo = jnp.concatenate(outs, axis=1)
    o_ref[...] = x + _dot(xo, wo_ref[...])


def _xattn(x, gain, w_xq, xk, xv, w_xo, tm):
    b, s, _ = x.shape
    m = xk.shape[1]
    cur = lambda bi, i: (bi, i, 0)
    mem = lambda bi, i: (bi, 0, 0)
    fixed = lambda bi, i: (0, 0)
    return pl.pallas_call(
        _xattn_kernel,
        grid=(b, s // tm),
        in_specs=[
            pl.BlockSpec((None, tm, D_MODEL), cur),
            pl.BlockSpec((1, D_MODEL), fixed),
            pl.BlockSpec((D_MODEL, D_MODEL), fixed),
            pl.BlockSpec((None, m, D_MODEL), mem),
            pl.BlockSpec((None, m, D_MODEL), mem),
            pl.BlockSpec((D_MODEL, D_MODEL), fixed),
        ],
        out_specs=pl.BlockSpec((None, tm, D_MODEL), cur),
        out_shape=jax.ShapeDtypeStruct((b, s, D_MODEL), F32),
        compiler_params=pltpu.CompilerParams(
            dimension_semantics=("arbitrary", "arbitrary"), vmem_limit_bytes=VMEM_LIMIT_BYTES),
        name="xattn",
    )(x, gain, w_xq, xk, xv, w_xo)


FFN_CHUNKS = ((0, 1024), (1024, 2048), (2048, FFN_HIDDEN))

def _ffn_kernel(x_ref, gain_ref, wgu_ref, wd_ref, gfin_ref, o_ref):
    x = x_ref[...]
    hf = (x * _rms_scale(x) * gain_ref[...]).astype(BF16)
    acc = x
    for lo, hi in FFN_CHUNKS:
        gate = _dot(hf, wgu_ref[:, lo:hi])
        upv = _dot(hf, wgu_ref[:, FFN_HIDDEN + lo:FFN_HIDDEN + hi])
        act = (_silu(gate) * upv).astype(BF16)
        acc = acc + _dot(act, wd_ref[lo:hi, :])
    o_ref[...] = acc * _rms_scale(acc) * gfin_ref[...]


def _ffn(x2d, gain, w_gu, w_d, gfin, tm):
    n = x2d.shape[0]
    row = lambda i: (i, 0)
    fixed = lambda i: (0, 0)
    return pl.pallas_call(
        _ffn_kernel,
        grid=(n // tm,),
        in_specs=[
            pl.BlockSpec((tm, D_MODEL), row),
            pl.BlockSpec((1, D_MODEL), fixed),
            pl.BlockSpec((D_MODEL, 2 * FFN_HIDDEN), fixed, pipeline_mode=pl.Buffered(1)),
            pl.BlockSpec((FFN_HIDDEN, D_MODEL), fixed, pipeline_mode=pl.Buffered(1)),
            pl.BlockSpec((1, D_MODEL), fixed),
        ],
        out_specs=pl.BlockSpec((tm, D_MODEL), row),
        out_shape=jax.ShapeDtypeStruct((n, D_MODEL), F32),
        compiler_params=pltpu.CompilerParams(
            dimension_semantics=("arbitrary",), vmem_limit_bytes=VMEM_LIMIT_BYTES),
        name="ffn",
    )(x2d, gain, w_gu, w_d, gfin)


def kernel(x, mem, positions, norm_mix, w_in, att_sinks, att_out_gain, hg_lb_logits, hg_out_gain,
           w_out, norm_xattn, norm_mem, w_xq, w_xkv, w_xo, norm_ffn, w_gate_up, w_down, norm_final):
    b, s, d = x.shape
    n = b * s
    assert w_in.shape[0] == 1 and hg_lb_logits.shape[0] == 2, "single-layer block only"
    assert all(a.dtype == F32 for a in (x, mem, w_in, w_out, w_xq, w_xkv, w_xo, w_gate_up, w_down))
    row = lambda v: v.reshape(1, -1)
    side = [w[0] for w in (w_xkv, w_xq, w_xo, w_gate_up, w_down)]
    x, (w_xkv_b, w_xq_b, w_xo_b, w_gu_b, w_d_b) = _mixer(
        x, row(norm_mix[0]), w_in[0], positions,
        w_out[0], att_sinks[0], row(att_out_gain[0]),
        hg_lb_logits, row(hg_out_gain[0]), side)
    xk, xv = _mem_kv(mem, row(norm_mem[0]), w_xkv_b)
    x = _xattn(x, row(norm_xattn[0]), w_xq_b, xk, xv, w_xo_b, tm=DENSE_TILE)
    y = _ffn(x.reshape(n, d), row(norm_ffn[0]), w_gu_b, w_d_b, row(norm_final), tm=DENSE_TILE)
    return y.reshape(b, s, d)
```

```python
import numpy as np
import jax
import jax.numpy as jnp
from jax import lax
from jax.experimental import pallas as pl
from jax.experimental.pallas import tpu as pltpu

F32 = jnp.float32
BF16 = jnp.bfloat16

D_MODEL = 1024
ATT_HEADS = 8
ATT_KV_HEADS = 2
ATT_GROUP = ATT_HEADS // ATT_KV_HEADS
ATT_HEAD_DIM = 64
ATT_WIDTH = ATT_HEADS * ATT_HEAD_DIM
ATT_KV_WIDTH = ATT_KV_HEADS * ATT_HEAD_DIM
ATT_BLOCK = 128
ROPE_THETA = 500000.0
ROPE_DIM = ATT_HEAD_DIM // 4
ROPE_HALF = ROPE_DIM // 2
HG_HEADS = 4
HG_DIM = 128
HG_WIDTH = HG_HEADS * HG_DIM
X_HEADS = 4
X_HEAD_DIM = D_MODEL // X_HEADS
FFN_HIDDEN = 2816
RMS_EPS = 1e-6
LANES = 128
SUBLANES = 8
MASK_VALUE = -1e30
LOG2_E = 1.4426950408889634

MIX_TILE = 256
HG_LEVELS = (128, 64, 32, 16, 8, 4, 2, 1)
KV_PAD_WIDTH = 2 * ATT_KV_HEADS * 2 * LANES
CAST_STRIDE = 4
DENSE_TILE = 1024
OWN_STAGE_ROWS = 128

VMEM_LIMIT_BYTES = 56 * 1024 * 1024


def _rms_scale(x):
    return lax.rsqrt(jnp.mean(x * x, axis=-1, keepdims=True) + RMS_EPS)


def _silu(x):
    h = 0.5 * x
    return h + h * jnp.tanh(h)


def _dot(a, b):
    return jnp.dot(a, b, preferred_element_type=F32)


def _dot_nt(a, b):
    return lax.dot_general(a, b, (((1,), (1,)), ((), ())), preferred_element_type=F32)


def _dot_tn(a, b):
    return lax.dot_general(a, b, (((0,), (0,)), ((), ())), preferred_element_type=F32)


def _stage_weight(src_hbm, dst_ref, stage_ref, sem_ref):
    rows = stage_ref.shape[1]
    n_chunks = dst_ref.shape[0] // rows
    assert n_chunks * rows == dst_ref.shape[0] and src_hbm.shape == dst_ref.shape

    def copy(c):
        return pltpu.make_async_copy(src_hbm.at[pl.ds(c * rows, rows), :], stage_ref.at[c % 2], sem_ref.at[c % 2])

    copy(0).start()
    for c in range(n_chunks):
        if c + 1 < n_chunks:
            copy(c + 1).start()
        copy(c).wait()
        dst_ref[c * rows:(c + 1) * rows, :] = stage_ref[c % 2].astype(BF16)


def _cast_step(t, issue, src_hbm, dst_hbm, in_stage, out_stage, in_sem, out_sem):
    rows = in_stage.shape[1]
    n = src_hbm.shape[0] // rows
    assert n * rows == src_hbm.shape[0] and src_hbm.shape == dst_hbm.shape
    slot = lax.rem(t, 2)

    def read(c, s):
        return pltpu.make_async_copy(src_hbm.at[pl.ds(c * rows, rows), :], in_stage.at[s], in_sem.at[s])

    def write(c, s):
        return pltpu.make_async_copy(out_stage.at[s], dst_hbm.at[pl.ds(c * rows, rows), :], out_sem.at[s])

    if issue:
        @pl.when(t == 0)
        def _():
            read(0, 0).start()

        @pl.when(t + 1 < n)
        def _():
            read(t + 1, 1 - slot).start()

        return

    @pl.when(t < n)
    def _():
        read(t, slot).wait()

        @pl.when(t >= 2)
        def _():
            write(t - 2, slot).wait()

        out_stage[slot] = in_stage[slot].astype(BF16)
        write(t, slot).start()

    @pl.when(t == n - 1)
    def _():
        write(t - 1, 1 - slot).wait()
        write(t, slot).wait()


def _hgrn_constants():
    r = np.arange(MIX_TILE)[:, None]
    j = np.arange(MIX_TILE)[None, :]
    tri = (j <= r).astype(np.float32)
    rr = np.arange(LANES)[:, None]
    jj = np.arange(LANES)[None, :]
    masks = []
    for m in HG_LEVELS:
        if 2 * m <= LANES:
            masks.append(((rr // (2 * m)) == (jj // (2 * m))) & ((rr % (2 * m)) >= m) & ((jj % (2 * m)) < m))
    lmask = np.concatenate(masks, axis=0).astype(np.float32)
    return jnp.asarray(tri, BF16), jnp.asarray(lmask, F32)


def _attention_bias():
    i = (np.arange(2 * ATT_BLOCK) % ATT_BLOCK)[:, None]
    j = np.arange(2 * ATT_BLOCK)[None, :]
    band = (j > i) & (j <= i + ATT_BLOCK)
    variants = (band, band & (j >= ATT_BLOCK))
    return jnp.asarray(np.stack([np.where(v | (j == 0), 0.0, MASK_VALUE) for v in variants]), F32)


def _rotary_constants():
    inv_freq = jnp.power(jnp.float32(ROPE_THETA),
                         -jnp.arange(ROPE_HALF, dtype=F32) * (2.0 / ROPE_DIM))
    invf = jnp.broadcast_to(inv_freq[:, None], (ROPE_HALF, MIX_TILE))
    lane = np.arange(LANES)
    in_head = lane % ATT_HEAD_DIM
    freq_row = (np.arange(ROPE_HALF)[:, None] == (in_head % ROPE_HALF)[None, :])
    rot = freq_row & (in_head < ROPE_DIM)[None, :]
    ecos = np.zeros((4 * ROPE_HALF, LANES), np.float32)
    ecos[:ROPE_HALF] = rot
    ecos[ROPE_HALF:2 * ROPE_HALF] = rot
    ecos[2 * ROPE_HALF] = in_head >= ROPE_DIM
    sign = np.where(in_head < ROPE_HALF, -1.0, 1.0)[None, :]
    esin = np.concatenate([rot * sign, rot * sign], axis=0).astype(np.float32)
    return invf, jnp.asarray(ecos, BF16), jnp.asarray(esin, BF16)


def _level_decay(m, b, f, row_ids):
    ts = b.shape[0]
    if m >= SUBLANES // 2:
        b3 = b.reshape(ts // (2 * m), 2 * m, HG_DIM)
        ref = b3[:, m - 1:m, :]
        if m >= SUBLANES:
            e3 = jnp.concatenate([ref - b3[:, :m, :], b3[:, m:, :] - ref], axis=1)
        else:
            e3 = -jnp.abs(b3 - ref)
        return jnp.exp2(e3).reshape(ts, HG_DIM)
    if m == 2:
        pos = row_ids & 3
        f_next = pltpu.roll(f, ts - 1, axis=0)
        f_prev = pltpu.roll(f, 1, axis=0)
        return jnp.where(pos == 0, f_next, jnp.where(pos == 1, 1.0, jnp.where(pos == 2, f, f * f_prev)))
    assert m == 1
    return jnp.where((row_ids & 1) != 0, f, 1.0)


def _select_rows(m, upper_src, lower_src, row_ids):
    c = upper_src.shape[0]
    if m >= SUBLANES:
        pieces = []
        for b0 in range(0, c, 2 * m):
            pieces.append(lower_src[b0:b0 + m])
            pieces.append(upper_src[b0 + m:b0 + 2 * m])
        return jnp.concatenate(pieces, axis=0)
    return jnp.where((row_ids & m) != 0, upper_src, lower_src)


def _rotary_tables(pos_ref, invf_ref, ecos_ref, esin_ref):
    ang = invf_ref[...] * pos_ref[...].astype(F32)

    def spread(t, e_ref, extra):
        t_hi = t.astype(BF16).astype(F32)
        rows = jnp.concatenate([t_hi, t - t_hi] + extra, axis=0).astype(BF16)
        return _dot_tn(rows, e_ref[...])

    ones = jnp.ones_like(ang)
    cos_t = spread(jnp.cos(ang), ecos_ref, [ones, jnp.zeros_like(ang)])
    sin_t = spread(jnp.sin(ang), esin_ref, [])
    return cos_t, sin_t


def _mixer_kernel(sinks_ref, x_ref, gmix_ref, win_hbm, pos_ref, invf_ref, ecos_ref, esin_ref, wout_hbm,
                  gatt_ref, lbl_ref, ghg_ref, tri_ref, lmask_ref, bias_ref, *rest):
    n_own = 8
    n_side = (len(rest) - 1 - n_own) // 6
    side_src, o_ref, side_dst = rest[:n_side], rest[n_side], rest[n_side + 1:2 * n_side + 1]
    own = rest[2 * n_side + 1:2 * n_side + 1 + n_own]
    state_ref, kvprev_ref, win_ref, wout_ref, win_stage, wout_stage, win_sem, wout_sem = own
    side_scratch = rest[2 * n_side + 1 + n_own:]
    step = pl.program_id(1)
    ts = MIX_TILE
    blk = ATT_BLOCK

    t = pl.program_id(0) * pl.num_programs(1) + step

    @pl.when(t == 0)
    def _():
        _stage_weight(win_hbm, win_ref, win_stage, win_sem)
        _stage_weight(wout_hbm, wout_ref, wout_stage, wout_sem)

    @pl.when(lax.rem(t, CAST_STRIDE) == 0)
    def _():
        for issue in (True, False):
            for k in range(n_side):
                _cast_step(lax.div(t, CAST_STRIDE), issue, side_src[k], side_dst[k],
                           *side_scratch[4 * k:4 * k + 4])

    @pl.when(step == 0)
    def _():
        state_ref[...] = jnp.zeros_like(state_ref)
        kvprev_ref[...] = jnp.zeros_like(kvprev_ref)

    x = x_ref[...]
    hn = (x * _rms_scale(x) * gmix_ref[...]).astype(BF16)

    def proj(c0, c1):
        return _dot(hn, win_ref[:, c0:c1])

    cos, sin = _rotary_tables(pos_ref, invf_ref, ecos_ref, esin_ref)
    lane = lax.broadcasted_iota(jnp.int32, cos.shape, 1)
    first_half = (lane & (ATT_HEAD_DIM - 1)) < ROPE_HALF
    lo_head = lane < ATT_HEAD_DIM

    def rotate(col):
        partner = jnp.where(first_half,
                            pltpu.roll(col, LANES - ROPE_HALF, axis=1),
                            pltpu.roll(col, ROPE_HALF, axis=1))
        return col * cos + partner * sin

    def padded(col):
        swapped = pltpu.roll(col, ATT_HEAD_DIM, axis=1)
        zero = jnp.zeros_like(col)
        return [v.astype(BF16) for v in (jnp.where(lo_head, col, zero), jnp.where(lo_head, zero, swapped),
                                         jnp.where(lo_head, swapped, zero), jnp.where(lo_head, zero, col))]

    pq = proj(0, ATT_WIDTH)
    q_cols = [(rotate(pq[:, j * LANES:(j + 1) * LANES]) * (ATT_HEAD_DIM ** -0.5 * LOG2_E)).astype(BF16)
              for j in range(ATT_WIDTH // LANES)]
    pkv = proj(ATT_WIDTH, ATT_WIDTH + 2 * ATT_KV_WIDTH)
    kv_new = padded(rotate(pkv[:, :LANES])) + padded(pkv[:, LANES:])
    kv_all = [jnp.concatenate([kvprev_ref[:, i * LANES:(i + 1) * LANES], v], axis=0)
              for i, v in enumerate(kv_new)]
    for i, v in enumerate(kv_new):
        kvprev_ref[:, i * LANES:(i + 1) * LANES] = v[ts - blk:]
    r0c = ATT_WIDTH + 2 * ATT_KV_WIDTH
    p_q, p_f, p_i, p_g = (proj(r0c + k * HG_WIDTH, r0c + (k + 1) * HG_WIDTH) for k in range(4))

    upper_half_rows = lax.broadcasted_iota(jnp.int32, (2 * blk, 1), 0) >= blk
    lane_ids = lax.broadcasted_iota(jnp.int32, (2 * blk, LANES), 1)
    lane_lo = lane_ids < ATT_HEAD_DIM
    ones_lo = jnp.where(lane_lo, 1.0, 0.0).astype(BF16)
    ones_hi = jnp.where(lane_lo, 0.0, 1.0).astype(BF16)
    bf16_rows = 2 * SUBLANES
    drop_row0 = jnp.where(lax.broadcasted_iota(jnp.int32, (bf16_rows, LANES), 0) > 0, 1.0, 0.0).astype(BF16)

    def sink_slot(op):
        return jnp.concatenate([op[:bf16_rows] * drop_row0, op[bf16_rows:]], axis=0)

    att_units = {}

    def att_scores(jb, gk):
        keys = slice(jb * blk, (jb + 2) * blk)
        bias = bias_ref[jnp.where(step > 0, 0, 1)] if jb == 0 else bias_ref[0]
        q2 = jnp.concatenate([q_cols[2 * gk][jb * blk:(jb + 1) * blk],
                              q_cols[2 * gk + 1][jb * blk:(jb + 1) * blk]], axis=0)
        k_lo = sink_slot(kv_all[2 * gk][keys])
        k_hi = sink_slot(kv_all[2 * gk + 1][keys])
        h0 = ATT_GROUP * gk
        probs = []
        for which, k_op in ((0, k_lo), (1, k_hi)):
            sink = jnp.where(upper_half_rows, sinks_ref[h0 + 2 + which], sinks_ref[h0 + which]) * LOG2_E
            s = _dot_nt(q2, k_op) + bias
            s_first = s[:, :LANES] + jnp.where(lane_ids == 0, sink, 0.0)
            m = jnp.max(jnp.maximum(s_first, s[:, LANES:]), axis=-1, keepdims=True)
            probs.append(jnp.exp2(jnp.concatenate([s_first, s[:, LANES:]], axis=1) - m).astype(BF16))
        att_units[jb, gk] = jnp.concatenate(probs, axis=1)

    def att_values(jb, gk):
        keys = slice(jb * blk, (jb + 2) * blk)
        r_lo = jnp.concatenate([sink_slot(kv_all[4 + 2 * gk][keys]), ones_lo], axis=1)
        r_hi = jnp.concatenate([sink_slot(kv_all[5 + 2 * gk][keys]), ones_hi], axis=1)
        out = _dot(att_units[jb, gk], jnp.concatenate([r_lo, r_hi], axis=0))
        att_units[jb, gk] = out[:, :LANES] / out[:, LANES:]

    units = [(jb, gk) for jb in range(ts // blk) for gk in range(ATT_KV_HEADS)]
    att_scores(*units[0])
    for prev_u, u in zip(units[:-1], units[1:]):
        att_scores(*u)
        att_values(*prev_u)
    att_values(*units[-1])
    att_rows = []
    for jb in range(ts // blk):
        cols_ = []
        for gk in range(ATT_KV_HEADS):
            cols_ += [att_units[jb, gk][:blk], att_units[jb, gk][blk:]]
        att_rows.append(jnp.concatenate(cols_, axis=1))
    att = jnp.concatenate(att_rows, axis=0)

    lbl = lbl_ref[...]
    lexp = jnp.exp(lbl - jnp.max(lbl, axis=0, keepdims=True))
    lb = lexp[0:1, :] / jnp.sum(lexp, axis=0, keepdims=True)
    tri = tri_ref[...]
    row_ids = lax.broadcasted_iota(jnp.int32, (ts, HG_DIM), 0)
    half = ts // 2
    f_all = 0.5 * (1.0 + lb) + (0.5 * (1.0 - lb)) * jnp.tanh(0.5 * p_f)
    g_all = jnp.log(f_all) * LOG2_E
    g_hi = g_all.astype(BF16)
    g_lo = (g_all - g_hi.astype(F32)).astype(BF16)
    b2 = _dot(tri, jnp.concatenate([g_hi, g_lo], axis=1))
    b_all = b2[:, :HG_WIDTH] + b2[:, HG_WIDTH:]
    n_tiles = half // SUBLANES
    hs = [dict() for _ in range(HG_HEADS)]

    def prep(h):
        cols = slice(h * HG_DIM, (h + 1) * HG_DIM)
        f = f_all[:, cols]
        b = b_all[:, cols]
        kk = 1.0 - f
        qf = _silu(p_q[:, cols])
        vv = p_i[:, cols].astype(BF16)
        w_incl = jnp.exp2(b)
        w_tail = jnp.exp2(b[ts - 1:ts, :] - b)
        st = state_ref[h]
        o = _dot_nt((qf * w_incl).astype(BF16), st.astype(BF16))
        state_ref[h] = st * w_incl[ts - 1:ts, :] + _dot_tn(vv, (kk * w_tail).astype(BF16))
        hs[h].update(f=f, b=b, kk=kk, qf=qf, vv=vv, o=o, cols=cols)

    def levels(h):
        d = hs[h]
        qf, kk, b, f = d["qf"], d["kk"], d["b"], d["f"]
        quad = [[None] * n_tiles, [None] * n_tiles]
        for li, m in enumerate(HG_LEVELS):
            x_f = _select_rows(m, qf, kk, row_ids) * _level_decay(m, b, f, row_ids)
            x_l = x_f.astype(BF16)
            if 2 * m == ts:
                d["s10"] = _dot_nt(x_l[half:], x_l[:half])
                continue
            if m >= SUBLANES:
                tiles = [t for t in range(n_tiles) if (t * SUBLANES // m) % 2 == 1]
            else:
                tiles = list(range(n_tiles))
            for hf in range(2):
                r0 = hf * half
                if len(tiles) == n_tiles:
                    lhs = x_l[r0:r0 + half]
                else:
                    lhs = jnp.concatenate([x_f[r0 + t * SUBLANES:r0 + (t + 1) * SUBLANES] for t in tiles],
                                          axis=0).astype(BF16)
                sc = _dot_nt(lhs, x_l[r0:r0 + half])
                for i, t in enumerate(tiles):
                    m0 = (li - 1) * LANES + t * SUBLANES
                    part = sc[i * SUBLANES:(i + 1) * SUBLANES] * lmask_ref[m0:m0 + SUBLANES, :]
                    quad[hf][t] = part if quad[hf][t] is None else quad[hf][t] + part
        d["s00"] = jnp.concatenate(quad[0], axis=0).astype(BF16)
        d["s1"] = jnp.concatenate([d["s10"], jnp.concatenate(quad[1], axis=0)], axis=1).astype(BF16)

    def finish(h):
        d = hs[h]
        vv = d["vv"]
        o_top = _dot(d["s00"], vv[:half])
        o_bot = _dot(d["s1"], vv)
        diag = jnp.sum(d["qf"] * d["kk"], axis=-1, keepdims=True)
        o = d["o"] + jnp.concatenate([o_top, o_bot], axis=0) + diag * p_i[:, d["cols"]]
        d["rec"] = o * _rms_scale(o)

    prep(0)
    prep(1)
    levels(0)
    prep(2)
    levels(1)
    prep(3)
    levels(2)
    finish(0)
    levels(3)
    finish(1)
    finish(2)
    finish(3)
    rec_cols = [hs[h]["rec"] for h in range(HG_HEADS)]
    rec_n = (jnp.concatenate(rec_cols, axis=1) * ghg_ref[...] * _silu(p_g)).astype(BF16)

    att_n = (att * _rms_scale(att) * gatt_ref[...]).astype(BF16)
    o_ref[...] = x + _dot(jnp.concatenate([att_n, rec_n], axis=1), wout_ref[...])


def _cast_chunk_rows(n_rows, n_steps):
    tile = 2 * SUBLANES
    rows = tile
    while n_rows % rows or n_rows // rows > n_steps:
        rows += tile
    return rows


def _mixer(x, gmix, w_in, positions, w_out, sinks, gatt, lbl, ghg, side_weights):
    b, s, _ = x.shape
    ts = MIX_TILE
    tri, lmask = _hgrn_constants()
    invf, ecos, esin = _rotary_constants()
    bias = _attention_bias()
    pos = positions.reshape(b * (s // ts), 1, ts)
    cur = lambda bi, i: (bi, i, 0)
    fixed = lambda bi, i: (0, 0)
    hbm = pl.BlockSpec(memory_space=pl.ANY)
    n_cast_steps = b * (s // ts) // CAST_STRIDE
    side_scratch = []
    for w in side_weights:
        stage = (2, _cast_chunk_rows(w.shape[0], n_cast_steps), w.shape[1])
        side_scratch += [pltpu.VMEM(stage, F32), pltpu.VMEM(stage, BF16),
                         pltpu.SemaphoreType.DMA((2,)), pltpu.SemaphoreType.DMA((2,))]
    outs = pl.pallas_call(
        _mixer_kernel,
        grid=(b, s // ts),
        in_specs=[
            pl.BlockSpec(memory_space=pltpu.SMEM),
            pl.BlockSpec((None, ts, D_MODEL), cur),
            pl.BlockSpec((1, D_MODEL), fixed),
            hbm,
            pl.BlockSpec((None, 1, ts), lambda bi, i: (bi * (s // ts) + i, 0, 0)),
            pl.BlockSpec(invf.shape, fixed),
            pl.BlockSpec(ecos.shape, fixed),
            pl.BlockSpec(esin.shape, fixed),
            hbm,
            pl.BlockSpec((1, ATT_WIDTH), fixed),
            pl.BlockSpec(lbl.shape, fixed),
            pl.BlockSpec((1, HG_WIDTH), fixed),
            pl.BlockSpec(tri.shape, fixed),
            pl.BlockSpec(lmask.shape, fixed),
            pl.BlockSpec(bias.shape, lambda bi, i: (0, 0, 0)),
        ] + [hbm] * len(side_weights),
        out_specs=[pl.BlockSpec((None, ts, D_MODEL), cur)] + [hbm] * len(side_weights),
        out_shape=[jax.ShapeDtypeStruct((b, s, D_MODEL), F32)]
        + [jax.ShapeDtypeStruct(w.shape, BF16) for w in side_weights],
        scratch_shapes=[
            pltpu.VMEM((HG_HEADS, HG_DIM, HG_DIM), F32),
            pltpu.VMEM((ATT_BLOCK, KV_PAD_WIDTH), BF16),
            pltpu.VMEM(w_in.shape, BF16),
            pltpu.VMEM(w_out.shape, BF16),
            pltpu.VMEM((2, OWN_STAGE_ROWS, w_in.shape[1]), F32),
            pltpu.VMEM((2, OWN_STAGE_ROWS, w_out.shape[1]), F32),
            pltpu.SemaphoreType.DMA((2,)),
            pltpu.SemaphoreType.DMA((2,)),
        ] + side_scratch,
        compiler_params=pltpu.CompilerParams(
            dimension_semantics=("arbitrary", "arbitrary"), vmem_limit_bytes=VMEM_LIMIT_BYTES),
        name="mixer",
    )(sinks, x, gmix, w_in, pos, invf, ecos, esin, w_out, gatt, lbl, ghg, tri, lmask, bias, *side_weights)
    return outs[0], outs[1:]


def _xattn_kernel(x_ref, gain_ref, wq_ref, mem_ref, gmem_ref, wkv_ref, wo_ref, o_ref, kv_ref):
    @pl.when(pl.program_id(1) == 0)
    def _():
        m = mem_ref[...]
        mn = (m * _rms_scale(m) * gmem_ref[...]).astype(BF16)
        kv_ref[...] = _dot(mn, wkv_ref[...]).astype(BF16)

    x = x_ref[...]
    hq = (x * _rms_scale(x) * gain_ref[...]).astype(BF16)
    q = (_dot(hq, wq_ref[...]) * (X_HEAD_DIM ** -0.5 * LOG2_E)).astype(BF16)
    probs = [None] * X_HEADS
    outs = [None] * X_HEADS

    def scores(h):
        cols = slice(h * X_HEAD_DIM, (h + 1) * X_HEAD_DIM)
        s = _dot_nt(q[:, cols], kv_ref[:, cols])
        probs[h] = jnp.exp2(s - jnp.max(s, axis=-1, keepdims=True))

    def values(h):
        p = probs[h]
        o = _dot(p.astype(BF16), kv_ref[:, D_MODEL + h * X_HEAD_DIM:D_MODEL + (h + 1) * X_HEAD_DIM])
        outs[h] = (o / jnp.sum(p, axis=-1, keepdims=True)).astype(BF16)

    scores(0)
    for h in range(1, X_HEADS):
        scores(h)
        values(h - 1)
    values(X_HEADS - 1)
    o_ref[...] = x + _dot(jnp.concatenate(outs, axis=1), wo_ref[...])


def _xattn(x, gain, w_xq, mem, gmem, w_xkv, w_xo, tm):
    b, s, _ = x.shape
    m = mem.shape[1]
    cur = lambda bi, i: (bi, i, 0)
    per_batch = lambda bi, i: (bi, 0, 0)
    fixed = lambda bi, i: (0, 0)
    once = dict(pipeline_mode=pl.Buffered(1))
    return pl.pallas_call(
        _xattn_kernel,
        grid=(b, s // tm),
        in_specs=[
            pl.BlockSpec((None, tm, D_MODEL), cur),
            pl.BlockSpec((1, D_MODEL), fixed),
            pl.BlockSpec((D_MODEL, D_MODEL), fixed, **once),
            pl.BlockSpec((None, m, D_MODEL), per_batch),
            pl.BlockSpec((1, D_MODEL), fixed),
            pl.BlockSpec((D_MODEL, 2 * D_MODEL), fixed, **once),
            pl.BlockSpec((D_MODEL, D_MODEL), fixed, **once),
        ],
        out_specs=pl.BlockSpec((None, tm, D_MODEL), cur),
        out_shape=jax.ShapeDtypeStruct((b, s, D_MODEL), F32),
        scratch_shapes=[pltpu.VMEM((m, 2 * D_MODEL), BF16)],
        compiler_params=pltpu.CompilerParams(
            dimension_semantics=("arbitrary", "arbitrary"), vmem_limit_bytes=VMEM_LIMIT_BYTES),
        name="xattn",
    )(x, gain, w_xq, mem, gmem, w_xkv, w_xo)


FFN_CHUNKS = ((0, 1024), (1024, 2048), (2048, FFN_HIDDEN))

def _ffn_kernel(x_ref, gain_ref, wgu_ref, wd_ref, gfin_ref, o_ref):
    n_sub = 2
    sub = x_ref.shape[0] // n_sub
    hf, acc = [], []
    for i in range(n_sub):
        x = x_ref[i * sub:(i + 1) * sub, :]
        hf.append((x * _rms_scale(x) * gain_ref[...]).astype(BF16))
        acc.append(x)
    for lo, hi in FFN_CHUNKS:
        for i in range(n_sub):
            gate = _dot(hf[i], wgu_ref[:, lo:hi])
            upv = _dot(hf[i], wgu_ref[:, FFN_HIDDEN + lo:FFN_HIDDEN + hi])
            act = (_silu(gate) * upv).astype(BF16)
            acc[i] = acc[i] + _dot(act, wd_ref[lo:hi, :])
    for i in range(n_sub):
        o_ref[i * sub:(i + 1) * sub, :] = acc[i] * _rms_scale(acc[i]) * gfin_ref[...]


def _ffn(x2d, gain, w_gu, w_d, gfin, tm):
    n = x2d.shape[0]
    row = lambda i: (i, 0)
    fixed = lambda i: (0, 0)
    return pl.pallas_call(
        _ffn_kernel,
        grid=(n // tm,),
        in_specs=[
            pl.BlockSpec((tm, D_MODEL), row),
            pl.BlockSpec((1, D_MODEL), fixed),
            pl.BlockSpec((D_MODEL, 2 * FFN_HIDDEN), fixed, pipeline_mode=pl.Buffered(1)),
            pl.BlockSpec((FFN_HIDDEN, D_MODEL), fixed, pipeline_mode=pl.Buffered(1)),
            pl.BlockSpec((1, D_MODEL), fixed),
        ],
        out_specs=pl.BlockSpec((tm, D_MODEL), row),
        out_shape=jax.ShapeDtypeStruct((n, D_MODEL), F32),
        compiler_params=pltpu.CompilerParams(
            dimension_semantics=("arbitrary",), vmem_limit_bytes=VMEM_LIMIT_BYTES),
        name="ffn",
    )(x2d, gain, w_gu, w_d, gfin)


def kernel(x, mem, positions, norm_mix, w_in, att_sinks, att_out_gain, hg_lb_logits, hg_out_gain,
           w_out, norm_xattn, norm_mem, w_xq, w_xkv, w_xo, norm_ffn, w_gate_up, w_down, norm_final):
    b, s, d = x.shape
    n = b * s
    assert w_in.shape[0] == 1 and hg_lb_logits.shape[0] == 2, "single-layer block only"
    assert all(a.dtype == F32 for a in (x, mem, w_in, w_out, w_xq, w_xkv, w_xo, w_gate_up, w_down))
    row = lambda v: v.reshape(1, -1)
    side = [w[0] for w in (w_xkv, w_xq, w_xo, w_gate_up, w_down)]
    x, (w_xkv_b, w_xq_b, w_xo_b, w_gu_b, w_d_b) = _mixer(
        x, row(norm_mix[0]), w_in[0], positions,
        w_out[0], att_sinks[0], row(att_out_gain[0]),
        hg_lb_logits, row(hg_out_gain[0]), side)
    x = _xattn(x, row(norm_xattn[0]), w_xq_b, mem, row(norm_mem[0]), w_xkv_b, w_xo_b, tm=DENSE_TILE)
    y = _ffn(x.reshape(n, d), row(norm_ffn[0]), w_gu_b, w_d_b, row(norm_final), tm=DENSE_TILE)
    return y.reshape(b, s, d)
```

```python
import numpy as np
import jax
import jax.numpy as jnp
from jax import lax
from jax.experimental import pallas as pl
from jax.experimental.pallas import tpu as pltpu

F32 = jnp.float32
BF16 = jnp.bfloat16

D_MODEL = 1024
ATT_HEADS = 8
ATT_KV_HEADS = 2
ATT_GROUP = ATT_HEADS // ATT_KV_HEADS
ATT_HEAD_DIM = 64
ATT_WIDTH = ATT_HEADS * ATT_HEAD_DIM
ATT_KV_WIDTH = ATT_KV_HEADS * ATT_HEAD_DIM
ATT_BLOCK = 128
ROPE_THETA = 500000.0
ROPE_DIM = ATT_HEAD_DIM // 4
ROPE_HALF = ROPE_DIM // 2
HG_HEADS = 4
HG_DIM = 128
HG_WIDTH = HG_HEADS * HG_DIM
X_HEADS = 4
X_HEAD_DIM = D_MODEL // X_HEADS
FFN_HIDDEN = 2816
RMS_EPS = 1e-6
LANES = 128
SUBLANES = 8
MASK_VALUE = -1e30
LOG2_E = 1.4426950408889634

MIX_TILE = 256
HG_LEVELS = (128, 64, 32, 16, 8, 4, 2, 1)
KV_PAD_WIDTH = 2 * ATT_KV_HEADS * 2 * LANES
TILES_PER_STEP = 2
CAST_STRIDE = 2
DENSE_TILE = 1024
OWN_STAGE_ROWS = 128

VMEM_LIMIT_BYTES = 56 * 1024 * 1024


def _rms_scale(x):
    return lax.rsqrt(jnp.mean(x * x, axis=-1, keepdims=True) + RMS_EPS)


def _silu(x):
    h = 0.5 * x
    return h + h * jnp.tanh(h)


def _dot(a, b):
    return jnp.dot(a, b, preferred_element_type=F32)


def _dot_nt(a, b):
    return lax.dot_general(a, b, (((1,), (1,)), ((), ())), preferred_element_type=F32)


def _dot_tn(a, b):
    return lax.dot_general(a, b, (((0,), (0,)), ((), ())), preferred_element_type=F32)


def _stage_weight(src_hbm, dst_ref, stage_ref, sem_ref):
    rows = stage_ref.shape[1]
    n_chunks = dst_ref.shape[0] // rows
    assert n_chunks * rows == dst_ref.shape[0] and src_hbm.shape == dst_ref.shape

    def copy(c):
        return pltpu.make_async_copy(src_hbm.at[pl.ds(c * rows, rows), :], stage_ref.at[c % 2], sem_ref.at[c % 2])

    copy(0).start()
    for c in range(n_chunks):
        if c + 1 < n_chunks:
            copy(c + 1).start()
        copy(c).wait()
        dst_ref[c * rows:(c + 1) * rows, :] = stage_ref[c % 2].astype(BF16)


def _cast_step(t, issue, src_hbm, dst_hbm, in_stage, out_stage, in_sem, out_sem):
    rows = in_stage.shape[1]
    n = src_hbm.shape[0] // rows
    assert n * rows == src_hbm.shape[0] and src_hbm.shape == dst_hbm.shape
    slot = lax.rem(t, 2)

    def read(c, s):
        return pltpu.make_async_copy(src_hbm.at[pl.ds(c * rows, rows), :], in_stage.at[s], in_sem.at[s])

    def write(c, s):
        return pltpu.make_async_copy(out_stage.at[s], dst_hbm.at[pl.ds(c * rows, rows), :], out_sem.at[s])

    if issue:
        @pl.when(t == 0)
        def _():
            read(0, 0).start()

        @pl.when(t + 1 < n)
        def _():
            read(t + 1, 1 - slot).start()

        return

    @pl.when(t < n)
    def _():
        read(t, slot).wait()

        @pl.when(t >= 2)
        def _():
            write(t - 2, slot).wait()

        out_stage[slot] = in_stage[slot].astype(BF16)
        write(t, slot).start()

    @pl.when(t == n - 1)
    def _():
        write(t - 1, 1 - slot).wait()
        write(t, slot).wait()


def _hgrn_constants():
    r = np.arange(MIX_TILE)[:, None]
    j = np.arange(MIX_TILE)[None, :]
    tri = (j <= r).astype(np.float32)
    rr = np.arange(LANES)[:, None]
    jj = np.arange(LANES)[None, :]
    masks = []
    for m in HG_LEVELS:
        if 2 * m <= LANES:
            masks.append(((rr // (2 * m)) == (jj // (2 * m))) & ((rr % (2 * m)) >= m) & ((jj % (2 * m)) < m))
    lmask = np.concatenate(masks, axis=0).astype(np.float32)
    return jnp.asarray(tri, BF16), jnp.asarray(lmask, F32)


def _attention_bias():
    i = (np.arange(2 * ATT_BLOCK) % ATT_BLOCK)[:, None]
    j = np.arange(2 * ATT_BLOCK)[None, :]
    band = (j > i) & (j <= i + ATT_BLOCK)
    variants = (band, band & (j >= ATT_BLOCK))
    return jnp.asarray(np.stack([np.where(v | (j == 0), 0.0, MASK_VALUE) for v in variants]), F32)


def _rotary_constants():
    inv_freq = jnp.power(jnp.float32(ROPE_THETA),
                         -jnp.arange(ROPE_HALF, dtype=F32) * (2.0 / ROPE_DIM))
    invf = jnp.broadcast_to(inv_freq[:, None], (ROPE_HALF, MIX_TILE))
    lane = np.arange(LANES)
    in_head = lane % ATT_HEAD_DIM
    freq_row = (np.arange(ROPE_HALF)[:, None] == (in_head % ROPE_HALF)[None, :])
    rot = freq_row & (in_head < ROPE_DIM)[None, :]
    ecos = np.zeros((4 * ROPE_HALF, LANES), np.float32)
    ecos[:ROPE_HALF] = rot
    ecos[ROPE_HALF:2 * ROPE_HALF] = rot
    ecos[2 * ROPE_HALF] = in_head >= ROPE_DIM
    sign = np.where(in_head < ROPE_HALF, -1.0, 1.0)[None, :]
    esin = np.concatenate([rot * sign, rot * sign], axis=0).astype(np.float32)
    return invf, jnp.asarray(ecos, BF16), jnp.asarray(esin, BF16)


def _level_decay(m, b, f, row_ids):
    ts = b.shape[0]
    if m >= SUBLANES // 2:
        b3 = b.reshape(ts // (2 * m), 2 * m, HG_DIM)
        ref = b3[:, m - 1:m, :]
        if m >= SUBLANES:
            e3 = jnp.concatenate([ref - b3[:, :m, :], b3[:, m:, :] - ref], axis=1)
        else:
            e3 = -jnp.abs(b3 - ref)
        return jnp.exp2(e3).reshape(ts, HG_DIM)
    if m == 2:
        pos = row_ids & 3
        f_next = pltpu.roll(f, ts - 1, axis=0)
        f_prev = pltpu.roll(f, 1, axis=0)
        return jnp.where(pos == 0, f_next, jnp.where(pos == 1, 1.0, jnp.where(pos == 2, f, f * f_prev)))
    assert m == 1
    return jnp.where((row_ids & 1) != 0, f, 1.0)


def _select_rows(m, upper_src, lower_src, row_ids):
    c = upper_src.shape[0]
    if m >= SUBLANES:
        pieces = []
        for b0 in range(0, c, 2 * m):
            pieces.append(lower_src[b0:b0 + m])
            pieces.append(upper_src[b0 + m:b0 + 2 * m])
        return jnp.concatenate(pieces, axis=0)
    return jnp.where((row_ids & m) != 0, upper_src, lower_src)


def _rotary_tables(pos_ref, invf_ref, ecos_ref, esin_ref):
    ang = invf_ref[...] * pos_ref[...].astype(F32)

    def spread(t, e_ref, extra):
        t_hi = t.astype(BF16).astype(F32)
        rows = jnp.concatenate([t_hi, t - t_hi] + extra, axis=0).astype(BF16)
        return _dot_tn(rows, e_ref[...])

    ones = jnp.ones_like(ang)
    cos_t = spread(jnp.cos(ang), ecos_ref, [ones, jnp.zeros_like(ang)])
    sin_t = spread(jnp.sin(ang), esin_ref, [])
    return cos_t, sin_t


def _mixer_kernel(sinks_ref, x_ref, gmix_ref, win_hbm, pos_ref, invf_ref, ecos_ref, esin_ref, wout_hbm,
                  gatt_ref, lbl_ref, ghg_ref, tri_ref, lmask_ref, bias_ref, *rest):
    n_own = 8
    n_side = (len(rest) - 1 - n_own) // 6
    side_src, o_ref, side_dst = rest[:n_side], rest[n_side], rest[n_side + 1:2 * n_side + 1]
    own = rest[2 * n_side + 1:2 * n_side + 1 + n_own]
    state_ref, kvprev_ref, win_ref, wout_ref, win_stage, wout_stage, win_sem, wout_sem = own
    side_scratch = rest[2 * n_side + 1 + n_own:]
    step = pl.program_id(1)
    ts = MIX_TILE
    blk = ATT_BLOCK

    t = pl.program_id(0) * pl.num_programs(1) + step

    @pl.when(t == 0)
    def _():
        _stage_weight(win_hbm, win_ref, win_stage, win_sem)
        _stage_weight(wout_hbm, wout_ref, wout_stage, wout_sem)

    @pl.when(lax.rem(t, CAST_STRIDE) == 0)
    def _():
        for issue in (True, False):
            for k in range(n_side):
                _cast_step(lax.div(t, CAST_STRIDE), issue, side_src[k], side_dst[k],
                           *side_scratch[4 * k:4 * k + 4])

    @pl.when(step == 0)
    def _():
        state_ref[...] = jnp.zeros_like(state_ref)
        kvprev_ref[...] = jnp.zeros_like(kvprev_ref)

    for k in range(TILES_PER_STEP):
        rows = pl.ds(k * ts, ts)
        no_prev = jnp.where(step > 0, 0, 1) if k == 0 else 0
        _mixer_tile(no_prev, sinks_ref, x_ref.at[rows, :], gmix_ref, win_ref, pos_ref.at[:, rows], invf_ref,
                    ecos_ref, esin_ref, wout_ref, gatt_ref, lbl_ref, ghg_ref, tri_ref, lmask_ref, bias_ref,
                    o_ref.at[rows, :], state_ref, kvprev_ref)


def _mixer_tile(no_prev, sinks_ref, x_ref, gmix_ref, win_ref, pos_ref, invf_ref, ecos_ref, esin_ref, wout_ref,
                gatt_ref, lbl_ref, ghg_ref, tri_ref, lmask_ref, bias_ref, o_ref, state_ref, kvprev_ref):
    ts = MIX_TILE
    blk = ATT_BLOCK

    x = x_ref[...]
    hn = (x * _rms_scale(x) * gmix_ref[...]).astype(BF16)

    def proj(c0, c1):
        return _dot(hn, win_ref[:, c0:c1])

    cos, sin = _rotary_tables(pos_ref, invf_ref, ecos_ref, esin_ref)
    lane = lax.broadcasted_iota(jnp.int32, cos.shape, 1)
    first_half = (lane & (ATT_HEAD_DIM - 1)) < ROPE_HALF
    lo_head = lane < ATT_HEAD_DIM

    def rotate(col):
        partner = jnp.where(first_half,
                            pltpu.roll(col, LANES - ROPE_HALF, axis=1),
                            pltpu.roll(col, ROPE_HALF, axis=1))
        return col * cos + partner * sin

    def padded(col):
        swapped = pltpu.roll(col, ATT_HEAD_DIM, axis=1)
        zero = jnp.zeros_like(col)
        return [v.astype(BF16) for v in (jnp.where(lo_head, col, zero), jnp.where(lo_head, zero, swapped),
                                         jnp.where(lo_head, swapped, zero), jnp.where(lo_head, zero, col))]

    pq = proj(0, ATT_WIDTH)
    q_cols = [(rotate(pq[:, j * LANES:(j + 1) * LANES]) * (ATT_HEAD_DIM ** -0.5 * LOG2_E)).astype(BF16)
              for j in range(ATT_WIDTH // LANES)]
    pkv = proj(ATT_WIDTH, ATT_WIDTH + 2 * ATT_KV_WIDTH)
    kv_new = padded(rotate(pkv[:, :LANES])) + padded(pkv[:, LANES:])
    kv_all = [jnp.concatenate([kvprev_ref[:, i * LANES:(i + 1) * LANES], v], axis=0)
              for i, v in enumerate(kv_new)]
    for i, v in enumerate(kv_new):
        kvprev_ref[:, i * LANES:(i + 1) * LANES] = v[ts - blk:]
    r0c = ATT_WIDTH + 2 * ATT_KV_WIDTH
    p_q, p_f, p_i, p_g = (proj(r0c + k * HG_WIDTH, r0c + (k + 1) * HG_WIDTH) for k in range(4))

    upper_half_rows = lax.broadcasted_iota(jnp.int32, (2 * blk, 1), 0) >= blk
    lane_ids = lax.broadcasted_iota(jnp.int32, (2 * blk, LANES), 1)
    lane_lo = lane_ids < ATT_HEAD_DIM
    ones_lo = jnp.where(lane_lo, 1.0, 0.0).astype(BF16)
    ones_hi = jnp.where(lane_lo, 0.0, 1.0).astype(BF16)
    bf16_rows = 2 * SUBLANES
    drop_row0 = jnp.where(lax.broadcasted_iota(jnp.int32, (bf16_rows, LANES), 0) > 0, 1.0, 0.0).astype(BF16)

    def sink_slot(op):
        return jnp.concatenate([op[:bf16_rows] * drop_row0, op[bf16_rows:]], axis=0)

    att_units = {}

    def att_scores(jb, gk):
        keys = slice(jb * blk, (jb + 2) * blk)
        bias = bias_ref[no_prev] if jb == 0 else bias_ref[0]
        q2 = jnp.concatenate([q_cols[2 * gk][jb * blk:(jb + 1) * blk],
                              q_cols[2 * gk + 1][jb * blk:(jb + 1) * blk]], axis=0)
        k_lo = sink_slot(kv_all[2 * gk][keys])
        k_hi = sink_slot(kv_all[2 * gk + 1][keys])
        h0 = ATT_GROUP * gk
        probs = []
        for which, k_op in ((0, k_lo), (1, k_hi)):
            sink = jnp.where(upper_half_rows, sinks_ref[h0 + 2 + which], sinks_ref[h0 + which]) * LOG2_E
            s = _dot_nt(q2, k_op) + bias
            s_first = s[:, :LANES] + jnp.where(lane_ids == 0, sink, 0.0)
            m = jnp.max(jnp.maximum(s_first, s[:, LANES:]), axis=-1, keepdims=True)
            probs.append(jnp.exp2(jnp.concatenate([s_first, s[:, LANES:]], axis=1) - m).astype(BF16))
        att_units[jb, gk] = jnp.concatenate(probs, axis=1)

    def att_values(jb, gk):
        keys = slice(jb * blk, (jb + 2) * blk)
        r_lo = jnp.concatenate([sink_slot(kv_all[4 + 2 * gk][keys]), ones_lo], axis=1)
        r_hi = jnp.concatenate([sink_slot(kv_all[5 + 2 * gk][keys]), ones_hi], axis=1)
        out = _dot(att_units[jb, gk], jnp.concatenate([r_lo, r_hi], axis=0))
        att_units[jb, gk] = out[:, :LANES] / out[:, LANES:]

    units = [(jb, gk) for jb in range(ts // blk) for gk in range(ATT_KV_HEADS)]
    att_scores(*units[0])
    for prev_u, u in zip(units[:-1], units[1:]):
        att_scores(*u)
        att_values(*prev_u)
    att_values(*units[-1])
    att_rows = []
    for jb in range(ts // blk):
        cols_ = []
        for gk in range(ATT_KV_HEADS):
            cols_ += [att_units[jb, gk][:blk], att_units[jb, gk][blk:]]
        att_rows.append(jnp.concatenate(cols_, axis=1))
    att = jnp.concatenate(att_rows, axis=0)

    lbl = lbl_ref[...]
    lexp = jnp.exp(lbl - jnp.max(lbl, axis=0, keepdims=True))
    lb = lexp[0:1, :] / jnp.sum(lexp, axis=0, keepdims=True)
    tri = tri_ref[...]
    row_ids = lax.broadcasted_iota(jnp.int32, (ts, HG_DIM), 0)
    half = ts // 2
    f_all = 0.5 * (1.0 + lb) + (0.5 * (1.0 - lb)) * jnp.tanh(0.5 * p_f)
    g_all = jnp.log(f_all) * LOG2_E
    g_hi = g_all.astype(BF16)
    g_lo = (g_all - g_hi.astype(F32)).astype(BF16)
    b2 = _dot(tri, jnp.concatenate([g_hi, g_lo], axis=1))
    b_all = b2[:, :HG_WIDTH] + b2[:, HG_WIDTH:]
    n_tiles = half // SUBLANES
    hs = [dict() for _ in range(HG_HEADS)]

    def prep(h):
        cols = slice(h * HG_DIM, (h + 1) * HG_DIM)
        f = f_all[:, cols]
        b = b_all[:, cols]
        kk = 1.0 - f
        qf = _silu(p_q[:, cols])
        vv = p_i[:, cols].astype(BF16)
        w_incl = jnp.exp2(b)
        w_tail = jnp.exp2(b[ts - 1:ts, :] - b)
        st = state_ref[h]
        o = _dot_nt((qf * w_incl).astype(BF16), st.astype(BF16))
        state_ref[h] = st * w_incl[ts - 1:ts, :] + _dot_tn(vv, (kk * w_tail).astype(BF16))
        hs[h].update(f=f, b=b, kk=kk, qf=qf, vv=vv, o=o, cols=cols)

    def levels(h):
        d = hs[h]
        qf, kk, b, f = d["qf"], d["kk"], d["b"], d["f"]
        quad = [[None] * n_tiles, [None] * n_tiles]
        for li, m in enumerate(HG_LEVELS):
            x_f = _select_rows(m, qf, kk, row_ids) * _level_decay(m, b, f, row_ids)
            x_l = x_f.astype(BF16)
            if 2 * m == ts:
                d["s10"] = _dot_nt(x_l[half:], x_l[:half])
                continue
            if m >= SUBLANES:
                tiles = [t for t in range(n_tiles) if (t * SUBLANES // m) % 2 == 1]
            else:
                tiles = list(range(n_tiles))
            for hf in range(2):
                r0 = hf * half
                if len(tiles) == n_tiles:
                    lhs = x_l[r0:r0 + half]
                else:
                    lhs = jnp.concatenate([x_f[r0 + t * SUBLANES:r0 + (t + 1) * SUBLANES] for t in tiles],
                                          axis=0).astype(BF16)
                sc = _dot_nt(lhs, x_l[r0:r0 + half])
                for i, t in enumerate(tiles):
                    m0 = (li - 1) * LANES + t * SUBLANES
                    part = sc[i * SUBLANES:(i + 1) * SUBLANES] * lmask_ref[m0:m0 + SUBLANES, :]
                    quad[hf][t] = part if quad[hf][t] is None else quad[hf][t] + part
        d["s00"] = jnp.concatenate(quad[0], axis=0).astype(BF16)
        d["s1"] = jnp.concatenate([d["s10"], jnp.concatenate(quad[1], axis=0)], axis=1).astype(BF16)

    def finish(h):
        d = hs[h]
        vv = d["vv"]
        o_top = _dot(d["s00"], vv[:half])
        o_bot = _dot(d["s1"], vv)
        diag = jnp.sum(d["qf"] * d["kk"], axis=-1, keepdims=True)
        o = d["o"] + jnp.concatenate([o_top, o_bot], axis=0) + diag * p_i[:, d["cols"]]
        d["rec"] = o * _rms_scale(o)

    prep(0)
    prep(1)
    levels(0)
    prep(2)
    levels(1)
    prep(3)
    levels(2)
    finish(0)
    levels(3)
    finish(1)
    finish(2)
    finish(3)
    rec_cols = [hs[h]["rec"] for h in range(HG_HEADS)]
    rec_n = (jnp.concatenate(rec_cols, axis=1) * ghg_ref[...] * _silu(p_g)).astype(BF16)

    att_n = (att * _rms_scale(att) * gatt_ref[...]).astype(BF16)
    o_ref[...] = x + _dot(jnp.concatenate([att_n, rec_n], axis=1), wout_ref[...])


def _cast_chunk_rows(n_rows, n_steps):
    tile = 2 * SUBLANES
    rows = tile
    while n_rows % rows or n_rows // rows > n_steps:
        rows += tile
    return rows


def _mixer(x, gmix, w_in, positions, w_out, sinks, gatt, lbl, ghg, side_weights):
    b, s, _ = x.shape
    ts = TILES_PER_STEP * MIX_TILE
    tri, lmask = _hgrn_constants()
    invf, ecos, esin = _rotary_constants()
    bias = _attention_bias()
    pos = positions.reshape(b * (s // ts), 1, ts)
    cur = lambda bi, i: (bi, i, 0)
    fixed = lambda bi, i: (0, 0)
    hbm = pl.BlockSpec(memory_space=pl.ANY)
    n_cast_steps = b * (s // ts) // CAST_STRIDE
    side_scratch = []
    for w in side_weights:
        stage = (2, _cast_chunk_rows(w.shape[0], n_cast_steps), w.shape[1])
        side_scratch += [pltpu.VMEM(stage, F32), pltpu.VMEM(stage, BF16),
                         pltpu.SemaphoreType.DMA((2,)), pltpu.SemaphoreType.DMA((2,))]
    outs = pl.pallas_call(
        _mixer_kernel,
        grid=(b, s // ts),
        in_specs=[
            pl.BlockSpec(memory_space=pltpu.SMEM),
            pl.BlockSpec((None, ts, D_MODEL), cur),
            pl.BlockSpec((1, D_MODEL), fixed),
            hbm,
            pl.BlockSpec((None, 1, ts), lambda bi, i: (bi * (s // ts) + i, 0, 0)),
            pl.BlockSpec(invf.shape, fixed),
            pl.BlockSpec(ecos.shape, fixed),
            pl.BlockSpec(esin.shape, fixed),
            hbm,
            pl.BlockSpec((1, ATT_WIDTH), fixed),
            pl.BlockSpec(lbl.shape, fixed),
            pl.BlockSpec((1, HG_WIDTH), fixed),
            pl.BlockSpec(tri.shape, fixed),
            pl.BlockSpec(lmask.shape, fixed),
            pl.BlockSpec(bias.shape, lambda bi, i: (0, 0, 0)),
        ] + [hbm] * len(side_weights),
        out_specs=[pl.BlockSpec((None, ts, D_MODEL), cur)] + [hbm] * len(side_weights),
        out_shape=[jax.ShapeDtypeStruct((b, s, D_MODEL), F32)]
        + [jax.ShapeDtypeStruct(w.shape, BF16) for w in side_weights],
        scratch_shapes=[
            pltpu.VMEM((HG_HEADS, HG_DIM, HG_DIM), F32),
            pltpu.VMEM((ATT_BLOCK, KV_PAD_WIDTH), BF16),
            pltpu.VMEM(w_in.shape, BF16),
            pltpu.VMEM(w_out.shape, BF16),
            pltpu.VMEM((2, OWN_STAGE_ROWS, w_in.shape[1]), F32),
            pltpu.VMEM((2, OWN_STAGE_ROWS, w_out.shape[1]), F32),
            pltpu.SemaphoreType.DMA((2,)),
            pltpu.SemaphoreType.DMA((2,)),
        ] + side_scratch,
        compiler_params=pltpu.CompilerParams(
            dimension_semantics=("arbitrary", "arbitrary"), vmem_limit_bytes=VMEM_LIMIT_BYTES),
        name="mixer",
    )(sinks, x, gmix, w_in, pos, invf, ecos, esin, w_out, gatt, lbl, ghg, tri, lmask, bias, *side_weights)
    return outs[0], outs[1:]


def _xattn_kernel(x_ref, gain_ref, wq_ref, mem_ref, gmem_ref, wkv_ref, wo_ref, o_ref, kv_ref):
    @pl.when(pl.program_id(1) == 0)
    def _():
        m = mem_ref[...]
        mn = (m * _rms_scale(m) * gmem_ref[...]).astype(BF16)
        kv_ref[...] = _dot(mn, wkv_ref[...]).astype(BF16)

    x = x_ref[...]
    hq = (x * _rms_scale(x) * gain_ref[...]).astype(BF16)
    q = (_dot(hq, wq_ref[...]) * (X_HEAD_DIM ** -0.5 * LOG2_E)).astype(BF16)
    probs = [None] * X_HEADS
    outs = [None] * X_HEADS

    def scores(h):
        cols = slice(h * X_HEAD_DIM, (h + 1) * X_HEAD_DIM)
        s = _dot_nt(q[:, cols], kv_ref[:, cols])
        probs[h] = jnp.exp2(s - jnp.max(s, axis=-1, keepdims=True))

    def values(h):
        p = probs[h]
        o = _dot(p.astype(BF16), kv_ref[:, D_MODEL + h * X_HEAD_DIM:D_MODEL + (h + 1) * X_HEAD_DIM])
        outs[h] = (o / jnp.sum(p, axis=-1, keepdims=True)).astype(BF16)

    scores(0)
    for h in range(1, X_HEADS):
        scores(h)
        values(h - 1)
    values(X_HEADS - 1)
    o_ref[...] = x + _dot(jnp.concatenate(outs, axis=1), wo_ref[...])


def _xattn(x, gain, w_xq, mem, gmem, w_xkv, w_xo, tm):
    b, s, _ = x.shape
    m = mem.shape[1]
    cur = lambda bi, i: (bi, i, 0)
    per_batch = lambda bi, i: (bi, 0, 0)
    fixed = lambda bi, i: (0, 0)
    once = dict(pipeline_mode=pl.Buffered(1))
    return pl.pallas_call(
        _xattn_kernel,
        grid=(b, s // tm),
        in_specs=[
            pl.BlockSpec((None, tm, D_MODEL), cur),
            pl.BlockSpec((1, D_MODEL), fixed),
            pl.BlockSpec((D_MODEL, D_MODEL), fixed, **once),
            pl.BlockSpec((None, m, D_MODEL), per_batch),
            pl.BlockSpec((1, D_MODEL), fixed),
            pl.BlockSpec((D_MODEL, 2 * D_MODEL), fixed, **once),
            pl.BlockSpec((D_MODEL, D_MODEL), fixed, **once),
        ],
        out_specs=pl.BlockSpec((None, tm, D_MODEL), cur),
        out_shape=jax.ShapeDtypeStruct((b, s, D_MODEL), F32),
        scratch_shapes=[pltpu.VMEM((m, 2 * D_MODEL), BF16)],
        compiler_params=pltpu.CompilerParams(
            dimension_semantics=("arbitrary", "arbitrary"), vmem_limit_bytes=VMEM_LIMIT_BYTES),
        name="xattn",
    )(x, gain, w_xq, mem, gmem, w_xkv, w_xo)


FFN_CHUNKS = ((0, 1024), (1024, 2048), (2048, FFN_HIDDEN))

def _ffn_kernel(x_ref, gain_ref, wgu_ref, wd_ref, gfin_ref, o_ref):
    n_sub = 2
    sub = x_ref.shape[0] // n_sub
    hf, acc = [], []
    for i in range(n_sub):
        x = x_ref[i * sub:(i + 1) * sub, :]
        hf.append((x * _rms_scale(x) * gain_ref[...]).astype(BF16))
        acc.append(x)
    for lo, hi in FFN_CHUNKS:
        for i in range(n_sub):
            gate = _dot(hf[i], wgu_ref[:, lo:hi])
            upv = _dot(hf[i], wgu_ref[:, FFN_HIDDEN + lo:FFN_HIDDEN + hi])
            act = (_silu(gate) * upv).astype(BF16)
            acc[i] = acc[i] + _dot(act, wd_ref[lo:hi, :])
    for i in range(n_sub):
        o_ref[i * sub:(i + 1) * sub, :] = acc[i] * _rms_scale(acc[i]) * gfin_ref[...]


def _ffn(x2d, gain, w_gu, w_d, gfin, tm):
    n = x2d.shape[0]
    row = lambda i: (i, 0)
    fixed = lambda i: (0, 0)
    return pl.pallas_call(
        _ffn_kernel,
        grid=(n // tm,),
        in_specs=[
            pl.BlockSpec((tm, D_MODEL), row),
            pl.BlockSpec((1, D_MODEL), fixed),
            pl.BlockSpec((D_MODEL, 2 * FFN_HIDDEN), fixed, pipeline_mode=pl.Buffered(1)),
            pl.BlockSpec((FFN_HIDDEN, D_MODEL), fixed, pipeline_mode=pl.Buffered(1)),
            pl.BlockSpec((1, D_MODEL), fixed),
        ],
        out_specs=pl.BlockSpec((tm, D_MODEL), row),
        out_shape=jax.ShapeDtypeStruct((n, D_MODEL), F32),
        compiler_params=pltpu.CompilerParams(
            dimension_semantics=("arbitrary",), vmem_limit_bytes=VMEM_LIMIT_BYTES),
        name="ffn",
    )(x2d, gain, w_gu, w_d, gfin)


def kernel(x, mem, positions, norm_mix, w_in, att_sinks, att_out_gain, hg_lb_logits, hg_out_gain,
           w_out, norm_xattn, norm_mem, w_xq, w_xkv, w_xo, norm_ffn, w_gate_up, w_down, norm_final):
    b, s, d = x.shape
    n = b * s
    assert w_in.shape[0] == 1 and hg_lb_logits.shape[0] == 2, "single-layer block only"
    assert all(a.dtype == F32 for a in (x, mem, w_in, w_out, w_xq, w_xkv, w_xo, w_gate_up, w_down))
    row = lambda v: v.reshape(1, -1)
    side = [w[0] for w in (w_xkv, w_xq, w_xo, w_gate_up, w_down)]
    x, (w_xkv_b, w_xq_b, w_xo_b, w_gu_b, w_d_b) = _mixer(
        x, row(norm_mix[0]), w_in[0], positions,
        w_out[0], att_sinks[0], row(att_out_gain[0]),
        hg_lb_logits, row(hg_out_gain[0]), side)
    x = _xattn(x, row(norm_xattn[0]), w_xq_b, mem, row(norm_mem[0]), w_xkv_b, w_xo_b, tm=DENSE_TILE)
    y = _ffn(x.reshape(n, d), row(norm_ffn[0]), w_gu_b, w_d_b, row(norm_final), tm=DENSE_TILE)
    return y.reshape(b, s, d)
```

```python
import numpy as np
import jax
import jax.numpy as jnp
from jax import lax
from jax.experimental import pallas as pl
from jax.experimental.pallas import tpu as pltpu

F32 = jnp.float32
BF16 = jnp.bfloat16

D_MODEL = 1024
ATT_HEADS = 8
ATT_KV_HEADS = 2
ATT_GROUP = ATT_HEADS // ATT_KV_HEADS
ATT_HEAD_DIM = 64
ATT_WIDTH = ATT_HEADS * ATT_HEAD_DIM
ATT_KV_WIDTH = ATT_KV_HEADS * ATT_HEAD_DIM
ATT_BLOCK = 128
ROPE_THETA = 500000.0
ROPE_DIM = ATT_HEAD_DIM // 4
ROPE_HALF = ROPE_DIM // 2
HG_HEADS = 4
HG_DIM = 128
HG_WIDTH = HG_HEADS * HG_DIM
X_HEADS = 4
X_HEAD_DIM = D_MODEL // X_HEADS
FFN_HIDDEN = 2816
RMS_EPS = 1e-6
LANES = 128
SUBLANES = 8
MASK_VALUE = -1e30
LOG2_E = 1.4426950408889634

MIX_TILE = 256
HG_LEVELS = (128, 64, 32, 16, 8, 4, 2, 1)
KV_PAD_WIDTH = 2 * ATT_KV_HEADS * 2 * LANES
TILES_PER_STEP = 4
CAST_STRIDE = 1
DENSE_TILE = 1024
OWN_STAGE_ROWS = 128

VMEM_LIMIT_BYTES = 56 * 1024 * 1024


def _rms_scale(x):
    return lax.rsqrt(jnp.mean(x * x, axis=-1, keepdims=True) + RMS_EPS)


def _silu(x):
    h = 0.5 * x
    return h + h * jnp.tanh(h)


def _dot(a, b):
    return jnp.dot(a, b, preferred_element_type=F32)


def _dot_nt(a, b):
    return lax.dot_general(a, b, (((1,), (1,)), ((), ())), preferred_element_type=F32)


def _dot_tn(a, b):
    return lax.dot_general(a, b, (((0,), (0,)), ((), ())), preferred_element_type=F32)


def _stage_weight(src_hbm, dst_ref, stage_ref, sem_ref):
    rows = stage_ref.shape[1]
    n_chunks = dst_ref.shape[0] // rows
    assert n_chunks * rows == dst_ref.shape[0] and src_hbm.shape == dst_ref.shape

    def copy(c):
        return pltpu.make_async_copy(src_hbm.at[pl.ds(c * rows, rows), :], stage_ref.at[c % 2], sem_ref.at[c % 2])

    copy(0).start()
    for c in range(n_chunks):
        if c + 1 < n_chunks:
            copy(c + 1).start()
        copy(c).wait()
        dst_ref[c * rows:(c + 1) * rows, :] = stage_ref[c % 2].astype(BF16)


def _cast_step(t, issue, src_hbm, dst_hbm, in_stage, out_stage, in_sem, out_sem):
    rows = in_stage.shape[1]
    n = src_hbm.shape[0] // rows
    assert n * rows == src_hbm.shape[0] and src_hbm.shape == dst_hbm.shape
    slot = lax.rem(t, 2)

    def read(c, s):
        return pltpu.make_async_copy(src_hbm.at[pl.ds(c * rows, rows), :], in_stage.at[s], in_sem.at[s])

    def write(c, s):
        return pltpu.make_async_copy(out_stage.at[s], dst_hbm.at[pl.ds(c * rows, rows), :], out_sem.at[s])

    if issue:
        @pl.when(t == 0)
        def _():
            read(0, 0).start()

        @pl.when(t + 1 < n)
        def _():
            read(t + 1, 1 - slot).start()

        return

    @pl.when(t < n)
    def _():
        read(t, slot).wait()

        @pl.when(t >= 2)
        def _():
            write(t - 2, slot).wait()

        out_stage[slot] = in_stage[slot].astype(BF16)
        write(t, slot).start()

    @pl.when(t == n - 1)
    def _():
        write(t - 1, 1 - slot).wait()
        write(t, slot).wait()


def _hgrn_constants():
    r = np.arange(MIX_TILE)[:, None]
    j = np.arange(MIX_TILE)[None, :]
    tri = (j <= r).astype(np.float32)
    rr = np.arange(LANES)[:, None]
    jj = np.arange(LANES)[None, :]
    masks = []
    for m in HG_LEVELS:
        if 2 * m <= LANES:
            masks.append(((rr // (2 * m)) == (jj // (2 * m))) & ((rr % (2 * m)) >= m) & ((jj % (2 * m)) < m))
    lmask = np.concatenate(masks, axis=0).astype(np.float32)
    return jnp.asarray(tri, BF16), jnp.asarray(lmask, F32)


def _attention_bias():
    i = (np.arange(2 * ATT_BLOCK) % ATT_BLOCK)[:, None]
    j = np.arange(2 * ATT_BLOCK)[None, :]
    band = (j > i) & (j <= i + ATT_BLOCK)
    variants = (band, band & (j >= ATT_BLOCK))
    return jnp.asarray(np.stack([np.where(v | (j == 0), 0.0, MASK_VALUE) for v in variants]), F32)


def _rotary_constants():
    inv_freq = jnp.power(jnp.float32(ROPE_THETA),
                         -jnp.arange(ROPE_HALF, dtype=F32) * (2.0 / ROPE_DIM))
    invf = jnp.broadcast_to(inv_freq[:, None], (ROPE_HALF, MIX_TILE))
    lane = np.arange(LANES)
    in_head = lane % ATT_HEAD_DIM
    freq_row = (np.arange(ROPE_HALF)[:, None] == (in_head % ROPE_HALF)[None, :])
    rot = freq_row & (in_head < ROPE_DIM)[None, :]
    ecos = np.zeros((4 * ROPE_HALF, LANES), np.float32)
    ecos[:ROPE_HALF] = rot
    ecos[ROPE_HALF:2 * ROPE_HALF] = rot
    ecos[2 * ROPE_HALF] = in_head >= ROPE_DIM
    sign = np.where(in_head < ROPE_HALF, -1.0, 1.0)[None, :]
    esin = np.concatenate([rot * sign, rot * sign], axis=0).astype(np.float32)
    return invf, jnp.asarray(ecos, BF16), jnp.asarray(esin, BF16)


def _level_decay(m, b, f, row_ids):
    ts = b.shape[0]
    if m >= SUBLANES // 2:
        b3 = b.reshape(ts // (2 * m), 2 * m, HG_DIM)
        ref = b3[:, m - 1:m, :]
        if m >= SUBLANES:
            e3 = jnp.concatenate([ref - b3[:, :m, :], b3[:, m:, :] - ref], axis=1)
        else:
            e3 = -jnp.abs(b3 - ref)
        return jnp.exp2(e3).reshape(ts, HG_DIM)
    if m == 2:
        pos = row_ids & 3
        f_next = pltpu.roll(f, ts - 1, axis=0)
        f_prev = pltpu.roll(f, 1, axis=0)
        return jnp.where(pos == 0, f_next, jnp.where(pos == 1, 1.0, jnp.where(pos == 2, f, f * f_prev)))
    assert m == 1
    return jnp.where((row_ids & 1) != 0, f, 1.0)


def _select_rows(m, upper_src, lower_src, row_ids):
    c = upper_src.shape[0]
    if m >= SUBLANES:
        pieces = []
        for b0 in range(0, c, 2 * m):
            pieces.append(lower_src[b0:b0 + m])
            pieces.append(upper_src[b0 + m:b0 + 2 * m])
        return jnp.concatenate(pieces, axis=0)
    return jnp.where((row_ids & m) != 0, upper_src, lower_src)


def _rotary_tables(pos_ref, invf_ref, ecos_ref, esin_ref):
    ang = invf_ref[...] * pos_ref[...].astype(F32)

    def spread(t, e_ref, extra):
        t_hi = t.astype(BF16).astype(F32)
        rows = jnp.concatenate([t_hi, t - t_hi] + extra, axis=0).astype(BF16)
        return _dot_tn(rows, e_ref[...])

    ones = jnp.ones_like(ang)
    cos_t = spread(jnp.cos(ang), ecos_ref, [ones, jnp.zeros_like(ang)])
    sin_t = spread(jnp.sin(ang), esin_ref, [])
    return cos_t, sin_t


def _mixer_kernel(sinks_ref, x_ref, gmix_ref, win_hbm, pos_ref, invf_ref, ecos_ref, esin_ref, wout_hbm,
                  gatt_ref, lbl_ref, ghg_ref, tri_ref, lmask_ref, bias_ref, *rest):
    n_own = 8
    n_side = (len(rest) - 1 - n_own) // 6
    side_src, o_ref, side_dst = rest[:n_side], rest[n_side], rest[n_side + 1:2 * n_side + 1]
    own = rest[2 * n_side + 1:2 * n_side + 1 + n_own]
    state_ref, kvprev_ref, win_ref, wout_ref, win_stage, wout_stage, win_sem, wout_sem = own
    side_scratch = rest[2 * n_side + 1 + n_own:]
    step = pl.program_id(1)
    ts = MIX_TILE
    blk = ATT_BLOCK

    t = pl.program_id(0) * pl.num_programs(1) + step

    @pl.when(t == 0)
    def _():
        _stage_weight(win_hbm, win_ref, win_stage, win_sem)
        _stage_weight(wout_hbm, wout_ref, wout_stage, wout_sem)

    @pl.when(lax.rem(t, CAST_STRIDE) == 0)
    def _():
        for issue in (True, False):
            for k in range(n_side):
                _cast_step(lax.div(t, CAST_STRIDE), issue, side_src[k], side_dst[k],
                           *side_scratch[4 * k:4 * k + 4])

    @pl.when(step == 0)
    def _():
        state_ref[...] = jnp.zeros_like(state_ref)
        kvprev_ref[...] = jnp.zeros_like(kvprev_ref)

    for k in range(TILES_PER_STEP):
        rows = pl.ds(k * ts, ts)
        no_prev = jnp.where(step > 0, 0, 1) if k == 0 else 0
        _mixer_tile(no_prev, sinks_ref, x_ref.at[rows, :], gmix_ref, win_ref, pos_ref.at[:, rows], invf_ref,
                    ecos_ref, esin_ref, wout_ref, gatt_ref, lbl_ref, ghg_ref, tri_ref, lmask_ref, bias_ref,
                    o_ref.at[rows, :], state_ref, kvprev_ref)


def _mixer_tile(no_prev, sinks_ref, x_ref, gmix_ref, win_ref, pos_ref, invf_ref, ecos_ref, esin_ref, wout_ref,
                gatt_ref, lbl_ref, ghg_ref, tri_ref, lmask_ref, bias_ref, o_ref, state_ref, kvprev_ref):
    ts = MIX_TILE
    blk = ATT_BLOCK

    x = x_ref[...]
    hn = (x * _rms_scale(x) * gmix_ref[...]).astype(BF16)

    def proj(c0, c1):
        return _dot(hn, win_ref[:, c0:c1])

    cos, sin = _rotary_tables(pos_ref, invf_ref, ecos_ref, esin_ref)
    lane = lax.broadcasted_iota(jnp.int32, cos.shape, 1)
    first_half = (lane & (ATT_HEAD_DIM - 1)) < ROPE_HALF
    lo_head = lane < ATT_HEAD_DIM

    def rotate(col):
        partner = jnp.where(first_half,
                            pltpu.roll(col, LANES - ROPE_HALF, axis=1),
                            pltpu.roll(col, ROPE_HALF, axis=1))
        return col * cos + partner * sin

    def padded(col):
        swapped = pltpu.roll(col, ATT_HEAD_DIM, axis=1)
        zero = jnp.zeros_like(col)
        return [v.astype(BF16) for v in (jnp.where(lo_head, col, zero), jnp.where(lo_head, zero, swapped),
                                         jnp.where(lo_head, swapped, zero), jnp.where(lo_head, zero, col))]

    pq = proj(0, ATT_WIDTH)
    q_cols = [(rotate(pq[:, j * LANES:(j + 1) * LANES]) * (ATT_HEAD_DIM ** -0.5 * LOG2_E)).astype(BF16)
              for j in range(ATT_WIDTH // LANES)]
    pkv = proj(ATT_WIDTH, ATT_WIDTH + 2 * ATT_KV_WIDTH)
    kv_new = padded(rotate(pkv[:, :LANES])) + padded(pkv[:, LANES:])
    kv_all = [jnp.concatenate([kvprev_ref[:, i * LANES:(i + 1) * LANES], v], axis=0)
              for i, v in enumerate(kv_new)]
    for i, v in enumerate(kv_new):
        kvprev_ref[:, i * LANES:(i + 1) * LANES] = v[ts - blk:]
    r0c = ATT_WIDTH + 2 * ATT_KV_WIDTH
    p_q, p_f, p_i, p_g = (proj(r0c + k * HG_WIDTH, r0c + (k + 1) * HG_WIDTH) for k in range(4))

    upper_half_rows = lax.broadcasted_iota(jnp.int32, (2 * blk, 1), 0) >= blk
    lane_ids = lax.broadcasted_iota(jnp.int32, (2 * blk, LANES), 1)
    lane_lo = lane_ids < ATT_HEAD_DIM
    ones_lo = jnp.where(lane_lo, 1.0, 0.0).astype(BF16)
    ones_hi = jnp.where(lane_lo, 0.0, 1.0).astype(BF16)
    bf16_rows = 2 * SUBLANES
    drop_row0 = jnp.where(lax.broadcasted_iota(jnp.int32, (bf16_rows, LANES), 0) > 0, 1.0, 0.0).astype(BF16)

    def sink_slot(op):
        return jnp.concatenate([op[:bf16_rows] * drop_row0, op[bf16_rows:]], axis=0)

    att_units = {}

    def att_scores(jb, gk):
        keys = slice(jb * blk, (jb + 2) * blk)
        bias = bias_ref[no_prev] if jb == 0 else bias_ref[0]
        q2 = jnp.concatenate([q_cols[2 * gk][jb * blk:(jb + 1) * blk],
                              q_cols[2 * gk + 1][jb * blk:(jb + 1) * blk]], axis=0)
        k_lo = sink_slot(kv_all[2 * gk][keys])
        k_hi = sink_slot(kv_all[2 * gk + 1][keys])
        h0 = ATT_GROUP * gk
        probs = []
        for which, k_op in ((0, k_lo), (1, k_hi)):
            sink = jnp.where(upper_half_rows, sinks_ref[h0 + 2 + which], sinks_ref[h0 + which]) * LOG2_E
            s = _dot_nt(q2, k_op) + bias
            s_first = s[:, :LANES] + jnp.where(lane_ids == 0, sink, 0.0)
            m = jnp.max(jnp.maximum(s_first, s[:, LANES:]), axis=-1, keepdims=True)
            probs.append(jnp.exp2(jnp.concatenate([s_first, s[:, LANES:]], axis=1) - m).astype(BF16))
        att_units[jb, gk] = jnp.concatenate(probs, axis=1)

    def att_values(jb, gk):
        keys = slice(jb * blk, (jb + 2) * blk)
        r_lo = jnp.concatenate([sink_slot(kv_all[4 + 2 * gk][keys]), ones_lo], axis=1)
        r_hi = jnp.concatenate([sink_slot(kv_all[5 + 2 * gk][keys]), ones_hi], axis=1)
        out = _dot(att_units[jb, gk], jnp.concatenate([r_lo, r_hi], axis=0))
        att_units[jb, gk] = out[:, :LANES] / out[:, LANES:]

    units = [(jb, gk) for jb in range(ts // blk) for gk in range(ATT_KV_HEADS)]
    att_scores(*units[0])
    for prev_u, u in zip(units[:-1], units[1:]):
        att_scores(*u)
        att_values(*prev_u)
    att_values(*units[-1])
    att_rows = []
    for jb in range(ts // blk):
        cols_ = []
        for gk in range(ATT_KV_HEADS):
            cols_ += [att_units[jb, gk][:blk], att_units[jb, gk][blk:]]
        att_rows.append(jnp.concatenate(cols_, axis=1))
    att = jnp.concatenate(att_rows, axis=0)

    lbl = lbl_ref[...]
    lexp = jnp.exp(lbl - jnp.max(lbl, axis=0, keepdims=True))
    lb = lexp[0:1, :] / jnp.sum(lexp, axis=0, keepdims=True)
    tri = tri_ref[...]
    row_ids = lax.broadcasted_iota(jnp.int32, (ts, HG_DIM), 0)
    half = ts // 2
    f_all = 0.5 * (1.0 + lb) + (0.5 * (1.0 - lb)) * jnp.tanh(0.5 * p_f)
    g_all = jnp.log(f_all) * LOG2_E
    g_hi = g_all.astype(BF16)
    g_lo = (g_all - g_hi.astype(F32)).astype(BF16)
    b2 = _dot(tri, jnp.concatenate([g_hi, g_lo], axis=1))
    b_all = b2[:, :HG_WIDTH] + b2[:, HG_WIDTH:]
    n_tiles = half // SUBLANES
    hs = [dict() for _ in range(HG_HEADS)]

    def prep(h):
        cols = slice(h * HG_DIM, (h + 1) * HG_DIM)
        f = f_all[:, cols]
        b = b_all[:, cols]
        kk = 1.0 - f
        qf = _silu(p_q[:, cols])
        vv = p_i[:, cols].astype(BF16)
        w_incl = jnp.exp2(b)
        w_tail = jnp.exp2(b[ts - 1:ts, :] - b)
        st = state_ref[h]
        o = _dot_nt((qf * w_incl).astype(BF16), st.astype(BF16))
        state_ref[h] = st * w_incl[ts - 1:ts, :] + _dot_tn(vv, (kk * w_tail).astype(BF16))
        hs[h].update(f=f, b=b, kk=kk, qf=qf, vv=vv, o=o, cols=cols)

    def levels(h):
        d = hs[h]
        qf, kk, b, f = d["qf"], d["kk"], d["b"], d["f"]
        quad = [[None] * n_tiles, [None] * n_tiles]
        for li, m in enumerate(HG_LEVELS):
            x_f = _select_rows(m, qf, kk, row_ids) * _level_decay(m, b, f, row_ids)
            x_l = x_f.astype(BF16)
            if 2 * m == ts:
                d["s10"] = _dot_nt(x_l[half:], x_l[:half])
                continue
            if m >= SUBLANES:
                tiles = [t for t in range(n_tiles) if (t * SUBLANES // m) % 2 == 1]
            else:
                tiles = list(range(n_tiles))
            for hf in range(2):
                r0 = hf * half
                if len(tiles) == n_tiles:
                    lhs = x_l[r0:r0 + half]
                else:
                    lhs = jnp.concatenate([x_f[r0 + t * SUBLANES:r0 + (t + 1) * SUBLANES] for t in tiles],
                                          axis=0).astype(BF16)
                sc = _dot_nt(lhs, x_l[r0:r0 + half])
                for i, t in enumerate(tiles):
                    m0 = (li - 1) * LANES + t * SUBLANES
                    part = sc[i * SUBLANES:(i + 1) * SUBLANES] * lmask_ref[m0:m0 + SUBLANES, :]
                    quad[hf][t] = part if quad[hf][t] is None else quad[hf][t] + part
        d["s00"] = jnp.concatenate(quad[0], axis=0).astype(BF16)
        d["s1"] = jnp.concatenate([d["s10"], jnp.concatenate(quad[1], axis=0)], axis=1).astype(BF16)

    def finish(h):
        d = hs[h]
        vv = d["vv"]
        o_top = _dot(d["s00"], vv[:half])
        o_bot = _dot(d["s1"], vv)
        diag = jnp.sum(d["qf"] * d["kk"], axis=-1, keepdims=True)
        o = d["o"] + jnp.concatenate([o_top, o_bot], axis=0) + diag * p_i[:, d["cols"]]
        d["rec"] = o * _rms_scale(o)

    prep(0)
    prep(1)
    levels(0)
    prep(2)
    levels(1)
    prep(3)
    levels(2)
    finish(0)
    levels(3)
    finish(1)
    finish(2)
    finish(3)
    rec_cols = [hs[h]["rec"] for h in range(HG_HEADS)]
    rec_n = (jnp.concatenate(rec_cols, axis=1) * ghg_ref[...] * _silu(p_g)).astype(BF16)

    att_n = (att * _rms_scale(att) * gatt_ref[...]).astype(BF16)
    o_ref[...] = x + _dot(jnp.concatenate([att_n, rec_n], axis=1), wout_ref[...])


def _cast_chunk_rows(n_rows, n_steps):
    tile = 2 * SUBLANES
    rows = tile
    while n_rows % rows or n_rows // rows > n_steps:
        rows += tile
    return rows


def _mixer(x, gmix, w_in, positions, w_out, sinks, gatt, lbl, ghg, side_weights):
    b, s, _ = x.shape
    ts = TILES_PER_STEP * MIX_TILE
    tri, lmask = _hgrn_constants()
    invf, ecos, esin = _rotary_constants()
    bias = _attention_bias()
    pos = positions.reshape(b * (s // ts), 1, ts)
    cur = lambda bi, i: (bi, i, 0)
    fixed = lambda bi, i: (0, 0)
    hbm = pl.BlockSpec(memory_space=pl.ANY)
    n_cast_steps = b * (s // ts) // CAST_STRIDE
    side_scratch = []
    for w in side_weights:
        stage = (2, _cast_chunk_rows(w.shape[0], n_cast_steps), w.shape[1])
        side_scratch += [pltpu.VMEM(stage, F32), pltpu.VMEM(stage, BF16),
                         pltpu.SemaphoreType.DMA((2,)), pltpu.SemaphoreType.DMA((2,))]
    outs = pl.pallas_call(
        _mixer_kernel,
        grid=(b, s // ts),
        in_specs=[
            pl.BlockSpec(memory_space=pltpu.SMEM),
            pl.BlockSpec((None, ts, D_MODEL), cur),
            pl.BlockSpec((1, D_MODEL), fixed),
            hbm,
            pl.BlockSpec((None, 1, ts), lambda bi, i: (bi * (s // ts) + i, 0, 0)),
            pl.BlockSpec(invf.shape, fixed),
            pl.BlockSpec(ecos.shape, fixed),
            pl.BlockSpec(esin.shape, fixed),
            hbm,
            pl.BlockSpec((1, ATT_WIDTH), fixed),
            pl.BlockSpec(lbl.shape, fixed),
            pl.BlockSpec((1, HG_WIDTH), fixed),
            pl.BlockSpec(tri.shape, fixed),
            pl.BlockSpec(lmask.shape, fixed),
            pl.BlockSpec(bias.shape, lambda bi, i: (0, 0, 0)),
        ] + [hbm] * len(side_weights),
        out_specs=[pl.BlockSpec((None, ts, D_MODEL), cur)] + [hbm] * len(side_weights),
        out_shape=[jax.ShapeDtypeStruct((b, s, D_MODEL), F32)]
        + [jax.ShapeDtypeStruct(w.shape, BF16) for w in side_weights],
        scratch_shapes=[
            pltpu.VMEM((HG_HEADS, HG_DIM, HG_DIM), F32),
            pltpu.VMEM((ATT_BLOCK, KV_PAD_WIDTH), BF16),
            pltpu.VMEM(w_in.shape, BF16),
            pltpu.VMEM(w_out.shape, BF16),
            pltpu.VMEM((2, OWN_STAGE_ROWS, w_in.shape[1]), F32),
            pltpu.VMEM((2, OWN_STAGE_ROWS, w_out.shape[1]), F32),
            pltpu.SemaphoreType.DMA((2,)),
            pltpu.SemaphoreType.DMA((2,)),
        ] + side_scratch,
        compiler_params=pltpu.CompilerParams(
            dimension_semantics=("arbitrary", "arbitrary"), vmem_limit_bytes=VMEM_LIMIT_BYTES),
        name="mixer",
    )(sinks, x, gmix, w_in, pos, invf, ecos, esin, w_out, gatt, lbl, ghg, tri, lmask, bias, *side_weights)
    return outs[0], outs[1:]


def _xattn_kernel(x_ref, gain_ref, wq_ref, mem_ref, gmem_ref, wkv_ref, wo_ref, o_ref, kv_ref):
    @pl.when(pl.program_id(1) == 0)
    def _():
        m = mem_ref[...]
        mn = (m * _rms_scale(m) * gmem_ref[...]).astype(BF16)
        kv_ref[...] = _dot(mn, wkv_ref[...]).astype(BF16)

    x = x_ref[...]
    hq = (x * _rms_scale(x) * gain_ref[...]).astype(BF16)
    q = (_dot(hq, wq_ref[...]) * (X_HEAD_DIM ** -0.5 * LOG2_E)).astype(BF16)
    probs = [None] * X_HEADS
    outs = [None] * X_HEADS

    def scores(h):
        cols = slice(h * X_HEAD_DIM, (h + 1) * X_HEAD_DIM)
        s = _dot_nt(q[:, cols], kv_ref[:, cols])
        probs[h] = jnp.exp2(s - jnp.max(s, axis=-1, keepdims=True))

    def values(h):
        p = probs[h]
        o = _dot(p.astype(BF16), kv_ref[:, D_MODEL + h * X_HEAD_DIM:D_MODEL + (h + 1) * X_HEAD_DIM])
        outs[h] = (o / jnp.sum(p, axis=-1, keepdims=True)).astype(BF16)

    scores(0)
    for h in range(1, X_HEADS):
        scores(h)
        values(h - 1)
    values(X_HEADS - 1)
    o_ref[...] = x + _dot(jnp.concatenate(outs, axis=1), wo_ref[...])


def _xattn(x, gain, w_xq, mem, gmem, w_xkv, w_xo, tm):
    b, s, _ = x.shape
    m = mem.shape[1]
    cur = lambda bi, i: (bi, i, 0)
    per_batch = lambda bi, i: (bi, 0, 0)
    fixed = lambda bi, i: (0, 0)
    once = dict(pipeline_mode=pl.Buffered(1))
    return pl.pallas_call(
        _xattn_kernel,
        grid=(b, s // tm),
        in_specs=[
            pl.BlockSpec((None, tm, D_MODEL), cur),
            pl.BlockSpec((1, D_MODEL), fixed),
            pl.BlockSpec((D_MODEL, D_MODEL), fixed, **once),
            pl.BlockSpec((None, m, D_MODEL), per_batch),
            pl.BlockSpec((1, D_MODEL), fixed),
            pl.BlockSpec((D_MODEL, 2 * D_MODEL), fixed, **once),
            pl.BlockSpec((D_MODEL, D_MODEL), fixed, **once),
        ],
        out_specs=pl.BlockSpec((None, tm, D_MODEL), cur),
        out_shape=jax.ShapeDtypeStruct((b, s, D_MODEL), F32),
        scratch_shapes=[pltpu.VMEM((m, 2 * D_MODEL), BF16)],
        compiler_params=pltpu.CompilerParams(
            dimension_semantics=("arbitrary", "arbitrary"), vmem_limit_bytes=VMEM_LIMIT_BYTES),
        name="xattn",
    )(x, gain, w_xq, mem, gmem, w_xkv, w_xo)


FFN_CHUNKS = ((0, 1024), (1024, 2048), (2048, FFN_HIDDEN))

def _ffn_kernel(x_ref, gain_ref, wgu_ref, wd_ref, gfin_ref, o_ref):
    n_sub = 2
    sub = x_ref.shape[0] // n_sub
    hf, acc = [], []
    for i in range(n_sub):
        x = x_ref[i * sub:(i + 1) * sub, :]
        hf.append((x * _rms_scale(x) * gain_ref[...]).astype(BF16))
        acc.append(x)
    for lo, hi in FFN_CHUNKS:
        for i in range(n_sub):
            gate = _dot(hf[i], wgu_ref[:, lo:hi])
            upv = _dot(hf[i], wgu_ref[:, FFN_HIDDEN + lo:FFN_HIDDEN + hi])
            act = (_silu(gate) * upv).astype(BF16)
            acc[i] = acc[i] + _dot(act, wd_ref[lo:hi, :])
    for i in range(n_sub):
        o_ref[i * sub:(i + 1) * sub, :] = acc[i] * _rms_scale(acc[i]) * gfin_ref[...]


def _ffn(x2d, gain, w_gu, w_d, gfin, tm):
    n = x2d.shape[0]
    row = lambda i: (i, 0)
    fixed = lambda i: (0, 0)
    return pl.pallas_call(
        _ffn_kernel,
        grid=(n // tm,),
        in_specs=[
            pl.BlockSpec((tm, D_MODEL), row),
            pl.BlockSpec((1, D_MODEL), fixed),
            pl.BlockSpec((D_MODEL, 2 * FFN_HIDDEN), fixed, pipeline_mode=pl.Buffered(1)),
            pl.BlockSpec((FFN_HIDDEN, D_MODEL), fixed, pipeline_mode=pl.Buffered(1)),
            pl.BlockSpec((1, D_MODEL), fixed),
        ],
        out_specs=pl.BlockSpec((tm, D_MODEL), row),
        out_shape=jax.ShapeDtypeStruct((n, D_MODEL), F32),
        compiler_params=pltpu.CompilerParams(
            dimension_semantics=("arbitrary",), vmem_limit_bytes=VMEM_LIMIT_BYTES),
        name="ffn",
    )(x2d, gain, w_gu, w_d, gfin)


def kernel(x, mem, positions, norm_mix, w_in, att_sinks, att_out_gain, hg_lb_logits, hg_out_gain,
           w_out, norm_xattn, norm_mem, w_xq, w_xkv, w_xo, norm_ffn, w_gate_up, w_down, norm_final):
    b, s, d = x.shape
    n = b * s
    assert w_in.shape[0] == 1 and hg_lb_logits.shape[0] == 2, "single-layer block only"
    assert all(a.dtype == F32 for a in (x, mem, w_in, w_out, w_xq, w_xkv, w_xo, w_gate_up, w_down))
    row = lambda v: v.reshape(1, -1)
    side = [w[0] for w in (w_xkv, w_xq, w_xo, w_gate_up, w_down)]
    x, (w_xkv_b, w_xq_b, w_xo_b, w_gu_b, w_d_b) = _mixer(
        x, row(norm_mix[0]), w_in[0], positions,
        w_out[0], att_sinks[0], row(att_out_gain[0]),
        hg_lb_logits, row(hg_out_gain[0]), side)
    x = _xattn(x, row(norm_xattn[0]), w_xq_b, mem, row(norm_mem[0]), w_xkv_b, w_xo_b, tm=DENSE_TILE)
    y = _ffn(x.reshape(n, d), row(norm_ffn[0]), w_gu_b, w_d_b, row(norm_final), tm=DENSE_TILE)
    return y.reshape(b, s, d)
```

```python
import numpy as np
import jax
import jax.numpy as jnp
from jax import lax
from jax.experimental import pallas as pl
from jax.experimental.pallas import tpu as pltpu

F32 = jnp.float32
BF16 = jnp.bfloat16

D_MODEL = 1024
ATT_HEADS = 8
ATT_KV_HEADS = 2
ATT_GROUP = ATT_HEADS // ATT_KV_HEADS
ATT_HEAD_DIM = 64
ATT_WIDTH = ATT_HEADS * ATT_HEAD_DIM
ATT_KV_WIDTH = ATT_KV_HEADS * ATT_HEAD_DIM
ATT_BLOCK = 128
ROPE_THETA = 500000.0
ROPE_DIM = ATT_HEAD_DIM // 4
ROPE_HALF = ROPE_DIM // 2
HG_HEADS = 4
HG_DIM = 128
HG_WIDTH = HG_HEADS * HG_DIM
X_HEADS = 4
X_HEAD_DIM = D_MODEL // X_HEADS
FFN_HIDDEN = 2816
RMS_EPS = 1e-6
LANES = 128
SUBLANES = 8
MASK_VALUE = -1e30
LOG2_E = 1.4426950408889634

MIX_TILE = 256
HG_LEVELS = (128, 64, 32, 16, 8, 4, 2, 1)
KV_PAD_WIDTH = 2 * ATT_KV_HEADS * 2 * LANES
TILES_PER_STEP = 4
CAST_STRIDE = 1
DENSE_TILE = 1024
OWN_STAGE_ROWS = 128

VMEM_LIMIT_BYTES = 56 * 1024 * 1024


def _rms_scale(x):
    return lax.rsqrt(jnp.mean(x * x, axis=-1, keepdims=True) + RMS_EPS)


def _silu(x):
    h = 0.5 * x
    return h + h * jnp.tanh(h)


def _dot(a, b):
    return jnp.dot(a, b, preferred_element_type=F32)


def _dot_nt(a, b):
    return lax.dot_general(a, b, (((1,), (1,)), ((), ())), preferred_element_type=F32)


def _dot_tn(a, b):
    return lax.dot_general(a, b, (((0,), (0,)), ((), ())), preferred_element_type=F32)


def _stage_weight(src_hbm, dst_ref, stage_ref, sem_ref):
    rows = stage_ref.shape[1]
    n_chunks = dst_ref.shape[0] // rows
    assert n_chunks * rows == dst_ref.shape[0] and src_hbm.shape == dst_ref.shape

    def copy(c):
        return pltpu.make_async_copy(src_hbm.at[pl.ds(c * rows, rows), :], stage_ref.at[c % 2], sem_ref.at[c % 2])

    copy(0).start()
    for c in range(n_chunks):
        if c + 1 < n_chunks:
            copy(c + 1).start()
        copy(c).wait()
        dst_ref[c * rows:(c + 1) * rows, :] = stage_ref[c % 2].astype(BF16)


def _cast_step(t, issue, src_hbm, dst_hbm, in_stage, out_stage, in_sem, out_sem):
    rows = in_stage.shape[1]
    n = src_hbm.shape[0] // rows
    assert n * rows == src_hbm.shape[0] and src_hbm.shape == dst_hbm.shape
    slot = lax.rem(t, 2)

    def read(c, s):
        return pltpu.make_async_copy(src_hbm.at[pl.ds(c * rows, rows), :], in_stage.at[s], in_sem.at[s])

    def write(c, s):
        return pltpu.make_async_copy(out_stage.at[s], dst_hbm.at[pl.ds(c * rows, rows), :], out_sem.at[s])

    if issue:
        @pl.when(t == 0)
        def _():
            read(0, 0).start()

        @pl.when(t + 1 < n)
        def _():
            read(t + 1, 1 - slot).start()

        return

    @pl.when(t < n)
    def _():
        read(t, slot).wait()

        @pl.when(t >= 2)
        def _():
            write(t - 2, slot).wait()

        out_stage[slot] = in_stage[slot].astype(BF16)
        write(t, slot).start()

    @pl.when(t == n - 1)
    def _():
        write(t - 1, 1 - slot).wait()
        write(t, slot).wait()


def _hgrn_constants():
    r = np.arange(MIX_TILE)[:, None]
    j = np.arange(MIX_TILE)[None, :]
    tri = (j <= r).astype(np.float32)
    rr = np.arange(LANES)[:, None]
    jj = np.arange(LANES)[None, :]
    masks = []
    for m in HG_LEVELS:
        if 2 * m <= LANES:
            masks.append(((rr // (2 * m)) == (jj // (2 * m))) & ((rr % (2 * m)) >= m) & ((jj % (2 * m)) < m))
    lmask = np.concatenate(masks, axis=0).astype(np.float32)
    return jnp.asarray(tri, BF16), jnp.asarray(lmask, F32)


def _attention_bias():
    i = (np.arange(2 * ATT_BLOCK) % ATT_BLOCK)[:, None]
    j = np.arange(2 * ATT_BLOCK)[None, :]
    band = (j > i) & (j <= i + ATT_BLOCK)
    variants = (band, band & (j >= ATT_BLOCK))
    return jnp.asarray(np.stack([np.where(v | (j == 0), 0.0, MASK_VALUE) for v in variants]), F32)


def _rotary_constants():
    inv_freq = jnp.power(jnp.float32(ROPE_THETA),
                         -jnp.arange(ROPE_HALF, dtype=F32) * (2.0 / ROPE_DIM))
    invf = jnp.broadcast_to(inv_freq[:, None], (ROPE_HALF, MIX_TILE))
    lane = np.arange(LANES)
    in_head = lane % ATT_HEAD_DIM
    freq_row = (np.arange(ROPE_HALF)[:, None] == (in_head % ROPE_HALF)[None, :])
    rot = freq_row & (in_head < ROPE_DIM)[None, :]
    ecos = np.zeros((4 * ROPE_HALF, LANES), np.float32)
    ecos[:ROPE_HALF] = rot
    ecos[ROPE_HALF:2 * ROPE_HALF] = rot
    ecos[2 * ROPE_HALF] = in_head >= ROPE_DIM
    sign = np.where(in_head < ROPE_HALF, -1.0, 1.0)[None, :]
    esin = np.concatenate([rot * sign, rot * sign], axis=0).astype(np.float32)
    return invf, jnp.asarray(ecos, BF16), jnp.asarray(esin, BF16)


def _level_decay(m, b, f, row_ids):
    ts = b.shape[0]
    if m >= SUBLANES // 2:
        b3 = b.reshape(ts // (2 * m), 2 * m, HG_DIM)
        ref = b3[:, m - 1:m, :]
        if m >= SUBLANES:
            e3 = jnp.concatenate([ref - b3[:, :m, :], b3[:, m:, :] - ref], axis=1)
        else:
            e3 = -jnp.abs(b3 - ref)
        return jnp.exp2(e3).reshape(ts, HG_DIM)
    if m == 2:
        pos = row_ids & 3
        f_next = pltpu.roll(f, ts - 1, axis=0)
        f_prev = pltpu.roll(f, 1, axis=0)
        return jnp.where(pos == 0, f_next, jnp.where(pos == 1, 1.0, jnp.where(pos == 2, f, f * f_prev)))
    assert m == 1
    return jnp.where((row_ids & 1) != 0, f, 1.0)


def _select_rows(m, upper_src, lower_src, row_ids):
    c = upper_src.shape[0]
    if m >= SUBLANES:
        pieces = []
        for b0 in range(0, c, 2 * m):
            pieces.append(lower_src[b0:b0 + m])
            pieces.append(upper_src[b0 + m:b0 + 2 * m])
        return jnp.concatenate(pieces, axis=0)
    return jnp.where((row_ids & m) != 0, upper_src, lower_src)


def _rotary_tables(pos_ref, invf_ref, ecos_ref, esin_ref):
    ang = invf_ref[...] * pos_ref[...].astype(F32)

    def spread(t, e_ref, extra):
        t_hi = t.astype(BF16).astype(F32)
        rows = jnp.concatenate([t_hi, t - t_hi] + extra, axis=0).astype(BF16)
        return _dot_tn(rows, e_ref[...])

    ones = jnp.ones_like(ang)
    cos_t = spread(jnp.cos(ang), ecos_ref, [ones, jnp.zeros_like(ang)])
    sin_t = spread(jnp.sin(ang), esin_ref, [])
    return cos_t, sin_t


def _mixer_kernel(sinks_ref, x_ref, gmix_ref, win_hbm, pos_ref, invf_ref, ecos_ref, esin_ref, wout_hbm,
                  gatt_ref, lbl_ref, ghg_ref, tri_ref, lmask_ref, bias_ref, *rest):
    n_own = 8
    n_side = (len(rest) - 1 - n_own) // 6
    side_src, o_ref, side_dst = rest[:n_side], rest[n_side], rest[n_side + 1:2 * n_side + 1]
    own = rest[2 * n_side + 1:2 * n_side + 1 + n_own]
    state_ref, kvprev_ref, win_ref, wout_ref, win_stage, wout_stage, win_sem, wout_sem = own
    side_scratch = rest[2 * n_side + 1 + n_own:]
    step = pl.program_id(1)
    ts = MIX_TILE
    blk = ATT_BLOCK

    t = pl.program_id(0) * pl.num_programs(1) + step

    @pl.when(t == 0)
    def _():
        _stage_weight(win_hbm, win_ref, win_stage, win_sem)
        _stage_weight(wout_hbm, wout_ref, wout_stage, wout_sem)

    @pl.when(lax.rem(t, CAST_STRIDE) == 0)
    def _():
        for issue in (True, False):
            for k in range(n_side):
                _cast_step(lax.div(t, CAST_STRIDE), issue, side_src[k], side_dst[k],
                           *side_scratch[4 * k:4 * k + 4])

    @pl.when(step == 0)
    def _():
        state_ref[...] = jnp.zeros_like(state_ref)
        kvprev_ref[...] = jnp.zeros_like(kvprev_ref)

    for k in range(TILES_PER_STEP):
        rows = pl.ds(k * ts, ts)
        no_prev = jnp.where(step > 0, 0, 1) if k == 0 else 0
        _mixer_tile(no_prev, sinks_ref, x_ref.at[rows, :], gmix_ref, win_ref, pos_ref.at[:, rows], invf_ref,
                    ecos_ref, esin_ref, wout_ref, gatt_ref, lbl_ref, ghg_ref, tri_ref, lmask_ref, bias_ref,
                    o_ref.at[rows, :], state_ref, kvprev_ref)


def _mixer_tile(no_prev, sinks_ref, x_ref, gmix_ref, win_ref, pos_ref, invf_ref, ecos_ref, esin_ref, wout_ref,
                gatt_ref, lbl_ref, ghg_ref, tri_ref, lmask_ref, bias_ref, o_ref, state_ref, kvprev_ref):
    ts = MIX_TILE
    blk = ATT_BLOCK

    x = x_ref[...]
    hn = (x * _rms_scale(x) * gmix_ref[...]).astype(BF16)

    def proj(c0, c1):
        return _dot(hn, win_ref[:, c0:c1])

    cos, sin = _rotary_tables(pos_ref, invf_ref, ecos_ref, esin_ref)
    lane = lax.broadcasted_iota(jnp.int32, cos.shape, 1)
    first_half = (lane & (ATT_HEAD_DIM - 1)) < ROPE_HALF
    lo_head = lane < ATT_HEAD_DIM

    def rotate(col):
        partner = jnp.where(first_half,
                            pltpu.roll(col, LANES - ROPE_HALF, axis=1),
                            pltpu.roll(col, ROPE_HALF, axis=1))
        return col * cos + partner * sin

    def padded(col):
        swapped = pltpu.roll(col, ATT_HEAD_DIM, axis=1)
        zero = jnp.zeros_like(col)
        return [v.astype(BF16) for v in (jnp.where(lo_head, col, zero), jnp.where(lo_head, zero, swapped),
                                         jnp.where(lo_head, swapped, zero), jnp.where(lo_head, zero, col))]

    pq = proj(0, ATT_WIDTH)
    q_cols = [(rotate(pq[:, j * LANES:(j + 1) * LANES]) * (ATT_HEAD_DIM ** -0.5 * LOG2_E)).astype(BF16)
              for j in range(ATT_WIDTH // LANES)]
    pkv = proj(ATT_WIDTH, ATT_WIDTH + 2 * ATT_KV_WIDTH)
    kv_new = padded(rotate(pkv[:, :LANES])) + padded(pkv[:, LANES:])
    kv_all = [jnp.concatenate([kvprev_ref[:, i * LANES:(i + 1) * LANES], v], axis=0)
              for i, v in enumerate(kv_new)]
    for i, v in enumerate(kv_new):
        kvprev_ref[:, i * LANES:(i + 1) * LANES] = v[ts - blk:]
    r0c = ATT_WIDTH + 2 * ATT_KV_WIDTH
    p_q, p_f, p_i, p_g = (proj(r0c + k * HG_WIDTH, r0c + (k + 1) * HG_WIDTH) for k in range(4))

    upper_half_rows = lax.broadcasted_iota(jnp.int32, (2 * blk, 1), 0) >= blk
    lane_ids = lax.broadcasted_iota(jnp.int32, (2 * blk, LANES), 1)
    lane_lo = lane_ids < ATT_HEAD_DIM
    ones_lo = jnp.where(lane_lo, 1.0, 0.0).astype(BF16)
    ones_hi = jnp.where(lane_lo, 0.0, 1.0).astype(BF16)
    bf16_rows = 2 * SUBLANES
    drop_row0 = jnp.where(lax.broadcasted_iota(jnp.int32, (bf16_rows, LANES), 0) > 0, 1.0, 0.0).astype(BF16)

    def sink_slot(op):
        return jnp.concatenate([op[:bf16_rows] * drop_row0, op[bf16_rows:]], axis=0)

    att_units = {}

    def att_scores(jb, gk):
        keys = slice(jb * blk, (jb + 2) * blk)
        bias = bias_ref[no_prev] if jb == 0 else bias_ref[0]
        q2 = jnp.concatenate([q_cols[2 * gk][jb * blk:(jb + 1) * blk],
                              q_cols[2 * gk + 1][jb * blk:(jb + 1) * blk]], axis=0)
        k_lo = sink_slot(kv_all[2 * gk][keys])
        k_hi = sink_slot(kv_all[2 * gk + 1][keys])
        h0 = ATT_GROUP * gk
        probs = []
        for which, k_op in ((0, k_lo), (1, k_hi)):
            sink = jnp.where(upper_half_rows, sinks_ref[h0 + 2 + which], sinks_ref[h0 + which]) * LOG2_E
            s = _dot_nt(q2, k_op) + bias
            s_first = s[:, :LANES] + jnp.where(lane_ids == 0, sink, 0.0)
            m = jnp.max(jnp.maximum(s_first, s[:, LANES:]), axis=-1, keepdims=True)
            probs.append(jnp.exp2(jnp.concatenate([s_first, s[:, LANES:]], axis=1) - m).astype(BF16))
        att_units[jb, gk] = jnp.concatenate(probs, axis=1)

    def att_values(jb, gk):
        keys = slice(jb * blk, (jb + 2) * blk)
        r_lo = jnp.concatenate([sink_slot(kv_all[4 + 2 * gk][keys]), ones_lo], axis=1)
        r_hi = jnp.concatenate([sink_slot(kv_all[5 + 2 * gk][keys]), ones_hi], axis=1)
        out = _dot(att_units[jb, gk], jnp.concatenate([r_lo, r_hi], axis=0))
        att_units[jb, gk] = out[:, :LANES] / out[:, LANES:]

    units = [(jb, gk) for jb in range(ts // blk) for gk in range(ATT_KV_HEADS)]
    att_scores(*units[0])
    for prev_u, u in zip(units[:-1], units[1:]):
        att_scores(*u)
        att_values(*prev_u)
    att_values(*units[-1])
    att_rows = []
    for jb in range(ts // blk):
        cols_ = []
        for gk in range(ATT_KV_HEADS):
            cols_ += [att_units[jb, gk][:blk], att_units[jb, gk][blk:]]
        att_rows.append(jnp.concatenate(cols_, axis=1))
    att = jnp.concatenate(att_rows, axis=0)

    lbl = lbl_ref[...]
    lexp = jnp.exp(lbl - jnp.max(lbl, axis=0, keepdims=True))
    lb = lexp[0:1, :] / jnp.sum(lexp, axis=0, keepdims=True)
    tri = tri_ref[...]
    row_ids = lax.broadcasted_iota(jnp.int32, (ts, HG_DIM), 0)
    half = ts // 2
    f_all = 0.5 * (1.0 + lb) + (0.5 * (1.0 - lb)) * jnp.tanh(0.5 * p_f)
    g_all = jnp.log(f_all) * LOG2_E
    g_hi = g_all.astype(BF16)
    g_lo = (g_all - g_hi.astype(F32)).astype(BF16)
    b2 = _dot(tri, jnp.concatenate([g_hi, g_lo], axis=1))
    b_all = b2[:, :HG_WIDTH] + b2[:, HG_WIDTH:]
    n_tiles = half // SUBLANES
    hs = [dict() for _ in range(HG_HEADS)]

    def prep(h):
        cols = slice(h * HG_DIM, (h + 1) * HG_DIM)
        f = f_all[:, cols]
        b = b_all[:, cols]
        kk = 1.0 - f
        qf = _silu(p_q[:, cols])
        vv = p_i[:, cols].astype(BF16)
        w_incl = jnp.exp2(b)
        w_tail = jnp.exp2(b[ts - 1:ts, :] - b)
        st = state_ref[h]
        o = _dot_nt((qf * w_incl).astype(BF16), st.astype(BF16))
        state_ref[h] = st * w_incl[ts - 1:ts, :] + _dot_tn(vv, (kk * w_tail).astype(BF16))
        hs[h].update(f=f, b=b, kk=kk, qf=qf, vv=vv, o=o, cols=cols)

    def levels(h):
        d = hs[h]
        qf, kk, b, f = d["qf"], d["kk"], d["b"], d["f"]
        quad = [[None] * n_tiles, [None] * n_tiles]
        for li, m in enumerate(HG_LEVELS):
            x_f = _select_rows(m, qf, kk, row_ids) * _level_decay(m, b, f, row_ids)
            x_l = x_f.astype(BF16)
            if 2 * m == ts:
                d["s10"] = _dot_nt(x_l[half:], x_l[:half])
                continue
            if m >= SUBLANES:
                tiles = [t for t in range(n_tiles) if (t * SUBLANES // m) % 2 == 1]
            else:
                tiles = list(range(n_tiles))
            for hf in range(2):
                r0 = hf * half
                if len(tiles) == n_tiles:
                    lhs = x_l[r0:r0 + half]
                else:
                    lhs = jnp.concatenate([x_f[r0 + t * SUBLANES:r0 + (t + 1) * SUBLANES] for t in tiles],
                                          axis=0).astype(BF16)
                sc = _dot_nt(lhs, x_l[r0:r0 + half])
                for i, t in enumerate(tiles):
                    m0 = (li - 1) * LANES + t * SUBLANES
                    part = sc[i * SUBLANES:(i + 1) * SUBLANES] * lmask_ref[m0:m0 + SUBLANES, :]
                    quad[hf][t] = part if quad[hf][t] is None else quad[hf][t] + part
        d["s00"] = jnp.concatenate(quad[0], axis=0).astype(BF16)
        d["s1"] = jnp.concatenate([d["s10"], jnp.concatenate(quad[1], axis=0)], axis=1).astype(BF16)

    def finish(h):
        d = hs[h]
        vv = d["vv"]
        o_top = _dot(d["s00"], vv[:half])
        o_bot = _dot(d["s1"], vv)
        diag = jnp.sum(d["qf"] * d["kk"], axis=-1, keepdims=True)
        o = d["o"] + jnp.concatenate([o_top, o_bot], axis=0) + diag * p_i[:, d["cols"]]
        d["rec"] = o * _rms_scale(o)

    prep(0)
    prep(1)
    levels(0)
    prep(2)
    levels(1)
    prep(3)
    levels(2)
    finish(0)
    levels(3)
    finish(1)
    finish(2)
    finish(3)
    rec_cols = [hs[h]["rec"] for h in range(HG_HEADS)]
    rec_n = (jnp.concatenate(rec_cols, axis=1) * ghg_ref[...] * _silu(p_g)).astype(BF16)

    att_n = (att * _rms_scale(att) * gatt_ref[...]).astype(BF16)
    o_ref[...] = x + _dot(jnp.concatenate([att_n, rec_n], axis=1), wout_ref[...])


def _cast_chunk_rows(n_rows, n_steps):
    tile = 2 * SUBLANES
    rows = tile
    while n_rows % rows or n_rows // rows > n_steps:
        rows += tile
    return rows


def _mixer(x, gmix, w_in, positions, w_out, sinks, gatt, lbl, ghg, side_weights):
    b, s, _ = x.shape
    ts = TILES_PER_STEP * MIX_TILE
    tri, lmask = _hgrn_constants()
    invf, ecos, esin = _rotary_constants()
    bias = _attention_bias()
    pos = positions.reshape(b * (s // ts), 1, ts)
    cur = lambda bi, i: (bi, i, 0)
    fixed = lambda bi, i: (0, 0)
    hbm = pl.BlockSpec(memory_space=pl.ANY)
    n_cast_steps = b * (s // ts) // CAST_STRIDE
    side_scratch = []
    for w in side_weights:
        stage = (2, _cast_chunk_rows(w.shape[0], n_cast_steps), w.shape[1])
        side_scratch += [pltpu.VMEM(stage, F32), pltpu.VMEM(stage, BF16),
                         pltpu.SemaphoreType.DMA((2,)), pltpu.SemaphoreType.DMA((2,))]
    outs = pl.pallas_call(
        _mixer_kernel,
        grid=(b, s // ts),
        in_specs=[
            pl.BlockSpec(memory_space=pltpu.SMEM),
            pl.BlockSpec((None, ts, D_MODEL), cur),
            pl.BlockSpec((1, D_MODEL), fixed),
            hbm,
            pl.BlockSpec((None, 1, ts), lambda bi, i: (bi * (s // ts) + i, 0, 0)),
            pl.BlockSpec(invf.shape, fixed),
            pl.BlockSpec(ecos.shape, fixed),
            pl.BlockSpec(esin.shape, fixed),
            hbm,
            pl.BlockSpec((1, ATT_WIDTH), fixed),
            pl.BlockSpec(lbl.shape, fixed),
            pl.BlockSpec((1, HG_WIDTH), fixed),
            pl.BlockSpec(tri.shape, fixed),
            pl.BlockSpec(lmask.shape, fixed),
            pl.BlockSpec(bias.shape, lambda bi, i: (0, 0, 0)),
        ] + [hbm] * len(side_weights),
        out_specs=[pl.BlockSpec((None, ts, D_MODEL), cur)] + [hbm] * len(side_weights),
        out_shape=[jax.ShapeDtypeStruct((b, s, D_MODEL), F32)]
        + [jax.ShapeDtypeStruct(w.shape, BF16) for w in side_weights],
        scratch_shapes=[
            pltpu.VMEM((HG_HEADS, HG_DIM, HG_DIM), F32),
            pltpu.VMEM((ATT_BLOCK, KV_PAD_WIDTH), BF16),
            pltpu.VMEM(w_in.shape, BF16),
            pltpu.VMEM(w_out.shape, BF16),
            pltpu.VMEM((2, OWN_STAGE_ROWS, w_in.shape[1]), F32),
            pltpu.VMEM((2, OWN_STAGE_ROWS, w_out.shape[1]), F32),
            pltpu.SemaphoreType.DMA((2,)),
            pltpu.SemaphoreType.DMA((2,)),
        ] + side_scratch,
        compiler_params=pltpu.CompilerParams(
            dimension_semantics=("arbitrary", "arbitrary"), vmem_limit_bytes=VMEM_LIMIT_BYTES),
        name="mixer",
    )(sinks, x, gmix, w_in, pos, invf, ecos, esin, w_out, gatt, lbl, ghg, tri, lmask, bias, *side_weights)
    return outs[0], outs[1:]


XATTN_SUB_TILES = 2


def _xattn_kernel(x_ref, gain_ref, wq_ref, mem_ref, gmem_ref, wkv_ref, wo_ref, o_ref, kv_ref):
    @pl.when(pl.program_id(1) == 0)
    def _():
        m = mem_ref[...]
        mn = (m * _rms_scale(m) * gmem_ref[...]).astype(BF16)
        kv_ref[...] = _dot(mn, wkv_ref[...]).astype(BF16)

    n_sub = XATTN_SUB_TILES
    sub = x_ref.shape[0] // n_sub
    q = [None] * n_sub
    probs = [[None] * X_HEADS for _ in range(n_sub)]
    outs = [[None] * X_HEADS for _ in range(n_sub)]

    def qproj(i):
        x = x_ref[i * sub:(i + 1) * sub, :]
        hq = (x * _rms_scale(x) * gain_ref[...]).astype(BF16)
        q[i] = (_dot(hq, wq_ref[...]) * (X_HEAD_DIM ** -0.5 * LOG2_E)).astype(BF16)

    def scores(i, h):
        cols = slice(h * X_HEAD_DIM, (h + 1) * X_HEAD_DIM)
        s = _dot_nt(q[i][:, cols], kv_ref[:, cols])
        probs[i][h] = jnp.exp2(s - jnp.max(s, axis=-1, keepdims=True))

    def values(i, h):
        p = probs[i][h]
        o = _dot(p.astype(BF16), kv_ref[:, D_MODEL + h * X_HEAD_DIM:D_MODEL + (h + 1) * X_HEAD_DIM])
        outs[i][h] = (o / jnp.sum(p, axis=-1, keepdims=True)).astype(BF16)

    def wo(i):
        rows = slice(i * sub, (i + 1) * sub)
        o_ref[rows, :] = x_ref[rows, :] + _dot(jnp.concatenate(outs[i], axis=1), wo_ref[...])

    for i in range(n_sub):
        qproj(i)
    for i in range(n_sub):
        scores(i, 0)
    for h in range(1, X_HEADS):
        for i in range(n_sub):
            scores(i, h)
            values(i, h - 1)
    for i in range(n_sub):
        values(i, X_HEADS - 1)
    for i in range(n_sub):
        wo(i)


def _xattn(x, gain, w_xq, mem, gmem, w_xkv, w_xo, tm):
    b, s, _ = x.shape
    m = mem.shape[1]
    cur = lambda bi, i: (bi, i, 0)
    per_batch = lambda bi, i: (bi, 0, 0)
    fixed = lambda bi, i: (0, 0)
    once = dict(pipeline_mode=pl.Buffered(1))
    return pl.pallas_call(
        _xattn_kernel,
        grid=(b, s // tm),
        in_specs=[
            pl.BlockSpec((None, tm, D_MODEL), cur),
            pl.BlockSpec((1, D_MODEL), fixed),
            pl.BlockSpec((D_MODEL, D_MODEL), fixed, **once),
            pl.BlockSpec((None, m, D_MODEL), per_batch),
            pl.BlockSpec((1, D_MODEL), fixed),
            pl.BlockSpec((D_MODEL, 2 * D_MODEL), fixed, **once),
            pl.BlockSpec((D_MODEL, D_MODEL), fixed, **once),
        ],
        out_specs=pl.BlockSpec((None, tm, D_MODEL), cur),
        out_shape=jax.ShapeDtypeStruct((b, s, D_MODEL), F32),
        scratch_shapes=[pltpu.VMEM((m, 2 * D_MODEL), BF16)],
        compiler_params=pltpu.CompilerParams(
            dimension_semantics=("arbitrary", "arbitrary"), vmem_limit_bytes=VMEM_LIMIT_BYTES),
        name="xattn",
    )(x, gain, w_xq, mem, gmem, w_xkv, w_xo)


FFN_CHUNKS = ((0, 1024), (1024, 2048), (2048, FFN_HIDDEN))
FFN_SUB_TILES = 4

def _ffn_kernel(x_ref, gain_ref, wgu_ref, wd_ref, gfin_ref, o_ref):
    n_sub = FFN_SUB_TILES
    sub = x_ref.shape[0] // n_sub
    hf, acc = [], []
    for i in range(n_sub):
        x = x_ref[i * sub:(i + 1) * sub, :]
        hf.append((x * _rms_scale(x) * gain_ref[...]).astype(BF16))
        acc.append(x)
    for lo, hi in FFN_CHUNKS:
        for i in range(n_sub):
            gate = _dot(hf[i], wgu_ref[:, lo:hi])
            upv = _dot(hf[i], wgu_ref[:, FFN_HIDDEN + lo:FFN_HIDDEN + hi])
            act = (_silu(gate) * upv).astype(BF16)
            acc[i] = acc[i] + _dot(act, wd_ref[lo:hi, :])
    for i in range(n_sub):
        o_ref[i * sub:(i + 1) * sub, :] = acc[i] * _rms_scale(acc[i]) * gfin_ref[...]


def _ffn(x2d, gain, w_gu, w_d, gfin, tm):
    n = x2d.shape[0]
    row = lambda i: (i, 0)
    fixed = lambda i: (0, 0)
    return pl.pallas_call(
        _ffn_kernel,
        grid=(n // tm,),
        in_specs=[
            pl.BlockSpec((tm, D_MODEL), row),
            pl.BlockSpec((1, D_MODEL), fixed),
            pl.BlockSpec((D_MODEL, 2 * FFN_HIDDEN), fixed, pipeline_mode=pl.Buffered(1)),
            pl.BlockSpec((FFN_HIDDEN, D_MODEL), fixed, pipeline_mode=pl.Buffered(1)),
            pl.BlockSpec((1, D_MODEL), fixed),
        ],
        out_specs=pl.BlockSpec((tm, D_MODEL), row),
        out_shape=jax.ShapeDtypeStruct((n, D_MODEL), F32),
        compiler_params=pltpu.CompilerParams(
            dimension_semantics=("arbitrary",), vmem_limit_bytes=VMEM_LIMIT_BYTES),
        name="ffn",
    )(x2d, gain, w_gu, w_d, gfin)


def kernel(x, mem, positions, norm_mix, w_in, att_sinks, att_out_gain, hg_lb_logits, hg_out_gain,
           w_out, norm_xattn, norm_mem, w_xq, w_xkv, w_xo, norm_ffn, w_gate_up, w_down, norm_final):
    b, s, d = x.shape
    n = b * s
    assert w_in.shape[0] == 1 and hg_lb_logits.shape[0] == 2, "single-layer block only"
    assert all(a.dtype == F32 for a in (x, mem, w_in, w_out, w_xq, w_xkv, w_xo, w_gate_up, w_down))
    row = lambda v: v.reshape(1, -1)
    side = [w[0] for w in (w_xkv, w_xq, w_xo, w_gate_up, w_down)]
    x, (w_xkv_b, w_xq_b, w_xo_b, w_gu_b, w_d_b) = _mixer(
        x, row(norm_mix[0]), w_in[0], positions,
        w_out[0], att_sinks[0], row(att_out_gain[0]),
        hg_lb_logits, row(hg_out_gain[0]), side)
    x = _xattn(x, row(norm_xattn[0]), w_xq_b, mem, row(norm_mem[0]), w_xkv_b, w_xo_b, tm=DENSE_TILE)
    y = _ffn(x.reshape(n, d), row(norm_ffn[0]), w_gu_b, w_d_b, row(norm_final), tm=DENSE_TILE)
    return y.reshape(b, s, d)
```

```python
import numpy as np
import jax
import jax.numpy as jnp
from jax import lax
from jax.experimental import pallas as pl
from jax.experimental.pallas import tpu as pltpu

F32 = jnp.float32
BF16 = jnp.bfloat16

D_MODEL = 1024
ATT_HEADS = 8
ATT_KV_HEADS = 2
ATT_GROUP = ATT_HEADS // ATT_KV_HEADS
ATT_HEAD_DIM = 64
ATT_WIDTH = ATT_HEADS * ATT_HEAD_DIM
ATT_KV_WIDTH = ATT_KV_HEADS * ATT_HEAD_DIM
ATT_BLOCK = 128
ROPE_THETA = 500000.0
ROPE_DIM = ATT_HEAD_DIM // 4
ROPE_HALF = ROPE_DIM // 2
HG_HEADS = 4
HG_DIM = 128
HG_WIDTH = HG_HEADS * HG_DIM
X_HEADS = 4
X_HEAD_DIM = D_MODEL // X_HEADS
FFN_HIDDEN = 2816
RMS_EPS = 1e-6
LANES = 128
SUBLANES = 8
MASK_VALUE = -1e30
LOG2_E = 1.4426950408889634

MIX_TILE = 256
HG_LEVELS = (128, 64, 32, 16, 8, 4, 2, 1)
KV_PAD_WIDTH = 2 * ATT_KV_HEADS * 2 * LANES
TILES_PER_STEP = 4
CAST_STRIDE = 1
DENSE_TILE = 1024
OWN_STAGE_ROWS = 128

VMEM_LIMIT_BYTES = 56 * 1024 * 1024


def _rms_scale(x):
    return lax.rsqrt(jnp.mean(x * x, axis=-1, keepdims=True) + RMS_EPS)


def _silu(x):
    h = 0.5 * x
    return h + h * jnp.tanh(h)


def _dot(a, b):
    return jnp.dot(a, b, preferred_element_type=F32)


def _dot_nt(a, b):
    return lax.dot_general(a, b, (((1,), (1,)), ((), ())), preferred_element_type=F32)


def _dot_tn(a, b):
    return lax.dot_general(a, b, (((0,), (0,)), ((), ())), preferred_element_type=F32)


def _stage_weight(src_hbm, dst_ref, stage_ref, sem_ref):
    rows = stage_ref.shape[1]
    n_chunks = dst_ref.shape[0] // rows
    assert n_chunks * rows == dst_ref.shape[0] and src_hbm.shape == dst_ref.shape

    def copy(c):
        return pltpu.make_async_copy(src_hbm.at[pl.ds(c * rows, rows), :], stage_ref.at[c % 2], sem_ref.at[c % 2])

    copy(0).start()
    for c in range(n_chunks):
        if c + 1 < n_chunks:
            copy(c + 1).start()
        copy(c).wait()
        dst_ref[c * rows:(c + 1) * rows, :] = stage_ref[c % 2].astype(BF16)


def _cast_step(t, issue, src_hbm, dst_hbm, in_stage, out_stage, in_sem, out_sem):
    rows = in_stage.shape[1]
    n = src_hbm.shape[0] // rows
    assert n * rows == src_hbm.shape[0] and src_hbm.shape == dst_hbm.shape
    slot = lax.rem(t, 2)

    def read(c, s):
        return pltpu.make_async_copy(src_hbm.at[pl.ds(c * rows, rows), :], in_stage.at[s], in_sem.at[s])

    def write(c, s):
        return pltpu.make_async_copy(out_stage.at[s], dst_hbm.at[pl.ds(c * rows, rows), :], out_sem.at[s])

    if issue:
        @pl.when(t == 0)
        def _():
            read(0, 0).start()

        @pl.when(t + 1 < n)
        def _():
            read(t + 1, 1 - slot).start()

        return

    @pl.when(t < n)
    def _():
        read(t, slot).wait()

        @pl.when(t >= 2)
        def _():
            write(t - 2, slot).wait()

        out_stage[slot] = in_stage[slot].astype(BF16)
        write(t, slot).start()

    @pl.when(t == n - 1)
    def _():
        write(t - 1, 1 - slot).wait()
        write(t, slot).wait()


def _hgrn_constants():
    r = np.arange(MIX_TILE)[:, None]
    j = np.arange(MIX_TILE)[None, :]
    tri = (j <= r).astype(np.float32)
    rr = np.arange(LANES)[:, None]
    jj = np.arange(LANES)[None, :]
    masks = []
    for m in HG_LEVELS:
        if 2 * m <= LANES:
            masks.append(((rr // (2 * m)) == (jj // (2 * m))) & ((rr % (2 * m)) >= m) & ((jj % (2 * m)) < m))
    lmask = np.concatenate(masks, axis=0).astype(np.float32)
    return jnp.asarray(tri, BF16), jnp.asarray(lmask, F32)


def _attention_bias():
    i = (np.arange(2 * ATT_BLOCK) % ATT_BLOCK)[:, None]
    j = np.arange(2 * ATT_BLOCK)[None, :]
    band = (j > i) & (j <= i + ATT_BLOCK)
    variants = (band, band & (j >= ATT_BLOCK))
    return jnp.asarray(np.stack([np.where(v | (j == 0), 0.0, MASK_VALUE) for v in variants]), F32)


def _rotary_constants():
    inv_freq = jnp.power(jnp.float32(ROPE_THETA),
                         -jnp.arange(ROPE_HALF, dtype=F32) * (2.0 / ROPE_DIM))
    invf = jnp.broadcast_to(inv_freq[:, None], (ROPE_HALF, MIX_TILE))
    lane = np.arange(LANES)
    in_head = lane % ATT_HEAD_DIM
    freq_row = (np.arange(ROPE_HALF)[:, None] == (in_head % ROPE_HALF)[None, :])
    rot = freq_row & (in_head < ROPE_DIM)[None, :]
    ecos = np.zeros((4 * ROPE_HALF, LANES), np.float32)
    ecos[:ROPE_HALF] = rot
    ecos[ROPE_HALF:2 * ROPE_HALF] = rot
    ecos[2 * ROPE_HALF] = in_head >= ROPE_DIM
    sign = np.where(in_head < ROPE_HALF, -1.0, 1.0)[None, :]
    esin = np.concatenate([rot * sign, rot * sign], axis=0).astype(np.float32)
    return invf, jnp.asarray(ecos, BF16), jnp.asarray(esin, BF16)


def _level_decay(m, b, f, row_ids):
    ts = b.shape[0]
    if m >= SUBLANES // 2:
        b3 = b.reshape(ts // (2 * m), 2 * m, HG_DIM)
        ref = b3[:, m - 1:m, :]
        if m >= SUBLANES:
            e3 = jnp.concatenate([ref - b3[:, :m, :], b3[:, m:, :] - ref], axis=1)
        else:
            e3 = -jnp.abs(b3 - ref)
        return jnp.exp2(e3).reshape(ts, HG_DIM)
    if m == 2:
        pos = row_ids & 3
        f_next = pltpu.roll(f, ts - 1, axis=0)
        f_prev = pltpu.roll(f, 1, axis=0)
        return jnp.where(pos == 0, f_next, jnp.where(pos == 1, 1.0, jnp.where(pos == 2, f, f * f_prev)))
    assert m == 1
    return jnp.where((row_ids & 1) != 0, f, 1.0)


def _select_rows(m, upper_src, lower_src, row_ids):
    c = upper_src.shape[0]
    if m >= SUBLANES:
        pieces = []
        for b0 in range(0, c, 2 * m):
            pieces.append(lower_src[b0:b0 + m])
            pieces.append(upper_src[b0 + m:b0 + 2 * m])
        return jnp.concatenate(pieces, axis=0)
    return jnp.where((row_ids & m) != 0, upper_src, lower_src)


def _rotary_tables(pos_ref, invf_ref, ecos_ref, esin_ref):
    ang = invf_ref[...] * pos_ref[...].astype(F32)

    def spread(t, e_ref, extra):
        t_hi = t.astype(BF16).astype(F32)
        rows = jnp.concatenate([t_hi, t - t_hi] + extra, axis=0).astype(BF16)
        return _dot_tn(rows, e_ref[...])

    ones = jnp.ones_like(ang)
    cos_t = spread(jnp.cos(ang), ecos_ref, [ones, jnp.zeros_like(ang)])
    sin_t = spread(jnp.sin(ang), esin_ref, [])
    return cos_t, sin_t


def _mixer_kernel(sinks_ref, x_ref, gmix_ref, win_hbm, pos_ref, invf_ref, ecos_ref, esin_ref, wout_hbm,
                  gatt_ref, lbl_ref, ghg_ref, tri_ref, lmask_ref, bias_ref, *rest):
    n_own = 8
    n_side = (len(rest) - 1 - n_own) // 6
    side_src, o_ref, side_dst = rest[:n_side], rest[n_side], rest[n_side + 1:2 * n_side + 1]
    own = rest[2 * n_side + 1:2 * n_side + 1 + n_own]
    state_ref, kvprev_ref, win_ref, wout_ref, win_stage, wout_stage, win_sem, wout_sem = own
    side_scratch = rest[2 * n_side + 1 + n_own:]
    step = pl.program_id(1)
    ts = MIX_TILE
    blk = ATT_BLOCK

    t = pl.program_id(0) * pl.num_programs(1) + step

    @pl.when(t == 0)
    def _():
        _stage_weight(win_hbm, win_ref, win_stage, win_sem)
        _stage_weight(wout_hbm, wout_ref, wout_stage, wout_sem)

    @pl.when(lax.rem(t, CAST_STRIDE) == 0)
    def _():
        for issue in (True, False):
            for k in range(n_side):
                _cast_step(lax.div(t, CAST_STRIDE), issue, side_src[k], side_dst[k],
                           *side_scratch[4 * k:4 * k + 4])

    @pl.when(step == 0)
    def _():
        state_ref[...] = jnp.zeros_like(state_ref)
        kvprev_ref[...] = jnp.zeros_like(kvprev_ref)

    tiles = []
    for k in range(TILES_PER_STEP):
        rows = pl.ds(k * ts, ts)
        no_prev = jnp.where(step > 0, 0, 1) if k == 0 else 0
        tiles.append(_mixer_tile(no_prev, sinks_ref, x_ref.at[rows, :], gmix_ref, win_ref, pos_ref.at[:, rows],
                                 invf_ref, ecos_ref, esin_ref, wout_ref, gatt_ref, lbl_ref, ghg_ref, tri_ref,
                                 lmask_ref, bias_ref, o_ref.at[rows, :], state_ref, kvprev_ref))
    next(tiles[0])
    for k in range(TILES_PER_STEP):
        next(tiles[k])
        if k + 1 < TILES_PER_STEP:
            next(tiles[k + 1])
        next(tiles[k], None)


def _mixer_tile(no_prev, sinks_ref, x_ref, gmix_ref, win_ref, pos_ref, invf_ref, ecos_ref, esin_ref, wout_ref,
                gatt_ref, lbl_ref, ghg_ref, tri_ref, lmask_ref, bias_ref, o_ref, state_ref, kvprev_ref):
    ts = MIX_TILE
    blk = ATT_BLOCK

    x = x_ref[...]
    hn = (x * _rms_scale(x) * gmix_ref[...]).astype(BF16)

    def proj(c0, c1):
        return _dot(hn, win_ref[:, c0:c1])

    cos, sin = _rotary_tables(pos_ref, invf_ref, ecos_ref, esin_ref)
    lane = lax.broadcasted_iota(jnp.int32, cos.shape, 1)
    first_half = (lane & (ATT_HEAD_DIM - 1)) < ROPE_HALF
    lo_head = lane < ATT_HEAD_DIM

    def rotate(col):
        partner = jnp.where(first_half,
                            pltpu.roll(col, LANES - ROPE_HALF, axis=1),
                            pltpu.roll(col, ROPE_HALF, axis=1))
        return col * cos + partner * sin

    def padded(col):
        swapped = pltpu.roll(col, ATT_HEAD_DIM, axis=1)
        zero = jnp.zeros_like(col)
        return [v.astype(BF16) for v in (jnp.where(lo_head, col, zero), jnp.where(lo_head, zero, swapped),
                                         jnp.where(lo_head, swapped, zero), jnp.where(lo_head, zero, col))]

    pq = proj(0, ATT_WIDTH)
    q_cols = [(rotate(pq[:, j * LANES:(j + 1) * LANES]) * (ATT_HEAD_DIM ** -0.5 * LOG2_E)).astype(BF16)
              for j in range(ATT_WIDTH // LANES)]
    pkv = proj(ATT_WIDTH, ATT_WIDTH + 2 * ATT_KV_WIDTH)
    kv_new = padded(rotate(pkv[:, :LANES])) + padded(pkv[:, LANES:])
    kv_all = [jnp.concatenate([kvprev_ref[:, i * LANES:(i + 1) * LANES], v], axis=0)
              for i, v in enumerate(kv_new)]
    for i, v in enumerate(kv_new):
        kvprev_ref[:, i * LANES:(i + 1) * LANES] = v[ts - blk:]
    r0c = ATT_WIDTH + 2 * ATT_KV_WIDTH
    p_q, p_f, p_i, p_g = (proj(r0c + k * HG_WIDTH, r0c + (k + 1) * HG_WIDTH) for k in range(4))

    yield

    upper_half_rows = lax.broadcasted_iota(jnp.int32, (2 * blk, 1), 0) >= blk
    lane_ids = lax.broadcasted_iota(jnp.int32, (2 * blk, LANES), 1)
    lane_lo = lane_ids < ATT_HEAD_DIM
    ones_lo = jnp.where(lane_lo, 1.0, 0.0).astype(BF16)
    ones_hi = jnp.where(lane_lo, 0.0, 1.0).astype(BF16)
    bf16_rows = 2 * SUBLANES
    drop_row0 = jnp.where(lax.broadcasted_iota(jnp.int32, (bf16_rows, LANES), 0) > 0, 1.0, 0.0).astype(BF16)

    def sink_slot(op):
        return jnp.concatenate([op[:bf16_rows] * drop_row0, op[bf16_rows:]], axis=0)

    att_units = {}

    def att_scores(jb, gk):
        keys = slice(jb * blk, (jb + 2) * blk)
        bias = bias_ref[no_prev] if jb == 0 else bias_ref[0]
        q2 = jnp.concatenate([q_cols[2 * gk][jb * blk:(jb + 1) * blk],
                              q_cols[2 * gk + 1][jb * blk:(jb + 1) * blk]], axis=0)
        k_lo = sink_slot(kv_all[2 * gk][keys])
        k_hi = sink_slot(kv_all[2 * gk + 1][keys])
        h0 = ATT_GROUP * gk
        probs = []
        for which, k_op in ((0, k_lo), (1, k_hi)):
            sink = jnp.where(upper_half_rows, sinks_ref[h0 + 2 + which], sinks_ref[h0 + which]) * LOG2_E
            s = _dot_nt(q2, k_op) + bias
            s_first = s[:, :LANES] + jnp.where(lane_ids == 0, sink, 0.0)
            m = jnp.max(jnp.maximum(s_first, s[:, LANES:]), axis=-1, keepdims=True)
            probs.append(jnp.exp2(jnp.concatenate([s_first, s[:, LANES:]], axis=1) - m).astype(BF16))
        att_units[jb, gk] = jnp.concatenate(probs, axis=1)

    def att_values(jb, gk):
        keys = slice(jb * blk, (jb + 2) * blk)
        r_lo = jnp.concatenate([sink_slot(kv_all[4 + 2 * gk][keys]), ones_lo], axis=1)
        r_hi = jnp.concatenate([sink_slot(kv_all[5 + 2 * gk][keys]), ones_hi], axis=1)
        out = _dot(att_units[jb, gk], jnp.concatenate([r_lo, r_hi], axis=0))
        att_units[jb, gk] = out[:, :LANES] / out[:, LANES:]

    units = [(jb, gk) for jb in range(ts // blk) for gk in range(ATT_KV_HEADS)]
    att_scores(*units[0])
    for prev_u, u in zip(units[:-1], units[1:]):
        att_scores(*u)
        att_values(*prev_u)
    att_values(*units[-1])
    att_rows = []
    for jb in range(ts // blk):
        cols_ = []
        for gk in range(ATT_KV_HEADS):
            cols_ += [att_units[jb, gk][:blk], att_units[jb, gk][blk:]]
        att_rows.append(jnp.concatenate(cols_, axis=1))
    att = jnp.concatenate(att_rows, axis=0)

    lbl = lbl_ref[...]
    lexp = jnp.exp(lbl - jnp.max(lbl, axis=0, keepdims=True))
    lb = lexp[0:1, :] / jnp.sum(lexp, axis=0, keepdims=True)
    tri = tri_ref[...]
    row_ids = lax.broadcasted_iota(jnp.int32, (ts, HG_DIM), 0)
    half = ts // 2
    f_all = 0.5 * (1.0 + lb) + (0.5 * (1.0 - lb)) * jnp.tanh(0.5 * p_f)
    g_all = jnp.log(f_all) * LOG2_E
    g_hi = g_all.astype(BF16)
    g_lo = (g_all - g_hi.astype(F32)).astype(BF16)
    b2 = _dot(tri, jnp.concatenate([g_hi, g_lo], axis=1))
    b_all = b2[:, :HG_WIDTH] + b2[:, HG_WIDTH:]
    n_tiles = half // SUBLANES
    hs = [dict() for _ in range(HG_HEADS)]

    def prep(h):
        cols = slice(h * HG_DIM, (h + 1) * HG_DIM)
        f = f_all[:, cols]
        b = b_all[:, cols]
        kk = 1.0 - f
        qf = _silu(p_q[:, cols])
        vv = p_i[:, cols].astype(BF16)
        w_incl = jnp.exp2(b)
        w_tail = jnp.exp2(b[ts - 1:ts, :] - b)
        st = state_ref[h]
        o = _dot_nt((qf * w_incl).astype(BF16), st.astype(BF16))
        state_ref[h] = st * w_incl[ts - 1:ts, :] + _dot_tn(vv, (kk * w_tail).astype(BF16))
        hs[h].update(f=f, b=b, kk=kk, qf=qf, vv=vv, o=o, cols=cols)

    def levels(h):
        d = hs[h]
        qf, kk, b, f = d["qf"], d["kk"], d["b"], d["f"]
        quad = [[None] * n_tiles, [None] * n_tiles]
        for li, m in enumerate(HG_LEVELS):
            x_f = _select_rows(m, qf, kk, row_ids) * _level_decay(m, b, f, row_ids)
            x_l = x_f.astype(BF16)
            if 2 * m == ts:
                d["s10"] = _dot_nt(x_l[half:], x_l[:half])
                continue
            if m >= SUBLANES:
                tiles = [t for t in range(n_tiles) if (t * SUBLANES // m) % 2 == 1]
            else:
                tiles = list(range(n_tiles))
            for hf in range(2):
                r0 = hf * half
                if len(tiles) == n_tiles:
                    lhs = x_l[r0:r0 + half]
                else:
                    lhs = jnp.concatenate([x_f[r0 + t * SUBLANES:r0 + (t + 1) * SUBLANES] for t in tiles],
                                          axis=0).astype(BF16)
                sc = _dot_nt(lhs, x_l[r0:r0 + half])
                for i, t in enumerate(tiles):
                    m0 = (li - 1) * LANES + t * SUBLANES
                    part = sc[i * SUBLANES:(i + 1) * SUBLANES] * lmask_ref[m0:m0 + SUBLANES, :]
                    quad[hf][t] = part if quad[hf][t] is None else quad[hf][t] + part
        d["s00"] = jnp.concatenate(quad[0], axis=0).astype(BF16)
        d["s1"] = jnp.concatenate([d["s10"], jnp.concatenate(quad[1], axis=0)], axis=1).astype(BF16)

    def finish(h):
        d = hs[h]
        vv = d["vv"]
        o_top = _dot(d["s00"], vv[:half])
        o_bot = _dot(d["s1"], vv)
        diag = jnp.sum(d["qf"] * d["kk"], axis=-1, keepdims=True)
        o = d["o"] + jnp.concatenate([o_top, o_bot], axis=0) + diag * p_i[:, d["cols"]]
        d["rec"] = o * _rms_scale(o)

    prep(0)
    prep(1)
    levels(0)
    prep(2)
    levels(1)
    prep(3)
    levels(2)
    finish(0)
    levels(3)
    finish(1)
    finish(2)
    finish(3)
    rec_cols = [hs[h]["rec"] for h in range(HG_HEADS)]
    rec_n = (jnp.concatenate(rec_cols, axis=1) * ghg_ref[...] * _silu(p_g)).astype(BF16)

    yield

    att_n = (att * _rms_scale(att) * gatt_ref[...]).astype(BF16)
    o_ref[...] = x + _dot(jnp.concatenate([att_n, rec_n], axis=1), wout_ref[...])


def _cast_chunk_rows(n_rows, n_steps):
    tile = 2 * SUBLANES
    rows = tile
    while n_rows % rows or n_rows // rows > n_steps:
        rows += tile
    return rows


def _mixer(x, gmix, w_in, positions, w_out, sinks, gatt, lbl, ghg, side_weights):
    b, s, _ = x.shape
    ts = TILES_PER_STEP * MIX_TILE
    tri, lmask = _hgrn_constants()
    invf, ecos, esin = _rotary_constants()
    bias = _attention_bias()
    pos = positions.reshape(b * (s // ts), 1, ts)
    cur = lambda bi, i: (bi, i, 0)
    fixed = lambda bi, i: (0, 0)
    hbm = pl.BlockSpec(memory_space=pl.ANY)
    n_cast_steps = b * (s // ts) // CAST_STRIDE
    side_scratch = []
    for w in side_weights:
        stage = (2, _cast_chunk_rows(w.shape[0], n_cast_steps), w.shape[1])
        side_scratch += [pltpu.VMEM(stage, F32), pltpu.VMEM(stage, BF16),
                         pltpu.SemaphoreType.DMA((2,)), pltpu.SemaphoreType.DMA((2,))]
    outs = pl.pallas_call(
        _mixer_kernel,
        grid=(b, s // ts),
        in_specs=[
            pl.BlockSpec(memory_space=pltpu.SMEM),
            pl.BlockSpec((None, ts, D_MODEL), cur),
            pl.BlockSpec((1, D_MODEL), fixed),
            hbm,
            pl.BlockSpec((None, 1, ts), lambda bi, i: (bi * (s // ts) + i, 0, 0)),
            pl.BlockSpec(invf.shape, fixed),
            pl.BlockSpec(ecos.shape, fixed),
            pl.BlockSpec(esin.shape, fixed),
            hbm,
            pl.BlockSpec((1, ATT_WIDTH), fixed),
            pl.BlockSpec(lbl.shape, fixed),
            pl.BlockSpec((1, HG_WIDTH), fixed),
            pl.BlockSpec(tri.shape, fixed),
            pl.BlockSpec(lmask.shape, fixed),
            pl.BlockSpec(bias.shape, lambda bi, i: (0, 0, 0)),
        ] + [hbm] * len(side_weights),
        out_specs=[pl.BlockSpec((None, ts, D_MODEL), cur)] + [hbm] * len(side_weights),
        out_shape=[jax.ShapeDtypeStruct((b, s, D_MODEL), F32)]
        + [jax.ShapeDtypeStruct(w.shape, BF16) for w in side_weights],
        scratch_shapes=[
            pltpu.VMEM((HG_HEADS, HG_DIM, HG_DIM), F32),
            pltpu.VMEM((ATT_BLOCK, KV_PAD_WIDTH), BF16),
            pltpu.VMEM(w_in.shape, BF16),
            pltpu.VMEM(w_out.shape, BF16),
            pltpu.VMEM((2, OWN_STAGE_ROWS, w_in.shape[1]), F32),
            pltpu.VMEM((2, OWN_STAGE_ROWS, w_out.shape[1]), F32),
            pltpu.SemaphoreType.DMA((2,)),
            pltpu.SemaphoreType.DMA((2,)),
        ] + side_scratch,
        compiler_params=pltpu.CompilerParams(
            dimension_semantics=("arbitrary", "arbitrary"), vmem_limit_bytes=VMEM_LIMIT_BYTES),
        name="mixer",
    )(sinks, x, gmix, w_in, pos, invf, ecos, esin, w_out, gatt, lbl, ghg, tri, lmask, bias, *side_weights)
    return outs[0], outs[1:]


XATTN_SUB_TILES = 2


def _xattn_kernel(x_ref, gain_ref, wq_ref, mem_ref, gmem_ref, wkv_ref, wo_ref, o_ref, kv_ref):
    @pl.when(pl.program_id(1) == 0)
    def _():
        m = mem_ref[...]
        mn = (m * _rms_scale(m) * gmem_ref[...]).astype(BF16)
        kv_ref[...] = _dot(mn, wkv_ref[...]).astype(BF16)

    n_sub = XATTN_SUB_TILES
    sub = x_ref.shape[0] // n_sub
    q = [None] * n_sub
    probs = [[None] * X_HEADS for _ in range(n_sub)]
    outs = [[None] * X_HEADS for _ in range(n_sub)]

    def qproj(i):
        x = x_ref[i * sub:(i + 1) * sub, :]
        hq = (x * _rms_scale(x) * gain_ref[...]).astype(BF16)
        q[i] = (_dot(hq, wq_ref[...]) * (X_HEAD_DIM ** -0.5 * LOG2_E)).astype(BF16)

    def scores(i, h):
        cols = slice(h * X_HEAD_DIM, (h + 1) * X_HEAD_DIM)
        s = _dot_nt(q[i][:, cols], kv_ref[:, cols])
        probs[i][h] = jnp.exp2(s - jnp.max(s, axis=-1, keepdims=True))

    def values(i, h):
        p = probs[i][h]
        o = _dot(p.astype(BF16), kv_ref[:, D_MODEL + h * X_HEAD_DIM:D_MODEL + (h + 1) * X_HEAD_DIM])
        outs[i][h] = (o / jnp.sum(p, axis=-1, keepdims=True)).astype(BF16)

    def wo(i):
        rows = slice(i * sub, (i + 1) * sub)
        o_ref[rows, :] = x_ref[rows, :] + _dot(jnp.concatenate(outs[i], axis=1), wo_ref[...])

    for i in range(n_sub):
        qproj(i)
    for i in range(n_sub):
        scores(i, 0)
    for h in range(1, X_HEADS):
        for i in range(n_sub):
            scores(i, h)
            values(i, h - 1)
    for i in range(n_sub):
        values(i, X_HEADS - 1)
    for i in range(n_sub):
        wo(i)


def _xattn(x, gain, w_xq, mem, gmem, w_xkv, w_xo, tm):
    b, s, _ = x.shape
    m = mem.shape[1]
    cur = lambda bi, i: (bi, i, 0)
    per_batch = lambda bi, i: (bi, 0, 0)
    fixed = lambda bi, i: (0, 0)
    once = dict(pipeline_mode=pl.Buffered(1))
    return pl.pallas_call(
        _xattn_kernel,
        grid=(b, s // tm),
        in_specs=[
            pl.BlockSpec((None, tm, D_MODEL), cur),
            pl.BlockSpec((1, D_MODEL), fixed),
            pl.BlockSpec((D_MODEL, D_MODEL), fixed, **once),
            pl.BlockSpec((None, m, D_MODEL), per_batch),
            pl.BlockSpec((1, D_MODEL), fixed),
            pl.BlockSpec((D_MODEL, 2 * D_MODEL), fixed, **once),
            pl.BlockSpec((D_MODEL, D_MODEL), fixed, **once),
        ],
        out_specs=pl.BlockSpec((None, tm, D_MODEL), cur),
        out_shape=jax.ShapeDtypeStruct((b, s, D_MODEL), F32),
        scratch_shapes=[pltpu.VMEM((m, 2 * D_MODEL), BF16)],
        compiler_params=pltpu.CompilerParams(
            dimension_semantics=("arbitrary", "arbitrary"), vmem_limit_bytes=VMEM_LIMIT_BYTES),
        name="xattn",
    )(x, gain, w_xq, mem, gmem, w_xkv, w_xo)


FFN_CHUNKS = ((0, 1024), (1024, 2048), (2048, FFN_HIDDEN))
FFN_SUB_TILES = 4

def _ffn_kernel(x_ref, gain_ref, wgu_ref, wd_ref, gfin_ref, o_ref):
    n_sub = FFN_SUB_TILES
    sub = x_ref.shape[0] // n_sub
    hf, acc = [], []
    for i in range(n_sub):
        x = x_ref[i * sub:(i + 1) * sub, :]
        hf.append((x * _rms_scale(x) * gain_ref[...]).astype(BF16))
        acc.append(x)
    for lo, hi in FFN_CHUNKS:
        for i in range(n_sub):
            gate = _dot(hf[i], wgu_ref[:, lo:hi])
            upv = _dot(hf[i], wgu_ref[:, FFN_HIDDEN + lo:FFN_HIDDEN + hi])
            act = (_silu(gate) * upv).astype(BF16)
            acc[i] = acc[i] + _dot(act, wd_ref[lo:hi, :])
    for i in range(n_sub):
        o_ref[i * sub:(i + 1) * sub, :] = acc[i] * _rms_scale(acc[i]) * gfin_ref[...]


def _ffn(x2d, gain, w_gu, w_d, gfin, tm):
    n = x2d.shape[0]
    row = lambda i: (i, 0)
    fixed = lambda i: (0, 0)
    return pl.pallas_call(
        _ffn_kernel,
        grid=(n // tm,),
        in_specs=[
            pl.BlockSpec((tm, D_MODEL), row),
            pl.BlockSpec((1, D_MODEL), fixed),
            pl.BlockSpec((D_MODEL, 2 * FFN_HIDDEN), fixed, pipeline_mode=pl.Buffered(1)),
            pl.BlockSpec((FFN_HIDDEN, D_MODEL), fixed, pipeline_mode=pl.Buffered(1)),
            pl.BlockSpec((1, D_MODEL), fixed),
        ],
        out_specs=pl.BlockSpec((tm, D_MODEL), row),
        out_shape=jax.ShapeDtypeStruct((n, D_MODEL), F32),
        compiler_params=pltpu.CompilerParams(
            dimension_semantics=("arbitrary",), vmem_limit_bytes=VMEM_LIMIT_BYTES),
        name="ffn",
    )(x2d, gain, w_gu, w_d, gfin)


def kernel(x, mem, positions, norm_mix, w_in, att_sinks, att_out_gain, hg_lb_logits, hg_out_gain,
           w_out, norm_xattn, norm_mem, w_xq, w_xkv, w_xo, norm_ffn, w_gate_up, w_down, norm_final):
    b, s, d = x.shape
    n = b * s
    assert w_in.shape[0] == 1 and hg_lb_logits.shape[0] == 2, "single-layer block only"
    assert all(a.dtype == F32 for a in (x, mem, w_in, w_out, w_xq, w_xkv, w_xo, w_gate_up, w_down))
    row = lambda v: v.reshape(1, -1)
    side = [w[0] for w in (w_xkv, w_xq, w_xo, w_gate_up, w_down)]
    x, (w_xkv_b, w_xq_b, w_xo_b, w_gu_b, w_d_b) = _mixer(
        x, row(norm_mix[0]), w_in[0], positions,
        w_out[0], att_sinks[0], row(att_out_gain[0]),
        hg_lb_logits, row(hg_out_gain[0]), side)
    x = _xattn(x, row(norm_xattn[0]), w_xq_b, mem, row(norm_mem[0]), w_xkv_b, w_xo_b, tm=DENSE_TILE)
    y = _ffn(x.reshape(n, d), row(norm_ffn[0]), w_gu_b, w_d_b, row(norm_final), tm=DENSE_TILE)
    return y.reshape(b, s, d)
```

```python
import numpy as np
import jax
import jax.numpy as jnp
from jax import lax
from jax.experimental import pallas as pl
from jax.experimental.pallas import tpu as pltpu

F32 = jnp.float32
BF16 = jnp.bfloat16

D_MODEL = 1024
ATT_HEADS = 8
ATT_KV_HEADS = 2
ATT_GROUP = ATT_HEADS // ATT_KV_HEADS
ATT_HEAD_DIM = 64
ATT_WIDTH = ATT_HEADS * ATT_HEAD_DIM
ATT_KV_WIDTH = ATT_KV_HEADS * ATT_HEAD_DIM
ATT_BLOCK = 128
ROPE_THETA = 500000.0
ROPE_DIM = ATT_HEAD_DIM // 4
ROPE_HALF = ROPE_DIM // 2
HG_HEADS = 4
HG_DIM = 128
HG_WIDTH = HG_HEADS * HG_DIM
X_HEADS = 4
X_HEAD_DIM = D_MODEL // X_HEADS
FFN_HIDDEN = 2816
RMS_EPS = 1e-6
LANES = 128
SUBLANES = 8
MASK_VALUE = -1e30
LOG2_E = 1.4426950408889634

MIX_TILE = 256
HG_LEVELS = (128, 64, 32, 16, 8, 4, 2, 1)
KV_PAD_WIDTH = 2 * ATT_KV_HEADS * 2 * LANES
TILES_PER_STEP = 4
CAST_STRIDE = 1
DENSE_TILE = 1024
OWN_STAGE_ROWS = 128

VMEM_LIMIT_BYTES = 56 * 1024 * 1024


def _rms_scale(x):
    return lax.rsqrt(jnp.mean(x * x, axis=-1, keepdims=True) + RMS_EPS)


def _silu(x):
    h = 0.5 * x
    return h + h * jnp.tanh(h)


def _dot(a, b):
    return jnp.dot(a, b, preferred_element_type=F32)


def _dot_nt(a, b):
    return lax.dot_general(a, b, (((1,), (1,)), ((), ())), preferred_element_type=F32)


def _dot_tn(a, b):
    return lax.dot_general(a, b, (((0,), (0,)), ((), ())), preferred_element_type=F32)


def _stage_weight(src_hbm, dst_ref, stage_ref, sem_ref):
    rows = stage_ref.shape[1]
    n_chunks = dst_ref.shape[0] // rows
    assert n_chunks * rows == dst_ref.shape[0] and src_hbm.shape == dst_ref.shape

    def copy(c):
        return pltpu.make_async_copy(src_hbm.at[pl.ds(c * rows, rows), :], stage_ref.at[c % 2], sem_ref.at[c % 2])

    copy(0).start()
    for c in range(n_chunks):
        if c + 1 < n_chunks:
            copy(c + 1).start()
        copy(c).wait()
        dst_ref[c * rows:(c + 1) * rows, :] = stage_ref[c % 2].astype(BF16)


def _cast_step(t, issue, src_hbm, dst_hbm, in_stage, out_stage, in_sem, out_sem):
    rows = in_stage.shape[1]
    n = src_hbm.shape[0] // rows
    assert n * rows == src_hbm.shape[0] and src_hbm.shape == dst_hbm.shape
    slot = lax.rem(t, 2)

    def read(c, s):
        return pltpu.make_async_copy(src_hbm.at[pl.ds(c * rows, rows), :], in_stage.at[s], in_sem.at[s])

    def write(c, s):
        return pltpu.make_async_copy(out_stage.at[s], dst_hbm.at[pl.ds(c * rows, rows), :], out_sem.at[s])

    if issue:
        @pl.when(t == 0)
        def _():
            read(0, 0).start()

        @pl.when(t + 1 < n)
        def _():
            read(t + 1, 1 - slot).start()

        return

    @pl.when(t < n)
    def _():
        read(t, slot).wait()

        @pl.when(t >= 2)
        def _():
            write(t - 2, slot).wait()

        out_stage[slot] = in_stage[slot].astype(BF16)
        write(t, slot).start()

    @pl.when(t == n - 1)
    def _():
        write(t - 1, 1 - slot).wait()
        write(t, slot).wait()


def _hgrn_constants():
    r = np.arange(MIX_TILE)[:, None]
    j = np.arange(MIX_TILE)[None, :]
    tri = (j <= r).astype(np.float32)
    rr = np.arange(LANES)[:, None]
    jj = np.arange(LANES)[None, :]
    masks = []
    for m in HG_LEVELS:
        if 2 * m <= LANES:
            masks.append(((rr // (2 * m)) == (jj // (2 * m))) & ((rr % (2 * m)) >= m) & ((jj % (2 * m)) < m))
    lmask = np.concatenate(masks, axis=0).astype(np.float32)
    return jnp.asarray(tri, BF16), jnp.asarray(lmask, F32)


def _attention_bias():
    i = (np.arange(2 * ATT_BLOCK) % ATT_BLOCK)[:, None]
    j = np.arange(2 * ATT_BLOCK)[None, :]
    band = (j > i) & (j <= i + ATT_BLOCK)
    variants = (band, band & (j >= ATT_BLOCK))
    return jnp.asarray(np.stack([np.where(v | (j == 0), 0.0, MASK_VALUE) for v in variants]), F32)


def _rotary_constants():
    inv_freq = jnp.power(jnp.float32(ROPE_THETA),
                         -jnp.arange(ROPE_HALF, dtype=F32) * (2.0 / ROPE_DIM))
    invf = jnp.broadcast_to(inv_freq[:, None], (ROPE_HALF, MIX_TILE))
    lane = np.arange(LANES)
    in_head = lane % ATT_HEAD_DIM
    freq_row = (np.arange(ROPE_HALF)[:, None] == (in_head % ROPE_HALF)[None, :])
    rot = freq_row & (in_head < ROPE_DIM)[None, :]
    ecos = np.zeros((4 * ROPE_HALF, LANES), np.float32)
    ecos[:ROPE_HALF] = rot
    ecos[ROPE_HALF:2 * ROPE_HALF] = rot
    ecos[2 * ROPE_HALF] = in_head >= ROPE_DIM
    sign = np.where(in_head < ROPE_HALF, -1.0, 1.0)[None, :]
    esin = np.concatenate([rot * sign, rot * sign], axis=0).astype(np.float32)
    return invf, jnp.asarray(ecos, BF16), jnp.asarray(esin, BF16)


def _level_decay(m, b, f, row_ids):
    ts = b.shape[0]
    if m >= SUBLANES // 2:
        b3 = b.reshape(ts // (2 * m), 2 * m, HG_DIM)
        ref = b3[:, m - 1:m, :]
        if m >= SUBLANES:
            e3 = jnp.concatenate([ref - b3[:, :m, :], b3[:, m:, :] - ref], axis=1)
        else:
            e3 = -jnp.abs(b3 - ref)
        return jnp.exp2(e3).reshape(ts, HG_DIM)
    if m == 2:
        pos = row_ids & 3
        f_next = pltpu.roll(f, ts - 1, axis=0)
        f_prev = pltpu.roll(f, 1, axis=0)
        return jnp.where(pos == 0, f_next, jnp.where(pos == 1, 1.0, jnp.where(pos == 2, f, f * f_prev)))
    assert m == 1
    return jnp.where((row_ids & 1) != 0, f, 1.0)


def _select_rows(m, upper_src, lower_src, row_ids):
    c = upper_src.shape[0]
    if m >= SUBLANES:
        pieces = []
        for b0 in range(0, c, 2 * m):
            pieces.append(lower_src[b0:b0 + m])
            pieces.append(upper_src[b0 + m:b0 + 2 * m])
        return jnp.concatenate(pieces, axis=0)
    return jnp.where((row_ids & m) != 0, upper_src, lower_src)


def _rotary_tables(pos_ref, invf_ref, ecos_ref, esin_ref):
    ang = invf_ref[...] * pos_ref[...].astype(F32)

    def spread(t, e_ref, extra):
        t_hi = t.astype(BF16).astype(F32)
        rows = jnp.concatenate([t_hi, t - t_hi] + extra, axis=0).astype(BF16)
        return _dot_tn(rows, e_ref[...])

    ones = jnp.ones_like(ang)
    cos_t = spread(jnp.cos(ang), ecos_ref, [ones, jnp.zeros_like(ang)])
    sin_t = spread(jnp.sin(ang), esin_ref, [])
    return cos_t, sin_t


def _mixer_kernel(sinks_ref, x_ref, gmix_ref, win_hbm, pos_ref, invf_ref, ecos_ref, esin_ref, wout_hbm,
                  gatt_ref, lbl_ref, ghg_ref, tri_ref, lmask_ref, bias_ref, *rest):
    n_own = 8
    n_side = (len(rest) - 1 - n_own) // 6
    side_src, o_ref, side_dst = rest[:n_side], rest[n_side], rest[n_side + 1:2 * n_side + 1]
    own = rest[2 * n_side + 1:2 * n_side + 1 + n_own]
    state_ref, kvprev_ref, win_ref, wout_ref, win_stage, wout_stage, win_sem, wout_sem = own
    side_scratch = rest[2 * n_side + 1 + n_own:]
    step = pl.program_id(1)
    ts = MIX_TILE
    blk = ATT_BLOCK

    t = pl.program_id(0) * pl.num_programs(1) + step

    def side_casts(issue):
        @pl.when(lax.rem(t, CAST_STRIDE) == 0)
        def _():
            for k in range(n_side):
                _cast_step(lax.div(t, CAST_STRIDE), issue, side_src[k], side_dst[k],
                           *side_scratch[4 * k:4 * k + 4])

    side_casts(issue=True)

    @pl.when(t == 0)
    def _():
        _stage_weight(win_hbm, win_ref, win_stage, win_sem)
        _stage_weight(wout_hbm, wout_ref, wout_stage, wout_sem)

    side_casts(issue=False)

    @pl.when(step == 0)
    def _():
        state_ref[...] = jnp.zeros_like(state_ref)
        kvprev_ref[...] = jnp.zeros_like(kvprev_ref)

    tiles = []
    for k in range(TILES_PER_STEP):
        rows = pl.ds(k * ts, ts)
        no_prev = jnp.where(step > 0, 0, 1) if k == 0 else 0
        tiles.append(_mixer_tile(no_prev, sinks_ref, x_ref.at[rows, :], gmix_ref, win_ref, pos_ref.at[:, rows],
                                 invf_ref, ecos_ref, esin_ref, wout_ref, gatt_ref, lbl_ref, ghg_ref, tri_ref,
                                 lmask_ref, bias_ref, o_ref.at[rows, :], state_ref, kvprev_ref))
    next(tiles[0])
    for k in range(TILES_PER_STEP):
        next(tiles[k])
        if k + 1 < TILES_PER_STEP:
            next(tiles[k + 1])
        next(tiles[k], None)


def _mixer_tile(no_prev, sinks_ref, x_ref, gmix_ref, win_ref, pos_ref, invf_ref, ecos_ref, esin_ref, wout_ref,
                gatt_ref, lbl_ref, ghg_ref, tri_ref, lmask_ref, bias_ref, o_ref, state_ref, kvprev_ref):
    ts = MIX_TILE
    blk = ATT_BLOCK

    x = x_ref[...]
    hn = (x * _rms_scale(x) * gmix_ref[...]).astype(BF16)

    def proj(c0, c1):
        return _dot(hn, win_ref[:, c0:c1])

    cos, sin = _rotary_tables(pos_ref, invf_ref, ecos_ref, esin_ref)
    lane = lax.broadcasted_iota(jnp.int32, cos.shape, 1)
    first_half = (lane & (ATT_HEAD_DIM - 1)) < ROPE_HALF
    lo_head = lane < ATT_HEAD_DIM

    def rotate(col):
        partner = jnp.where(first_half,
                            pltpu.roll(col, LANES - ROPE_HALF, axis=1),
                            pltpu.roll(col, ROPE_HALF, axis=1))
        return col * cos + partner * sin

    def padded(col):
        swapped = pltpu.roll(col, ATT_HEAD_DIM, axis=1)
        zero = jnp.zeros_like(col)
        return [v.astype(BF16) for v in (jnp.where(lo_head, col, zero), jnp.where(lo_head, zero, swapped),
                                         jnp.where(lo_head, swapped, zero), jnp.where(lo_head, zero, col))]

    pq = proj(0, ATT_WIDTH)
    q_cols = [(rotate(pq[:, j * LANES:(j + 1) * LANES]) * (ATT_HEAD_DIM ** -0.5 * LOG2_E)).astype(BF16)
              for j in range(ATT_WIDTH // LANES)]
    pkv = proj(ATT_WIDTH, ATT_WIDTH + 2 * ATT_KV_WIDTH)
    kv_new = padded(rotate(pkv[:, :LANES])) + padded(pkv[:, LANES:])
    kv_all = [jnp.concatenate([kvprev_ref[:, i * LANES:(i + 1) * LANES], v], axis=0)
              for i, v in enumerate(kv_new)]
    for i, v in enumerate(kv_new):
        kvprev_ref[:, i * LANES:(i + 1) * LANES] = v[ts - blk:]
    r0c = ATT_WIDTH + 2 * ATT_KV_WIDTH
    p_q, p_f, p_i, p_g = (proj(r0c + k * HG_WIDTH, r0c + (k + 1) * HG_WIDTH) for k in range(4))

    yield

    upper_half_rows = lax.broadcasted_iota(jnp.int32, (2 * blk, 1), 0) >= blk
    lane_ids = lax.broadcasted_iota(jnp.int32, (2 * blk, LANES), 1)
    lane_lo = lane_ids < ATT_HEAD_DIM
    ones_lo = jnp.where(lane_lo, 1.0, 0.0).astype(BF16)
    ones_hi = jnp.where(lane_lo, 0.0, 1.0).astype(BF16)
    bf16_rows = 2 * SUBLANES
    drop_row0 = jnp.where(lax.broadcasted_iota(jnp.int32, (bf16_rows, LANES), 0) > 0, 1.0, 0.0).astype(BF16)

    def sink_slot(op):
        return jnp.concatenate([op[:bf16_rows] * drop_row0, op[bf16_rows:]], axis=0)

    att_units = {}

    def att_scores(jb, gk):
        keys = slice(jb * blk, (jb + 2) * blk)
        bias = bias_ref[no_prev] if jb == 0 else bias_ref[0]
        q2 = jnp.concatenate([q_cols[2 * gk][jb * blk:(jb + 1) * blk],
                              q_cols[2 * gk + 1][jb * blk:(jb + 1) * blk]], axis=0)
        k_lo = sink_slot(kv_all[2 * gk][keys])
        k_hi = sink_slot(kv_all[2 * gk + 1][keys])
        h0 = ATT_GROUP * gk
        probs = []
        for which, k_op in ((0, k_lo), (1, k_hi)):
            sink = jnp.where(upper_half_rows, sinks_ref[h0 + 2 + which], sinks_ref[h0 + which]) * LOG2_E
            s = _dot_nt(q2, k_op) + bias
            s_first = s[:, :LANES] + jnp.where(lane_ids == 0, sink, 0.0)
            m = jnp.max(jnp.maximum(s_first, s[:, LANES:]), axis=-1, keepdims=True)
            probs.append(jnp.exp2(jnp.concatenate([s_first, s[:, LANES:]], axis=1) - m).astype(BF16))
        att_units[jb, gk] = jnp.concatenate(probs, axis=1)

    def att_values(jb, gk):
        keys = slice(jb * blk, (jb + 2) * blk)
        r_lo = jnp.concatenate([sink_slot(kv_all[4 + 2 * gk][keys]), ones_lo], axis=1)
        r_hi = jnp.concatenate([sink_slot(kv_all[5 + 2 * gk][keys]), ones_hi], axis=1)
        out = _dot(att_units[jb, gk], jnp.concatenate([r_lo, r_hi], axis=0))
        att_units[jb, gk] = out[:, :LANES] / out[:, LANES:]

    units = [(jb, gk) for jb in range(ts // blk) for gk in range(ATT_KV_HEADS)]
    att_scores(*units[0])
    for prev_u, u in zip(units[:-1], units[1:]):
        att_scores(*u)
        att_values(*prev_u)
    att_values(*units[-1])
    att_rows = []
    for jb in range(ts // blk):
        cols_ = []
        for gk in range(ATT_KV_HEADS):
            cols_ += [att_units[jb, gk][:blk], att_units[jb, gk][blk:]]
        att_rows.append(jnp.concatenate(cols_, axis=1))
    att = jnp.concatenate(att_rows, axis=0)

    lbl = lbl_ref[...]
    lexp = jnp.exp(lbl - jnp.max(lbl, axis=0, keepdims=True))
    lb = lexp[0:1, :] / jnp.sum(lexp, axis=0, keepdims=True)
    tri = tri_ref[...]
    row_ids = lax.broadcasted_iota(jnp.int32, (ts, HG_DIM), 0)
    half = ts // 2
    f_all = 0.5 * (1.0 + lb) + (0.5 * (1.0 - lb)) * jnp.tanh(0.5 * p_f)
    g_all = jnp.log(f_all) * LOG2_E
    g_hi = g_all.astype(BF16)
    g_lo = (g_all - g_hi.astype(F32)).astype(BF16)
    b2 = _dot(tri, jnp.concatenate([g_hi, g_lo], axis=1))
    b_all = b2[:, :HG_WIDTH] + b2[:, HG_WIDTH:]
    n_tiles = half // SUBLANES
    hs = [dict() for _ in range(HG_HEADS)]

    def prep(h):
        cols = slice(h * HG_DIM, (h + 1) * HG_DIM)
        f = f_all[:, cols]
        b = b_all[:, cols]
        kk = 1.0 - f
        qf = _silu(p_q[:, cols])
        vv = p_i[:, cols].astype(BF16)
        w_incl = jnp.exp2(b)
        w_tail = jnp.exp2(b[ts - 1:ts, :] - b)
        st = state_ref[h]
        o = _dot_nt((qf * w_incl).astype(BF16), st.astype(BF16))
        state_ref[h] = st * w_incl[ts - 1:ts, :] + _dot_tn(vv, (kk * w_tail).astype(BF16))
        hs[h].update(f=f, b=b, kk=kk, qf=qf, vv=vv, o=o, cols=cols)

    def levels(h):
        d = hs[h]
        qf, kk, b, f = d["qf"], d["kk"], d["b"], d["f"]
        quad = [[None] * n_tiles, [None] * n_tiles]
        for li, m in enumerate(HG_LEVELS):
            x_f = _select_rows(m, qf, kk, row_ids) * _level_decay(m, b, f, row_ids)
            x_l = x_f.astype(BF16)
            if 2 * m == ts:
                d["s10"] = _dot_nt(x_l[half:], x_l[:half])
                continue
            if m >= SUBLANES:
                tiles = [t for t in range(n_tiles) if (t * SUBLANES // m) % 2 == 1]
            else:
                tiles = list(range(n_tiles))
            for hf in range(2):
                r0 = hf * half
                if len(tiles) == n_tiles:
                    lhs = x_l[r0:r0 + half]
                else:
                    lhs = jnp.concatenate([x_f[r0 + t * SUBLANES:r0 + (t + 1) * SUBLANES] for t in tiles],
                                          axis=0).astype(BF16)
                sc = _dot_nt(lhs, x_l[r0:r0 + half])
                for i, t in enumerate(tiles):
                    m0 = (li - 1) * LANES + t * SUBLANES
                    part = sc[i * SUBLANES:(i + 1) * SUBLANES] * lmask_ref[m0:m0 + SUBLANES, :]
                    quad[hf][t] = part if quad[hf][t] is None else quad[hf][t] + part
        d["s00"] = jnp.concatenate(quad[0], axis=0).astype(BF16)
        d["s1"] = jnp.concatenate([d["s10"], jnp.concatenate(quad[1], axis=0)], axis=1).astype(BF16)

    def finish(h):
        d = hs[h]
        vv = d["vv"]
        o_top = _dot(d["s00"], vv[:half])
        o_bot = _dot(d["s1"], vv)
        diag = jnp.sum(d["qf"] * d["kk"], axis=-1, keepdims=True)
        o = d["o"] + jnp.concatenate([o_top, o_bot], axis=0) + diag * p_i[:, d["cols"]]
        d["rec"] = o * _rms_scale(o)

    prep(0)
    prep(1)
    levels(0)
    prep(2)
    levels(1)
    prep(3)
    levels(2)
    finish(0)
    levels(3)
    finish(1)
    finish(2)
    finish(3)
    rec_cols = [hs[h]["rec"] for h in range(HG_HEADS)]
    rec_n = (jnp.concatenate(rec_cols, axis=1) * ghg_ref[...] * _silu(p_g)).astype(BF16)

    yield

    att_n = (att * _rms_scale(att) * gatt_ref[...]).astype(BF16)
    o_ref[...] = x + _dot(jnp.concatenate([att_n, rec_n], axis=1), wout_ref[...])


def _cast_chunk_rows(n_rows, n_steps):
    tile = 2 * SUBLANES
    rows = tile
    while n_rows % rows or n_rows // rows > n_steps:
        rows += tile
    return rows


def _mixer(x, gmix, w_in, positions, w_out, sinks, gatt, lbl, ghg, side_weights):
    b, s, _ = x.shape
    ts = TILES_PER_STEP * MIX_TILE
    tri, lmask = _hgrn_constants()
    invf, ecos, esin = _rotary_constants()
    bias = _attention_bias()
    pos = positions.reshape(b * (s // ts), 1, ts)
    cur = lambda bi, i: (bi, i, 0)
    fixed = lambda bi, i: (0, 0)
    hbm = pl.BlockSpec(memory_space=pl.ANY)
    n_cast_steps = b * (s // ts) // CAST_STRIDE
    side_scratch = []
    for w in side_weights:
        stage = (2, _cast_chunk_rows(w.shape[0], n_cast_steps), w.shape[1])
        side_scratch += [pltpu.VMEM(stage, F32), pltpu.VMEM(stage, BF16),
                         pltpu.SemaphoreType.DMA((2,)), pltpu.SemaphoreType.DMA((2,))]
    outs = pl.pallas_call(
        _mixer_kernel,
        grid=(b, s // ts),
        in_specs=[
            pl.BlockSpec(memory_space=pltpu.SMEM),
            pl.BlockSpec((None, ts, D_MODEL), cur),
            pl.BlockSpec((1, D_MODEL), fixed),
            hbm,
            pl.BlockSpec((None, 1, ts), lambda bi, i: (bi * (s // ts) + i, 0, 0)),
            pl.BlockSpec(invf.shape, fixed),
            pl.BlockSpec(ecos.shape, fixed),
            pl.BlockSpec(esin.shape, fixed),
            hbm,
            pl.BlockSpec((1, ATT_WIDTH), fixed),
            pl.BlockSpec(lbl.shape, fixed),
            pl.BlockSpec((1, HG_WIDTH), fixed),
            pl.BlockSpec(tri.shape, fixed),
            pl.BlockSpec(lmask.shape, fixed),
            pl.BlockSpec(bias.shape, lambda bi, i: (0, 0, 0)),
        ] + [hbm] * len(side_weights),
        out_specs=[pl.BlockSpec((None, ts, D_MODEL), cur)] + [hbm] * len(side_weights),
        out_shape=[jax.ShapeDtypeStruct((b, s, D_MODEL), F32)]
        + [jax.ShapeDtypeStruct(w.shape, BF16) for w in side_weights],
        scratch_shapes=[
            pltpu.VMEM((HG_HEADS, HG_DIM, HG_DIM), F32),
            pltpu.VMEM((ATT_BLOCK, KV_PAD_WIDTH), BF16),
            pltpu.VMEM(w_in.shape, BF16),
            pltpu.VMEM(w_out.shape, BF16),
            pltpu.VMEM((2, OWN_STAGE_ROWS, w_in.shape[1]), F32),
            pltpu.VMEM((2, OWN_STAGE_ROWS, w_out.shape[1]), F32),
            pltpu.SemaphoreType.DMA((2,)),
            pltpu.SemaphoreType.DMA((2,)),
        ] + side_scratch,
        compiler_params=pltpu.CompilerParams(
            dimension_semantics=("arbitrary", "arbitrary"), vmem_limit_bytes=VMEM_LIMIT_BYTES),
        name="mixer",
    )(sinks, x, gmix, w_in, pos, invf, ecos, esin, w_out, gatt, lbl, ghg, tri, lmask, bias, *side_weights)
    return outs[0], outs[1:]


XATTN_SUB_TILES = 2


def _xattn_kernel(x_ref, gain_ref, wq_ref, mem_ref, gmem_ref, wkv_ref, wo_ref, o_ref, kv_ref):
    @pl.when(pl.program_id(1) == 0)
    def _():
        m = mem_ref[...]
        mn = (m * _rms_scale(m) * gmem_ref[...]).astype(BF16)
        kv_ref[...] = _dot(mn, wkv_ref[...]).astype(BF16)

    n_sub = XATTN_SUB_TILES
    sub = x_ref.shape[0] // n_sub
    q = [None] * n_sub
    probs = [[None] * X_HEADS for _ in range(n_sub)]
    outs = [[None] * X_HEADS for _ in range(n_sub)]

    def qproj(i):
        x = x_ref[i * sub:(i + 1) * sub, :]
        hq = (x * _rms_scale(x) * gain_ref[...]).astype(BF16)
        q[i] = (_dot(hq, wq_ref[...]) * (X_HEAD_DIM ** -0.5 * LOG2_E)).astype(BF16)

    def scores(i, h):
        cols = slice(h * X_HEAD_DIM, (h + 1) * X_HEAD_DIM)
        s = _dot_nt(q[i][:, cols], kv_ref[:, cols])
        probs[i][h] = jnp.exp2(s - jnp.max(s, axis=-1, keepdims=True))

    def values(i, h):
        p = probs[i][h]
        o = _dot(p.astype(BF16), kv_ref[:, D_MODEL + h * X_HEAD_DIM:D_MODEL + (h + 1) * X_HEAD_DIM])
        outs[i][h] = (o / jnp.sum(p, axis=-1, keepdims=True)).astype(BF16)

    def wo(i):
        rows = slice(i * sub, (i + 1) * sub)
        o_ref[rows, :] = x_ref[rows, :] + _dot(jnp.concatenate(outs[i], axis=1), wo_ref[...])

    for i in range(n_sub):
        qproj(i)
    for i in range(n_sub):
        scores(i, 0)
    for h in range(1, X_HEADS):
        for i in range(n_sub):
            scores(i, h)
            values(i, h - 1)
    for i in range(n_sub):
        values(i, X_HEADS - 1)
    for i in range(n_sub):
        wo(i)


def _xattn(x, gain, w_xq, mem, gmem, w_xkv, w_xo, tm):
    b, s, _ = x.shape
    m = mem.shape[1]
    cur = lambda bi, i: (bi, i, 0)
    per_batch = lambda bi, i: (bi, 0, 0)
    fixed = lambda bi, i: (0, 0)
    once = dict(pipeline_mode=pl.Buffered(1))
    return pl.pallas_call(
        _xattn_kernel,
        grid=(b, s // tm),
        in_specs=[
            pl.BlockSpec((None, tm, D_MODEL), cur),
            pl.BlockSpec((1, D_MODEL), fixed),
            pl.BlockSpec((D_MODEL, D_MODEL), fixed, **once),
            pl.BlockSpec((None, m, D_MODEL), per_batch),
            pl.BlockSpec((1, D_MODEL), fixed),
            pl.BlockSpec((D_MODEL, 2 * D_MODEL), fixed, **once),
            pl.BlockSpec((D_MODEL, D_MODEL), fixed, **once),
        ],
        out_specs=pl.BlockSpec((None, tm, D_MODEL), cur),
        out_shape=jax.ShapeDtypeStruct((b, s, D_MODEL), F32),
        scratch_shapes=[pltpu.VMEM((m, 2 * D_MODEL), BF16)],
        compiler_params=pltpu.CompilerParams(
            dimension_semantics=("arbitrary", "arbitrary"), vmem_limit_bytes=VMEM_LIMIT_BYTES),
        name="xattn",
    )(x, gain, w_xq, mem, gmem, w_xkv, w_xo)


FFN_CHUNKS = ((0, 1024), (1024, 2048), (2048, FFN_HIDDEN))
FFN_SUB_TILES = 4

def _ffn_kernel(x_ref, gain_ref, wgu_ref, wd_ref, gfin_ref, o_ref):
    n_sub = FFN_SUB_TILES
    sub = x_ref.shape[0] // n_sub
    hf, acc = [], []
    for i in range(n_sub):
        x = x_ref[i * sub:(i + 1) * sub, :]
        hf.append((x * _rms_scale(x) * gain_ref[...]).astype(BF16))
        acc.append(x)
    for lo, hi in FFN_CHUNKS:
        for i in range(n_sub):
            gate = _dot(hf[i], wgu_ref[:, lo:hi])
            upv = _dot(hf[i], wgu_ref[:, FFN_HIDDEN + lo:FFN_HIDDEN + hi])
            act = (_silu(gate) * upv).astype(BF16)
            acc[i] = acc[i] + _dot(act, wd_ref[lo:hi, :])
    for i in range(n_sub):
        o_ref[i * sub:(i + 1) * sub, :] = acc[i] * _rms_scale(acc[i]) * gfin_ref[...]


def _ffn(x2d, gain, w_gu, w_d, gfin, tm):
    n = x2d.shape[0]
    row = lambda i: (i, 0)
    fixed = lambda i: (0, 0)
    return pl.pallas_call(
        _ffn_kernel,
        grid=(n // tm,),
        in_specs=[
            pl.BlockSpec((tm, D_MODEL), row),
            pl.BlockSpec((1, D_MODEL), fixed),
            pl.BlockSpec((D_MODEL, 2 * FFN_HIDDEN), fixed, pipeline_mode=pl.Buffered(1)),
            pl.BlockSpec((FFN_HIDDEN, D_MODEL), fixed, pipeline_mode=pl.Buffered(1)),
            pl.BlockSpec((1, D_MODEL), fixed),
        ],
        out_specs=pl.BlockSpec((tm, D_MODEL), row),
        out_shape=jax.ShapeDtypeStruct((n, D_MODEL), F32),
        compiler_params=pltpu.CompilerParams(
            dimension_semantics=("arbitrary",), vmem_limit_bytes=VMEM_LIMIT_BYTES),
        name="ffn",
    )(x2d, gain, w_gu, w_d, gfin)


def kernel(x, mem, positions, norm_mix, w_in, att_sinks, att_out_gain, hg_lb_logits, hg_out_gain,
           w_out, norm_xattn, norm_mem, w_xq, w_xkv, w_xo, norm_ffn, w_gate_up, w_down, norm_final):
    b, s, d = x.shape
    n = b * s
    assert w_in.shape[0] == 1 and hg_lb_logits.shape[0] == 2, "single-layer block only"
    assert all(a.dtype == F32 for a in (x, mem, w_in, w_out, w_xq, w_xkv, w_xo, w_gate_up, w_down))
    row = lambda v: v.reshape(1, -1)
    side = [w[0] for w in (w_xkv, w_xq, w_xo, w_gate_up, w_down)]
    x, (w_xkv_b, w_xq_b, w_xo_b, w_gu_b, w_d_b) = _mixer(
        x, row(norm_mix[0]), w_in[0], positions,
        w_out[0], att_sinks[0], row(att_out_gain[0]),
        hg_lb_logits, row(hg_out_gain[0]), side)
    x = _xattn(x, row(norm_xattn[0]), w_xq_b, mem, row(norm_mem[0]), w_xkv_b, w_xo_b, tm=DENSE_TILE)
    y = _ffn(x.reshape(n, d), row(norm_ffn[0]), w_gu_b, w_d_b, row(norm_final), tm=DENSE_TILE)
    return y.reshape(b, s, d)
```

```python
import numpy as np
import jax
import jax.numpy as jnp
from jax import lax
from jax.experimental import pallas as pl
from jax.experimental.pallas import tpu as pltpu

F32 = jnp.float32
BF16 = jnp.bfloat16

D_MODEL = 1024
ATT_HEADS = 8
ATT_KV_HEADS = 2
ATT_GROUP = ATT_HEADS // ATT_KV_HEADS
ATT_HEAD_DIM = 64
ATT_WIDTH = ATT_HEADS * ATT_HEAD_DIM
ATT_KV_WIDTH = ATT_KV_HEADS * ATT_HEAD_DIM
ATT_BLOCK = 128
ROPE_THETA = 500000.0
ROPE_DIM = ATT_HEAD_DIM // 4
ROPE_HALF = ROPE_DIM // 2
HG_HEADS = 4
HG_DIM = 128
HG_WIDTH = HG_HEADS * HG_DIM
X_HEADS = 4
X_HEAD_DIM = D_MODEL // X_HEADS
FFN_HIDDEN = 2816
RMS_EPS = 1e-6
LANES = 128
SUBLANES = 8
MASK_VALUE = -1e30
LOG2_E = 1.4426950408889634

MIX_TILE = 256
HG_LEVELS = (128, 64, 32, 16, 8, 4, 2, 1)
KV_PAD_WIDTH = 2 * ATT_KV_HEADS * 2 * LANES
TILES_PER_STEP = 4
CAST_STRIDE = 1
DENSE_TILE = 1024
OWN_STAGE_ROWS = 128

VMEM_LIMIT_BYTES = 56 * 1024 * 1024


def _rms_scale(x):
    return lax.rsqrt(jnp.mean(x * x, axis=-1, keepdims=True) + RMS_EPS)


def _silu(x):
    h = 0.5 * x
    return h + h * jnp.tanh(h)


def _dot(a, b):
    return jnp.dot(a, b, preferred_element_type=F32)


def _dot_nt(a, b):
    return lax.dot_general(a, b, (((1,), (1,)), ((), ())), preferred_element_type=F32)


def _dot_tn(a, b):
    return lax.dot_general(a, b, (((0,), (0,)), ((), ())), preferred_element_type=F32)


def _stage_weight(src_hbm, dst_ref, stage_ref, sem_ref):
    rows = stage_ref.shape[1]
    n_chunks = dst_ref.shape[0] // rows
    assert n_chunks * rows == dst_ref.shape[0] and src_hbm.shape == dst_ref.shape

    def copy(c):
        return pltpu.make_async_copy(src_hbm.at[pl.ds(c * rows, rows), :], stage_ref.at[c % 2], sem_ref.at[c % 2])

    copy(0).start()
    for c in range(n_chunks):
        if c + 1 < n_chunks:
            copy(c + 1).start()
        copy(c).wait()
        dst_ref[c * rows:(c + 1) * rows, :] = stage_ref[c % 2].astype(BF16)


def _cast_step(t, issue, src_hbm, dst_hbm, in_stage, out_stage, in_sem, out_sem):
    rows = in_stage.shape[1]
    n = src_hbm.shape[0] // rows
    assert n * rows == src_hbm.shape[0] and src_hbm.shape == dst_hbm.shape
    slot = lax.rem(t, 2)

    def read(c, s):
        return pltpu.make_async_copy(src_hbm.at[pl.ds(c * rows, rows), :], in_stage.at[s], in_sem.at[s])

    def write(c, s):
        return pltpu.make_async_copy(out_stage.at[s], dst_hbm.at[pl.ds(c * rows, rows), :], out_sem.at[s])

    if issue:
        @pl.when(t == 0)
        def _():
            read(0, 0).start()

        @pl.when(t + 1 < n)
        def _():
            read(t + 1, 1 - slot).start()

        return

    @pl.when(t < n)
    def _():
        read(t, slot).wait()

        @pl.when(t >= 2)
        def _():
            write(t - 2, slot).wait()

        out_stage[slot] = in_stage[slot].astype(BF16)
        write(t, slot).start()

    @pl.when(t == n - 1)
    def _():
        write(t - 1, 1 - slot).wait()
        write(t, slot).wait()


def _hgrn_constants():
    r = np.arange(MIX_TILE)[:, None]
    j = np.arange(MIX_TILE)[None, :]
    tri = (j <= r).astype(np.float32)
    rr = np.arange(LANES)[:, None]
    jj = np.arange(LANES)[None, :]
    masks = []
    for m in HG_LEVELS:
        if 2 * m <= LANES:
            masks.append(((rr // (2 * m)) == (jj // (2 * m))) & ((rr % (2 * m)) >= m) & ((jj % (2 * m)) < m))
    lmask = np.concatenate(masks, axis=0).astype(np.float32)
    return jnp.asarray(tri, BF16), jnp.asarray(lmask, F32)


def _attention_bias():
    i = (np.arange(2 * ATT_BLOCK) % ATT_BLOCK)[:, None]
    j = np.arange(2 * ATT_BLOCK)[None, :]
    band = (j > i) & (j <= i + ATT_BLOCK)
    variants = (band, band & (j >= ATT_BLOCK))
    return jnp.asarray(np.stack([np.where(v | (j == 0), 0.0, MASK_VALUE) for v in variants]), F32)


def _rotary_constants():
    inv_freq = jnp.power(jnp.float32(ROPE_THETA),
                         -jnp.arange(ROPE_HALF, dtype=F32) * (2.0 / ROPE_DIM))
    invf = jnp.broadcast_to(inv_freq[:, None], (ROPE_HALF, MIX_TILE))
    lane = np.arange(LANES)
    in_head = lane % ATT_HEAD_DIM
    freq_row = (np.arange(ROPE_HALF)[:, None] == (in_head % ROPE_HALF)[None, :])
    rot = freq_row & (in_head < ROPE_DIM)[None, :]
    ecos = np.zeros((4 * ROPE_HALF, LANES), np.float32)
    ecos[:ROPE_HALF] = rot
    ecos[ROPE_HALF:2 * ROPE_HALF] = rot
    ecos[2 * ROPE_HALF] = in_head >= ROPE_DIM
    sign = np.where(in_head < ROPE_HALF, -1.0, 1.0)[None, :]
    esin = np.concatenate([rot * sign, rot * sign], axis=0).astype(np.float32)
    return invf, jnp.asarray(ecos, BF16), jnp.asarray(esin, BF16)


def _level_decay(m, b, f, row_ids):
    ts = b.shape[0]
    if m >= SUBLANES // 2:
        b3 = b.reshape(ts // (2 * m), 2 * m, HG_DIM)
        ref = b3[:, m - 1:m, :]
        if m >= SUBLANES:
            e3 = jnp.concatenate([ref - b3[:, :m, :], b3[:, m:, :] - ref], axis=1)
        else:
            e3 = -jnp.abs(b3 - ref)
        return jnp.exp2(e3).reshape(ts, HG_DIM)
    if m == 2:
        pos = row_ids & 3
        f_next = pltpu.roll(f, ts - 1, axis=0)
        f_prev = pltpu.roll(f, 1, axis=0)
        return jnp.where(pos == 0, f_next, jnp.where(pos == 1, 1.0, jnp.where(pos == 2, f, f * f_prev)))
    assert m == 1
    return jnp.where((row_ids & 1) != 0, f, 1.0)


def _select_rows(m, upper_src, lower_src, row_ids):
    c = upper_src.shape[0]
    if m >= SUBLANES:
        pieces = []
        for b0 in range(0, c, 2 * m):
            pieces.append(lower_src[b0:b0 + m])
            pieces.append(upper_src[b0 + m:b0 + 2 * m])
        return jnp.concatenate(pieces, axis=0)
    return jnp.where((row_ids & m) != 0, upper_src, lower_src)


def _rotary_tables(pos_ref, invf_ref, ecos_ref, esin_ref):
    ang = invf_ref[...] * pos_ref[...].astype(F32)

    def spread(t, e_ref, extra):
        t_hi = t.astype(BF16).astype(F32)
        rows = jnp.concatenate([t_hi, t - t_hi] + extra, axis=0).astype(BF16)
        return _dot_tn(rows, e_ref[...])

    ones = jnp.ones_like(ang)
    cos_t = spread(jnp.cos(ang), ecos_ref, [ones, jnp.zeros_like(ang)])
    sin_t = spread(jnp.sin(ang), esin_ref, [])
    return cos_t, sin_t


def _mixer_kernel(sinks_ref, x_ref, gmix_ref, win_hbm, pos_ref, invf_ref, ecos_ref, esin_ref, wout_hbm,
                  gatt_ref, lbl_ref, ghg_ref, tri_ref, lmask_ref, bias_ref, *rest):
    n_own = 8
    n_side = (len(rest) - 1 - n_own) // 6
    side_src, o_ref, side_dst = rest[:n_side], rest[n_side], rest[n_side + 1:2 * n_side + 1]
    own = rest[2 * n_side + 1:2 * n_side + 1 + n_own]
    state_ref, kvprev_ref, win_ref, wout_ref, win_stage, wout_stage, win_sem, wout_sem = own
    side_scratch = rest[2 * n_side + 1 + n_own:]
    step = pl.program_id(1)
    ts = MIX_TILE
    blk = ATT_BLOCK

    t = pl.program_id(0) * pl.num_programs(1) + step

    @pl.when(t == 0)
    def _():
        _stage_weight(win_hbm, win_ref, win_stage, win_sem)
        _stage_weight(wout_hbm, wout_ref, wout_stage, wout_sem)

    @pl.when(lax.rem(t, CAST_STRIDE) == 0)
    def _():
        for issue in (True, False):
            for k in range(n_side):
                _cast_step(lax.div(t, CAST_STRIDE), issue, side_src[k], side_dst[k],
                           *side_scratch[4 * k:4 * k + 4])

    @pl.when(step == 0)
    def _():
        state_ref[...] = jnp.zeros_like(state_ref)
        kvprev_ref[...] = jnp.zeros_like(kvprev_ref)

    tiles = []
    for k in range(TILES_PER_STEP):
        rows = pl.ds(k * ts, ts)
        no_prev = jnp.where(step > 0, 0, 1) if k == 0 else 0
        tiles.append(_mixer_tile(no_prev, sinks_ref, x_ref.at[rows, :], gmix_ref, win_ref, pos_ref.at[:, rows],
                                 invf_ref, ecos_ref, esin_ref, wout_ref, gatt_ref, lbl_ref, ghg_ref, tri_ref,
                                 lmask_ref, bias_ref, o_ref.at[rows, :], state_ref, kvprev_ref))
    next(tiles[0])
    for k in range(TILES_PER_STEP):
        next(tiles[k])
        if k + 1 < TILES_PER_STEP:
            next(tiles[k + 1])
        next(tiles[k], None)


def _mixer_tile(no_prev, sinks_ref, x_ref, gmix_ref, win_ref, pos_ref, invf_ref, ecos_ref, esin_ref, wout_ref,
                gatt_ref, lbl_ref, ghg_ref, tri_ref, lmask_ref, bias_ref, o_ref, state_ref, kvprev_ref):
    ts = MIX_TILE
    blk = ATT_BLOCK

    x = x_ref[...]
    hn = (x * _rms_scale(x) * gmix_ref[...]).astype(BF16)

    def proj(c0, c1):
        return _dot(hn, win_ref[:, c0:c1])

    cos, sin = _rotary_tables(pos_ref, invf_ref, ecos_ref, esin_ref)
    lane = lax.broadcasted_iota(jnp.int32, cos.shape, 1)
    first_half = (lane & (ATT_HEAD_DIM - 1)) < ROPE_HALF
    lo_head = lane < ATT_HEAD_DIM

    def rotate(col):
        partner = jnp.where(first_half,
                            pltpu.roll(col, LANES - ROPE_HALF, axis=1),
                            pltpu.roll(col, ROPE_HALF, axis=1))
        return col * cos + partner * sin

    def padded(col):
        swapped = pltpu.roll(col, ATT_HEAD_DIM, axis=1)
        zero = jnp.zeros_like(col)
        return [v.astype(BF16) for v in (jnp.where(lo_head, col, zero), jnp.where(lo_head, zero, swapped),
                                         jnp.where(lo_head, swapped, zero), jnp.where(lo_head, zero, col))]

    pq = proj(0, ATT_WIDTH)
    q_cols = [(rotate(pq[:, j * LANES:(j + 1) * LANES]) * (ATT_HEAD_DIM ** -0.5 * LOG2_E)).astype(BF16)
              for j in range(ATT_WIDTH // LANES)]
    pkv = proj(ATT_WIDTH, ATT_WIDTH + 2 * ATT_KV_WIDTH)
    kv_new = padded(rotate(pkv[:, :LANES])) + padded(pkv[:, LANES:])
    kv_all = [jnp.concatenate([kvprev_ref[:, i * LANES:(i + 1) * LANES], v], axis=0)
              for i, v in enumerate(kv_new)]
    for i, v in enumerate(kv_new):
        kvprev_ref[:, i * LANES:(i + 1) * LANES] = v[ts - blk:]
    r0c = ATT_WIDTH + 2 * ATT_KV_WIDTH
    p_q, p_f, p_i, p_g = (proj(r0c + k * HG_WIDTH, r0c + (k + 1) * HG_WIDTH) for k in range(4))

    yield

    upper_half_rows = lax.broadcasted_iota(jnp.int32, (2 * blk, 1), 0) >= blk
    lane_ids = lax.broadcasted_iota(jnp.int32, (2 * blk, LANES), 1)
    lane_lo = lane_ids < ATT_HEAD_DIM
    ones_lo = jnp.where(lane_lo, 1.0, 0.0).astype(BF16)
    ones_hi = jnp.where(lane_lo, 0.0, 1.0).astype(BF16)
    bf16_rows = 2 * SUBLANES
    drop_row0 = jnp.where(lax.broadcasted_iota(jnp.int32, (bf16_rows, LANES), 0) > 0, 1.0, 0.0).astype(BF16)

    def sink_slot(op):
        return jnp.concatenate([op[:bf16_rows] * drop_row0, op[bf16_rows:]], axis=0)

    att_units = {}

    def att_scores(jb, gk):
        keys = slice(jb * blk, (jb + 2) * blk)
        bias = bias_ref[no_prev] if jb == 0 else bias_ref[0]
        q2 = jnp.concatenate([q_cols[2 * gk][jb * blk:(jb + 1) * blk],
                              q_cols[2 * gk + 1][jb * blk:(jb + 1) * blk]], axis=0)
        k_lo = sink_slot(kv_all[2 * gk][keys])
        k_hi = sink_slot(kv_all[2 * gk + 1][keys])
        h0 = ATT_GROUP * gk
        probs = []
        for which, k_op in ((0, k_lo), (1, k_hi)):
            sink = jnp.where(upper_half_rows, sinks_ref[h0 + 2 + which], sinks_ref[h0 + which]) * LOG2_E
            s = _dot_nt(q2, k_op) + bias
            s_first = s[:, :LANES] + jnp.where(lane_ids == 0, sink, 0.0)
            m = jnp.max(jnp.maximum(s_first, s[:, LANES:]), axis=-1, keepdims=True)
            probs.append(jnp.exp2(jnp.concatenate([s_first, s[:, LANES:]], axis=1) - m).astype(BF16))
        att_units[jb, gk] = jnp.concatenate(probs, axis=1)

    def att_values(jb, gk):
        keys = slice(jb * blk, (jb + 2) * blk)
        r_lo = jnp.concatenate([sink_slot(kv_all[4 + 2 * gk][keys]), ones_lo], axis=1)
        r_hi = jnp.concatenate([sink_slot(kv_all[5 + 2 * gk][keys]), ones_hi], axis=1)
        out = _dot(att_units[jb, gk], jnp.concatenate([r_lo, r_hi], axis=0))
        att_units[jb, gk] = out[:, :LANES] / out[:, LANES:]

    units = [(jb, gk) for jb in range(ts // blk) for gk in range(ATT_KV_HEADS)]
    att_scores(*units[0])
    for prev_u, u in zip(units[:-1], units[1:]):
        att_scores(*u)
        att_values(*prev_u)
    att_values(*units[-1])
    att_rows = []
    for jb in range(ts // blk):
        cols_ = []
        for gk in range(ATT_KV_HEADS):
            cols_ += [att_units[jb, gk][:blk], att_units[jb, gk][blk:]]
        att_rows.append(jnp.concatenate(cols_, axis=1))
    att = jnp.concatenate(att_rows, axis=0)

    lbl = lbl_ref[...]
    lexp = jnp.exp(lbl - jnp.max(lbl, axis=0, keepdims=True))
    lb = lexp[0:1, :] / jnp.sum(lexp, axis=0, keepdims=True)
    tri = tri_ref[...]
    row_ids = lax.broadcasted_iota(jnp.int32, (ts, HG_DIM), 0)
    half = ts // 2
    f_all = lb + (1.0 - lb) * (1.0 / (1.0 + jnp.exp(-p_f)))
    g_all = jnp.log(f_all) * LOG2_E
    g_hi = g_all.astype(BF16)
    g_lo = (g_all - g_hi.astype(F32)).astype(BF16)
    b2 = _dot(tri, jnp.concatenate([g_hi, g_lo], axis=1))
    b_all = b2[:, :HG_WIDTH] + b2[:, HG_WIDTH:]
    n_tiles = half // SUBLANES
    hs = [dict() for _ in range(HG_HEADS)]

    def prep(h):
        cols = slice(h * HG_DIM, (h + 1) * HG_DIM)
        f = f_all[:, cols]
        b = b_all[:, cols]
        kk = 1.0 - f
        qf = _silu(p_q[:, cols])
        vv = p_i[:, cols].astype(BF16)
        w_incl = jnp.exp2(b)
        w_tail = jnp.exp2(b[ts - 1:ts, :] - b)
        st = state_ref[h]
        o = _dot_nt((qf * w_incl).astype(BF16), st.astype(BF16))
        state_ref[h] = st * w_incl[ts - 1:ts, :] + _dot_tn(vv, (kk * w_tail).astype(BF16))
        hs[h].update(f=f, b=b, kk=kk, qf=qf, vv=vv, o=o, cols=cols)

    def levels(h):
        d = hs[h]
        qf, kk, b, f = d["qf"], d["kk"], d["b"], d["f"]
        quad = [[None] * n_tiles, [None] * n_tiles]
        for li, m in enumerate(HG_LEVELS):
            x_f = _select_rows(m, qf, kk, row_ids) * _level_decay(m, b, f, row_ids)
            x_l = x_f.astype(BF16)
            if 2 * m == ts:
                d["s10"] = _dot_nt(x_l[half:], x_l[:half])
                continue
            if m >= SUBLANES:
                tiles = [t for t in range(n_tiles) if (t * SUBLANES // m) % 2 == 1]
            else:
                tiles = list(range(n_tiles))
            for hf in range(2):
                r0 = hf * half
                if len(tiles) == n_tiles:
                    lhs = x_l[r0:r0 + half]
                else:
                    lhs = jnp.concatenate([x_f[r0 + t * SUBLANES:r0 + (t + 1) * SUBLANES] for t in tiles],
                                          axis=0).astype(BF16)
                sc = _dot_nt(lhs, x_l[r0:r0 + half])
                for i, t in enumerate(tiles):
                    m0 = (li - 1) * LANES + t * SUBLANES
                    part = sc[i * SUBLANES:(i + 1) * SUBLANES] * lmask_ref[m0:m0 + SUBLANES, :]
                    quad[hf][t] = part if quad[hf][t] is None else quad[hf][t] + part
        d["s00"] = jnp.concatenate(quad[0], axis=0).astype(BF16)
        d["s1"] = jnp.concatenate([d["s10"], jnp.concatenate(quad[1], axis=0)], axis=1).astype(BF16)

    def finish(h):
        d = hs[h]
        vv = d["vv"]
        o_top = _dot(d["s00"], vv[:half])
        o_bot = _dot(d["s1"], vv)
        diag = jnp.sum(d["qf"] * d["kk"], axis=-1, keepdims=True)
        o = d["o"] + jnp.concatenate([o_top, o_bot], axis=0) + diag * p_i[:, d["cols"]]
        d["rec"] = o * _rms_scale(o)

    prep(0)
    prep(1)
    levels(0)
    prep(2)
    levels(1)
    prep(3)
    levels(2)
    finish(0)
    levels(3)
    finish(1)
    finish(2)
    finish(3)
    rec_cols = [hs[h]["rec"] for h in range(HG_HEADS)]
    rec_n = (jnp.concatenate(rec_cols, axis=1) * ghg_ref[...] * _silu(p_g)).astype(BF16)

    yield

    att_n = (att * _rms_scale(att) * gatt_ref[...]).astype(BF16)
    o_ref[...] = x + _dot(jnp.concatenate([att_n, rec_n], axis=1), wout_ref[...])


def _cast_chunk_rows(n_rows, n_steps):
    tile = 2 * SUBLANES
    rows = tile
    while n_rows % rows or n_rows // rows > n_steps:
        rows += tile
    return rows


def _mixer(x, gmix, w_in, positions, w_out, sinks, gatt, lbl, ghg, side_weights):
    b, s, _ = x.shape
    ts = TILES_PER_STEP * MIX_TILE
    tri, lmask = _hgrn_constants()
    invf, ecos, esin = _rotary_constants()
    bias = _attention_bias()
    pos = positions.reshape(b * (s // ts), 1, ts)
    cur = lambda bi, i: (bi, i, 0)
    fixed = lambda bi, i: (0, 0)
    hbm = pl.BlockSpec(memory_space=pl.ANY)
    n_cast_steps = b * (s // ts) // CAST_STRIDE
    side_scratch = []
    for w in side_weights:
        stage = (2, _cast_chunk_rows(w.shape[0], n_cast_steps), w.shape[1])
        side_scratch += [pltpu.VMEM(stage, F32), pltpu.VMEM(stage, BF16),
                         pltpu.SemaphoreType.DMA((2,)), pltpu.SemaphoreType.DMA((2,))]
    outs = pl.pallas_call(
        _mixer_kernel,
        grid=(b, s // ts),
        in_specs=[
            pl.BlockSpec(memory_space=pltpu.SMEM),
            pl.BlockSpec((None, ts, D_MODEL), cur),
            pl.BlockSpec((1, D_MODEL), fixed),
            hbm,
            pl.BlockSpec((None, 1, ts), lambda bi, i: (bi * (s // ts) + i, 0, 0)),
            pl.BlockSpec(invf.shape, fixed),
            pl.BlockSpec(ecos.shape, fixed),
            pl.BlockSpec(esin.shape, fixed),
            hbm,
            pl.BlockSpec((1, ATT_WIDTH), fixed),
            pl.BlockSpec(lbl.shape, fixed),
            pl.BlockSpec((1, HG_WIDTH), fixed),
            pl.BlockSpec(tri.shape, fixed),
            pl.BlockSpec(lmask.shape, fixed),
            pl.BlockSpec(bias.shape, lambda bi, i: (0, 0, 0)),
        ] + [hbm] * len(side_weights),
        out_specs=[pl.BlockSpec((None, ts, D_MODEL), cur)] + [hbm] * len(side_weights),
        out_shape=[jax.ShapeDtypeStruct((b, s, D_MODEL), F32)]
        + [jax.ShapeDtypeStruct(w.shape, BF16) for w in side_weights],
        scratch_shapes=[
            pltpu.VMEM((HG_HEADS, HG_DIM, HG_DIM), F32),
            pltpu.VMEM((ATT_BLOCK, KV_PAD_WIDTH), BF16),
            pltpu.VMEM(w_in.shape, BF16),
            pltpu.VMEM(w_out.shape, BF16),
            pltpu.VMEM((2, OWN_STAGE_ROWS, w_in.shape[1]), F32),
            pltpu.VMEM((2, OWN_STAGE_ROWS, w_out.shape[1]), F32),
            pltpu.SemaphoreType.DMA((2,)),
            pltpu.SemaphoreType.DMA((2,)),
        ] + side_scratch,
        compiler_params=pltpu.CompilerParams(
            dimension_semantics=("arbitrary", "arbitrary"), vmem_limit_bytes=VMEM_LIMIT_BYTES),
        name="mixer",
    )(sinks, x, gmix, w_in, pos, invf, ecos, esin, w_out, gatt, lbl, ghg, tri, lmask, bias, *side_weights)
    return outs[0], outs[1:]


XATTN_SUB_TILES = 2


def _xattn_kernel(x_ref, gain_ref, wq_ref, mem_ref, gmem_ref, wkv_ref, wo_ref, o_ref, kv_ref):
    @pl.when(pl.program_id(1) == 0)
    def _():
        m = mem_ref[...]
        mn = (m * _rms_scale(m) * gmem_ref[...]).astype(BF16)
        kv_ref[...] = _dot(mn, wkv_ref[...]).astype(BF16)

    n_sub = XATTN_SUB_TILES
    sub = x_ref.shape[0] // n_sub
    q = [None] * n_sub
    probs = [[None] * X_HEADS for _ in range(n_sub)]
    outs = [[None] * X_HEADS for _ in range(n_sub)]

    def qproj(i):
        x = x_ref[i * sub:(i + 1) * sub, :]
        hq = (x * _rms_scale(x) * gain_ref[...]).astype(BF16)
        q[i] = (_dot(hq, wq_ref[...]) * (X_HEAD_DIM ** -0.5 * LOG2_E)).astype(BF16)

    def scores(i, h):
        cols = slice(h * X_HEAD_DIM, (h + 1) * X_HEAD_DIM)
        s = _dot_nt(q[i][:, cols], kv_ref[:, cols])
        probs[i][h] = jnp.exp2(s - jnp.max(s, axis=-1, keepdims=True))

    def values(i, h):
        p = probs[i][h]
        o = _dot(p.astype(BF16), kv_ref[:, D_MODEL + h * X_HEAD_DIM:D_MODEL + (h + 1) * X_HEAD_DIM])
        outs[i][h] = (o / jnp.sum(p, axis=-1, keepdims=True)).astype(BF16)

    def wo(i):
        rows = slice(i * sub, (i + 1) * sub)
        o_ref[rows, :] = x_ref[rows, :] + _dot(jnp.concatenate(outs[i], axis=1), wo_ref[...])

    for i in range(n_sub):
        qproj(i)
    for i in range(n_sub):
        scores(i, 0)
    for h in range(1, X_HEADS):
        for i in range(n_sub):
            scores(i, h)
            values(i, h - 1)
    for i in range(n_sub):
        values(i, X_HEADS - 1)
    for i in range(n_sub):
        wo(i)


def _xattn(x, gain, w_xq, mem, gmem, w_xkv, w_xo, tm):
    b, s, _ = x.shape
    m = mem.shape[1]
    cur = lambda bi, i: (bi, i, 0)
    per_batch = lambda bi, i: (bi, 0, 0)
    fixed = lambda bi, i: (0, 0)
    once = dict(pipeline_mode=pl.Buffered(1))
    return pl.pallas_call(
        _xattn_kernel,
        grid=(b, s // tm),
        in_specs=[
            pl.BlockSpec((None, tm, D_MODEL), cur),
            pl.BlockSpec((1, D_MODEL), fixed),
            pl.BlockSpec((D_MODEL, D_MODEL), fixed, **once),
            pl.BlockSpec((None, m, D_MODEL), per_batch),
            pl.BlockSpec((1, D_MODEL), fixed),
            pl.BlockSpec((D_MODEL, 2 * D_MODEL), fixed, **once),
            pl.BlockSpec((D_MODEL, D_MODEL), fixed, **once),
        ],
        out_specs=pl.BlockSpec((None, tm, D_MODEL), cur),
        out_shape=jax.ShapeDtypeStruct((b, s, D_MODEL), F32),
        scratch_shapes=[pltpu.VMEM((m, 2 * D_MODEL), BF16)],
        compiler_params=pltpu.CompilerParams(
            dimension_semantics=("arbitrary", "arbitrary"), vmem_limit_bytes=VMEM_LIMIT_BYTES),
        name="xattn",
    )(x, gain, w_xq, mem, gmem, w_xkv, w_xo)


FFN_CHUNKS = ((0, 1024), (1024, 2048), (2048, FFN_HIDDEN))
FFN_SUB_TILES = 4

def _ffn_kernel(x_ref, gain_ref, wgu_ref, wd_ref, gfin_ref, o_ref):
    n_sub = FFN_SUB_TILES
    sub = x_ref.shape[0] // n_sub
    hf, acc = [], []
    for i in range(n_sub):
        x = x_ref[i * sub:(i + 1) * sub, :]
        hf.append((x * _rms_scale(x) * gain_ref[...]).astype(BF16))
        acc.append(x)
    for lo, hi in FFN_CHUNKS:
        for i in range(n_sub):
            gate = _dot(hf[i], wgu_ref[:, lo:hi])
            upv = _dot(hf[i], wgu_ref[:, FFN_HIDDEN + lo:FFN_HIDDEN + hi])
            act = (_silu(gate) * upv).astype(BF16)
            acc[i] = acc[i] + _dot(act, wd_ref[lo:hi, :])
    for i in range(n_sub):
        o_ref[i * sub:(i + 1) * sub, :] = acc[i] * _rms_scale(acc[i]) * gfin_ref[...]


def _ffn(x2d, gain, w_gu, w_d, gfin, tm):
    n = x2d.shape[0]
    row = lambda i: (i, 0)
    fixed = lambda i: (0, 0)
    return pl.pallas_call(
        _ffn_kernel,
        grid=(n // tm,),
        in_specs=[
            pl.BlockSpec((tm, D_MODEL), row),
            pl.BlockSpec((1, D_MODEL), fixed),
            pl.BlockSpec((D_MODEL, 2 * FFN_HIDDEN), fixed, pipeline_mode=pl.Buffered(1)),
            pl.BlockSpec((FFN_HIDDEN, D_MODEL), fixed, pipeline_mode=pl.Buffered(1)),
            pl.BlockSpec((1, D_MODEL), fixed),
        ],
        out_specs=pl.BlockSpec((tm, D_MODEL), row),
        out_shape=jax.ShapeDtypeStruct((n, D_MODEL), F32),
        compiler_params=pltpu.CompilerParams(
            dimension_semantics=("arbitrary",), vmem_limit_bytes=VMEM_LIMIT_BYTES),
        name="ffn",
    )(x2d, gain, w_gu, w_d, gfin)


def kernel(x, mem, positions, norm_mix, w_in, att_sinks, att_out_gain, hg_lb_logits, hg_out_gain,
           w_out, norm_xattn, norm_mem, w_xq, w_xkv, w_xo, norm_ffn, w_gate_up, w_down, norm_final):
    b, s, d = x.shape
    n = b * s
    assert w_in.shape[0] == 1 and hg_lb_logits.shape[0] == 2, "single-layer block only"
    assert all(a.dtype == F32 for a in (x, mem, w_in, w_out, w_xq, w_xkv, w_xo, w_gate_up, w_down))
    row = lambda v: v.reshape(1, -1)
    side = [w[0] for w in (w_xkv, w_xq, w_xo, w_gate_up, w_down)]
    x, (w_xkv_b, w_xq_b, w_xo_b, w_gu_b, w_d_b) = _mixer(
        x, row(norm_mix[0]), w_in[0], positions,
        w_out[0], att_sinks[0], row(att_out_gain[0]),
        hg_lb_logits, row(hg_out_gain[0]), side)
    x = _xattn(x, row(norm_xattn[0]), w_xq_b, mem, row(norm_mem[0]), w_xkv_b, w_xo_b, tm=DENSE_TILE)
    y = _ffn(x.reshape(n, d), row(norm_ffn[0]), w_gu_b, w_d_b, row(norm_final), tm=DENSE_TILE)
    return y.reshape(b, s, d)
```

```python
import numpy as np
import jax
import jax.numpy as jnp
from jax import lax
from jax.experimental import pallas as pl
from jax.experimental.pallas import tpu as pltpu

F32 = jnp.float32
BF16 = jnp.bfloat16

D_MODEL = 1024
ATT_HEADS = 8
ATT_KV_HEADS = 2
ATT_GROUP = ATT_HEADS // ATT_KV_HEADS
ATT_HEAD_DIM = 64
ATT_WIDTH = ATT_HEADS * ATT_HEAD_DIM
ATT_KV_WIDTH = ATT_KV_HEADS * ATT_HEAD_DIM
ATT_BLOCK = 128
ROPE_THETA = 500000.0
ROPE_DIM = ATT_HEAD_DIM // 4
ROPE_HALF = ROPE_DIM // 2
HG_HEADS = 4
HG_DIM = 128
HG_WIDTH = HG_HEADS * HG_DIM
X_HEADS = 4
X_HEAD_DIM = D_MODEL // X_HEADS
FFN_HIDDEN = 2816
RMS_EPS = 1e-6
LANES = 128
SUBLANES = 8
MASK_VALUE = -1e30
LOG2_E = 1.4426950408889634

MIX_TILE = 256
HG_LEVELS = (128, 64, 32, 16, 8, 4, 2, 1)
KV_PAD_WIDTH = 2 * ATT_KV_HEADS * 2 * LANES
TILES_PER_STEP = 4
CAST_STRIDE = 1
DENSE_TILE = 1024
OWN_STAGE_ROWS = 128

VMEM_LIMIT_BYTES = 56 * 1024 * 1024


def _rms_scale(x):
    return lax.rsqrt(jnp.mean(x * x, axis=-1, keepdims=True) + RMS_EPS)


def _silu(x):
    h = 0.5 * x
    return h + h * jnp.tanh(h)


def _dot(a, b):
    return jnp.dot(a, b, preferred_element_type=F32)


def _dot_nt(a, b):
    return lax.dot_general(a, b, (((1,), (1,)), ((), ())), preferred_element_type=F32)


def _dot_tn(a, b):
    return lax.dot_general(a, b, (((0,), (0,)), ((), ())), preferred_element_type=F32)


def _stage_weight(src_hbm, dst_ref, stage_ref, sem_ref):
    rows = stage_ref.shape[1]
    n_chunks = dst_ref.shape[0] // rows
    assert n_chunks * rows == dst_ref.shape[0] and src_hbm.shape == dst_ref.shape

    def copy(c):
        return pltpu.make_async_copy(src_hbm.at[pl.ds(c * rows, rows), :], stage_ref.at[c % 2], sem_ref.at[c % 2])

    copy(0).start(priority=0)
    for c in range(n_chunks):
        if c + 1 < n_chunks:
            copy(c + 1).start(priority=(c + 1) % 2)
        copy(c).wait()
        dst_ref[c * rows:(c + 1) * rows, :] = stage_ref[c % 2].astype(BF16)


def _cast_step(t, issue, prio, src_hbm, dst_hbm, in_stage, out_stage, in_sem, out_sem):
    rows = in_stage.shape[1]
    n = src_hbm.shape[0] // rows
    assert n * rows == src_hbm.shape[0] and src_hbm.shape == dst_hbm.shape
    slot = lax.rem(t, 2)

    def read(c, s):
        return pltpu.make_async_copy(src_hbm.at[pl.ds(c * rows, rows), :], in_stage.at[s], in_sem.at[s])

    def write(c, s):
        return pltpu.make_async_copy(out_stage.at[s], dst_hbm.at[pl.ds(c * rows, rows), :], out_sem.at[s])

    if issue:
        @pl.when(t == 0)
        def _():
            read(0, 0).start(priority=prio)

        @pl.when(t + 1 < n)
        def _():
            read(t + 1, 1 - slot).start(priority=prio)

        return

    @pl.when(t < n)
    def _():
        read(t, slot).wait()

        @pl.when(t >= 2)
        def _():
            write(t - 2, slot).wait()

        out_stage[slot] = in_stage[slot].astype(BF16)
        write(t, slot).start(priority=prio)

    @pl.when(t == n - 1)
    def _():
        write(t - 1, 1 - slot).wait()
        write(t, slot).wait()


def _hgrn_constants():
    r = np.arange(MIX_TILE)[:, None]
    j = np.arange(MIX_TILE)[None, :]
    tri = (j <= r).astype(np.float32)
    rr = np.arange(LANES)[:, None]
    jj = np.arange(LANES)[None, :]
    masks = []
    for m in HG_LEVELS:
        if 2 * m <= LANES:
            masks.append(((rr // (2 * m)) == (jj // (2 * m))) & ((rr % (2 * m)) >= m) & ((jj % (2 * m)) < m))
    lmask = np.concatenate(masks, axis=0).astype(np.float32)
    return jnp.asarray(tri, BF16), jnp.asarray(lmask, F32)


def _attention_bias():
    i = (np.arange(2 * ATT_BLOCK) % ATT_BLOCK)[:, None]
    j = np.arange(2 * ATT_BLOCK)[None, :]
    band = (j > i) & (j <= i + ATT_BLOCK)
    variants = (band, band & (j >= ATT_BLOCK))
    return jnp.asarray(np.stack([np.where(v | (j == 0), 0.0, MASK_VALUE) for v in variants]), F32)


def _rotary_constants():
    inv_freq = jnp.power(jnp.float32(ROPE_THETA),
                         -jnp.arange(ROPE_HALF, dtype=F32) * (2.0 / ROPE_DIM))
    invf = jnp.broadcast_to(inv_freq[:, None], (ROPE_HALF, MIX_TILE))
    lane = np.arange(LANES)
    in_head = lane % ATT_HEAD_DIM
    freq_row = (np.arange(ROPE_HALF)[:, None] == (in_head % ROPE_HALF)[None, :])
    rot = freq_row & (in_head < ROPE_DIM)[None, :]
    ecos = np.zeros((4 * ROPE_HALF, LANES), np.float32)
    ecos[:ROPE_HALF] = rot
    ecos[ROPE_HALF:2 * ROPE_HALF] = rot
    ecos[2 * ROPE_HALF] = in_head >= ROPE_DIM
    sign = np.where(in_head < ROPE_HALF, -1.0, 1.0)[None, :]
    esin = np.concatenate([rot * sign, rot * sign], axis=0).astype(np.float32)
    return invf, jnp.asarray(ecos, BF16), jnp.asarray(esin, BF16)


def _level_decay(m, b, f, row_ids):
    ts = b.shape[0]
    if m >= SUBLANES // 2:
        b3 = b.reshape(ts // (2 * m), 2 * m, HG_DIM)
        ref = b3[:, m - 1:m, :]
        if m >= SUBLANES:
            e3 = jnp.concatenate([ref - b3[:, :m, :], b3[:, m:, :] - ref], axis=1)
        else:
            e3 = -jnp.abs(b3 - ref)
        return jnp.exp2(e3).reshape(ts, HG_DIM)
    if m == 2:
        pos = row_ids & 3
        f_next = pltpu.roll(f, ts - 1, axis=0)
        f_prev = pltpu.roll(f, 1, axis=0)
        return jnp.where(pos == 0, f_next, jnp.where(pos == 1, 1.0, jnp.where(pos == 2, f, f * f_prev)))
    assert m == 1
    return jnp.where((row_ids & 1) != 0, f, 1.0)


def _select_rows(m, upper_src, lower_src, row_ids):
    c = upper_src.shape[0]
    if m >= SUBLANES:
        pieces = []
        for b0 in range(0, c, 2 * m):
            pieces.append(lower_src[b0:b0 + m])
            pieces.append(upper_src[b0 + m:b0 + 2 * m])
        return jnp.concatenate(pieces, axis=0)
    return jnp.where((row_ids & m) != 0, upper_src, lower_src)


def _rotary_tables(pos_ref, invf_ref, ecos_ref, esin_ref):
    ang = invf_ref[...] * pos_ref[...].astype(F32)

    def spread(t, e_ref, extra):
        t_hi = t.astype(BF16).astype(F32)
        rows = jnp.concatenate([t_hi, t - t_hi] + extra, axis=0).astype(BF16)
        return _dot_tn(rows, e_ref[...])

    ones = jnp.ones_like(ang)
    cos_t = spread(jnp.cos(ang), ecos_ref, [ones, jnp.zeros_like(ang)])
    sin_t = spread(jnp.sin(ang), esin_ref, [])
    return cos_t, sin_t


def _mixer_kernel(sinks_ref, x_ref, gmix_ref, win_hbm, pos_ref, invf_ref, ecos_ref, esin_ref, wout_hbm,
                  gatt_ref, lbl_ref, ghg_ref, tri_ref, lmask_ref, bias_ref, *rest):
    n_own = 8
    n_side = (len(rest) - 1 - n_own) // 6
    side_src, o_ref, side_dst = rest[:n_side], rest[n_side], rest[n_side + 1:2 * n_side + 1]
    own = rest[2 * n_side + 1:2 * n_side + 1 + n_own]
    state_ref, kvprev_ref, win_ref, wout_ref, win_stage, wout_stage, win_sem, wout_sem = own
    side_scratch = rest[2 * n_side + 1 + n_own:]
    step = pl.program_id(1)
    ts = MIX_TILE
    blk = ATT_BLOCK

    t = pl.program_id(0) * pl.num_programs(1) + step

    @pl.when(t == 0)
    def _():
        _stage_weight(win_hbm, win_ref, win_stage, win_sem)
        _stage_weight(wout_hbm, wout_ref, wout_stage, wout_sem)

    @pl.when(lax.rem(t, CAST_STRIDE) == 0)
    def _():
        for issue in (True, False):
            for k in range(n_side):
                _cast_step(lax.div(t, CAST_STRIDE), issue, k % 2, side_src[k], side_dst[k],
                           *side_scratch[4 * k:4 * k + 4])

    @pl.when(step == 0)
    def _():
        state_ref[...] = jnp.zeros_like(state_ref)
        kvprev_ref[...] = jnp.zeros_like(kvprev_ref)

    tiles = []
    for k in range(TILES_PER_STEP):
        rows = pl.ds(k * ts, ts)
        no_prev = jnp.where(step > 0, 0, 1) if k == 0 else 0
        tiles.append(_mixer_tile(no_prev, sinks_ref, x_ref.at[rows, :], gmix_ref, win_ref, pos_ref.at[:, rows],
                                 invf_ref, ecos_ref, esin_ref, wout_ref, gatt_ref, lbl_ref, ghg_ref, tri_ref,
                                 lmask_ref, bias_ref, o_ref.at[rows, :], state_ref, kvprev_ref))
    next(tiles[0])
    for k in range(TILES_PER_STEP):
        next(tiles[k])
        if k + 1 < TILES_PER_STEP:
            next(tiles[k + 1])
        next(tiles[k], None)


def _mixer_tile(no_prev, sinks_ref, x_ref, gmix_ref, win_ref, pos_ref, invf_ref, ecos_ref, esin_ref, wout_ref,
                gatt_ref, lbl_ref, ghg_ref, tri_ref, lmask_ref, bias_ref, o_ref, state_ref, kvprev_ref):
    ts = MIX_TILE
    blk = ATT_BLOCK

    x = x_ref[...]
    hn = (x * _rms_scale(x) * gmix_ref[...]).astype(BF16)

    def proj(c0, c1):
        return _dot(hn, win_ref[:, c0:c1])

    cos, sin = _rotary_tables(pos_ref, invf_ref, ecos_ref, esin_ref)
    lane = lax.broadcasted_iota(jnp.int32, cos.shape, 1)
    first_half = (lane & (ATT_HEAD_DIM - 1)) < ROPE_HALF
    lo_head = lane < ATT_HEAD_DIM

    def rotate(col):
        partner = jnp.where(first_half,
                            pltpu.roll(col, LANES - ROPE_HALF, axis=1),
                            pltpu.roll(col, ROPE_HALF, axis=1))
        return col * cos + partner * sin

    def padded(col):
        swapped = pltpu.roll(col, ATT_HEAD_DIM, axis=1)
        zero = jnp.zeros_like(col)
        return [v.astype(BF16) for v in (jnp.where(lo_head, col, zero), jnp.where(lo_head, zero, swapped),
                                         jnp.where(lo_head, swapped, zero), jnp.where(lo_head, zero, col))]

    pq = proj(0, ATT_WIDTH)
    q_cols = [(rotate(pq[:, j * LANES:(j + 1) * LANES]) * (ATT_HEAD_DIM ** -0.5 * LOG2_E)).astype(BF16)
              for j in range(ATT_WIDTH // LANES)]
    pkv = proj(ATT_WIDTH, ATT_WIDTH + 2 * ATT_KV_WIDTH)
    kv_new = padded(rotate(pkv[:, :LANES])) + padded(pkv[:, LANES:])
    kv_all = [jnp.concatenate([kvprev_ref[:, i * LANES:(i + 1) * LANES], v], axis=0)
              for i, v in enumerate(kv_new)]
    for i, v in enumerate(kv_new):
        kvprev_ref[:, i * LANES:(i + 1) * LANES] = v[ts - blk:]
    r0c = ATT_WIDTH + 2 * ATT_KV_WIDTH
    p_q, p_f, p_i, p_g = (proj(r0c + k * HG_WIDTH, r0c + (k + 1) * HG_WIDTH) for k in range(4))

    yield

    upper_half_rows = lax.broadcasted_iota(jnp.int32, (2 * blk, 1), 0) >= blk
    lane_ids = lax.broadcasted_iota(jnp.int32, (2 * blk, LANES), 1)
    lane_lo = lane_ids < ATT_HEAD_DIM
    ones_lo = jnp.where(lane_lo, 1.0, 0.0).astype(BF16)
    ones_hi = jnp.where(lane_lo, 0.0, 1.0).astype(BF16)
    bf16_rows = 2 * SUBLANES
    drop_row0 = jnp.where(lax.broadcasted_iota(jnp.int32, (bf16_rows, LANES), 0) > 0, 1.0, 0.0).astype(BF16)

    def sink_slot(op):
        return jnp.concatenate([op[:bf16_rows] * drop_row0, op[bf16_rows:]], axis=0)

    att_units = {}

    def att_scores(jb, gk):
        keys = slice(jb * blk, (jb + 2) * blk)
        bias = bias_ref[no_prev] if jb == 0 else bias_ref[0]
        q2 = jnp.concatenate([q_cols[2 * gk][jb * blk:(jb + 1) * blk],
                              q_cols[2 * gk + 1][jb * blk:(jb + 1) * blk]], axis=0)
        k_lo = sink_slot(kv_all[2 * gk][keys])
        k_hi = sink_slot(kv_all[2 * gk + 1][keys])
        h0 = ATT_GROUP * gk
        probs = []
        for which, k_op in ((0, k_lo), (1, k_hi)):
            sink = jnp.where(upper_half_rows, sinks_ref[h0 + 2 + which], sinks_ref[h0 + which]) * LOG2_E
            s = _dot_nt(q2, k_op) + bias
            s_first = s[:, :LANES] + jnp.where(lane_ids == 0, sink, 0.0)
            m = jnp.max(jnp.maximum(s_first, s[:, LANES:]), axis=-1, keepdims=True)
            probs.append(jnp.exp2(jnp.concatenate([s_first, s[:, LANES:]], axis=1) - m).astype(BF16))
        att_units[jb, gk] = jnp.concatenate(probs, axis=1)

    def att_values(jb, gk):
        keys = slice(jb * blk, (jb + 2) * blk)
        r_lo = jnp.concatenate([sink_slot(kv_all[4 + 2 * gk][keys]), ones_lo], axis=1)
        r_hi = jnp.concatenate([sink_slot(kv_all[5 + 2 * gk][keys]), ones_hi], axis=1)
        out = _dot(att_units[jb, gk], jnp.concatenate([r_lo, r_hi], axis=0))
        att_units[jb, gk] = out[:, :LANES] / out[:, LANES:]

    units = [(jb, gk) for jb in range(ts // blk) for gk in range(ATT_KV_HEADS)]
    att_scores(*units[0])
    for prev_u, u in zip(units[:-1], units[1:]):
        att_scores(*u)
        att_values(*prev_u)
    att_values(*units[-1])
    att_rows = []
    for jb in range(ts // blk):
        cols_ = []
        for gk in range(ATT_KV_HEADS):
            cols_ += [att_units[jb, gk][:blk], att_units[jb, gk][blk:]]
        att_rows.append(jnp.concatenate(cols_, axis=1))
    att = jnp.concatenate(att_rows, axis=0)

    lbl = lbl_ref[...]
    lexp = jnp.exp(lbl - jnp.max(lbl, axis=0, keepdims=True))
    lb = lexp[0:1, :] / jnp.sum(lexp, axis=0, keepdims=True)
    tri = tri_ref[...]
    row_ids = lax.broadcasted_iota(jnp.int32, (ts, HG_DIM), 0)
    half = ts // 2
    f_all = lb + (1.0 - lb) * (1.0 / (1.0 + jnp.exp(-p_f)))
    g_all = jnp.log(f_all) * LOG2_E
    g_hi = g_all.astype(BF16)
    g_lo = (g_all - g_hi.astype(F32)).astype(BF16)
    b2 = _dot(tri, jnp.concatenate([g_hi, g_lo], axis=1))
    b_all = b2[:, :HG_WIDTH] + b2[:, HG_WIDTH:]
    n_tiles = half // SUBLANES
    hs = [dict() for _ in range(HG_HEADS)]

    def prep(h):
        cols = slice(h * HG_DIM, (h + 1) * HG_DIM)
        f = f_all[:, cols]
        b = b_all[:, cols]
        kk = 1.0 - f
        qf = _silu(p_q[:, cols])
        vv = p_i[:, cols].astype(BF16)
        w_incl = jnp.exp2(b)
        w_tail = jnp.exp2(b[ts - 1:ts, :] - b)
        st = state_ref[h]
        o = _dot_nt((qf * w_incl).astype(BF16), st.astype(BF16))
        state_ref[h] = st * w_incl[ts - 1:ts, :] + _dot_tn(vv, (kk * w_tail).astype(BF16))
        hs[h].update(f=f, b=b, kk=kk, qf=qf, vv=vv, o=o, cols=cols)

    def levels(h):
        d = hs[h]
        qf, kk, b, f = d["qf"], d["kk"], d["b"], d["f"]
        quad = [[None] * n_tiles, [None] * n_tiles]
        for li, m in enumerate(HG_LEVELS):
            x_f = _select_rows(m, qf, kk, row_ids) * _level_decay(m, b, f, row_ids)
            x_l = x_f.astype(BF16)
            if 2 * m == ts:
                d["s10"] = _dot_nt(x_l[half:], x_l[:half])
                continue
            if m >= SUBLANES:
                tiles = [t for t in range(n_tiles) if (t * SUBLANES // m) % 2 == 1]
            else:
                tiles = list(range(n_tiles))
            for hf in range(2):
                r0 = hf * half
                if len(tiles) == n_tiles:
                    lhs = x_l[r0:r0 + half]
                else:
                    lhs = jnp.concatenate([x_f[r0 + t * SUBLANES:r0 + (t + 1) * SUBLANES] for t in tiles],
                                          axis=0).astype(BF16)
                sc = _dot_nt(lhs, x_l[r0:r0 + half])
                for i, t in enumerate(tiles):
                    m0 = (li - 1) * LANES + t * SUBLANES
                    part = sc[i * SUBLANES:(i + 1) * SUBLANES] * lmask_ref[m0:m0 + SUBLANES, :]
                    quad[hf][t] = part if quad[hf][t] is None else quad[hf][t] + part
        d["s00"] = jnp.concatenate(quad[0], axis=0).astype(BF16)
        d["s1"] = jnp.concatenate([d["s10"], jnp.concatenate(quad[1], axis=0)], axis=1).astype(BF16)

    def finish(h):
        d = hs[h]
        vv = d["vv"]
        o_top = _dot(d["s00"], vv[:half])
        o_bot = _dot(d["s1"], vv)
        diag = jnp.sum(d["qf"] * d["kk"], axis=-1, keepdims=True)
        o = d["o"] + jnp.concatenate([o_top, o_bot], axis=0) + diag * p_i[:, d["cols"]]
        d["rec"] = o * _rms_scale(o)

    prep(0)
    prep(1)
    levels(0)
    prep(2)
    levels(1)
    prep(3)
    levels(2)
    finish(0)
    levels(3)
    finish(1)
    finish(2)
    finish(3)
    rec_cols = [hs[h]["rec"] for h in range(HG_HEADS)]
    rec_n = (jnp.concatenate(rec_cols, axis=1) * ghg_ref[...] * _silu(p_g)).astype(BF16)

    yield

    att_n = (att * _rms_scale(att) * gatt_ref[...]).astype(BF16)
    o_ref[...] = x + _dot(jnp.concatenate([att_n, rec_n], axis=1), wout_ref[...])


def _cast_chunk_rows(n_rows, n_steps):
    tile = 2 * SUBLANES
    rows = tile
    while n_rows % rows or n_rows // rows > n_steps:
        rows += tile
    return rows


def _mixer(x, gmix, w_in, positions, w_out, sinks, gatt, lbl, ghg, side_weights):
    b, s, _ = x.shape
    ts = TILES_PER_STEP * MIX_TILE
    tri, lmask = _hgrn_constants()
    invf, ecos, esin = _rotary_constants()
    bias = _attention_bias()
    pos = positions.reshape(b * (s // ts), 1, ts)
    cur = lambda bi, i: (bi, i, 0)
    fixed = lambda bi, i: (0, 0)
    hbm = pl.BlockSpec(memory_space=pl.ANY)
    n_cast_steps = b * (s // ts) // CAST_STRIDE
    side_scratch = []
    for w in side_weights:
        stage = (2, _cast_chunk_rows(w.shape[0], n_cast_steps), w.shape[1])
        side_scratch += [pltpu.VMEM(stage, F32), pltpu.VMEM(stage, BF16),
                         pltpu.SemaphoreType.DMA((2,)), pltpu.SemaphoreType.DMA((2,))]
    outs = pl.pallas_call(
        _mixer_kernel,
        grid=(b, s // ts),
        in_specs=[
            pl.BlockSpec(memory_space=pltpu.SMEM),
            pl.BlockSpec((None, ts, D_MODEL), cur),
            pl.BlockSpec((1, D_MODEL), fixed),
            hbm,
            pl.BlockSpec((None, 1, ts), lambda bi, i: (bi * (s // ts) + i, 0, 0)),
            pl.BlockSpec(invf.shape, fixed),
            pl.BlockSpec(ecos.shape, fixed),
            pl.BlockSpec(esin.shape, fixed),
            hbm,
            pl.BlockSpec((1, ATT_WIDTH), fixed),
            pl.BlockSpec(lbl.shape, fixed),
            pl.BlockSpec((1, HG_WIDTH), fixed),
            pl.BlockSpec(tri.shape, fixed),
            pl.BlockSpec(lmask.shape, fixed),
            pl.BlockSpec(bias.shape, lambda bi, i: (0, 0, 0)),
        ] + [hbm] * len(side_weights),
        out_specs=[pl.BlockSpec((None, ts, D_MODEL), cur)] + [hbm] * len(side_weights),
        out_shape=[jax.ShapeDtypeStruct((b, s, D_MODEL), F32)]
        + [jax.ShapeDtypeStruct(w.shape, BF16) for w in side_weights],
        scratch_shapes=[
            pltpu.VMEM((HG_HEADS, HG_DIM, HG_DIM), F32),
            pltpu.VMEM((ATT_BLOCK, KV_PAD_WIDTH), BF16),
            pltpu.VMEM(w_in.shape, BF16),
            pltpu.VMEM(w_out.shape, BF16),
            pltpu.VMEM((2, OWN_STAGE_ROWS, w_in.shape[1]), F32),
            pltpu.VMEM((2, OWN_STAGE_ROWS, w_out.shape[1]), F32),
            pltpu.SemaphoreType.DMA((2,)),
            pltpu.SemaphoreType.DMA((2,)),
        ] + side_scratch,
        compiler_params=pltpu.CompilerParams(
            dimension_semantics=("arbitrary", "arbitrary"), vmem_limit_bytes=VMEM_LIMIT_BYTES),
        name="mixer",
    )(sinks, x, gmix, w_in, pos, invf, ecos, esin, w_out, gatt, lbl, ghg, tri, lmask, bias, *side_weights)
    return outs[0], outs[1:]


XATTN_SUB_TILES = 2


def _xattn_kernel(x_ref, gain_ref, wq_ref, mem_ref, gmem_ref, wkv_ref, wo_ref, o_ref, kv_ref):
    @pl.when(pl.program_id(1) == 0)
    def _():
        m = mem_ref[...]
        mn = (m * _rms_scale(m) * gmem_ref[...]).astype(BF16)
        kv_ref[...] = _dot(mn, wkv_ref[...]).astype(BF16)

    n_sub = XATTN_SUB_TILES
    sub = x_ref.shape[0] // n_sub
    q = [None] * n_sub
    probs = [[None] * X_HEADS for _ in range(n_sub)]
    outs = [[None] * X_HEADS for _ in range(n_sub)]

    def qproj(i):
        x = x_ref[i * sub:(i + 1) * sub, :]
        hq = (x * _rms_scale(x) * gain_ref[...]).astype(BF16)
        q[i] = (_dot(hq, wq_ref[...]) * (X_HEAD_DIM ** -0.5 * LOG2_E)).astype(BF16)

    def scores(i, h):
        cols = slice(h * X_HEAD_DIM, (h + 1) * X_HEAD_DIM)
        s = _dot_nt(q[i][:, cols], kv_ref[:, cols])
        probs[i][h] = jnp.exp2(s - jnp.max(s, axis=-1, keepdims=True))

    def values(i, h):
        p = probs[i][h]
        o = _dot(p.astype(BF16), kv_ref[:, D_MODEL + h * X_HEAD_DIM:D_MODEL + (h + 1) * X_HEAD_DIM])
        outs[i][h] = (o / jnp.sum(p, axis=-1, keepdims=True)).astype(BF16)

    def wo(i):
        rows = slice(i * sub, (i + 1) * sub)
        o_ref[rows, :] = x_ref[rows, :] + _dot(jnp.concatenate(outs[i], axis=1), wo_ref[...])

    for i in range(n_sub):
        qproj(i)
    for i in range(n_sub):
        scores(i, 0)
    for h in range(1, X_HEADS):
        for i in range(n_sub):
            scores(i, h)
            values(i, h - 1)
    for i in range(n_sub):
        values(i, X_HEADS - 1)
    for i in range(n_sub):
        wo(i)


def _xattn(x, gain, w_xq, mem, gmem, w_xkv, w_xo, tm):
    b, s, _ = x.shape
    m = mem.shape[1]
    cur = lambda bi, i: (bi, i, 0)
    per_batch = lambda bi, i: (bi, 0, 0)
    fixed = lambda bi, i: (0, 0)
    once = dict(pipeline_mode=pl.Buffered(1))
    return pl.pallas_call(
        _xattn_kernel,
        grid=(b, s // tm),
        in_specs=[
            pl.BlockSpec((None, tm, D_MODEL), cur),
            pl.BlockSpec((1, D_MODEL), fixed),
            pl.BlockSpec((D_MODEL, D_MODEL), fixed, **once),
            pl.BlockSpec((None, m, D_MODEL), per_batch),
            pl.BlockSpec((1, D_MODEL), fixed),
            pl.BlockSpec((D_MODEL, 2 * D_MODEL), fixed, **once),
            pl.BlockSpec((D_MODEL, D_MODEL), fixed, **once),
        ],
        out_specs=pl.BlockSpec((None, tm, D_MODEL), cur),
        out_shape=jax.ShapeDtypeStruct((b, s, D_MODEL), F32),
        scratch_shapes=[pltpu.VMEM((m, 2 * D_MODEL), BF16)],
        compiler_params=pltpu.CompilerParams(
            dimension_semantics=("arbitrary", "arbitrary"), vmem_limit_bytes=VMEM_LIMIT_BYTES),
        name="xattn",
    )(x, gain, w_xq, mem, gmem, w_xkv, w_xo)


FFN_CHUNKS = ((0, 1024), (1024, 2048), (2048, FFN_HIDDEN))
FFN_SUB_TILES = 4

def _ffn_kernel(x_ref, gain_ref, wgu_ref, wd_ref, gfin_ref, o_ref):
    n_sub = FFN_SUB_TILES
    sub = x_ref.shape[0] // n_sub
    hf, acc = [], []
    for i in range(n_sub):
        x = x_ref[i * sub:(i + 1) * sub, :]
        hf.append((x * _rms_scale(x) * gain_ref[...]).astype(BF16))
        acc.append(x)
    for lo, hi in FFN_CHUNKS:
        for i in range(n_sub):
            gate = _dot(hf[i], wgu_ref[:, lo:hi])
            upv = _dot(hf[i], wgu_ref[:, FFN_HIDDEN + lo:FFN_HIDDEN + hi])
            act = (_silu(gate) * upv).astype(BF16)
            acc[i] = acc[i] + _dot(act, wd_ref[lo:hi, :])
    for i in range(n_sub):
        o_ref[i * sub:(i + 1) * sub, :] = acc[i] * _rms_scale(acc[i]) * gfin_ref[...]


def _ffn(x2d, gain, w_gu, w_d, gfin, tm):
    n = x2d.shape[0]
    row = lambda i: (i, 0)
    fixed = lambda i: (0, 0)
    return pl.pallas_call(
        _ffn_kernel,
        grid=(n // tm,),
        in_specs=[
            pl.BlockSpec((tm, D_MODEL), row),
            pl.BlockSpec((1, D_MODEL), fixed),
            pl.BlockSpec((D_MODEL, 2 * FFN_HIDDEN), fixed, pipeline_mode=pl.Buffered(1)),
            pl.BlockSpec((FFN_HIDDEN, D_MODEL), fixed, pipeline_mode=pl.Buffered(1)),
            pl.BlockSpec((1, D_MODEL), fixed),
        ],
        out_specs=pl.BlockSpec((tm, D_MODEL), row),
        out_shape=jax.ShapeDtypeStruct((n, D_MODEL), F32),
        compiler_params=pltpu.CompilerParams(
            dimension_semantics=("arbitrary",), vmem_limit_bytes=VMEM_LIMIT_BYTES),
        name="ffn",
    )(x2d, gain, w_gu, w_d, gfin)


def kernel(x, mem, positions, norm_mix, w_in, att_sinks, att_out_gain, hg_lb_logits, hg_out_gain,
           w_out, norm_xattn, norm_mem, w_xq, w_xkv, w_xo, norm_ffn, w_gate_up, w_down, norm_final):
    b, s, d = x.shape
    n = b * s
    assert w_in.shape[0] == 1 and hg_lb_logits.shape[0] == 2, "single-layer block only"
    assert all(a.dtype == F32 for a in (x, mem, w_in, w_out, w_xq, w_xkv, w_xo, w_gate_up, w_down))
    row = lambda v: v.reshape(1, -1)
    side = [w[0] for w in (w_xkv, w_xq, w_xo, w_gate_up, w_down)]
    x, (w_xkv_b, w_xq_b, w_xo_b, w_gu_b, w_d_b) = _mixer(
        x, row(norm_mix[0]), w_in[0], positions,
        w_out[0], att_sinks[0], row(att_out_gain[0]),
        hg_lb_logits, row(hg_out_gain[0]), side)
    x = _xattn(x, row(norm_xattn[0]), w_xq_b, mem, row(norm_mem[0]), w_xkv_b, w_xo_b, tm=DENSE_TILE)
    y = _ffn(x.reshape(n, d), row(norm_ffn[0]), w_gu_b, w_d_b, row(norm_final), tm=DENSE_TILE)
    return y.reshape(b, s, d)
```
